```python
import math
import jax
import jax.numpy as jnp
from jax import lax
import numpy as np

D_MODEL = 4096
BATCH = 4
SEQ = 2048
DEPTH = 2
DEC_BATCH = 8
DEC_SEQ = 1
PAST_LEN = 16384
PAGE_SIZE = 128

N_EVEN = (DEPTH + 1) // 2
N_ODD = DEPTH // 2
HEAD_DIM = 128
A_HEADS = D_MODEL // (2 * HEAD_DIM)
D_A = A_HEADS * HEAD_DIM
A_WINDOWS = (128, 512, 2048)
A_DILATIONS = (1, 4, 16)
N_DIL = len(A_WINDOWS)
ROPE_THETA = 10000.0
D_B = D_MODEL - D_A
SSM_GROUP = 16
SSM_GROUPS = D_B // SSM_GROUP
SSM_STATE = 64
D_IN_EVEN = N_DIL * 3 * D_A + D_B
D_CONV = D_MODEL
CONV_WIDTH = 31
D_FF = 11008
FFN_CONV_WIDTH = 3
RMS_EPS = 1e-6
LN_EPS = 1e-5

kernel_name = 'hybrid_dilated_s5_conformer_decode_step'


def _a_buf_lens():
    return tuple(min(w, PAST_LEN) for w in A_WINDOWS)


def _rmsnorm(x, g):
    xf = x.astype(jnp.float32)
    y = xf * lax.rsqrt(jnp.mean(xf * xf, axis=-1, keepdims=True) + RMS_EPS)
    return (y * g.astype(jnp.float32)).astype(x.dtype)


def _layernorm(x, g, b):
    xf = x.astype(jnp.float32)
    mu = jnp.mean(xf, axis=-1, keepdims=True)
    var = jnp.mean(jnp.square(xf - mu), axis=-1, keepdims=True)
    y = (xf - mu) * lax.rsqrt(var + LN_EPS) * g.astype(jnp.float32) + b.astype(jnp.float32)
    return y.astype(x.dtype)


def _rope(x, pos):
    half = HEAD_DIM // 2
    inv_freq = ROPE_THETA ** (-jnp.arange(half, dtype=jnp.float32) / half)
    ang = pos.astype(jnp.float32)[:, None] * inv_freq[None, :]
    cos = jnp.cos(ang)[None, :, None, :]
    sin = jnp.sin(ang)[None, :, None, :]
    xf = x.astype(jnp.float32)
    x1, x2 = xf[..., :half], xf[..., half:]
    return jnp.concatenate([x1 * cos - x2 * sin, x1 * sin + x2 * cos], axis=-1).astype(x.dtype)


def _dilated_band_attention(q, k, v, dilation, steps):
    n, s, h, hd = q.shape
    sub = s // dilation
    bd = n * dilation

    def to_sub(t):
        return t.reshape(n, sub, dilation, h, hd).transpose(0, 2, 1, 3, 4).reshape(bd, sub, h, hd)

    qs, ks, vs = to_sub(q), to_sub(k), to_sub(v)
    nb = -(-sub // steps)
    pad_r = nb * steps - sub
    qs = jnp.pad(qs, ((0, 0), (0, pad_r), (0, 0), (0, 0)))
    kpad = ((0, 0), (steps, pad_r), (0, 0), (0, 0))
    kb = jnp.pad(ks, kpad).reshape(bd, nb + 1, steps, h, hd)
    vb = jnp.pad(vs, kpad).reshape(bd, nb + 1, steps, h, hd)
    k_win = jnp.concatenate([kb[:, :-1], kb[:, 1:]], axis=2)
    v_win = jnp.concatenate([vb[:, :-1], vb[:, 1:]], axis=2)
    q_blk = qs.reshape(bd, nb, steps, h, hd)
    scores = jnp.einsum('bnqhd,bnkhd->bnhqk', q_blk, k_win,
                        preferred_element_type=jnp.float32) * (hd ** -0.5)
    qi = jnp.arange(steps)[:, None]
    ki = jnp.arange(2 * steps)[None, :]
    key_sub = jnp.arange(nb)[:, None, None] * steps + ki[None] - steps
    valid = (ki >= qi)[None] & (ki <= qi + steps)[None] & (key_sub >= 0)
    scores = jnp.where(valid[None, :, None], scores, -jnp.inf)
    m = jnp.max(scores, axis=-1, keepdims=True)
    p = jnp.exp(scores - m)
    l = jnp.sum(p, axis=-1, keepdims=True)
    o = jnp.einsum('bnhqk,bnkhd->bnqhd', p.astype(v.dtype), v_win,
                   preferred_element_type=jnp.float32)
    o = o / jnp.swapaxes(l, 2, 3)
    lse = jnp.swapaxes((m + jnp.log(l))[..., 0], 2, 3)
    o = o.reshape(bd, nb * steps, h, hd)[:, :sub]
    o = o.reshape(n, dilation, sub, h, hd).transpose(0, 2, 1, 3, 4).reshape(n, s, h, hd)
    lse = lse.reshape(bd, nb * steps, h)[:, :sub]
    lse = lse.reshape(n, dilation, sub, h).transpose(0, 2, 1, 3).reshape(n, s, h)
    return o, lse


def _dilated_step_attention(q, k, v, kv_buf, dilation, window):
    n, t, h, hd = q.shape
    buf_len = kv_buf.shape[1]
    k_all = jnp.concatenate([kv_buf[:, :, 0].astype(k.dtype), k], axis=1)
    v_all = jnp.concatenate([kv_buf[:, :, 1].astype(v.dtype), v], axis=1)
    offs = jnp.arange(window // dilation + 1) * dilation
    rows = (buf_len + jnp.arange(t))[:, None] - offs[None, :]
    valid = rows >= 0
    rows = jnp.maximum(rows, 0)
    k_g = jnp.take(k_all, rows, axis=1)
    v_g = jnp.take(v_all, rows, axis=1)
    scores = jnp.einsum('bthd,btjhd->bthj', q, k_g,
                        preferred_element_type=jnp.float32) * (hd ** -0.5)
    scores = jnp.where(valid[None, :, None, :], scores, -jnp.inf)
    m = jnp.max(scores, axis=-1, keepdims=True)
    p = jnp.exp(scores - m)
    l = jnp.sum(p, axis=-1, keepdims=True)
    o = jnp.einsum('bthj,btjhd->bthd', p.astype(v.dtype), v_g,
                   preferred_element_type=jnp.float32) / l
    lse = (m + jnp.log(l))[..., 0]
    return o, lse


def _last_rows(prev, new, length):
    ext = new if prev is None else jnp.concatenate([prev.astype(new.dtype), new], axis=1)
    r = ext.shape[1]
    if r >= length:
        return ext[:, r - length:]
    pad = [(0, 0)] * ext.ndim
    pad[1] = (length - r, 0)
    return jnp.pad(ext, pad)


def _ssm_combine(e1, e2):
    ar1, ai1, br1, bi1 = e1
    ar2, ai2, br2, bi2 = e2
    return (ar2 * ar1 - ai2 * ai1,
            ar2 * ai1 + ai2 * ar1,
            ar2 * br1 - ai2 * bi1 + br2,
            ar2 * bi1 + ai2 * br1 + bi2)


def _s5(u, a_re, a_im, b_re, b_im, c_re, c_im, d_skip, log_dt, h0):
    f32 = jnp.float32
    lr, li = a_re.astype(f32), a_im.astype(f32)
    dt = jnp.exp(log_dt.astype(f32))[:, None]
    mag = jnp.exp(lr * dt)
    ab_re, ab_im = mag * jnp.cos(li * dt), mag * jnp.sin(li * dt)
    den = lr * lr + li * li
    z_re = ((ab_re - 1.0) * lr + ab_im * li) / den
    z_im = (ab_im * lr - (ab_re - 1.0) * li) / den
    br, bi = b_re.astype(f32), b_im.astype(f32)
    bb_re = z_re[..., None] * br - z_im[..., None] * bi
    bb_im = z_re[..., None] * bi + z_im[..., None] * br
    uf = u.astype(f32)
    bu_re = jnp.einsum('ntgc,gpc->ntgp', uf, bb_re)
    bu_im = jnp.einsum('ntgc,gpc->ntgp', uf, bb_im)
    h0r, h0i = h0[..., 0].astype(f32), h0[..., 1].astype(f32)
    bu_re = bu_re.at[:, 0].add(ab_re * h0r - ab_im * h0i)
    bu_im = bu_im.at[:, 0].add(ab_re * h0i + ab_im * h0r)
    a_r = jnp.broadcast_to(ab_re, bu_re.shape)
    a_i = jnp.broadcast_to(ab_im, bu_re.shape)
    _, _, h_re, h_im = lax.associative_scan(_ssm_combine, (a_r, a_i, bu_re, bu_im), axis=1)
    y = (jnp.einsum('ntgp,gcp->ntgc', h_re, c_re.astype(f32))
         - jnp.einsum('ntgp,gcp->ntgc', h_im, c_im.astype(f32))
         + d_skip.astype(f32) * uf)
    h_last = jnp.stack([h_re[:, -1], h_im[:, -1]], axis=-1)
    return y, h_last


def _causal_depthwise(ext, w, t):
    w = w.astype(ext.dtype)
    out = ext[:, 0:t] * w[0]
    for j in range(1, w.shape[0]):
        out = out + ext[:, j:j + t] * w[j]
    return out


def _even_mixer(hn, pos, kv_bufs, h0, w_in, w_out, a_re, a_im, b_re, b_im, c_re, c_im,
                d_skip, log_dt, w_glu, b_glu):
    n, t, _ = hn.shape
    z = hn @ w_in
    n_qkv = N_DIL * 3 * D_A
    qkv = z[..., :n_qkv].reshape(n, t, N_DIL, 3, A_HEADS, HEAD_DIM)
    u = z[..., n_qkv:].reshape(n, t, SSM_GROUPS, SSM_GROUP)
    buf_lens = _a_buf_lens()
    outs, lses, new_kv = [], [], []
    for g in range(N_DIL):
        q = _rope(qkv[:, :, g, 0], pos)
        k = _rope(qkv[:, :, g, 1], pos)
        v = qkv[:, :, g, 2]
        if kv_bufs is None:
            o, lse = _dilated_band_attention(q, k, v, A_DILATIONS[g], A_WINDOWS[g] // A_DILATIONS[g])
            prev = None
        else:
            o, lse = _dilated_step_attention(q, k, v, kv_bufs[g], A_DILATIONS[g], A_WINDOWS[g])
            prev = kv_bufs[g]
        new_kv.append(_last_rows(prev, jnp.stack([k, v], axis=2), buf_lens[g]))
        outs.append(o)
        lses.append(lse)
    mix_w = jax.nn.softmax(jnp.stack(lses), axis=0)
    attn = jnp.einsum('gnth,gnthd->nthd', mix_w, jnp.stack(outs)).reshape(n, t, D_A)
    if h0 is None:
        h0 = jnp.zeros((n, SSM_GROUPS, SSM_STATE, 2), jnp.float32)
    y, h_new = _s5(u, a_re, a_im, b_re, b_im, c_re, c_im, d_skip, log_dt, h0)
    y = jax.nn.gelu(y.reshape(n, t, D_B))
    ssm_out = y * jax.nn.sigmoid(y @ w_glu.astype(jnp.float32) + b_glu.astype(jnp.float32))
    cat = jnp.concatenate([attn, ssm_out], axis=-1).astype(hn.dtype)
    return cat @ w_out, new_kv, h_new


def _conv_module(hn, buf, w_pw1, b_pw1, w_dw, b_dw, ln_g, ln_b, w_pw2, b_pw2):
    n, t, _ = hn.shape
    z = hn @ w_pw1 + b_pw1
    a, gate = jnp.split(z, 2, axis=-1)
    u = a * jax.nn.sigmoid(gate)
    if buf is None:
        buf = jnp.zeros((n, CONV_WIDTH - 1, D_CONV), u.dtype)
    ext = jnp.concatenate([buf.astype(u.dtype), u], axis=1)
    y = _causal_depthwise(ext, w_dw, t) + b_dw
    y = _layernorm(y, ln_g, ln_b)
    y = y * jax.nn.sigmoid(y)
    return y @ w_pw2 + b_pw2, ext[:, -(CONV_WIDTH - 1):]


def _conv_ffn(hn, buf, w_up, w_dw, b_dw, w_down):
    n, t, _ = hn.shape
    z = hn @ w_up
    if buf is None:
        buf = jnp.zeros((n, FFN_CONV_WIDTH - 1, 2 * D_FF), z.dtype)
    ext = jnp.concatenate([buf.astype(z.dtype), z], axis=1)
    zc = _causal_depthwise(ext, w_dw, t) + b_dw
    g, v = jnp.split(zc, 2, axis=-1)
    return (jax.nn.silu(g) * v) @ w_down, ext[:, -(FFN_CONV_WIDTH - 1):]


def _trunk(x, pos, st, p):
    new_a = [[] for _ in range(N_DIL)]
    new_ssm, new_conv, new_ffn = [], [], []
    for i in range(DEPTH):
        j = i // 2
        hn = _rmsnorm(x, p['norm_mix'][i])
        if i % 2 == 0:
            kv_bufs = None if st is None else [st['a'][g][j] for g in range(N_DIL)]
            h0 = None if st is None else st['ssm'][j]
            mix, kvs, h_new = _even_mixer(
                hn, pos, kv_bufs, h0, p['w_in_even'][j], p['w_out_even'][j],
                p['ssm_a_re'][j], p['ssm_a_im'][j], p['ssm_b_re'][j], p['ssm_b_im'][j],
                p['ssm_c_re'][j], p['ssm_c_im'][j], p['ssm_d'][j], p['ssm_log_dt'][j],
                p['w_glu'][j], p['b_glu'][j])
            for g in range(N_DIL):
                new_a[g].append(kvs[g])
            new_ssm.append(h_new)
        else:
            cb = None if st is None else st['conv'][j]
            mix, c_new = _conv_module(
                hn, cb, p['w_pw1'][j], p['b_pw1'][j], p['w_dw'][j], p['b_dw'][j],
                p['ln_g'][j], p['ln_b'][j], p['w_pw2'][j], p['b_pw2'][j])
            new_conv.append(c_new)
        x = x + mix.astype(x.dtype)
        fb = None if st is None else st['ffn'][i]
        f, f_new = _conv_ffn(_rmsnorm(x, p['norm_ffn'][i]), fb, p['w_up'][i],
                             p['ffn_dw'][i], p['ffn_dw_b'][i], p['w_down'][i])
        new_ffn.append(f_new)
        x = x + f.astype(x.dtype)
    y = _rmsnorm(x, p['norm_final'])
    return (y, [jnp.stack(a) for a in new_a], jnp.stack(new_ssm),
            jnp.stack(new_conv), jnp.stack(new_ffn))


def setup_inputs(seed: int = 0) -> dict:
    key = jax.random.key(seed)
    keys = iter(jax.random.split(key, 40))
    f32 = jnp.float32

    def nrm(shape, scale):
        return jax.random.normal(next(keys), shape, f32) * scale

    buf = _a_buf_lens()
    x_prompt = nrm((BATCH, SEQ, D_MODEL), 1.0)
    x_sample = nrm((DEC_BATCH, DEC_SEQ, D_MODEL), 1.0)
    cache_a0 = nrm((N_EVEN, DEC_BATCH, buf[0], 2, A_HEADS, HEAD_DIM), 1.0)
    cache_a1 = nrm((N_EVEN, DEC_BATCH, buf[1], 2, A_HEADS, HEAD_DIM), 1.0)
    cache_a2 = nrm((N_EVEN, DEC_BATCH, buf[2], 2, A_HEADS, HEAD_DIM), 1.0)
    state_ssm = nrm((N_EVEN, DEC_BATCH, SSM_GROUPS, SSM_STATE, 2), 0.1)
    state_conv = nrm((N_ODD, DEC_BATCH, CONV_WIDTH - 1, D_CONV), 0.5)
    state_ffn = nrm((DEPTH, DEC_BATCH, FFN_CONV_WIDTH - 1, 2 * D_FF), 1.0)
    norm_mix = 1.0 + nrm((DEPTH, D_MODEL), 0.02)
    norm_ffn = 1.0 + nrm((DEPTH, D_MODEL), 0.02)
    norm_final = 1.0 + nrm((D_MODEL,), 0.02)
    w_in_even = nrm((N_EVEN, D_MODEL, D_IN_EVEN), D_MODEL ** -0.5)
    w_out_even = nrm((N_EVEN, D_A + D_B, D_MODEL), (D_A + D_B) ** -0.5)
    ssm_a_re = -0.5 + nrm((N_EVEN, SSM_GROUPS, SSM_STATE), 0.01)
    ssm_a_im = jnp.pi * jnp.arange(SSM_STATE, dtype=f32) + nrm((N_EVEN, SSM_GROUPS, SSM_STATE), 0.01)
    ssm_b_re = nrm((N_EVEN, SSM_GROUPS, SSM_STATE, SSM_GROUP), (2 * SSM_GROUP) ** -0.5)
    ssm_b_im = nrm((N_EVEN, SSM_GROUPS, SSM_STATE, SSM_GROUP), (2 * SSM_GROUP) ** -0.5)
    ssm_c_re = nrm((N_EVEN, SSM_GROUPS, SSM_GROUP, SSM_STATE), SSM_STATE ** -0.5)
    ssm_c_im = nrm((N_EVEN, SSM_GROUPS, SSM_GROUP, SSM_STATE), SSM_STATE ** -0.5)
    ssm_d = nrm((N_EVEN, SSM_GROUPS, SSM_GROUP), 1.0)
    ssm_log_dt = jax.random.uniform(next(keys), (N_EVEN, SSM_GROUPS), f32,
                                    minval=math.log(1e-3), maxval=math.log(1e-1))
    w_glu = nrm((N_EVEN, D_B, D_B), D_B ** -0.5)
    b_glu = nrm((N_EVEN, D_B), 0.02)
    w_pw1 = nrm((N_ODD, D_MODEL, 2 * D_CONV), D_MODEL ** -0.5)
    b_pw1 = nrm((N_ODD, 2 * D_CONV), 0.02)
    w_dw = nrm((N_ODD, CONV_WIDTH, D_CONV), CONV_WIDTH ** -0.5)
    b_dw = nrm((N_ODD, D_CONV), 0.02)
    ln_g = 1.0 + nrm((N_ODD, D_CONV), 0.02)
    ln_b = nrm((N_ODD, D_CONV), 0.02)
    w_pw2 = nrm((N_ODD, D_CONV, D_MODEL), D_CONV ** -0.5)
    b_pw2 = nrm((N_ODD, D_MODEL), 0.02)
    w_up = nrm((DEPTH, D_MODEL, 2 * D_FF), D_MODEL ** -0.5)
    ffn_dw = nrm((DEPTH, FFN_CONV_WIDTH, 2 * D_FF), FFN_CONV_WIDTH ** -0.5)
    ffn_dw_b = nrm((DEPTH, 2 * D_FF), 0.02)
    w_down = nrm((DEPTH, D_FF, D_MODEL), D_FF ** -0.5)
    return {
        'x_prompt': x_prompt, 'x_sample': x_sample,
        'cache_a0': cache_a0, 'cache_a1': cache_a1, 'cache_a2': cache_a2,
        'state_ssm': state_ssm, 'state_conv': state_conv, 'state_ffn': state_ffn,
        'norm_mix': norm_mix, 'norm_ffn': norm_ffn, 'norm_final': norm_final,
        'w_in_even': w_in_even, 'w_out_even': w_out_even,
        'ssm_a_re': ssm_a_re, 'ssm_a_im': ssm_a_im, 'ssm_b_re': ssm_b_re, 'ssm_b_im': ssm_b_im,
        'ssm_c_re': ssm_c_re, 'ssm_c_im': ssm_c_im, 'ssm_d': ssm_d, 'ssm_log_dt': ssm_log_dt,
        'w_glu': w_glu, 'b_glu': b_glu,
        'w_pw1': w_pw1, 'b_pw1': b_pw1, 'w_dw': w_dw, 'b_dw': b_dw,
        'ln_g': ln_g, 'ln_b': ln_b, 'w_pw2': w_pw2, 'b_pw2': b_pw2,
        'w_up': w_up, 'ffn_dw': ffn_dw, 'ffn_dw_b': ffn_dw_b, 'w_down': w_down,
    }


def reference(x_prompt, x_sample, cache_a0, cache_a1, cache_a2, state_ssm, state_conv, state_ffn,
              norm_mix, norm_ffn, norm_final, w_in_even, w_out_even,
              ssm_a_re, ssm_a_im, ssm_b_re, ssm_b_im, ssm_c_re, ssm_c_im, ssm_d, ssm_log_dt,
              w_glu, b_glu, w_pw1, b_pw1, w_dw, b_dw, ln_g, ln_b, w_pw2, b_pw2,
              w_up, ffn_dw, ffn_dw_b, w_down):
    p = {
        'norm_mix': norm_mix, 'norm_ffn': norm_ffn, 'norm_final': norm_final,
        'w_in_even': w_in_even, 'w_out_even': w_out_even,
        'ssm_a_re': ssm_a_re, 'ssm_a_im': ssm_a_im, 'ssm_b_re': ssm_b_re, 'ssm_b_im': ssm_b_im,
        'ssm_c_re': ssm_c_re, 'ssm_c_im': ssm_c_im, 'ssm_d': ssm_d, 'ssm_log_dt': ssm_log_dt,
        'w_glu': w_glu, 'b_glu': b_glu,
        'w_pw1': w_pw1, 'b_pw1': b_pw1, 'w_dw': w_dw, 'b_dw': b_dw,
        'ln_g': ln_g, 'ln_b': ln_b, 'w_pw2': w_pw2, 'b_pw2': b_pw2,
        'w_up': w_up, 'ffn_dw': ffn_dw, 'ffn_dw_b': ffn_dw_b, 'w_down': w_down,
    }
    pos_prompt = jnp.arange(SEQ, dtype=jnp.int32)
    pos_sample = PAST_LEN + jnp.arange(DEC_SEQ, dtype=jnp.int32)
    y_prompt, a_p, ssm_p, conv_p, ffn_p = _trunk(x_prompt, pos_prompt, None, p)
    st = {'a': (cache_a0, cache_a1, cache_a2), 'ssm': state_ssm,
          'conv': state_conv, 'ffn': state_ffn}
    y_sample, a_s, ssm_s, conv_s, ffn_s = _trunk(x_sample, pos_sample, st, p)
    return (y_prompt, y_sample, a_p[0], a_s[0], a_p[1], a_s[1], a_p[2], a_s[2],
            ssm_p, ssm_s, conv_p, conv_s, ffn_p, ffn_s)
```

```python
import functools
import math

import jax
import jax.numpy as jnp
from jax import lax
from jax.experimental import pallas as pl
from jax.experimental.pallas import tpu as pltpu

F32 = jnp.float32
BF16 = jnp.bfloat16

HEAD_DIM = 128
A_HEADS = 16
D_A = A_HEADS * HEAD_DIM
A_WINDOWS = (128, 512, 2048)
A_DILATIONS = (1, 4, 16)
BAND = 128
N_DIL = 3
QKV_W = 3 * D_A
ROPE_THETA = 10000.0
PAST_LEN = 16384
SSM_GROUP = 16
SSM_STATE = 64
SLAB_G = 8
SLAB_S = SLAB_G * SSM_STATE
CHUNK = 8
CONV_WIDTH = 31
CONV_HALO = 32
FFN_HALO = 16
RMS_EPS = 1e-6
LN_EPS = 1e-5
NEG_BIG = -1e30
VMEM_LIMIT_V7X = 56 * 1024 * 1024


def _cp(*sem):
    return pltpu.CompilerParams(dimension_semantics=sem, vmem_limit_bytes=VMEM_LIMIT_V7X)


def _dot(a, b, **kw):
    return jnp.dot(a, b, preferred_element_type=F32, **kw)


def _dot_nt(a, b, **kw):
    return lax.dot_general(a, b, (((1,), (1,)), ((), ())), preferred_element_type=F32, **kw)


def _sigmoid(x):
    return 1.0 / (1.0 + jnp.exp(-x))


def _rmsnorm_kernel(x_ref, g_ref, o_ref):
    x = x_ref[...]
    ms = jnp.mean(x * x, axis=-1, keepdims=True)
    o_ref[...] = (x * lax.rsqrt(ms + RMS_EPS) * g_ref[...]).astype(o_ref.dtype)


def _rmsnorm(x, g, out_dtype):
    m, d = x.shape
    tm = min(m, 256)
    return pl.pallas_call(
        _rmsnorm_kernel,
        grid=(m // tm,),
        in_specs=[pl.BlockSpec((tm, d), lambda i: (i, 0)),
                  pl.BlockSpec((1, d), lambda i: (0, 0))],
        out_specs=pl.BlockSpec((tm, d), lambda i: (i, 0)),
        out_shape=jax.ShapeDtypeStruct((m, d), out_dtype),
        compiler_params=_cp("parallel"),
        name="rmsnorm",
    )(x, g.reshape(1, d))


def _rope_table_kernel(invf_ref, cos_ref, sin_ref, *, pos0, step):
    rows = cos_ref.shape[0]
    row = lax.broadcasted_iota(jnp.int32, (rows, HEAD_DIM), 0)
    lane = lax.broadcasted_iota(jnp.int32, (rows, HEAD_DIM), 1)
    pos = (row * step + pos0).astype(F32)
    ang = pos * invf_ref[...]
    cos_ref[...] = jnp.cos(ang)
    s = jnp.sin(ang)
    sin_ref[...] = jnp.where(lane < HEAD_DIM // 2, -s, s)


def _rope_tables(rows, pos0, step):
    half = HEAD_DIM // 2
    inv_freq = ROPE_THETA ** (-jnp.arange(half, dtype=F32) / half)
    invf = jnp.concatenate([inv_freq, inv_freq]).reshape(1, HEAD_DIM)
    shp = jax.ShapeDtypeStruct((rows, HEAD_DIM), F32)
    return pl.pallas_call(
        functools.partial(_rope_table_kernel, pos0=pos0, step=step),
        out_shape=(shp, shp),
        name="rope_tables",
    )(invf)


def _inproj_kernel(x_ref, w_ref, cos_ref, sin_ref, o_ref, *, tn):
    j = pl.program_id(1)
    acc = _dot(x_ref[...], w_ref[...])
    seg = (j * tn) // D_A
    is_rope = jnp.logical_and(seg < 3 * N_DIL, seg % 3 != 2)

    @pl.when(is_rope)
    def _():
        cos = cos_ref[...]
        sin = sin_ref[...]
        for h in range(tn // HEAD_DIM):
            sl = slice(h * HEAD_DIM, (h + 1) * HEAD_DIM)
            xh = acc[:, sl]
            o_ref[:, sl] = xh * cos + pltpu.roll(xh, HEAD_DIM // 2, 1) * sin

    @pl.when(jnp.logical_not(is_rope))
    def _():
        o_ref[...] = acc


def _inproj(hn, w, cos, sin, tm, tn):
    m, k = hn.shape
    n = w.shape[1]
    ct = cos.shape[0] // tm
    return pl.pallas_call(
        functools.partial(_inproj_kernel, tn=tn),
        grid=(m // tm, n // tn),
        in_specs=[pl.BlockSpec((tm, k), lambda i, j: (i, 0)),
                  pl.BlockSpec((k, tn), lambda i, j: (0, j)),
                  pl.BlockSpec((tm, HEAD_DIM), lambda i, j: (i % ct, 0)),
                  pl.BlockSpec((tm, HEAD_DIM), lambda i, j: (i % ct, 0))],
        out_specs=pl.BlockSpec((tm, tn), lambda i, j: (i, j)),
        out_shape=jax.ShapeDtypeStruct((m, n), F32),
        compiler_params=_cp("parallel", "arbitrary"),
        name="inproj_rope",
    )(hn, w, cos, sin)


def _band_attn_kernel(q_ref, kp_ref, kc_ref, vp_ref, vc_ref, o_ref, lse_ref):
    has_prev = pl.program_id(2) > 0
    qi = lax.broadcasted_iota(jnp.int32, (BAND, 2 * BAND), 0)
    ki = lax.broadcasted_iota(jnp.int32, (BAND, 2 * BAND), 1)
    valid = jnp.logical_and(jnp.logical_and(ki >= qi, ki <= qi + BAND),
                            jnp.logical_or(ki >= BAND, has_prev))
    lane = lax.broadcasted_iota(jnp.int32, (BAND, HEAD_DIM), 1)
    lse_tile = jnp.zeros((BAND, HEAD_DIM), F32)
    scale = HEAD_DIM ** -0.5
    for h in range(A_HEADS):
        sl = slice(h * HEAD_DIM, (h + 1) * HEAD_DIM)
        q = q_ref[:, sl].astype(BF16)
        k = jnp.concatenate([kp_ref[:, sl], kc_ref[:, sl]], axis=0).astype(BF16)
        v = jnp.concatenate([vp_ref[:, sl], vc_ref[:, sl]], axis=0).astype(BF16)
        s = _dot_nt(q, k) * scale
        s = jnp.where(valid, s, NEG_BIG)
        m = jnp.max(s, axis=-1, keepdims=True)
        p = jnp.exp(s - m)
        l = jnp.sum(p, axis=-1, keepdims=True)
        o_ref[:, sl] = _dot(p.astype(BF16), v) / l
        lse_tile = jnp.where(lane == h, m + jnp.log(l), lse_tile)
    lse_ref[...] = lse_tile


def _band_attn(z, n_seq, t, g):
    d = A_DILATIONS[g]
    width = z.shape[1]
    sub = t // d
    nb = sub // BAND
    zv = z.reshape(n_seq, sub, d * width)
    cpb = width // D_A
    qc = 3 * g

    def cur(off):
        return pl.BlockSpec((None, BAND, D_A), lambda n, r, b: (n, b, r * cpb + qc + off))

    def prev(off):
        return pl.BlockSpec((None, BAND, D_A),
                            lambda n, r, b: (n, jnp.maximum(b - 1, 0), r * cpb + qc + off))

    o, lse = pl.pallas_call(
        _band_attn_kernel,
        grid=(n_seq, d, nb),
        in_specs=[cur(0), prev(1), cur(1), prev(2), cur(2)],
        out_specs=[pl.BlockSpec((None, BAND, D_A), lambda n, r, b: (n, b, r)),
                   pl.BlockSpec((None, BAND, HEAD_DIM), lambda n, r, b: (n, b, r))],
        out_shape=[jax.ShapeDtypeStruct((n_seq, sub, d * D_A), F32),
                   jax.ShapeDtypeStruct((n_seq, sub, d * HEAD_DIM), F32)],
        compiler_params=_cp("parallel", "parallel", "arbitrary"),
        name=f"band_attn_d{d}",
    )(zv, zv, zv, zv, zv)
    return o.reshape(n_seq * t, D_A), lse.reshape(n_seq * t, HEAD_DIM)


def _combine_kernel(o0_ref, o1_ref, o2_ref, l0_ref, l1_ref, l2_ref, out_ref):
    a0, a1, a2 = l0_ref[...], l1_ref[...], l2_ref[...]
    m = jnp.maximum(jnp.maximum(a0, a1), a2)
    e0, e1, e2 = jnp.exp(a0 - m), jnp.exp(a1 - m), jnp.exp(a2 - m)
    inv = 1.0 / (e0 + e1 + e2)
    w0, w1, w2 = e0 * inv, e1 * inv, e2 * inv
    for h in range(A_HEADS):
        sl = slice(h * HEAD_DIM, (h + 1) * HEAD_DIM)
        out_ref[:, sl] = (w0[:, h:h + 1] * o0_ref[:, sl] + w1[:, h:h + 1] * o1_ref[:, sl]
                          + w2[:, h:h + 1] * o2_ref[:, sl]).astype(out_ref.dtype)


def _combine(os_, lses):
    m = os_[0].shape[0]
    tm = 256
    ob = pl.BlockSpec((tm, D_A), lambda i: (i, 0))
    lb = pl.BlockSpec((tm, HEAD_DIM), lambda i: (i, 0))
    return pl.pallas_call(
        _combine_kernel,
        grid=(m // tm,),
        in_specs=[ob, ob, ob, lb, lb, lb],
        out_specs=ob,
        out_shape=jax.ShapeDtypeStruct((m, D_A), BF16),
        compiler_params=_cp("parallel"),
        name="attn_combine",
    )(*os_, *lses)


def _step_attn_kernel(z_ref, c0_ref, c1_ref, c2_ref, seg_ref, segt_ref, o_ref):
    hi = lax.Precision.HIGHEST
    seg = seg_ref[...]
    segt = segt_ref[...]
    scale = HEAD_DIM ** -0.5
    ms, ls, accs = [], [], []
    for g, c_ref in enumerate((c0_ref, c1_ref, c2_ref)):
        base = g * QKV_W
        q = z_ref[:, base:base + D_A]
        kn = z_ref[:, base + D_A:base + 2 * D_A]
        vn = z_ref[:, base + 2 * D_A:base + 3 * D_A]
        kc = c_ref[:, 0:D_A]
        vc = c_ref[:, D_A:2 * D_A]
        s = _dot(kc * q, seg, precision=hi) * scale
        s_new = _dot(kn * q, seg, precision=hi) * scale
        m = jnp.maximum(jnp.max(s, axis=0, keepdims=True), s_new)
        p = jnp.exp(s - m)
        p_new = jnp.exp(s_new - m)
        l = jnp.sum(p, axis=0, keepdims=True) + p_new
        acc = (jnp.sum(_dot(p.astype(BF16), segt) * vc, axis=0, keepdims=True)
               + _dot(p_new.astype(BF16), segt) * vn)
        ms.append(m)
        ls.append(l)
        accs.append(acc)
    mm = jnp.maximum(jnp.maximum(ms[0], ms[1]), ms[2])
    es = [jnp.exp(m - mm) for m in ms]
    den = es[0] * ls[0] + es[1] * ls[1] + es[2] * ls[2]
    num = sum(_dot(e, segt.astype(F32), precision=hi) * a for e, a in zip(es, accs))
    o_ref[...] = (num * _dot(1.0 / den, segt.astype(F32), precision=hi)).astype(o_ref.dtype)


def _step_attn(zs, caches):
    n = zs.shape[0]
    width = zs.shape[1]
    cviews = []
    for g, c in enumerate(caches):
        d = A_DILATIONS[g]
        cviews.append(c.reshape(n, c.shape[1] // d, d * 2 * D_A))
    head_of_lane = jnp.arange(D_A, dtype=jnp.int32) // HEAD_DIM
    seg = (head_of_lane[:, None] == jnp.arange(A_HEADS, dtype=jnp.int32)[None, :]).astype(F32)
    segt = seg.T.astype(BF16)
    cb = pl.BlockSpec((None, BAND, 2 * D_A), lambda i: (i, 0, 0))
    out = pl.pallas_call(
        _step_attn_kernel,
        grid=(n,),
        in_specs=[pl.BlockSpec((None, 1, width), lambda i: (i, 0, 0)), cb, cb, cb,
                  pl.BlockSpec((D_A, A_HEADS), lambda i: (0, 0)),
                  pl.BlockSpec((A_HEADS, D_A), lambda i: (0, 0))],
        out_specs=pl.BlockSpec((None, 1, D_A), lambda i: (i, 0, 0)),
        out_shape=jax.ShapeDtypeStruct((n, 1, D_A), BF16),
        compiler_params=_cp("parallel"),
        name="step_attn",
    )(zs.reshape(n, 1, width), *cviews, seg, segt)
    return out.reshape(n, D_A)


def _ssm_discretise(lre, lim, ldt):
    dt = jnp.exp(ldt)
    mag = jnp.exp(lre * dt)
    ar = mag * jnp.cos(lim * dt)
    ai = mag * jnp.sin(lim * dt)
    den = lre * lre + lim * lim
    zr = ((ar - 1.0) * lre + ai * lim) / den
    zi = (ai * lre - (ar - 1.0) * lim) / den
    return ar, ai, zr, zi


def _slab_mask():
    r = lax.broadcasted_iota(jnp.int32, (SLAB_G * SSM_GROUP, SLAB_S), 0) // SSM_GROUP
    c = lax.broadcasted_iota(jnp.int32, (SLAB_G * SSM_GROUP, SLAB_S), 1) // SSM_STATE
    return (r == c).astype(F32)


def _block_diag(x16, mask):
    return jnp.concatenate([x16] * SLAB_G, axis=0) * mask


def _ssm_prompt_kernel(*refs, n_seq, n_chunk):
    xs = refs[:CHUNK]
    (lre_ref, lim_ref, ldt_ref, bre_ref, bim_ref, cre_ref, cim_ref, d_ref,
     y_ref, hlr_ref, hli_ref,
     wer_ref, wei_ref, tz_ref, bdr_ref, bdi_ref, er_ref, ei_ref) = refs[CHUNK:]
    jp = pl.program_id(1)
    hi = lax.Precision.HIGHEST

    @pl.when(jp == 0)
    def _():
        ar, ai, zr, zi = _ssm_discretise(lre_ref[...], lim_ref[...], ldt_ref[...])
        mask = _slab_mask()
        bre, bim = bre_ref[...], bim_ref[...]
        cre, cim = cre_ref[...], cim_ref[...]
        cre_bd = _block_diag(cre, mask)
        cim_bd = _block_diag(cim, mask)
        pr, pim = jnp.ones_like(ar), jnp.zeros_like(ar)
        for m in range(CHUNK + 1):
            if m < CHUNK:
                wr = zr * pr - zi * pim
                wi = zr * pim + zi * pr
                we_r = _block_diag(bre * wr - bim * wi, mask)
                we_i = _block_diag(bre * wi + bim * wr, mask)
                wer_ref[m] = we_r.astype(BF16)
                wei_ref[m] = we_i.astype(BF16)
                tz = _dot_nt(we_r, cre_bd, precision=hi) - _dot_nt(we_i, cim_bd, precision=hi)
                tz_ref[m] = tz.astype(BF16)
            if m >= 1:
                bdr_ref[m - 1] = _block_diag(cre * pr - cim * pim, mask).astype(BF16)
                bdi_ref[m - 1] = _block_diag(-(cre * pim + cim * pr), mask).astype(BF16)
            pr, pim = pr * ar - pim * ai, pr * ai + pim * ar
        a8r, a8i = ar, ai
        for _ in range(CHUNK - 1):
            a8r, a8i = a8r * ar - a8i * ai, a8r * ai + a8i * ar

        er = jnp.zeros(er_ref.shape, F32)
        ei = jnp.zeros(ei_ref.shape, F32)
        for j in range(CHUNK):
            xb = xs[j][...].astype(BF16)
            er = er + _dot(xb, wer_ref[CHUNK - 1 - j])
            ei = ei + _dot(xb, wei_ref[CHUNK - 1 - j])
        er_ref[...] = er
        ei_ref[...] = ei

        def body(k, carry):
            out = []
            for n in range(n_seq):
                hr, him = carry[2 * n], carry[2 * n + 1]
                row = n * n_chunk + k
                e_r = er_ref[pl.ds(row, 1), :]
                e_i = ei_ref[pl.ds(row, 1), :]
                er_ref[pl.ds(row, 1), :] = hr
                ei_ref[pl.ds(row, 1), :] = him
                out.append(a8r * hr - a8i * him + e_r)
                out.append(a8r * him + a8i * hr + e_i)
            return tuple(out)

        zero = jnp.zeros((1, SLAB_S), F32)
        fin = lax.fori_loop(0, n_chunk, body, (zero,) * (2 * n_seq))
        for n in range(n_seq):
            hlr_ref[pl.ds(n, 1), :] = fin[2 * n]
            hli_ref[pl.ds(n, 1), :] = fin[2 * n + 1]

    y_ref[...] = (_dot_nt(er_ref[...].astype(BF16), bdr_ref[jp])
                  + _dot_nt(ei_ref[...].astype(BF16), bdi_ref[jp]))
    for j in range(CHUNK):
        @pl.when(j <= jp)
        def _(j=j):
            y_ref[...] += _dot(xs[j][...].astype(BF16), tz_ref[jp - j])

        @pl.when(j == jp)
        def _(j=j):
            y_ref[...] += d_ref[...] * xs[j][...]


def _ssm_params(a_re, a_im, b_re, b_im, c_re, c_im, d_skip, log_dt):
    g = a_re.shape[0]
    ns = g // SLAB_G

    def row(x):
        return x.astype(F32).reshape(ns, 1, SLAB_S)

    def chan_rows(x):
        return x.astype(F32).reshape(ns, SLAB_G, SSM_GROUP, SSM_STATE).transpose(0, 2, 1, 3) \
            .reshape(ns, SSM_GROUP, SLAB_S)

    ldt = jnp.broadcast_to(log_dt.astype(F32)[:, None], (g, SSM_STATE))
    return (row(a_re), row(a_im), row(ldt),
            chan_rows(b_re.transpose(0, 2, 1)), chan_rows(b_im.transpose(0, 2, 1)),
            chan_rows(c_re), chan_rows(c_im),
            d_skip.astype(F32).reshape(ns, 1, SLAB_G * SSM_GROUP))


def _ssm_prompt(z, n_seq, t, u_col0, sp):
    m, width = z.shape
    rows = m // CHUNK
    n_chunk = t // CHUNK
    lanes = SLAB_G * SSM_GROUP
    ns = sp[0].shape[0]
    zc = z.reshape(rows, CHUNK * width)
    wb = width // lanes
    ub = u_col0 // lanes

    def xspec(j):
        return pl.BlockSpec((rows, lanes), lambda s, jp: (0, j * wb + ub + s))

    def pspec(r, c):
        return pl.BlockSpec((None, r, c), lambda s, jp: (s, 0, 0))

    y, hlr, hli = pl.pallas_call(
        functools.partial(_ssm_prompt_kernel, n_seq=n_seq, n_chunk=n_chunk),
        grid=(ns, CHUNK),
        in_specs=[xspec(j) for j in range(CHUNK)] + [
            pspec(1, SLAB_S), pspec(1, SLAB_S), pspec(1, SLAB_S),
            pspec(SSM_GROUP, SLAB_S), pspec(SSM_GROUP, SLAB_S),
            pspec(SSM_GROUP, SLAB_S), pspec(SSM_GROUP, SLAB_S), pspec(1, lanes)],
        out_specs=[pl.BlockSpec((rows, lanes), lambda s, jp: (0, jp * ns + s)),
                   pl.BlockSpec((None, n_seq, SLAB_S), lambda s, jp: (s, 0, 0)),
                   pl.BlockSpec((None, n_seq, SLAB_S), lambda s, jp: (s, 0, 0))],
        out_shape=[jax.ShapeDtypeStruct((rows, CHUNK * ns * lanes), F32),
                   jax.ShapeDtypeStruct((ns, n_seq, SLAB_S), F32),
                   jax.ShapeDtypeStruct((ns, n_seq, SLAB_S), F32)],
        scratch_shapes=[pltpu.VMEM((CHUNK, lanes, SLAB_S), BF16),
                        pltpu.VMEM((CHUNK, lanes, SLAB_S), BF16),
                        pltpu.VMEM((CHUNK, lanes, lanes), BF16),
                        pltpu.VMEM((CHUNK, lanes, SLAB_S), BF16),
                        pltpu.VMEM((CHUNK, lanes, SLAB_S), BF16),
                        pltpu.VMEM((rows, SLAB_S), F32),
                        pltpu.VMEM((rows, SLAB_S), F32)],
        compiler_params=_cp("parallel", "arbitrary"),
        name="ssm_prompt",
    )(*([zc] * CHUNK), *sp)
    return y.reshape(m, ns * lanes), hlr, hli


def _ssm_step_kernel(u_ref, h0r_ref, h0i_ref, lre_ref, lim_ref, ldt_ref, bre_ref, bim_ref,
                     cre_ref, cim_ref, d_ref, y_ref, hr_ref, hi_ref):
    ar, ai, zr, zi = _ssm_discretise(lre_ref[...], lim_ref[...], ldt_ref[...])
    mask = _slab_mask()
    bre, bim = bre_ref[...], bim_ref[...]
    we_r = _block_diag(bre * zr - bim * zi, mask).astype(BF16)
    we_i = _block_diag(bre * zi + bim * zr, mask).astype(BF16)
    u = u_ref[...]
    ub = u.astype(BF16)
    h0r, h0i = h0r_ref[...], h0i_ref[...]
    hr = ar * h0r - ai * h0i + _dot(ub, we_r)
    him = ar * h0i + ai * h0r + _dot(ub, we_i)
    hr_ref[...] = hr
    hi_ref[...] = him
    cr_bd = _block_diag(cre_ref[...], mask).astype(BF16)
    ci_bd = _block_diag(cim_ref[...], mask).astype(BF16)
    y_ref[...] = (_dot_nt(hr.astype(BF16), cr_bd) - _dot_nt(him.astype(BF16), ci_bd)
                  + d_ref[...] * u)


def _ssm_step(zs, u_col0, h0r, h0i, sp):
    n = zs.shape[0]
    lanes = SLAB_G * SSM_GROUP
    ns = sp[0].shape[0]
    ub = u_col0 // lanes

    def pspec(r, c):
        return pl.BlockSpec((None, r, c), lambda s: (s, 0, 0))

    hspec = pl.BlockSpec((None, n, SLAB_S), lambda s: (s, 0, 0))
    return pl.pallas_call(
        _ssm_step_kernel,
        grid=(ns,),
        in_specs=[pl.BlockSpec((n, lanes), lambda s: (0, ub + s)), hspec, hspec,
                  pspec(1, SLAB_S), pspec(1, SLAB_S), pspec(1, SLAB_S),
                  pspec(SSM_GROUP, SLAB_S), pspec(SSM_GROUP, SLAB_S),
                  pspec(SSM_GROUP, SLAB_S), pspec(SSM_GROUP, SLAB_S), pspec(1, lanes)],
        out_specs=[pl.BlockSpec((n, lanes), lambda s: (0, s)), hspec, hspec],
        out_shape=[jax.ShapeDtypeStruct((n, ns * lanes), F32),
                   jax.ShapeDtypeStruct((ns, n, SLAB_S), F32),
                   jax.ShapeDtypeStruct((ns, n, SLAB_S), F32)],
        compiler_params=_cp("parallel"),
        name="ssm_step",
    )(zs, h0r, h0i, *sp)


def _gelu_tanh(x):
    return 0.5 * x * (1.0 + jnp.tanh(math.sqrt(2.0 / math.pi) * (x + 0.044715 * (x * x * x))))


def _ssm_gate_kernel(y_ref, w_ref, b_ref, o_ref):
    yg = _gelu_tanh(y_ref[...])
    gate = _dot(yg.astype(BF16), w_ref[...]) + b_ref[...]
    o_ref[...] = (yg * _sigmoid(gate)).astype(o_ref.dtype)


def _ssm_gate(y, w, b, tm):
    m, d = y.shape
    return pl.pallas_call(
        _ssm_gate_kernel,
        grid=(m // tm,),
        in_specs=[pl.BlockSpec((tm, d), lambda i: (i, 0)),
                  pl.BlockSpec((d, d), lambda i: (0, 0)),
                  pl.BlockSpec((1, d), lambda i: (0, 0))],
        out_specs=pl.BlockSpec((tm, d), lambda i: (i, 0)),
        out_shape=jax.ShapeDtypeStruct((m, d), BF16),
        compiler_params=_cp("parallel"),
        name="ssm_gate",
    )(y, w, b.reshape(1, d).astype(F32))


def _outproj_kernel(a_ref, s_ref, wa_ref, ws_ref, x_ref, o_ref):
    o_ref[...] = x_ref[...] + _dot(a_ref[...], wa_ref[...]) + _dot(s_ref[...], ws_ref[...])


def _outproj(attn, ssm, w, x, tm, tn):
    m, ka = attn.shape
    n = w.shape[1]
    return pl.pallas_call(
        _outproj_kernel,
        grid=(m // tm, n // tn),
        in_specs=[pl.BlockSpec((tm, ka), lambda i, j: (i, 0)),
                  pl.BlockSpec((tm, ka), lambda i, j: (i, 0)),
                  pl.BlockSpec((ka, tn), lambda i, j: (0, j)),
                  pl.BlockSpec((ka, tn), lambda i, j: (1, j)),
                  pl.BlockSpec((tm, tn), lambda i, j: (i, j))],
        out_specs=pl.BlockSpec((tm, tn), lambda i, j: (i, j)),
        out_shape=jax.ShapeDtypeStruct((m, n), F32),
        compiler_params=_cp("parallel", "arbitrary"),
        name="outproj",
    )(attn, ssm, w, w, x)


def _mm_res_kernel(a_ref, w_ref, b_ref, x_ref, o_ref):
    o_ref[...] = x_ref[...] + (_dot(a_ref[...], w_ref[...]) + b_ref[...])


def _mm_res(a, w, b, x, tm, tn, name):
    m, k = a.shape
    n = w.shape[1]
    return pl.pallas_call(
        _mm_res_kernel,
        grid=(m // tm, n // tn),
        in_specs=[pl.BlockSpec((tm, k), lambda i, j: (i, 0)),
                  pl.BlockSpec((k, tn), lambda i, j: (0, j)),
                  pl.BlockSpec((1, tn), lambda i, j: (0, j)),
                  pl.BlockSpec((tm, tn), lambda i, j: (i, j))],
        out_specs=pl.BlockSpec((tm, tn), lambda i, j: (i, j)),
        out_shape=jax.ShapeDtypeStruct((m, n), F32),
        compiler_params=_cp("parallel", "arbitrary"),
        name=name,
    )(a, w, b.reshape(1, n).astype(F32), x)


def _pw1_kernel(x_ref, wa_ref, wg_ref, ba_ref, bg_ref, o_ref):
    x = x_ref[...]
    a = _dot(x, wa_ref[...]) + ba_ref[...]
    gate = _dot(x, wg_ref[...]) + bg_ref[...]
    o_ref[...] = a * _sigmoid(gate)


def _pw1(hn, w, b, tm, tn):
    m, k = hn.shape
    n = w.shape[1] // 2
    nj = n // tn
    b2 = b.reshape(1, 2 * n).astype(F32)
    return pl.pallas_call(
        _pw1_kernel,
        grid=(m // tm, nj),
        in_specs=[pl.BlockSpec((tm, k), lambda i, j: (i, 0)),
                  pl.BlockSpec((k, tn), lambda i, j: (0, j)),
                  pl.BlockSpec((k, tn), lambda i, j: (0, j + nj)),
                  pl.BlockSpec((1, tn), lambda i, j: (0, j)),
                  pl.BlockSpec((1, tn), lambda i, j: (0, j + nj))],
        out_specs=pl.BlockSpec((tm, tn), lambda i, j: (i, j)),
        out_shape=jax.ShapeDtypeStruct((m, n), F32),
        compiler_params=_cp("parallel", "arbitrary"),
        name="conv_pw1_glu",
    )(hn, w, w, b2, b2)


def _ln_swish(y, g, b):
    mu = jnp.mean(y, axis=-1, keepdims=True)
    yc = y - mu
    var = jnp.mean(yc * yc, axis=-1, keepdims=True)
    yn = yc * lax.rsqrt(var + LN_EPS) * g + b
    return yn * _sigmoid(yn)


def _conv31_kernel(u_ref, halo_ref, w_ref, b_ref, g_ref, bb_ref, o_ref, ext_ref, y_ref,
                   *, tm, tiles_per_seq, cw, rw):
    seq_start = (pl.program_id(0) % tiles_per_seq) == 0
    ext_ref[0:CONV_HALO, :] = jnp.where(seq_start, 0.0, halo_ref[...])
    ext_ref[CONV_HALO:, :] = u_ref[...]
    d = u_ref.shape[1]
    off = CONV_HALO - (CONV_WIDTH - 1)
    for r in range(tm // rw):
        for c in range(d // cw):
            cs = slice(c * cw, (c + 1) * cw)
            acc = w_ref[0:1, cs] * ext_ref[off + r * rw:off + r * rw + rw, cs]
            for j in range(1, CONV_WIDTH):
                acc = acc + w_ref[j:j + 1, cs] * ext_ref[off + r * rw + j:off + r * rw + j + rw, cs]
            y_ref[r * rw:(r + 1) * rw, cs] = acc + b_ref[:, cs]
    o_ref[...] = _ln_swish(y_ref[...], g_ref[...], bb_ref[...]).astype(o_ref.dtype)


def _conv31(u, t, w, b, g, bb, tm):
    m, d = u.shape
    row = lambda v: v.reshape(1, d).astype(F32)
    hb = tm // CONV_HALO
    return pl.pallas_call(
        functools.partial(_conv31_kernel, tm=tm, tiles_per_seq=t // tm, cw=512, rw=64),
        grid=(m // tm,),
        in_specs=[pl.BlockSpec((tm, d), lambda i: (i, 0)),
                  pl.BlockSpec((CONV_HALO, d), lambda i: (jnp.maximum(i * hb - 1, 0), 0)),
                  pl.BlockSpec((CONV_WIDTH, d), lambda i: (0, 0)),
                  pl.BlockSpec((1, d), lambda i: (0, 0)),
                  pl.BlockSpec((1, d), lambda i: (0, 0)),
                  pl.BlockSpec((1, d), lambda i: (0, 0))],
        out_specs=pl.BlockSpec((tm, d), lambda i: (i, 0)),
        out_shape=jax.ShapeDtypeStruct((m, d), BF16),
        scratch_shapes=[pltpu.VMEM((tm + CONV_HALO, d), F32), pltpu.VMEM((tm, d), F32)],
        compiler_params=_cp("parallel"),
        name="conv31_ln_swish",
    )(u, u, w.astype(F32), row(b), row(g), row(bb))


def _conv31_step_kernel(u_ref, st_ref, w_ref, b_ref, g_ref, bb_ref, o_ref):
    acc = w_ref[0:1, :] * st_ref[0]
    for j in range(1, CONV_WIDTH - 1):
        acc = acc + w_ref[j:j + 1, :] * st_ref[j]
    acc = acc + w_ref[CONV_WIDTH - 1:CONV_WIDTH, :] * u_ref[...] + b_ref[...]
    o_ref[...] = _ln_swish(acc, g_ref[...], bb_ref[...]).astype(o_ref.dtype)


def _conv31_step(u, state_t, w, b, g, bb):
    n, d = u.shape
    row = lambda v: v.reshape(1, d).astype(F32)
    return pl.pallas_call(
        _conv31_step_kernel,
        out_shape=jax.ShapeDtypeStruct((n, d), BF16),
        compiler_params=pltpu.CompilerParams(vmem_limit_bytes=VMEM_LIMIT_V7X),
        name="conv31_step",
    )(u, state_t, w.astype(F32), row(b), row(g), row(bb))


def _ffn_up_kernel(x_ref, halo_ref, wg_ref, wv_ref, cwg_ref, cwv_ref, cbg_ref, cbv_ref,
                   act_ref, zlg_ref, zlv_ref, xs_ref, *, tm, tiles_per_seq):
    i = pl.program_id(0)

    @pl.when(pl.program_id(1) == 0)
    def _():
        xs_ref[0:FFN_HALO, :] = halo_ref[...]
        xs_ref[FFN_HALO:, :] = x_ref[...]

    seq_start = (i % tiles_per_seq) == 0

    def half(w_ref, cw_ref, cb_ref, zl_ref):
        z = _dot(xs_ref[...], w_ref[...])
        row = lax.broadcasted_iota(jnp.int32, z.shape, 0)
        z = jnp.where(jnp.logical_and(row < FFN_HALO, seq_start), 0.0, z)
        zl_ref[...] = z[FFN_HALO + tm - 8:, :]
        zc = (cw_ref[2:3, :] * z + cw_ref[1:2, :] * pltpu.roll(z, 1, 0)
              + cw_ref[0:1, :] * pltpu.roll(z, 2, 0) + cb_ref[...])
        return zc[FFN_HALO:, :]

    gate = half(wg_ref, cwg_ref, cbg_ref, zlg_ref)
    val = half(wv_ref, cwv_ref, cbv_ref, zlv_ref)
    act_ref[...] = (gate * _sigmoid(gate) * val).astype(act_ref.dtype)


def _ffn_up(hn, t, w, cw, cb, tm, tn):
    m, k = hn.shape
    dff = w.shape[1] // 2
    nj = dff // tn
    cw = cw.astype(F32)
    cb2 = cb.reshape(1, 2 * dff).astype(F32)
    hb = tm // FFN_HALO
    zl = jax.ShapeDtypeStruct((m // tm, 8, dff), F32)
    return pl.pallas_call(
        functools.partial(_ffn_up_kernel, tm=tm, tiles_per_seq=t // tm),
        grid=(m // tm, nj),
        in_specs=[pl.BlockSpec((tm, k), lambda i, j: (i, 0)),
                  pl.BlockSpec((FFN_HALO, k), lambda i, j: (jnp.maximum(i * hb - 1, 0), 0)),
                  pl.BlockSpec((k, tn), lambda i, j: (0, j)),
                  pl.BlockSpec((k, tn), lambda i, j: (0, j + nj)),
                  pl.BlockSpec((3, tn), lambda i, j: (0, j)),
                  pl.BlockSpec((3, tn), lambda i, j: (0, j + nj)),
                  pl.BlockSpec((1, tn), lambda i, j: (0, j)),
                  pl.BlockSpec((1, tn), lambda i, j: (0, j + nj))],
        out_specs=[pl.BlockSpec((tm, tn), lambda i, j: (i, j)),
                   pl.BlockSpec((None, 8, tn), lambda i, j: (i, 0, j)),
                   pl.BlockSpec((None, 8, tn), lambda i, j: (i, 0, j))],
        out_shape=[jax.ShapeDtypeStruct((m, dff), BF16), zl, zl],
        scratch_shapes=[pltpu.VMEM((tm + FFN_HALO, k), BF16)],
        compiler_params=_cp("parallel", "arbitrary"),
        name="ffn_up_conv_glu",
    )(hn, hn, w, w, cw, cw, cb2, cb2)


def _ffn_up_step_kernel(x_ref, wg_ref, wv_ref, sg_ref, sv_ref, cwg_ref, cwv_ref, cbg_ref, cbv_ref,
                        act_ref, zg_ref, zv_ref):
    x = x_ref[...]

    def half(w_ref, s_ref, cw_ref, cb_ref, z_ref):
        z = _dot(x, w_ref[...])
        z_ref[...] = z
        return cw_ref[2:3, :] * z + cw_ref[1:2, :] * s_ref[1] + cw_ref[0:1, :] * s_ref[0] + cb_ref[...]

    gate = half(wg_ref, sg_ref, cwg_ref, cbg_ref, zg_ref)
    val = half(wv_ref, sv_ref, cwv_ref, cbv_ref, zv_ref)
    act_ref[...] = (gate * _sigmoid(gate) * val).astype(act_ref.dtype)


def _ffn_up_step(hn, state_t, w, cw, cb, tn):
    n, k = hn.shape
    dff = w.shape[1] // 2
    nj = dff // tn
    cw = cw.astype(F32)
    cb2 = cb.reshape(1, 2 * dff).astype(F32)
    zs = jax.ShapeDtypeStruct((n, dff), F32)
    return pl.pallas_call(
        _ffn_up_step_kernel,
        grid=(nj,),
        in_specs=[pl.BlockSpec((n, k), lambda j: (0, 0)),
                  pl.BlockSpec((k, tn), lambda j: (0, j)),
                  pl.BlockSpec((k, tn), lambda j: (0, j + nj)),
                  pl.BlockSpec((2, n, tn), lambda j: (0, 0, j)),
                  pl.BlockSpec((2, n, tn), lambda j: (0, 0, j + nj)),
                  pl.BlockSpec((3, tn), lambda j: (0, j)),
                  pl.BlockSpec((3, tn), lambda j: (0, j + nj)),
                  pl.BlockSpec((1, tn), lambda j: (0, j)),
                  pl.BlockSpec((1, tn), lambda j: (0, j + nj))],
        out_specs=[pl.BlockSpec((n, tn), lambda j: (0, j)),
                   pl.BlockSpec((n, tn), lambda j: (0, j)),
                   pl.BlockSpec((n, tn), lambda j: (0, j))],
        out_shape=[jax.ShapeDtypeStruct((n, dff), BF16), zs, zs],
        compiler_params=_cp("arbitrary"),
        name="ffn_up_step",
    )(hn, w, w, state_t, state_t, cw, cw, cb2, cb2)


def _tile_plan(m):
    return 1024 if m >= 1024 else m


def _even_mixer(x, hn, z, attn, y_ssm, p, tm):
    ssm_out = _ssm_gate(y_ssm, p['w_glu'], p['b_glu'], min(tm, 512))
    return _outproj(attn, ssm_out, p['w_out'], x, tm, 1024 if tm >= 1024 else 2048)


def _ffn(x, t, prev_state_t, norm_g, w_up, cw, cb, w_down, tm):
    m = x.shape[0]
    hn = _rmsnorm(x, norm_g, BF16)
    dff = w_down.shape[0]
    if prev_state_t is None:
        act, zlg, zlv = _ffn_up(hn, t, w_up, cw, cb, tm, 256)
        per = t // tm
        last = jnp.concatenate([zlg[per - 1::per, 6:8], zlv[per - 1::per, 6:8]], axis=-1)
        xo = _mm_res(act, w_down, jnp.zeros((w_down.shape[1],), F32), x, 512, 512, "ffn_down")
    else:
        act, zg, zv = _ffn_up_step(hn, prev_state_t, w_up, cw, cb, 256)
        znew = jnp.concatenate([zg, zv], axis=-1)
        last = jnp.concatenate([prev_state_t[1][:, None], znew[:, None]], axis=1)
        xo = _mm_res(act, w_down, jnp.zeros((w_down.shape[1],), F32), x, m, 512, "ffn_down_step")
    return xo, last


def _conv_module(x, t, prev_state, norm_g, p, tm):
    m = x.shape[0]
    hn = _rmsnorm(x, norm_g, BF16)
    u = _pw1(hn, p['w_pw1'], p['b_pw1'], tm, 512 if tm >= 1024 else 1024)
    if prev_state is None:
        c = _conv31(u, t, p['w_dw'], p['b_dw'], p['ln_g'], p['ln_b'], 128)
        n_seq = m // t
        new_state = u.reshape(n_seq, t, -1)[:, t - (CONV_WIDTH - 1):]
        xo = _mm_res(c, p['w_pw2'], p['b_pw2'], x, tm, 1024, "conv_pw2")
    else:
        c = _conv31_step(u, prev_state.transpose(1, 0, 2), p['w_dw'], p['b_dw'], p['ln_g'], p['ln_b'])
        new_state = jnp.concatenate([prev_state[:, 1:], u[:, None]], axis=1)
        xo = _mm_res(c, p['w_pw2'], p['b_pw2'], x, m, 2048, "conv_pw2_step")
    return xo, new_state


def kernel(x_prompt, x_sample, cache_a0, cache_a1, cache_a2, state_ssm, state_conv, state_ffn, norm_mix, norm_ffn, norm_final, w_in_even, w_out_even, ssm_a_re, ssm_a_im, ssm_b_re, ssm_b_im, ssm_c_re, ssm_c_im, ssm_d, ssm_log_dt, w_glu, b_glu, w_pw1, b_pw1, w_dw, b_dw, ln_g, ln_b, w_pw2, b_pw2, w_up, ffn_dw, ffn_dw_b, w_down):
    n_p, t_p, d_model = x_prompt.shape
    n_s = x_sample.shape[0]
    caches = (cache_a0, cache_a1, cache_a2)
    d_in = w_in_even.shape[-1]
    u_col0 = N_DIL * QKV_W

    w_in_b = w_in_even[0].astype(BF16)
    pe = {'w_out': w_out_even[0].astype(BF16), 'w_glu': w_glu[0].astype(BF16), 'b_glu': b_glu[0]}
    pc = {'w_pw1': w_pw1[0].astype(BF16), 'b_pw1': b_pw1[0], 'w_dw': w_dw[0], 'b_dw': b_dw[0],
          'ln_g': ln_g[0], 'ln_b': ln_b[0], 'w_pw2': w_pw2[0].astype(BF16), 'b_pw2': b_pw2[0]}
    w_up_b = [w_up[i].astype(BF16) for i in range(2)]
    w_down_b = [w_down[i].astype(BF16) for i in range(2)]
    sp = _ssm_params(ssm_a_re[0], ssm_a_im[0], ssm_b_re[0], ssm_b_im[0], ssm_c_re[0], ssm_c_im[0],
                     ssm_d[0], ssm_log_dt[0])
    ns = sp[0].shape[0]

    mp = n_p * t_p
    tm = _tile_plan(mp)
    x = x_prompt.reshape(mp, d_model)
    cos, sin = _rope_tables(t_p, 0, 1)
    hn = _rmsnorm(x, norm_mix[0], BF16)
    z = _inproj(hn, w_in_b, cos, sin, tm, 1024)
    outs, lses = [], []
    for g in range(N_DIL):
        o, lse = _band_attn(z, n_p, t_p, g)
        outs.append(o)
        lses.append(lse)
    attn = _combine(outs, lses)
    y_ssm, hlr, hli = _ssm_prompt(z, n_p, t_p, u_col0, sp)
    x = _even_mixer(x, hn, z, attn, y_ssm, pe, tm)
    x, ffn0_p = _ffn(x, t_p, None, norm_ffn[0], w_up_b[0], ffn_dw[0], ffn_dw_b[0], w_down_b[0], tm)
    x, conv_p = _conv_module(x, t_p, None, norm_mix[1], pc, tm)
    x, ffn1_p = _ffn(x, t_p, None, norm_ffn[1], w_up_b[1], ffn_dw[1], ffn_dw_b[1], w_down_b[1], tm)
    y_prompt = _rmsnorm(x, norm_final, F32).reshape(n_p, t_p, d_model)

    a_p = []
    for g in range(N_DIL):
        kv = z[:, g * QKV_W + D_A:(g + 1) * QKV_W].reshape(n_p, t_p, 2, A_HEADS, HEAD_DIM)
        length = min(A_WINDOWS[g], PAST_LEN)
        a_p.append(kv[None, :, t_p - length:])

    def ssm_state(hr, him, n):
        def unslab(h):
            return h.reshape(ns, n, SLAB_G, SSM_STATE).transpose(1, 0, 2, 3).reshape(n, ns * SLAB_G, SSM_STATE)
        return jnp.stack([unslab(hr), unslab(him)], axis=-1)[None]

    ssm_p = ssm_state(hlr, hli, n_p)
    conv_p = conv_p[None]
    ffn_p = jnp.stack([ffn0_p, ffn1_p])

    xs = x_sample.reshape(n_s, d_model)
    cos_s, sin_s = _rope_tables(n_s, PAST_LEN, 0)
    hn = _rmsnorm(xs, norm_mix[0], BF16)
    zs = _inproj(hn, w_in_b, cos_s, sin_s, n_s, 2048)
    attn_s = _step_attn(zs, [c[0] for c in caches])

    def slab(h):
        return h.reshape(n_s, ns, SLAB_S).transpose(1, 0, 2)

    y_s, hr_s, hi_s = _ssm_step(zs, u_col0, slab(state_ssm[0, ..., 0]), slab(state_ssm[0, ..., 1]), sp)
    xs = _even_mixer(xs, hn, zs, attn_s, y_s, pe, n_s)
    xs, ffn0_s = _ffn(xs, 1, state_ffn[0].transpose(1, 0, 2), norm_ffn[0], w_up_b[0], ffn_dw[0],
                      ffn_dw_b[0], w_down_b[0], n_s)
    xs, conv_s = _conv_module(xs, 1, state_conv[0], norm_mix[1], pc, n_s)
    xs, ffn1_s = _ffn(xs, 1, state_ffn[1].transpose(1, 0, 2), norm_ffn[1], w_up_b[1], ffn_dw[1],
                      ffn_dw_b[1], w_down_b[1], n_s)
    y_sample = _rmsnorm(xs, norm_final, F32).reshape(n_s, 1, d_model)

    a_s = []
    for g in range(N_DIL):
        kv = zs[:, g * QKV_W + D_A:(g + 1) * QKV_W].reshape(1, n_s, 1, 2, A_HEADS, HEAD_DIM)
        a_s.append(jnp.concatenate([caches[g][:, :, 1:], kv], axis=2))
    ssm_s = ssm_state(hr_s, hi_s, n_s)
    conv_s = conv_s[None]
    ffn_s = jnp.stack([ffn0_s, ffn1_s])

    return (y_prompt, y_sample, a_p[0], a_s[0], a_p[1], a_s[1], a_p[2], a_s[2],
            ssm_p, ssm_s, conv_p, conv_s, ffn_p, ffn_s)
```

```python
import functools
import math

import jax
import jax.numpy as jnp
from jax import lax
from jax.experimental import pallas as pl
from jax.experimental.pallas import tpu as pltpu

F32 = jnp.float32
BF16 = jnp.bfloat16

HEAD_DIM = 128
A_HEADS = 16
D_A = A_HEADS * HEAD_DIM
A_WINDOWS = (128, 512, 2048)
A_DILATIONS = (1, 4, 16)
BAND = 128
N_DIL = 3
QKV_W = 3 * D_A
ROPE_THETA = 10000.0
PAST_LEN = 16384
SSM_GROUP = 16
SSM_STATE = 64
SLAB_G = 8
SLAB_S = SLAB_G * SSM_STATE
CHUNK = 8
CONV_WIDTH = 31
CONV_HALO = 32
FFN_HALO = 16
RMS_EPS = 1e-6
LN_EPS = 1e-5
NEG_BIG = -1e30
VMEM_LIMIT_V7X = 56 * 1024 * 1024


def _cp(*sem):
    return pltpu.CompilerParams(dimension_semantics=sem, vmem_limit_bytes=VMEM_LIMIT_V7X)


def _dot(a, b, **kw):
    return jnp.dot(a, b, preferred_element_type=F32, **kw)


def _dot_nt(a, b, **kw):
    return lax.dot_general(a, b, (((1,), (1,)), ((), ())), preferred_element_type=F32, **kw)


def _sigmoid(x):
    return 1.0 / (1.0 + jnp.exp(-x))


def _rmsnorm_kernel(x_ref, g_ref, o_ref):
    x = x_ref[...]
    ms = jnp.mean(x * x, axis=-1, keepdims=True)
    o_ref[...] = (x * lax.rsqrt(ms + RMS_EPS) * g_ref[...]).astype(o_ref.dtype)


def _rmsnorm(x, g, out_dtype):
    m, d = x.shape
    tm = min(m, 256)
    return pl.pallas_call(
        _rmsnorm_kernel,
        grid=(m // tm,),
        in_specs=[pl.BlockSpec((tm, d), lambda i: (i, 0)),
                  pl.BlockSpec((1, d), lambda i: (0, 0))],
        out_specs=pl.BlockSpec((tm, d), lambda i: (i, 0)),
        out_shape=jax.ShapeDtypeStruct((m, d), out_dtype),
        compiler_params=_cp("parallel"),
        name="rmsnorm",
    )(x, g.reshape(1, d))


def _rope_table_kernel(invf_ref, cos_ref, sin_ref, *, pos0, step):
    rows = cos_ref.shape[0]
    row = lax.broadcasted_iota(jnp.int32, (rows, HEAD_DIM), 0)
    lane = lax.broadcasted_iota(jnp.int32, (rows, HEAD_DIM), 1)
    pos = (row * step + pos0).astype(F32)
    ang = pos * invf_ref[...]
    cos_ref[...] = jnp.cos(ang)
    s = jnp.sin(ang)
    sin_ref[...] = jnp.where(lane < HEAD_DIM // 2, -s, s)


def _rope_tables(rows, pos0, step):
    half = HEAD_DIM // 2
    inv_freq = ROPE_THETA ** (-jnp.arange(half, dtype=F32) / half)
    invf = jnp.concatenate([inv_freq, inv_freq]).reshape(1, HEAD_DIM)
    shp = jax.ShapeDtypeStruct((rows, HEAD_DIM), F32)
    return pl.pallas_call(
        functools.partial(_rope_table_kernel, pos0=pos0, step=step),
        out_shape=(shp, shp),
        name="rope_tables",
    )(invf)


def _inproj_kernel(x_ref, w_ref, cos_ref, sin_ref, o_ref, *, tn):
    j = pl.program_id(1)
    acc = _dot(x_ref[...], w_ref[...])
    seg = (j * tn) // D_A
    is_rope = jnp.logical_and(seg < 3 * N_DIL, seg % 3 != 2)

    @pl.when(is_rope)
    def _():
        cos = cos_ref[...]
        sin = sin_ref[...]
        for h in range(tn // HEAD_DIM):
            sl = slice(h * HEAD_DIM, (h + 1) * HEAD_DIM)
            xh = acc[:, sl]
            o_ref[:, sl] = xh * cos + pltpu.roll(xh, HEAD_DIM // 2, 1) * sin

    @pl.when(jnp.logical_not(is_rope))
    def _():
        o_ref[...] = acc


def _inproj(hn, w, cos, sin, tm, tn):
    m, k = hn.shape
    n = w.shape[1]
    ct = cos.shape[0] // tm
    return pl.pallas_call(
        functools.partial(_inproj_kernel, tn=tn),
        grid=(m // tm, n // tn),
        in_specs=[pl.BlockSpec((tm, k), lambda i, j: (i, 0)),
                  pl.BlockSpec((k, tn), lambda i, j: (0, j)),
                  pl.BlockSpec((tm, HEAD_DIM), lambda i, j: (i % ct, 0)),
                  pl.BlockSpec((tm, HEAD_DIM), lambda i, j: (i % ct, 0))],
        out_specs=pl.BlockSpec((tm, tn), lambda i, j: (i, j)),
        out_shape=jax.ShapeDtypeStruct((m, n), F32),
        compiler_params=_cp("parallel", "arbitrary"),
        name="inproj_rope",
    )(hn, w, cos, sin)


def _band_attn_kernel(q_ref, kp_ref, kc_ref, vp_ref, vc_ref, o_ref, lse_ref):
    has_prev = pl.program_id(2) > 0
    qi = lax.broadcasted_iota(jnp.int32, (BAND, 2 * BAND), 0)
    ki = lax.broadcasted_iota(jnp.int32, (BAND, 2 * BAND), 1)
    valid = jnp.logical_and(jnp.logical_and(ki >= qi, ki <= qi + BAND),
                            jnp.logical_or(ki >= BAND, has_prev))
    lane = lax.broadcasted_iota(jnp.int32, (BAND, HEAD_DIM), 1)
    lse_tile = jnp.zeros((BAND, HEAD_DIM), F32)
    scale = HEAD_DIM ** -0.5
    for h in range(A_HEADS):
        sl = slice(h * HEAD_DIM, (h + 1) * HEAD_DIM)
        q = q_ref[:, sl].astype(BF16)
        k = jnp.concatenate([kp_ref[:, sl], kc_ref[:, sl]], axis=0).astype(BF16)
        v = jnp.concatenate([vp_ref[:, sl], vc_ref[:, sl]], axis=0).astype(BF16)
        s = _dot_nt(q, k) * scale
        s = jnp.where(valid, s, NEG_BIG)
        m = jnp.max(s, axis=-1, keepdims=True)
        p = jnp.exp(s - m)
        l = jnp.sum(p, axis=-1, keepdims=True)
        o_ref[:, sl] = _dot(p.astype(BF16), v) / l
        lse_tile = jnp.where(lane == h, m + jnp.log(l), lse_tile)
    lse_ref[...] = lse_tile


def _band_attn_dil_kernel(*refs, d, hpb, use_prev):
    if use_prev:
        q_ref, kp_ref, kc_ref, vp_ref, vc_ref, o_ref, lse_ref, qs, ks, vs, os_ = refs
    else:
        q_ref, kc_ref, vc_ref, o_ref, lse_ref, qs, ks, vs, os_ = refs
    rows = BAND * d
    hb = pl.program_id(2)
    has_prev = pl.program_id(1) > 0
    for h in range(hpb):
        sl = slice(h * HEAD_DIM, (h + 1) * HEAD_DIM)
        qs[h] = q_ref[:, sl]
        if use_prev:
            ks[h, 0:rows] = kp_ref[:, sl]
            ks[h, rows:2 * rows] = kc_ref[:, sl]
            vs[h, 0:rows] = vp_ref[:, sl]
            vs[h, rows:2 * rows] = vc_ref[:, sl]
        else:
            ks[h] = kc_ref[:, sl]
            vs[h] = vc_ref[:, sl]

    @pl.when(hb == 0)
    def _():
        lse_ref[...] = jnp.zeros(lse_ref.shape, F32)

    nk = 2 * BAND if use_prev else BAND
    qi = lax.broadcasted_iota(jnp.int32, (BAND, nk), 0)
    ki = lax.broadcasted_iota(jnp.int32, (BAND, nk), 1)
    if use_prev:
        valid = jnp.logical_and(jnp.logical_and(ki >= qi, ki <= qi + BAND),
                                jnp.logical_or(ki >= BAND, has_prev))
    else:
        valid = ki <= qi
    lane = lax.broadcasted_iota(jnp.int32, (BAND, HEAD_DIM), 1)
    scale = HEAD_DIM ** -0.5

    def body(it, carry):
        h = it // d
        r = it % d
        own = pl.ds(r, BAND, stride=d)
        q = qs[h, own, :].astype(BF16)
        if use_prev:
            cur = pl.ds(rows + r, BAND, stride=d)
            k = jnp.concatenate([ks[h, own, :], ks[h, cur, :]], axis=0).astype(BF16)
            v = jnp.concatenate([vs[h, own, :], vs[h, cur, :]], axis=0).astype(BF16)
        else:
            k = ks[h, own, :].astype(BF16)
            v = vs[h, own, :].astype(BF16)
        s = _dot_nt(q, k) * scale
        s = jnp.where(valid, s, NEG_BIG)
        m = jnp.max(s, axis=-1, keepdims=True)
        p = jnp.exp(s - m)
        l = jnp.sum(p, axis=-1, keepdims=True)
        os_[h, own, :] = _dot(p.astype(BF16), v) / l
        lse_ref[own, :] = jnp.where(lane == hb * hpb + h, m + jnp.log(l), lse_ref[own, :])
        return carry

    lax.fori_loop(0, hpb * d, body, 0)
    for h in range(hpb):
        o_ref[:, h * HEAD_DIM:(h + 1) * HEAD_DIM] = os_[h]


def _band_attn(z, n_seq, t, g):
    d = A_DILATIONS[g]
    width = z.shape[1]
    zv = z.reshape(n_seq, t, width)
    if d == 1:
        qc = 3 * g

        def cur(off):
            return pl.BlockSpec((None, BAND, D_A), lambda n, r, b: (n, b, qc + off))

        def prev(off):
            return pl.BlockSpec((None, BAND, D_A), lambda n, r, b: (n, jnp.maximum(b - 1, 0), qc + off))

        o, lse = pl.pallas_call(
            _band_attn_kernel,
            grid=(n_seq, 1, t // BAND),
            in_specs=[cur(0), prev(1), cur(1), prev(2), cur(2)],
            out_specs=[pl.BlockSpec((None, BAND, D_A), lambda n, r, b: (n, b, 0)),
                       pl.BlockSpec((None, BAND, HEAD_DIM), lambda n, r, b: (n, b, 0))],
            out_shape=[jax.ShapeDtypeStruct((n_seq, t, D_A), F32),
                       jax.ShapeDtypeStruct((n_seq, t, HEAD_DIM), F32)],
            compiler_params=_cp("parallel", "parallel", "arbitrary"),
            name="band_attn_d1",
        )(zv, zv, zv, zv, zv)
        return o.reshape(n_seq * t, D_A), lse.reshape(n_seq * t, HEAD_DIM)

    rows = BAND * d
    nb = t // rows
    use_prev = nb > 1
    hpb = 8 if d <= 4 else 2
    cols = hpb * HEAD_DIM
    cpq = D_A // cols
    qc = 3 * g * cpq

    def cur(off):
        return pl.BlockSpec((None, rows, cols), lambda n, b, hb: (n, b, qc + off * cpq + hb))

    def prev(off):
        return pl.BlockSpec((None, rows, cols),
                            lambda n, b, hb: (n, jnp.maximum(b - 1, 0), qc + off * cpq + hb))

    in_specs = [cur(0), prev(1), cur(1), prev(2), cur(2)] if use_prev else [cur(0), cur(1), cur(2)]
    kv_rows = 2 * rows if use_prev else rows
    o, lse = pl.pallas_call(
        functools.partial(_band_attn_dil_kernel, d=d, hpb=hpb, use_prev=use_prev),
        grid=(n_seq, nb, cpq),
        in_specs=in_specs,
        out_specs=[pl.BlockSpec((None, rows, cols), lambda n, b, hb: (n, b, hb)),
                   pl.BlockSpec((None, rows, HEAD_DIM), lambda n, b, hb: (n, b, 0))],
        out_shape=[jax.ShapeDtypeStruct((n_seq, t, D_A), F32),
                   jax.ShapeDtypeStruct((n_seq, t, HEAD_DIM), F32)],
        scratch_shapes=[pltpu.VMEM((hpb, rows, HEAD_DIM), F32),
                        pltpu.VMEM((hpb, kv_rows, HEAD_DIM), F32),
                        pltpu.VMEM((hpb, kv_rows, HEAD_DIM), F32),
                        pltpu.VMEM((hpb, rows, HEAD_DIM), F32)],
        compiler_params=_cp("parallel", "parallel", "arbitrary"),
        name=f"band_attn_d{d}",
    )(*([zv] * len(in_specs)))
    return o.reshape(n_seq * t, D_A), lse.reshape(n_seq * t, HEAD_DIM)


def _combine_kernel(o0_ref, o1_ref, o2_ref, l0_ref, l1_ref, l2_ref, out_ref):
    a0, a1, a2 = l0_ref[...], l1_ref[...], l2_ref[...]
    m = jnp.maximum(jnp.maximum(a0, a1), a2)
    e0, e1, e2 = jnp.exp(a0 - m), jnp.exp(a1 - m), jnp.exp(a2 - m)
    inv = 1.0 / (e0 + e1 + e2)
    w0, w1, w2 = e0 * inv, e1 * inv, e2 * inv
    for h in range(A_HEADS):
        sl = slice(h * HEAD_DIM, (h + 1) * HEAD_DIM)
        out_ref[:, sl] = (w0[:, h:h + 1] * o0_ref[:, sl] + w1[:, h:h + 1] * o1_ref[:, sl]
                          + w2[:, h:h + 1] * o2_ref[:, sl]).astype(out_ref.dtype)


def _combine(os_, lses):
    m = os_[0].shape[0]
    tm = 256
    ob = pl.BlockSpec((tm, D_A), lambda i: (i, 0))
    lb = pl.BlockSpec((tm, HEAD_DIM), lambda i: (i, 0))
    return pl.pallas_call(
        _combine_kernel,
        grid=(m // tm,),
        in_specs=[ob, ob, ob, lb, lb, lb],
        out_specs=ob,
        out_shape=jax.ShapeDtypeStruct((m, D_A), BF16),
        compiler_params=_cp("parallel"),
        name="attn_combine",
    )(*os_, *lses)


def _step_attn_kernel(z_ref, c0_ref, c1_ref, c2_ref, o_ref):
    scale = HEAD_DIM ** -0.5
    ms, ls, accs = [], [], []
    for g, c_ref in enumerate((c0_ref, c1_ref, c2_ref)):
        base = 3 * g * A_HEADS
        q = z_ref[base:base + A_HEADS, :]
        kn = z_ref[base + A_HEADS:base + 2 * A_HEADS, :]
        vn = z_ref[base + 2 * A_HEADS:base + 3 * A_HEADS, :]
        s = jnp.sum(c_ref[:, 0] * q[None], axis=-1, keepdims=True) * scale
        s_new = jnp.sum(kn * q, axis=-1, keepdims=True) * scale
        m = jnp.maximum(jnp.max(s, axis=0), s_new)
        p = jnp.exp(s - m[None])
        p_new = jnp.exp(s_new - m)
        ls.append(jnp.sum(p, axis=0) + p_new)
        accs.append(jnp.sum(p * c_ref[:, 1], axis=0) + p_new * vn)
        ms.append(m)
    mm = jnp.maximum(jnp.maximum(ms[0], ms[1]), ms[2])
    es = [jnp.exp(m - mm) for m in ms]
    den = es[0] * ls[0] + es[1] * ls[1] + es[2] * ls[2]
    num = es[0] * accs[0] + es[1] * accs[1] + es[2] * accs[2]
    o_ref[...] = (num / den).astype(o_ref.dtype)


def _step_attn(zs, caches):
    n, width = zs.shape
    cviews, cspecs = [], []
    for g, c in enumerate(caches):
        d = A_DILATIONS[g]
        cviews.append(c.reshape(n, c.shape[1] // d, d, 2, A_HEADS, HEAD_DIM))
        cspecs.append(pl.BlockSpec((None, BAND, None, 2, A_HEADS, HEAD_DIM),
                                   lambda i: (i, 0, 0, 0, 0, 0)))
    zrows = width // HEAD_DIM
    out = pl.pallas_call(
        _step_attn_kernel,
        grid=(n,),
        in_specs=[pl.BlockSpec((None, zrows, HEAD_DIM), lambda i: (i, 0, 0))] + cspecs,
        out_specs=pl.BlockSpec((None, A_HEADS, HEAD_DIM), lambda i: (i, 0, 0)),
        out_shape=jax.ShapeDtypeStruct((n, A_HEADS, HEAD_DIM), BF16),
        compiler_params=_cp("parallel"),
        name="step_attn",
    )(zs.reshape(n, zrows, HEAD_DIM), *cviews)
    return out.reshape(n, D_A)


def _cache_shift_kernel(c_ref, nxt_ref, new_ref, o_ref, *, lb):
    last = pl.program_id(1) == pl.num_programs(1) - 1
    o_ref[0:lb - 1] = c_ref[1:lb]

    @pl.when(last)
    def _():
        o_ref[lb - 1] = new_ref[0]

    @pl.when(jnp.logical_not(last))
    def _():
        o_ref[lb - 1] = nxt_ref[0]


def _cache_shift(cache, new_kv):
    n, length = cache.shape[:2]
    lb = BAND
    nb = length // lb
    tail = cache.shape[2:]
    zeros = (0,) * len(tail)
    return pl.pallas_call(
        functools.partial(_cache_shift_kernel, lb=lb),
        grid=(n, nb),
        in_specs=[pl.BlockSpec((None, lb) + tail, lambda i, b: (i, b) + zeros),
                  pl.BlockSpec((None, 1) + tail, lambda i, b: (i, jnp.minimum(b + 1, nb - 1) * lb) + zeros),
                  pl.BlockSpec((None, 1) + tail, lambda i, b: (i, 0) + zeros)],
        out_specs=pl.BlockSpec((None, lb) + tail, lambda i, b: (i, b) + zeros),
        out_shape=jax.ShapeDtypeStruct(cache.shape, cache.dtype),
        compiler_params=_cp("parallel", "parallel"),
        name="cache_shift",
    )(cache, cache, new_kv)


def _kv_rows_kernel(x_ref, o_ref):
    for h in range(A_HEADS):
        o_ref[:, h, :] = x_ref[:, h * HEAD_DIM:(h + 1) * HEAD_DIM]


def _kv_rows(z, n_seq, t, g, length):
    width = z.shape[1]
    zv = z.reshape(n_seq, t, width)
    tm = BAND
    r0 = (t - length) // tm
    c0 = 3 * g + 1
    return pl.pallas_call(
        _kv_rows_kernel,
        grid=(n_seq, length // tm, 2),
        in_specs=[pl.BlockSpec((None, tm, D_A), lambda n, i, kv: (n, r0 + i, c0 + kv))],
        out_specs=pl.BlockSpec((None, tm, None, A_HEADS, HEAD_DIM), lambda n, i, kv: (n, i, kv, 0, 0)),
        out_shape=jax.ShapeDtypeStruct((n_seq, length, 2, A_HEADS, HEAD_DIM), F32),
        compiler_params=_cp("parallel", "parallel", "parallel"),
        name="kv_rows",
    )(zv)


def _ssm_discretise(lre, lim, ldt):
    dt = jnp.exp(ldt)
    mag = jnp.exp(lre * dt)
    ar = mag * jnp.cos(lim * dt)
    ai = mag * jnp.sin(lim * dt)
    den = lre * lre + lim * lim
    zr = ((ar - 1.0) * lre + ai * lim) / den
    zi = (ai * lre - (ar - 1.0) * lim) / den
    return ar, ai, zr, zi


def _slab_mask():
    r = lax.broadcasted_iota(jnp.int32, (SLAB_G * SSM_GROUP, SLAB_S), 0) // SSM_GROUP
    c = lax.broadcasted_iota(jnp.int32, (SLAB_G * SSM_GROUP, SLAB_S), 1) // SSM_STATE
    return (r == c).astype(F32)


def _block_diag(x16, mask):
    return jnp.concatenate([x16] * SLAB_G, axis=0) * mask


def _ssm_prompt_kernel(u_ref, lre_ref, lim_ref, ldt_ref, bre_ref, bim_ref, cre_ref, cim_ref, d_ref,
                       y_ref, hlr_ref, hli_ref,
                       wer_ref, wei_ref, tz_ref, bdr_ref, bdi_ref, er_ref, ei_ref, *, n_seq, n_chunk):
    hi = lax.Precision.HIGHEST
    rows = n_seq * n_chunk

    def step_rows(j):
        return pl.ds(j, rows, stride=CHUNK)

    ar, ai, zr, zi = _ssm_discretise(lre_ref[...], lim_ref[...], ldt_ref[...])
    mask = _slab_mask()
    bre, bim = bre_ref[...], bim_ref[...]
    cre, cim = cre_ref[...], cim_ref[...]
    cre_bd = _block_diag(cre, mask)
    cim_bd = _block_diag(cim, mask)
    pr, pim = jnp.ones_like(ar), jnp.zeros_like(ar)
    for m in range(CHUNK + 1):
        if m < CHUNK:
            wr = zr * pr - zi * pim
            wi = zr * pim + zi * pr
            we_r = _block_diag(bre * wr - bim * wi, mask)
            we_i = _block_diag(bre * wi + bim * wr, mask)
            wer_ref[m] = we_r.astype(BF16)
            wei_ref[m] = we_i.astype(BF16)
            tz = _dot_nt(we_r, cre_bd, precision=hi) - _dot_nt(we_i, cim_bd, precision=hi)
            tz_ref[m] = tz.astype(BF16)
        if m >= 1:
            bdr_ref[m - 1] = _block_diag(cre * pr - cim * pim, mask).astype(BF16)
            bdi_ref[m - 1] = _block_diag(-(cre * pim + cim * pr), mask).astype(BF16)
        if m == CHUNK:
            a8r, a8i = pr, pim
        pr, pim = pr * ar - pim * ai, pr * ai + pim * ar

    er = jnp.zeros(er_ref.shape, F32)
    ei = jnp.zeros(ei_ref.shape, F32)
    for j in range(CHUNK):
        xb = u_ref[step_rows(j), :].astype(BF16)
        er = er + _dot(xb, wer_ref[CHUNK - 1 - j])
        ei = ei + _dot(xb, wei_ref[CHUNK - 1 - j])
    er_ref[...] = er
    ei_ref[...] = ei

    def body(k, carry):
        out = []
        for n in range(n_seq):
            hr, him = carry[2 * n], carry[2 * n + 1]
            row = n * n_chunk + k
            e_r = er_ref[pl.ds(row, 1), :]
            e_i = ei_ref[pl.ds(row, 1), :]
            er_ref[pl.ds(row, 1), :] = hr
            ei_ref[pl.ds(row, 1), :] = him
            out.append(a8r * hr - a8i * him + e_r)
            out.append(a8r * him + a8i * hr + e_i)
        return tuple(out)

    zero = jnp.zeros((1, SLAB_S), F32)
    fin = lax.fori_loop(0, n_chunk, body, (zero,) * (2 * n_seq))
    for n in range(n_seq):
        hlr_ref[pl.ds(n, 1), :] = fin[2 * n]
        hli_ref[pl.ds(n, 1), :] = fin[2 * n + 1]

    hrb = er_ref[...].astype(BF16)
    hib = ei_ref[...].astype(BF16)
    for jp in range(CHUNK):
        y = _dot_nt(hrb, bdr_ref[jp]) + _dot_nt(hib, bdi_ref[jp])
        for j in range(jp + 1):
            y = y + _dot(u_ref[step_rows(j), :].astype(BF16), tz_ref[jp - j])
        y_ref[step_rows(jp), :] = y + d_ref[...] * u_ref[step_rows(jp), :]


def _ssm_params(a_re, a_im, b_re, b_im, c_re, c_im, d_skip, log_dt):
    g = a_re.shape[0]
    ns = g // SLAB_G

    def row(x):
        return x.astype(F32).reshape(ns, 1, SLAB_S)

    def chan_rows(x):
        return x.astype(F32).reshape(ns, SLAB_G, SSM_GROUP, SSM_STATE).transpose(0, 2, 1, 3) \
            .reshape(ns, SSM_GROUP, SLAB_S)

    ldt = jnp.broadcast_to(log_dt.astype(F32)[:, None], (g, SSM_STATE))
    return (row(a_re), row(a_im), row(ldt),
            chan_rows(b_re.transpose(0, 2, 1)), chan_rows(b_im.transpose(0, 2, 1)),
            chan_rows(c_re), chan_rows(c_im),
            d_skip.astype(F32).reshape(ns, 1, SLAB_G * SSM_GROUP))


def _ssm_prompt(z, n_seq, t, u_col0, sp):
    m, width = z.shape
    rows = m // CHUNK
    n_chunk = t // CHUNK
    lanes = SLAB_G * SSM_GROUP
    ns = sp[0].shape[0]
    ub = u_col0 // lanes

    def pspec(r, c):
        return pl.BlockSpec((None, r, c), lambda s: (s, 0, 0))

    y, hlr, hli = pl.pallas_call(
        functools.partial(_ssm_prompt_kernel, n_seq=n_seq, n_chunk=n_chunk),
        grid=(ns,),
        in_specs=[pl.BlockSpec((m, lanes), lambda s: (0, ub + s)),
                  pspec(1, SLAB_S), pspec(1, SLAB_S), pspec(1, SLAB_S),
                  pspec(SSM_GROUP, SLAB_S), pspec(SSM_GROUP, SLAB_S),
                  pspec(SSM_GROUP, SLAB_S), pspec(SSM_GROUP, SLAB_S), pspec(1, lanes)],
        out_specs=[pl.BlockSpec((m, lanes), lambda s: (0, s)),
                   pl.BlockSpec((None, n_seq, SLAB_S), lambda s: (s, 0, 0)),
                   pl.BlockSpec((None, n_seq, SLAB_S), lambda s: (s, 0, 0))],
        out_shape=[jax.ShapeDtypeStruct((m, ns * lanes), F32),
                   jax.ShapeDtypeStruct((ns, n_seq, SLAB_S), F32),
                   jax.ShapeDtypeStruct((ns, n_seq, SLAB_S), F32)],
        scratch_shapes=[pltpu.VMEM((CHUNK, lanes, SLAB_S), BF16),
                        pltpu.VMEM((CHUNK, lanes, SLAB_S), BF16),
                        pltpu.VMEM((CHUNK, lanes, lanes), BF16),
                        pltpu.VMEM((CHUNK, lanes, SLAB_S), BF16),
                        pltpu.VMEM((CHUNK, lanes, SLAB_S), BF16),
                        pltpu.VMEM((rows, SLAB_S), F32),
                        pltpu.VMEM((rows, SLAB_S), F32)],
        compiler_params=_cp("parallel"),
        name="ssm_prompt",
    )(z, *sp)
    return y, hlr, hli


def _ssm_step_kernel(u_ref, h0r_ref, h0i_ref, lre_ref, lim_ref, ldt_ref, bre_ref, bim_ref,
                     cre_ref, cim_ref, d_ref, y_ref, hr_ref, hi_ref):
    ar, ai, zr, zi = _ssm_discretise(lre_ref[...], lim_ref[...], ldt_ref[...])
    mask = _slab_mask()
    bre, bim = bre_ref[...], bim_ref[...]
    we_r = _block_diag(bre * zr - bim * zi, mask).astype(BF16)
    we_i = _block_diag(bre * zi + bim * zr, mask).astype(BF16)
    u = u_ref[...]
    ub = u.astype(BF16)
    h0r, h0i = h0r_ref[...], h0i_ref[...]
    hr = ar * h0r - ai * h0i + _dot(ub, we_r)
    him = ar * h0i + ai * h0r + _dot(ub, we_i)
    hr_ref[...] = hr
    hi_ref[...] = him
    cr_bd = _block_diag(cre_ref[...], mask).astype(BF16)
    ci_bd = _block_diag(cim_ref[...], mask).astype(BF16)
    y_ref[...] = (_dot_nt(hr.astype(BF16), cr_bd) - _dot_nt(him.astype(BF16), ci_bd)
                  + d_ref[...] * u)


def _ssm_step(zs, u_col0, h0r, h0i, sp):
    n = zs.shape[0]
    lanes = SLAB_G * SSM_GROUP
    ns = sp[0].shape[0]
    ub = u_col0 // lanes

    def pspec(r, c):
        return pl.BlockSpec((None, r, c), lambda s: (s, 0, 0))

    hspec = pl.BlockSpec((None, n, SLAB_S), lambda s: (s, 0, 0))
    return pl.pallas_call(
        _ssm_step_kernel,
        grid=(ns,),
        in_specs=[pl.BlockSpec((n, lanes), lambda s: (0, ub + s)), hspec, hspec,
                  pspec(1, SLAB_S), pspec(1, SLAB_S), pspec(1, SLAB_S),
                  pspec(SSM_GROUP, SLAB_S), pspec(SSM_GROUP, SLAB_S),
                  pspec(SSM_GROUP, SLAB_S), pspec(SSM_GROUP, SLAB_S), pspec(1, lanes)],
        out_specs=[pl.BlockSpec((n, lanes), lambda s: (0, s)), hspec, hspec],
        out_shape=[jax.ShapeDtypeStruct((n, ns * lanes), F32),
                   jax.ShapeDtypeStruct((ns, n, SLAB_S), F32),
                   jax.ShapeDtypeStruct((ns, n, SLAB_S), F32)],
        compiler_params=_cp("parallel"),
        name="ssm_step",
    )(zs, h0r, h0i, *sp)


def _gelu_tanh(x):
    return 0.5 * x * (1.0 + jnp.tanh(math.sqrt(2.0 / math.pi) * (x + 0.044715 * (x * x * x))))


def _ssm_gate_kernel(y_ref, w_ref, b_ref, o_ref):
    yg = _gelu_tanh(y_ref[...])
    gate = _dot(yg.astype(BF16), w_ref[...]) + b_ref[...]
    o_ref[...] = (yg * _sigmoid(gate)).astype(o_ref.dtype)


def _ssm_gate(y, w, b, tm):
    m, d = y.shape
    return pl.pallas_call(
        _ssm_gate_kernel,
        grid=(m // tm,),
        in_specs=[pl.BlockSpec((tm, d), lambda i: (i, 0)),
                  pl.BlockSpec((d, d), lambda i: (0, 0)),
                  pl.BlockSpec((1, d), lambda i: (0, 0))],
        out_specs=pl.BlockSpec((tm, d), lambda i: (i, 0)),
        out_shape=jax.ShapeDtypeStruct((m, d), BF16),
        compiler_params=_cp("parallel"),
        name="ssm_gate",
    )(y, w, b.reshape(1, d).astype(F32))


def _outproj_kernel(a_ref, s_ref, wa_ref, ws_ref, x_ref, o_ref):
    o_ref[...] = x_ref[...] + _dot(a_ref[...], wa_ref[...]) + _dot(s_ref[...], ws_ref[...])


def _outproj(attn, ssm, w, x, tm, tn):
    m, ka = attn.shape
    n = w.shape[1]
    return pl.pallas_call(
        _outproj_kernel,
        grid=(m // tm, n // tn),
        in_specs=[pl.BlockSpec((tm, ka), lambda i, j: (i, 0)),
                  pl.BlockSpec((tm, ka), lambda i, j: (i, 0)),
                  pl.BlockSpec((ka, tn), lambda i, j: (0, j)),
                  pl.BlockSpec((ka, tn), lambda i, j: (1, j)),
                  pl.BlockSpec((tm, tn), lambda i, j: (i, j))],
        out_specs=pl.BlockSpec((tm, tn), lambda i, j: (i, j)),
        out_shape=jax.ShapeDtypeStruct((m, n), F32),
        compiler_params=_cp("parallel", "arbitrary"),
        name="outproj",
    )(attn, ssm, w, w, x)


def _mm_res_kernel(a_ref, w_ref, b_ref, x_ref, o_ref):
    o_ref[...] = x_ref[...] + (_dot(a_ref[...], w_ref[...]) + b_ref[...])


def _mm_res(a, w, b, x, tm, tn, name):
    m, k = a.shape
    n = w.shape[1]
    return pl.pallas_call(
        _mm_res_kernel,
        grid=(m // tm, n // tn),
        in_specs=[pl.BlockSpec((tm, k), lambda i, j: (i, 0)),
                  pl.BlockSpec((k, tn), lambda i, j: (0, j)),
                  pl.BlockSpec((1, tn), lambda i, j: (0, j)),
                  pl.BlockSpec((tm, tn), lambda i, j: (i, j))],
        out_specs=pl.BlockSpec((tm, tn), lambda i, j: (i, j)),
        out_shape=jax.ShapeDtypeStruct((m, n), F32),
        compiler_params=_cp("parallel", "arbitrary"),
        name=name,
    )(a, w, b.reshape(1, n).astype(F32), x)


def _pw1_kernel(x_ref, wa_ref, wg_ref, ba_ref, bg_ref, o_ref):
    x = x_ref[...]
    a = _dot(x, wa_ref[...]) + ba_ref[...]
    gate = _dot(x, wg_ref[...]) + bg_ref[...]
    o_ref[...] = a * _sigmoid(gate)


def _pw1(hn, w, b, tm, tn):
    m, k = hn.shape
    n = w.shape[1] // 2
    nj = n // tn
    b2 = b.reshape(1, 2 * n).astype(F32)
    return pl.pallas_call(
        _pw1_kernel,
        grid=(m // tm, nj),
        in_specs=[pl.BlockSpec((tm, k), lambda i, j: (i, 0)),
                  pl.BlockSpec((k, tn), lambda i, j: (0, j)),
                  pl.BlockSpec((k, tn), lambda i, j: (0, j + nj)),
                  pl.BlockSpec((1, tn), lambda i, j: (0, j)),
                  pl.BlockSpec((1, tn), lambda i, j: (0, j + nj))],
        out_specs=pl.BlockSpec((tm, tn), lambda i, j: (i, j)),
        out_shape=jax.ShapeDtypeStruct((m, n), F32),
        compiler_params=_cp("parallel", "arbitrary"),
        name="conv_pw1_glu",
    )(hn, w, w, b2, b2)


def _ln_swish(y, g, b):
    mu = jnp.mean(y, axis=-1, keepdims=True)
    yc = y - mu
    var = jnp.mean(yc * yc, axis=-1, keepdims=True)
    yn = yc * lax.rsqrt(var + LN_EPS) * g + b
    return yn * _sigmoid(yn)


def _conv31_kernel(u_ref, halo_ref, w_ref, b_ref, g_ref, bb_ref, o_ref, ext_ref, y_ref,
                   *, tm, tiles_per_seq, cw, rw):
    seq_start = (pl.program_id(0) % tiles_per_seq) == 0
    ext_ref[0:CONV_HALO, :] = jnp.where(seq_start, 0.0, halo_ref[...])
    ext_ref[CONV_HALO:, :] = u_ref[...]
    d = u_ref.shape[1]
    off = CONV_HALO - (CONV_WIDTH - 1)
    for r in range(tm // rw):
        for c in range(d // cw):
            cs = slice(c * cw, (c + 1) * cw)
            acc = w_ref[0:1, cs] * ext_ref[off + r * rw:off + r * rw + rw, cs]
            for j in range(1, CONV_WIDTH):
                acc = acc + w_ref[j:j + 1, cs] * ext_ref[off + r * rw + j:off + r * rw + j + rw, cs]
            y_ref[r * rw:(r + 1) * rw, cs] = acc + b_ref[:, cs]
    o_ref[...] = _ln_swish(y_ref[...], g_ref[...], bb_ref[...]).astype(o_ref.dtype)


def _conv31(u, t, w, b, g, bb, tm):
    m, d = u.shape
    row = lambda v: v.reshape(1, d).astype(F32)
    hb = tm // CONV_HALO
    return pl.pallas_call(
        functools.partial(_conv31_kernel, tm=tm, tiles_per_seq=t // tm, cw=512, rw=64),
        grid=(m // tm,),
        in_specs=[pl.BlockSpec((tm, d), lambda i: (i, 0)),
                  pl.BlockSpec((CONV_HALO, d), lambda i: (jnp.maximum(i * hb - 1, 0), 0)),
                  pl.BlockSpec((CONV_WIDTH, d), lambda i: (0, 0)),
                  pl.BlockSpec((1, d), lambda i: (0, 0)),
                  pl.BlockSpec((1, d), lambda i: (0, 0)),
                  pl.BlockSpec((1, d), lambda i: (0, 0))],
        out_specs=pl.BlockSpec((tm, d), lambda i: (i, 0)),
        out_shape=jax.ShapeDtypeStruct((m, d), BF16),
        scratch_shapes=[pltpu.VMEM((tm + CONV_HALO, d), F32), pltpu.VMEM((tm, d), F32)],
        compiler_params=_cp("parallel"),
        name="conv31_ln_swish",
    )(u, u, w.astype(F32), row(b), row(g), row(bb))


def _conv31_step_kernel(u_ref, st_ref, w_ref, b_ref, g_ref, bb_ref, o_ref):
    acc = w_ref[0:1, :] * st_ref[0]
    for j in range(1, CONV_WIDTH - 1):
        acc = acc + w_ref[j:j + 1, :] * st_ref[j]
    acc = acc + w_ref[CONV_WIDTH - 1:CONV_WIDTH, :] * u_ref[...] + b_ref[...]
    o_ref[...] = _ln_swish(acc, g_ref[...], bb_ref[...]).astype(o_ref.dtype)


def _conv31_step(u, state_t, w, b, g, bb):
    n, d = u.shape
    row = lambda v: v.reshape(1, d).astype(F32)
    return pl.pallas_call(
        _conv31_step_kernel,
        out_shape=jax.ShapeDtypeStruct((n, d), BF16),
        compiler_params=pltpu.CompilerParams(vmem_limit_bytes=VMEM_LIMIT_V7X),
        name="conv31_step",
    )(u, state_t, w.astype(F32), row(b), row(g), row(bb))


def _ffn_up_kernel(x_ref, halo_ref, wg_ref, wv_ref, cwg_ref, cwv_ref, cbg_ref, cbv_ref,
                   act_ref, zlg_ref, zlv_ref, xs_ref, *, tm, tiles_per_seq):
    i = pl.program_id(0)

    @pl.when(pl.program_id(1) == 0)
    def _():
        xs_ref[0:FFN_HALO, :] = halo_ref[...]
        xs_ref[FFN_HALO:, :] = x_ref[...]

    seq_start = (i % tiles_per_seq) == 0

    def half(w_ref, cw_ref, cb_ref, zl_ref):
        z = _dot(xs_ref[...], w_ref[...])
        row = lax.broadcasted_iota(jnp.int32, z.shape, 0)
        z = jnp.where(jnp.logical_and(row < FFN_HALO, seq_start), 0.0, z)
        zl_ref[...] = z[FFN_HALO + tm - 8:, :]
        zc = (cw_ref[2:3, :] * z + cw_ref[1:2, :] * pltpu.roll(z, 1, 0)
              + cw_ref[0:1, :] * pltpu.roll(z, 2, 0) + cb_ref[...])
        return zc[FFN_HALO:, :]

    gate = half(wg_ref, cwg_ref, cbg_ref, zlg_ref)
    val = half(wv_ref, cwv_ref, cbv_ref, zlv_ref)
    act_ref[...] = (gate * _sigmoid(gate) * val).astype(act_ref.dtype)


def _ffn_up(hn, t, w, cw, cb, tm, tn):
    m, k = hn.shape
    dff = w.shape[1] // 2
    nj = dff // tn
    cw = cw.astype(F32)
    cb2 = cb.reshape(1, 2 * dff).astype(F32)
    hb = tm // FFN_HALO
    zl = jax.ShapeDtypeStruct((m // tm, 8, dff), F32)
    return pl.pallas_call(
        functools.partial(_ffn_up_kernel, tm=tm, tiles_per_seq=t // tm),
        grid=(m // tm, nj),
        in_specs=[pl.BlockSpec((tm, k), lambda i, j: (i, 0)),
                  pl.BlockSpec((FFN_HALO, k), lambda i, j: (jnp.maximum(i * hb - 1, 0), 0)),
                  pl.BlockSpec((k, tn), lambda i, j: (0, j)),
                  pl.BlockSpec((k, tn), lambda i, j: (0, j + nj)),
                  pl.BlockSpec((3, tn), lambda i, j: (0, j)),
                  pl.BlockSpec((3, tn), lambda i, j: (0, j + nj)),
                  pl.BlockSpec((1, tn), lambda i, j: (0, j)),
                  pl.BlockSpec((1, tn), lambda i, j: (0, j + nj))],
        out_specs=[pl.BlockSpec((tm, tn), lambda i, j: (i, j)),
                   pl.BlockSpec((None, 8, tn), lambda i, j: (i, 0, j)),
                   pl.BlockSpec((None, 8, tn), lambda i, j: (i, 0, j))],
        out_shape=[jax.ShapeDtypeStruct((m, dff), BF16), zl, zl],
        scratch_shapes=[pltpu.VMEM((tm + FFN_HALO, k), BF16)],
        compiler_params=_cp("parallel", "arbitrary"),
        name="ffn_up_conv_glu",
    )(hn, hn, w, w, cw, cw, cb2, cb2)


def _ffn_up_step_kernel(x_ref, wg_ref, wv_ref, sg_ref, sv_ref, cwg_ref, cwv_ref, cbg_ref, cbv_ref,
                        act_ref, zg_ref, zv_ref):
    x = x_ref[...]

    def half(w_ref, s_ref, cw_ref, cb_ref, z_ref):
        z = _dot(x, w_ref[...])
        z_ref[...] = z
        return cw_ref[2:3, :] * z + cw_ref[1:2, :] * s_ref[1] + cw_ref[0:1, :] * s_ref[0] + cb_ref[...]

    gate = half(wg_ref, sg_ref, cwg_ref, cbg_ref, zg_ref)
    val = half(wv_ref, sv_ref, cwv_ref, cbv_ref, zv_ref)
    act_ref[...] = (gate * _sigmoid(gate) * val).astype(act_ref.dtype)


def _ffn_up_step(hn, state_t, w, cw, cb, tn):
    n, k = hn.shape
    dff = w.shape[1] // 2
    nj = dff // tn
    cw = cw.astype(F32)
    cb2 = cb.reshape(1, 2 * dff).astype(F32)
    zs = jax.ShapeDtypeStruct((n, dff), F32)
    return pl.pallas_call(
        _ffn_up_step_kernel,
        grid=(nj,),
        in_specs=[pl.BlockSpec((n, k), lambda j: (0, 0)),
                  pl.BlockSpec((k, tn), lambda j: (0, j)),
                  pl.BlockSpec((k, tn), lambda j: (0, j + nj)),
                  pl.BlockSpec((2, n, tn), lambda j: (0, 0, j)),
                  pl.BlockSpec((2, n, tn), lambda j: (0, 0, j + nj)),
                  pl.BlockSpec((3, tn), lambda j: (0, j)),
                  pl.BlockSpec((3, tn), lambda j: (0, j + nj)),
                  pl.BlockSpec((1, tn), lambda j: (0, j)),
                  pl.BlockSpec((1, tn), lambda j: (0, j + nj))],
        out_specs=[pl.BlockSpec((n, tn), lambda j: (0, j)),
                   pl.BlockSpec((n, tn), lambda j: (0, j)),
                   pl.BlockSpec((n, tn), lambda j: (0, j))],
        out_shape=[jax.ShapeDtypeStruct((n, dff), BF16), zs, zs],
        compiler_params=_cp("arbitrary"),
        name="ffn_up_step",
    )(hn, w, w, state_t, state_t, cw, cw, cb2, cb2)


def _tile_plan(m):
    return 1024 if m >= 1024 else m


def _even_mixer(x, hn, z, attn, y_ssm, p, tm):
    ssm_out = _ssm_gate(y_ssm, p['w_glu'], p['b_glu'], min(tm, 512))
    return _outproj(attn, ssm_out, p['w_out'], x, tm, 1024 if tm >= 1024 else 2048)


def _ffn(x, t, prev_state_t, norm_g, w_up, cw, cb, w_down, tm):
    m = x.shape[0]
    hn = _rmsnorm(x, norm_g, BF16)
    dff = w_down.shape[0]
    if prev_state_t is None:
        act, zlg, zlv = _ffn_up(hn, t, w_up, cw, cb, tm, 256)
        per = t // tm
        last = jnp.concatenate([zlg[per - 1::per, 6:8], zlv[per - 1::per, 6:8]], axis=-1)
        xo = _mm_res(act, w_down, jnp.zeros((w_down.shape[1],), F32), x, 512, 512, "ffn_down")
    else:
        act, zg, zv = _ffn_up_step(hn, prev_state_t, w_up, cw, cb, 256)
        znew = jnp.concatenate([zg, zv], axis=-1)
        last = jnp.concatenate([prev_state_t[1][:, None], znew[:, None]], axis=1)
        xo = _mm_res(act, w_down, jnp.zeros((w_down.shape[1],), F32), x, m, 512, "ffn_down_step")
    return xo, last


def _conv_module(x, t, prev_state, norm_g, p, tm):
    m = x.shape[0]
    hn = _rmsnorm(x, norm_g, BF16)
    u = _pw1(hn, p['w_pw1'], p['b_pw1'], tm, 512 if tm >= 1024 else 1024)
    if prev_state is None:
        c = _conv31(u, t, p['w_dw'], p['b_dw'], p['ln_g'], p['ln_b'], 128)
        n_seq = m // t
        new_state = u.reshape(n_seq, t, -1)[:, t - (CONV_WIDTH - 1):]
        xo = _mm_res(c, p['w_pw2'], p['b_pw2'], x, tm, 1024, "conv_pw2")
    else:
        c = _conv31_step(u, prev_state.transpose(1, 0, 2), p['w_dw'], p['b_dw'], p['ln_g'], p['ln_b'])
        new_state = jnp.concatenate([prev_state[:, 1:], u[:, None]], axis=1)
        xo = _mm_res(c, p['w_pw2'], p['b_pw2'], x, m, 2048, "conv_pw2_step")
    return xo, new_state


def kernel(x_prompt, x_sample, cache_a0, cache_a1, cache_a2, state_ssm, state_conv, state_ffn, norm_mix, norm_ffn, norm_final, w_in_even, w_out_even, ssm_a_re, ssm_a_im, ssm_b_re, ssm_b_im, ssm_c_re, ssm_c_im, ssm_d, ssm_log_dt, w_glu, b_glu, w_pw1, b_pw1, w_dw, b_dw, ln_g, ln_b, w_pw2, b_pw2, w_up, ffn_dw, ffn_dw_b, w_down):
    n_p, t_p, d_model = x_prompt.shape
    n_s = x_sample.shape[0]
    caches = (cache_a0, cache_a1, cache_a2)
    d_in = w_in_even.shape[-1]
    u_col0 = N_DIL * QKV_W

    w_in_b = w_in_even[0].astype(BF16)
    pe = {'w_out': w_out_even[0].astype(BF16), 'w_glu': w_glu[0].astype(BF16), 'b_glu': b_glu[0]}
    pc = {'w_pw1': w_pw1[0].astype(BF16), 'b_pw1': b_pw1[0], 'w_dw': w_dw[0], 'b_dw': b_dw[0],
          'ln_g': ln_g[0], 'ln_b': ln_b[0], 'w_pw2': w_pw2[0].astype(BF16), 'b_pw2': b_pw2[0]}
    w_up_b = [w_up[i].astype(BF16) for i in range(2)]
    w_down_b = [w_down[i].astype(BF16) for i in range(2)]
    sp = _ssm_params(ssm_a_re[0], ssm_a_im[0], ssm_b_re[0], ssm_b_im[0], ssm_c_re[0], ssm_c_im[0],
                     ssm_d[0], ssm_log_dt[0])
    ns = sp[0].shape[0]

    mp = n_p * t_p
    tm = _tile_plan(mp)
    x = x_prompt.reshape(mp, d_model)
    cos, sin = _rope_tables(t_p, 0, 1)
    hn = _rmsnorm(x, norm_mix[0], BF16)
    z = _inproj(hn, w_in_b, cos, sin, tm, 1024)
    outs, lses = [], []
    for g in range(N_DIL):
        o, lse = _band_attn(z, n_p, t_p, g)
        outs.append(o)
        lses.append(lse)
    attn = _combine(outs, lses)
    y_ssm, hlr, hli = _ssm_prompt(z, n_p, t_p, u_col0, sp)
    x = _even_mixer(x, hn, z, attn, y_ssm, pe, tm)
    x, ffn0_p = _ffn(x, t_p, None, norm_ffn[0], w_up_b[0], ffn_dw[0], ffn_dw_b[0], w_down_b[0], tm)
    x, conv_p = _conv_module(x, t_p, None, norm_mix[1], pc, tm)
    x, ffn1_p = _ffn(x, t_p, None, norm_ffn[1], w_up_b[1], ffn_dw[1], ffn_dw_b[1], w_down_b[1], tm)
    y_prompt = _rmsnorm(x, norm_final, F32).reshape(n_p, t_p, d_model)

    a_p = []
    for g in range(N_DIL):
        a_p.append(_kv_rows(z, n_p, t_p, g, min(A_WINDOWS[g], PAST_LEN))[None])

    def ssm_state(hr, him, n):
        def unslab(h):
            return h.reshape(ns, n, SLAB_G, SSM_STATE).transpose(1, 0, 2, 3).reshape(n, ns * SLAB_G, SSM_STATE)
        return jnp.stack([unslab(hr), unslab(him)], axis=-1)[None]

    ssm_p = ssm_state(hlr, hli, n_p)
    conv_p = conv_p[None]
    ffn_p = jnp.stack([ffn0_p, ffn1_p])

    xs = x_sample.reshape(n_s, d_model)
    cos_s, sin_s = _rope_tables(n_s, PAST_LEN, 0)
    hn = _rmsnorm(xs, norm_mix[0], BF16)
    zs = _inproj(hn, w_in_b, cos_s, sin_s, n_s, 2048)
    attn_s = _step_attn(zs, [c[0] for c in caches])

    def slab(h):
        return h.reshape(n_s, ns, SLAB_S).transpose(1, 0, 2)

    y_s, hr_s, hi_s = _ssm_step(zs, u_col0, slab(state_ssm[0, ..., 0]), slab(state_ssm[0, ..., 1]), sp)
    xs = _even_mixer(xs, hn, zs, attn_s, y_s, pe, n_s)
    xs, ffn0_s = _ffn(xs, 1, state_ffn[0].transpose(1, 0, 2), norm_ffn[0], w_up_b[0], ffn_dw[0],
                      ffn_dw_b[0], w_down_b[0], n_s)
    xs, conv_s = _conv_module(xs, 1, state_conv[0], norm_mix[1], pc, n_s)
    xs, ffn1_s = _ffn(xs, 1, state_ffn[1].transpose(1, 0, 2), norm_ffn[1], w_up_b[1], ffn_dw[1],
                      ffn_dw_b[1], w_down_b[1], n_s)
    y_sample = _rmsnorm(xs, norm_final, F32).reshape(n_s, 1, d_model)

    a_s = []
    for g in range(N_DIL):
        kv = zs[:, g * QKV_W + D_A:(g + 1) * QKV_W].reshape(n_s, 1, 2, A_HEADS, HEAD_DIM)
        a_s.append(_cache_shift(caches[g][0], kv)[None])
    ssm_s = ssm_state(hr_s, hi_s, n_s)
    conv_s = conv_s[None]
    ffn_s = jnp.stack([ffn0_s, ffn1_s])

    return (y_prompt, y_sample, a_p[0], a_s[0], a_p[1], a_s[1], a_p[2], a_s[2],
            ssm_p, ssm_s, conv_p, conv_s, ffn_p, ffn_s)
```

```python
import functools
import math

import jax
import jax.numpy as jnp
from jax import lax
from jax.experimental import pallas as pl
from jax.experimental.pallas import tpu as pltpu

F32 = jnp.float32
BF16 = jnp.bfloat16

HEAD_DIM = 128
A_HEADS = 16
D_A = A_HEADS * HEAD_DIM
A_WINDOWS = (128, 512, 2048)
A_DILATIONS = (1, 4, 16)
BAND = 128
N_DIL = 3
QKV_W = 3 * D_A
ROPE_THETA = 10000.0
PAST_LEN = 16384
SSM_GROUP = 16
SSM_STATE = 64
SLAB_G = 8
SLAB_S = SLAB_G * SSM_STATE
CHUNK = 8
CONV_WIDTH = 31
CONV_HALO = 32
FFN_HALO = 16
RMS_EPS = 1e-6
LN_EPS = 1e-5
NEG_BIG = -1e30
VMEM_LIMIT_V7X = 56 * 1024 * 1024


def _cp(*sem):
    return pltpu.CompilerParams(dimension_semantics=sem, vmem_limit_bytes=VMEM_LIMIT_V7X)


def _dot(a, b, **kw):
    return jnp.dot(a, b, preferred_element_type=F32, **kw)


def _dot_nt(a, b, **kw):
    return lax.dot_general(a, b, (((1,), (1,)), ((), ())), preferred_element_type=F32, **kw)


def _sigmoid(x):
    return 1.0 / (1.0 + jnp.exp(-x))


def _rmsnorm_kernel(x_ref, g_ref, o_ref):
    x = x_ref[...]
    ms = jnp.mean(x * x, axis=-1, keepdims=True)
    o_ref[...] = (x * lax.rsqrt(ms + RMS_EPS) * g_ref[...]).astype(o_ref.dtype)


def _rmsnorm(x, g, out_dtype):
    m, d = x.shape
    tm = min(m, 256)
    return pl.pallas_call(
        _rmsnorm_kernel,
        grid=(m // tm,),
        in_specs=[pl.BlockSpec((tm, d), lambda i: (i, 0)),
                  pl.BlockSpec((1, d), lambda i: (0, 0))],
        out_specs=pl.BlockSpec((tm, d), lambda i: (i, 0)),
        out_shape=jax.ShapeDtypeStruct((m, d), out_dtype),
        compiler_params=_cp("parallel"),
        name="rmsnorm",
    )(x, g.reshape(1, d))


def _rope_table_kernel(invf_ref, cos_ref, sin_ref, *, pos0, step):
    rows = cos_ref.shape[0]
    row = lax.broadcasted_iota(jnp.int32, (rows, HEAD_DIM), 0)
    lane = lax.broadcasted_iota(jnp.int32, (rows, HEAD_DIM), 1)
    pos = (row * step + pos0).astype(F32)
    ang = pos * invf_ref[...]
    cos_ref[...] = jnp.cos(ang)
    s = jnp.sin(ang)
    sin_ref[...] = jnp.where(lane < HEAD_DIM // 2, -s, s)


def _rope_tables(rows, pos0, step):
    half = HEAD_DIM // 2
    inv_freq = ROPE_THETA ** (-jnp.arange(half, dtype=F32) / half)
    invf = jnp.concatenate([inv_freq, inv_freq]).reshape(1, HEAD_DIM)
    shp = jax.ShapeDtypeStruct((rows, HEAD_DIM), F32)
    return pl.pallas_call(
        functools.partial(_rope_table_kernel, pos0=pos0, step=step),
        out_shape=(shp, shp),
        name="rope_tables",
    )(invf)


def _wspec(w, layer, kb, tn, kblk, jmap):
    if layer is None:
        return pl.BlockSpec((kb, tn), lambda *g: (kblk, jmap(*g)))
    return pl.BlockSpec((None, kb, tn), lambda *g: (layer, kblk, jmap(*g)))


def _use_w(w_ref, wb_ref, cols=slice(None)):
    w = w_ref[:, cols]
    if wb_ref is None:
        return w
    wb = w.astype(BF16)
    wb_ref[:, cols] = wb
    return wb


def _inproj_kernel(x_ref, w_ref, cos_ref, sin_ref, o_ref, *wb, tn, part):
    wb_ref = wb[0] if wb else None
    seg = (pl.program_id(1) * tn) // D_A
    is_rope = jnp.logical_and(seg < 3 * N_DIL, seg % 3 != 2)
    cos = cos_ref[...]
    sin = sin_ref[...]
    x = x_ref[...]
    for c0 in range(0, tn, part):
        acc = _dot(x, _use_w(w_ref, wb_ref, slice(c0, c0 + part)))
        for h in range(part // HEAD_DIM):
            xh = acc[:, h * HEAD_DIM:(h + 1) * HEAD_DIM]
            roped = xh * cos + pltpu.roll(xh, HEAD_DIM // 2, 1) * sin
            o_ref[:, c0 + h * HEAD_DIM:c0 + (h + 1) * HEAD_DIM] = jnp.where(is_rope, roped, xh)


def _inproj(hn, w, layer, cos, sin, tm, tn):
    m, k = hn.shape
    n = w.shape[-1]
    ct = cos.shape[0] // tm
    emit = layer is not None
    out_specs = [pl.BlockSpec((tm, tn), lambda i, j: (i, j))]
    out_shape = [jax.ShapeDtypeStruct((m, n), F32)]
    if emit:
        out_specs.append(pl.BlockSpec((k, tn), lambda i, j: (0, j)))
        out_shape.append(jax.ShapeDtypeStruct((k, n), BF16))
    res = pl.pallas_call(
        functools.partial(_inproj_kernel, tn=tn, part=min(tn, 256)),
        grid=(m // tm, n // tn),
        in_specs=[pl.BlockSpec((tm, k), lambda i, j: (i, 0)),
                  _wspec(w, layer, k, tn, 0, lambda i, j: j),
                  pl.BlockSpec((tm, HEAD_DIM), lambda i, j: (i % ct, 0)),
                  pl.BlockSpec((tm, HEAD_DIM), lambda i, j: (i % ct, 0))],
        out_specs=out_specs,
        out_shape=out_shape,
        compiler_params=_cp("parallel", "arbitrary"),
        name="inproj_rope",
    )(hn, w, cos, sin)
    return res if emit else res[0]


def _band_attn_kernel(q_ref, kp_ref, kc_ref, vp_ref, vc_ref, o_ref, lse_ref):
    has_prev = pl.program_id(2) > 0
    qi = lax.broadcasted_iota(jnp.int32, (BAND, 2 * BAND), 0)
    ki = lax.broadcasted_iota(jnp.int32, (BAND, 2 * BAND), 1)
    valid = jnp.logical_and(jnp.logical_and(ki >= qi, ki <= qi + BAND),
                            jnp.logical_or(ki >= BAND, has_prev))
    lane = lax.broadcasted_iota(jnp.int32, (BAND, HEAD_DIM), 1)
    lse_tile = jnp.zeros((BAND, HEAD_DIM), F32)
    scale = HEAD_DIM ** -0.5
    for h in range(A_HEADS):
        sl = slice(h * HEAD_DIM, (h + 1) * HEAD_DIM)
        q = q_ref[:, sl].astype(BF16)
        k = jnp.concatenate([kp_ref[:, sl], kc_ref[:, sl]], axis=0).astype(BF16)
        v = jnp.concatenate([vp_ref[:, sl], vc_ref[:, sl]], axis=0).astype(BF16)
        s = _dot_nt(q, k) * scale
        s = jnp.where(valid, s, NEG_BIG)
        m = jnp.max(s, axis=-1, keepdims=True)
        p = jnp.exp(s - m)
        l = jnp.sum(p, axis=-1, keepdims=True)
        o_ref[:, sl] = _dot(p.astype(BF16), v) / l
        lse_tile = jnp.where(lane == h, m + jnp.log(l), lse_tile)
    lse_ref[...] = lse_tile


def _band_attn_dil_kernel(*refs, d, hpb, use_prev):
    if use_prev:
        q_ref, kp_ref, kc_ref, vp_ref, vc_ref, o_ref, lse_ref, qs, ks, vs, os_ = refs
    else:
        q_ref, kc_ref, vc_ref, o_ref, lse_ref, qs, ks, vs, os_ = refs
    rows = BAND * d
    hb = pl.program_id(2)
    has_prev = pl.program_id(1) > 0
    for h in range(hpb):
        sl = slice(h * HEAD_DIM, (h + 1) * HEAD_DIM)
        qs[h] = q_ref[:, sl]
        if use_prev:
            ks[h, 0:rows] = kp_ref[:, sl]
            ks[h, rows:2 * rows] = kc_ref[:, sl]
            vs[h, 0:rows] = vp_ref[:, sl]
            vs[h, rows:2 * rows] = vc_ref[:, sl]
        else:
            ks[h] = kc_ref[:, sl]
            vs[h] = vc_ref[:, sl]

    @pl.when(hb == 0)
    def _():
        lse_ref[...] = jnp.zeros(lse_ref.shape, F32)

    nk = 2 * BAND if use_prev else BAND
    qi = lax.broadcasted_iota(jnp.int32, (BAND, nk), 0)
    ki = lax.broadcasted_iota(jnp.int32, (BAND, nk), 1)
    if use_prev:
        valid = jnp.logical_and(jnp.logical_and(ki >= qi, ki <= qi + BAND),
                                jnp.logical_or(ki >= BAND, has_prev))
    else:
        valid = ki <= qi
    lane = lax.broadcasted_iota(jnp.int32, (BAND, HEAD_DIM), 1)
    scale = HEAD_DIM ** -0.5

    def body(it, carry):
        h = it // d
        r = it % d
        own = pl.ds(r, BAND, stride=d)
        q = qs[h, own, :].astype(BF16)
        if use_prev:
            cur = pl.ds(rows + r, BAND, stride=d)
            k = jnp.concatenate([ks[h, own, :], ks[h, cur, :]], axis=0).astype(BF16)
            v = jnp.concatenate([vs[h, own, :], vs[h, cur, :]], axis=0).astype(BF16)
        else:
            k = ks[h, own, :].astype(BF16)
            v = vs[h, own, :].astype(BF16)
        s = _dot_nt(q, k) * scale
        s = jnp.where(valid, s, NEG_BIG)
        m = jnp.max(s, axis=-1, keepdims=True)
        p = jnp.exp(s - m)
        l = jnp.sum(p, axis=-1, keepdims=True)
        os_[h, own, :] = _dot(p.astype(BF16), v) / l
        lse_ref[own, :] = jnp.where(lane == hb * hpb + h, m + jnp.log(l), lse_ref[own, :])
        return carry

    lax.fori_loop(0, hpb * d, body, 0, unroll=4)
    for h in range(hpb):
        o_ref[:, h * HEAD_DIM:(h + 1) * HEAD_DIM] = os_[h]


def _band_attn(z, n_seq, t, g):
    d = A_DILATIONS[g]
    width = z.shape[1]
    zv = z.reshape(n_seq, t, width)
    if d == 1:
        qc = 3 * g

        def cur(off):
            return pl.BlockSpec((None, BAND, D_A), lambda n, r, b: (n, b, qc + off))

        def prev(off):
            return pl.BlockSpec((None, BAND, D_A), lambda n, r, b: (n, jnp.maximum(b - 1, 0), qc + off))

        o, lse = pl.pallas_call(
            _band_attn_kernel,
            grid=(n_seq, 1, t // BAND),
            in_specs=[cur(0), prev(1), cur(1), prev(2), cur(2)],
            out_specs=[pl.BlockSpec((None, BAND, D_A), lambda n, r, b: (n, b, 0)),
                       pl.BlockSpec((None, BAND, HEAD_DIM), lambda n, r, b: (n, b, 0))],
            out_shape=[jax.ShapeDtypeStruct((n_seq, t, D_A), F32),
                       jax.ShapeDtypeStruct((n_seq, t, HEAD_DIM), F32)],
            compiler_params=_cp("parallel", "parallel", "arbitrary"),
            name="band_attn_d1",
        )(zv, zv, zv, zv, zv)
        return o.reshape(n_seq * t, D_A), lse.reshape(n_seq * t, HEAD_DIM)

    rows = BAND * d
    nb = t // rows
    use_prev = nb > 1
    hpb = 8 if d <= 4 else 2
    cols = hpb * HEAD_DIM
    cpq = D_A // cols
    qc = 3 * g * cpq

    def cur(off):
        return pl.BlockSpec((None, rows, cols), lambda n, b, hb: (n, b, qc + off * cpq + hb))

    def prev(off):
        return pl.BlockSpec((None, rows, cols),
                            lambda n, b, hb: (n, jnp.maximum(b - 1, 0), qc + off * cpq + hb))

    in_specs = [cur(0), prev(1), cur(1), prev(2), cur(2)] if use_prev else [cur(0), cur(1), cur(2)]
    kv_rows = 2 * rows if use_prev else rows
    o, lse = pl.pallas_call(
        functools.partial(_band_attn_dil_kernel, d=d, hpb=hpb, use_prev=use_prev),
        grid=(n_seq, nb, cpq),
        in_specs=in_specs,
        out_specs=[pl.BlockSpec((None, rows, cols), lambda n, b, hb: (n, b, hb)),
                   pl.BlockSpec((None, rows, HEAD_DIM), lambda n, b, hb: (n, b, 0))],
        out_shape=[jax.ShapeDtypeStruct((n_seq, t, D_A), F32),
                   jax.ShapeDtypeStruct((n_seq, t, HEAD_DIM), F32)],
        scratch_shapes=[pltpu.VMEM((hpb, rows, HEAD_DIM), F32),
                        pltpu.VMEM((hpb, kv_rows, HEAD_DIM), F32),
                        pltpu.VMEM((hpb, kv_rows, HEAD_DIM), F32),
                        pltpu.VMEM((hpb, rows, HEAD_DIM), F32)],
        compiler_params=_cp("parallel", "parallel", "arbitrary"),
        name=f"band_attn_d{d}",
    )(*([zv] * len(in_specs)))
    return o.reshape(n_seq * t, D_A), lse.reshape(n_seq * t, HEAD_DIM)


def _combine_kernel(o0_ref, o1_ref, o2_ref, l0_ref, l1_ref, l2_ref, out_ref):
    a0, a1, a2 = l0_ref[...], l1_ref[...], l2_ref[...]
    m = jnp.maximum(jnp.maximum(a0, a1), a2)
    e0, e1, e2 = jnp.exp(a0 - m), jnp.exp(a1 - m), jnp.exp(a2 - m)
    inv = 1.0 / (e0 + e1 + e2)
    w0, w1, w2 = e0 * inv, e1 * inv, e2 * inv
    for h in range(A_HEADS):
        sl = slice(h * HEAD_DIM, (h + 1) * HEAD_DIM)
        out_ref[:, sl] = (w0[:, h:h + 1] * o0_ref[:, sl] + w1[:, h:h + 1] * o1_ref[:, sl]
                          + w2[:, h:h + 1] * o2_ref[:, sl]).astype(out_ref.dtype)


def _combine(os_, lses):
    m = os_[0].shape[0]
    tm = 256
    ob = pl.BlockSpec((tm, D_A), lambda i: (i, 0))
    lb = pl.BlockSpec((tm, HEAD_DIM), lambda i: (i, 0))
    return pl.pallas_call(
        _combine_kernel,
        grid=(m // tm,),
        in_specs=[ob, ob, ob, lb, lb, lb],
        out_specs=ob,
        out_shape=jax.ShapeDtypeStruct((m, D_A), BF16),
        compiler_params=_cp("parallel"),
        name="attn_combine",
    )(*os_, *lses)


def _step_attn_kernel(z_ref, c0_ref, c1_ref, c2_ref, o_ref):
    scale = HEAD_DIM ** -0.5
    ms, ls, accs = [], [], []
    for g, c_ref in enumerate((c0_ref, c1_ref, c2_ref)):
        base = 3 * g * A_HEADS
        q = z_ref[base:base + A_HEADS, :]
        kn = z_ref[base + A_HEADS:base + 2 * A_HEADS, :]
        vn = z_ref[base + 2 * A_HEADS:base + 3 * A_HEADS, :]
        s = jnp.sum(c_ref[:, 0] * q[None], axis=-1, keepdims=True) * scale
        s_new = jnp.sum(kn * q, axis=-1, keepdims=True) * scale
        m = jnp.maximum(jnp.max(s, axis=0), s_new)
        p = jnp.exp(s - m[None])
        p_new = jnp.exp(s_new - m)
        ls.append(jnp.sum(p, axis=0) + p_new)
        accs.append(jnp.sum(p * c_ref[:, 1], axis=0) + p_new * vn)
        ms.append(m)
    mm = jnp.maximum(jnp.maximum(ms[0], ms[1]), ms[2])
    es = [jnp.exp(m - mm) for m in ms]
    den = es[0] * ls[0] + es[1] * ls[1] + es[2] * ls[2]
    num = es[0] * accs[0] + es[1] * accs[1] + es[2] * accs[2]
    o_ref[...] = (num / den).astype(o_ref.dtype)


def _step_attn(zs, caches):
    n, width = zs.shape
    cviews, cspecs = [], []
    for g, c in enumerate(caches):
        d = A_DILATIONS[g]
        cviews.append(c.reshape(n, c.shape[1] // d, d, 2, A_HEADS, HEAD_DIM))
        cspecs.append(pl.BlockSpec((None, BAND, None, 2, A_HEADS, HEAD_DIM),
                                   lambda i: (i, 0, 0, 0, 0, 0)))
    zrows = width // HEAD_DIM
    out = pl.pallas_call(
        _step_attn_kernel,
        grid=(n,),
        in_specs=[pl.BlockSpec((None, zrows, HEAD_DIM), lambda i: (i, 0, 0))] + cspecs,
        out_specs=pl.BlockSpec((None, A_HEADS, HEAD_DIM), lambda i: (i, 0, 0)),
        out_shape=jax.ShapeDtypeStruct((n, A_HEADS, HEAD_DIM), BF16),
        compiler_params=_cp("parallel"),
        name="step_attn",
    )(zs.reshape(n, zrows, HEAD_DIM), *cviews)
    return out.reshape(n, D_A)


def _cache_shift_kernel(c_ref, nxt_ref, new_ref, o_ref, *, lb):
    last = pl.program_id(1) == pl.num_programs(1) - 1
    o_ref[0:lb - 1] = c_ref[1:lb]

    @pl.when(last)
    def _():
        o_ref[lb - 1] = new_ref[0]

    @pl.when(jnp.logical_not(last))
    def _():
        o_ref[lb - 1] = nxt_ref[0]


def _cache_shift(cache, new_kv):
    n, length = cache.shape[:2]
    lb = BAND
    nb = length // lb
    tail = cache.shape[2:]
    zeros = (0,) * len(tail)
    return pl.pallas_call(
        functools.partial(_cache_shift_kernel, lb=lb),
        grid=(n, nb),
        in_specs=[pl.BlockSpec((None, lb) + tail, lambda i, b: (i, b) + zeros),
                  pl.BlockSpec((None, 1) + tail, lambda i, b: (i, jnp.minimum(b + 1, nb - 1) * lb) + zeros),
                  pl.BlockSpec((None, 1) + tail, lambda i, b: (i, 0) + zeros)],
        out_specs=pl.BlockSpec((None, lb) + tail, lambda i, b: (i, b) + zeros),
        out_shape=jax.ShapeDtypeStruct(cache.shape, cache.dtype),
        compiler_params=_cp("parallel", "parallel"),
        name="cache_shift",
    )(cache, cache, new_kv)


def _kv_rows_kernel(x_ref, o_ref):
    for h in range(A_HEADS):
        o_ref[:, h, :] = x_ref[:, h * HEAD_DIM:(h + 1) * HEAD_DIM]


def _kv_rows(z, n_seq, t, g, length):
    width = z.shape[1]
    zv = z.reshape(n_seq, t, width)
    tm = BAND
    r0 = (t - length) // tm
    c0 = 3 * g + 1
    return pl.pallas_call(
        _kv_rows_kernel,
        grid=(n_seq, length // tm, 2),
        in_specs=[pl.BlockSpec((None, tm, D_A), lambda n, i, kv: (n, r0 + i, c0 + kv))],
        out_specs=pl.BlockSpec((None, tm, None, A_HEADS, HEAD_DIM), lambda n, i, kv: (n, i, kv, 0, 0)),
        out_shape=jax.ShapeDtypeStruct((n_seq, length, 2, A_HEADS, HEAD_DIM), F32),
        compiler_params=_cp("parallel", "parallel", "parallel"),
        name="kv_rows",
    )(zv)


def _ssm_discretise(lre, lim, ldt):
    dt = jnp.exp(ldt)
    mag = jnp.exp(lre * dt)
    ar = mag * jnp.cos(lim * dt)
    ai = mag * jnp.sin(lim * dt)
    den = lre * lre + lim * lim
    zr = ((ar - 1.0) * lre + ai * lim) / den
    zi = (ai * lre - (ar - 1.0) * lim) / den
    return ar, ai, zr, zi


def _slab_mask():
    r = lax.broadcasted_iota(jnp.int32, (SLAB_G * SSM_GROUP, SLAB_S), 0) // SSM_GROUP
    c = lax.broadcasted_iota(jnp.int32, (SLAB_G * SSM_GROUP, SLAB_S), 1) // SSM_STATE
    return (r == c).astype(F32)


def _block_diag(x16, mask):
    return jnp.concatenate([x16] * SLAB_G, axis=0) * mask


def _ssm_prompt_kernel(u_ref, lre_ref, lim_ref, ldt_ref, bre_ref, bim_ref, cre_ref, cim_ref, d_ref,
                       y_ref, hlr_ref, hli_ref,
                       wer_ref, wei_ref, tz_ref, bdr_ref, bdi_ref, er_ref, ei_ref, *, n_seq, n_chunk):
    hi = lax.Precision.HIGHEST
    rows = n_seq * n_chunk

    def step_rows(j):
        return pl.ds(j, rows, stride=CHUNK)

    ar, ai, zr, zi = _ssm_discretise(lre_ref[...], lim_ref[...], ldt_ref[...])
    mask = _slab_mask()
    bre, bim = bre_ref[...], bim_ref[...]
    cre, cim = cre_ref[...], cim_ref[...]
    cre_bd = _block_diag(cre, mask)
    cim_bd = _block_diag(cim, mask)
    pr, pim = jnp.ones_like(ar), jnp.zeros_like(ar)
    for m in range(CHUNK + 1):
        if m < CHUNK:
            wr = zr * pr - zi * pim
            wi = zr * pim + zi * pr
            we_r = _block_diag(bre * wr - bim * wi, mask)
            we_i = _block_diag(bre * wi + bim * wr, mask)
            wer_ref[m] = we_r.astype(BF16)
            wei_ref[m] = we_i.astype(BF16)
            tz = _dot_nt(we_r, cre_bd, precision=hi) - _dot_nt(we_i, cim_bd, precision=hi)
            tz_ref[m] = tz.astype(BF16)
        if m >= 1:
            bdr_ref[m - 1] = _block_diag(cre * pr - cim * pim, mask).astype(BF16)
            bdi_ref[m - 1] = _block_diag(-(cre * pim + cim * pr), mask).astype(BF16)
        if m == CHUNK:
            a8r, a8i = pr, pim
        pr, pim = pr * ar - pim * ai, pr * ai + pim * ar

    er = jnp.zeros(er_ref.shape, F32)
    ei = jnp.zeros(ei_ref.shape, F32)
    for j in range(CHUNK):
        xb = u_ref[step_rows(j), :].astype(BF16)
        er = er + _dot(xb, wer_ref[CHUNK - 1 - j])
        ei = ei + _dot(xb, wei_ref[CHUNK - 1 - j])
    er_ref[...] = er
    ei_ref[...] = ei

    def body(k, carry):
        out = []
        for n in range(n_seq):
            hr, him = carry[2 * n], carry[2 * n + 1]
            row = n * n_chunk + k
            e_r = er_ref[pl.ds(row, 1), :]
            e_i = ei_ref[pl.ds(row, 1), :]
            er_ref[pl.ds(row, 1), :] = hr
            ei_ref[pl.ds(row, 1), :] = him
            out.append(a8r * hr - a8i * him + e_r)
            out.append(a8r * him + a8i * hr + e_i)
        return tuple(out)

    zero = jnp.zeros((1, SLAB_S), F32)
    fin = lax.fori_loop(0, n_chunk, body, (zero,) * (2 * n_seq))
    for n in range(n_seq):
        hlr_ref[pl.ds(n, 1), :] = fin[2 * n]
        hli_ref[pl.ds(n, 1), :] = fin[2 * n + 1]

    hrb = er_ref[...].astype(BF16)
    hib = ei_ref[...].astype(BF16)
    for jp in range(CHUNK):
        y = _dot_nt(hrb, bdr_ref[jp]) + _dot_nt(hib, bdi_ref[jp])
        for j in range(jp + 1):
            y = y + _dot(u_ref[step_rows(j), :].astype(BF16), tz_ref[jp - j])
        y_ref[step_rows(jp), :] = y + d_ref[...] * u_ref[step_rows(jp), :]


def _ssm_params(a_re, a_im, b_re, b_im, c_re, c_im, d_skip, log_dt):
    g = a_re.shape[0]
    ns = g // SLAB_G

    def row(x):
        return x.astype(F32).reshape(ns, 1, SLAB_S)

    def chan_rows(x):
        return x.astype(F32).reshape(ns, SLAB_G, SSM_GROUP, SSM_STATE).transpose(0, 2, 1, 3) \
            .reshape(ns, SSM_GROUP, SLAB_S)

    ldt = jnp.broadcast_to(log_dt.astype(F32)[:, None], (g, SSM_STATE))
    return (row(a_re), row(a_im), row(ldt),
            chan_rows(b_re.transpose(0, 2, 1)), chan_rows(b_im.transpose(0, 2, 1)),
            chan_rows(c_re), chan_rows(c_im),
            d_skip.astype(F32).reshape(ns, 1, SLAB_G * SSM_GROUP))


def _ssm_prompt(z, n_seq, t, u_col0, sp):
    m, width = z.shape
    rows = m // CHUNK
    n_chunk = t // CHUNK
    lanes = SLAB_G * SSM_GROUP
    ns = sp[0].shape[0]
    ub = u_col0 // lanes

    def pspec(r, c):
        return pl.BlockSpec((None, r, c), lambda s: (s, 0, 0))

    y, hlr, hli = pl.pallas_call(
        functools.partial(_ssm_prompt_kernel, n_seq=n_seq, n_chunk=n_chunk),
        grid=(ns,),
        in_specs=[pl.BlockSpec((m, lanes), lambda s: (0, ub + s)),
                  pspec(1, SLAB_S), pspec(1, SLAB_S), pspec(1, SLAB_S),
                  pspec(SSM_GROUP, SLAB_S), pspec(SSM_GROUP, SLAB_S),
                  pspec(SSM_GROUP, SLAB_S), pspec(SSM_GROUP, SLAB_S), pspec(1, lanes)],
        out_specs=[pl.BlockSpec((m, lanes), lambda s: (0, s)),
                   pl.BlockSpec((None, n_seq, SLAB_S), lambda s: (s, 0, 0)),
                   pl.BlockSpec((None, n_seq, SLAB_S), lambda s: (s, 0, 0))],
        out_shape=[jax.ShapeDtypeStruct((m, ns * lanes), F32),
                   jax.ShapeDtypeStruct((ns, n_seq, SLAB_S), F32),
                   jax.ShapeDtypeStruct((ns, n_seq, SLAB_S), F32)],
        scratch_shapes=[pltpu.VMEM((CHUNK, lanes, SLAB_S), BF16),
                        pltpu.VMEM((CHUNK, lanes, SLAB_S), BF16),
                        pltpu.VMEM((CHUNK, lanes, lanes), BF16),
                        pltpu.VMEM((CHUNK, lanes, SLAB_S), BF16),
                        pltpu.VMEM((CHUNK, lanes, SLAB_S), BF16),
                        pltpu.VMEM((rows, SLAB_S), F32),
                        pltpu.VMEM((rows, SLAB_S), F32)],
        compiler_params=_cp("parallel"),
        name="ssm_prompt",
    )(z, *sp)
    return y, hlr, hli


def _ssm_step_kernel(u_ref, h0r_ref, h0i_ref, lre_ref, lim_ref, ldt_ref, bre_ref, bim_ref,
                     cre_ref, cim_ref, d_ref, y_ref, hr_ref, hi_ref):
    ar, ai, zr, zi = _ssm_discretise(lre_ref[...], lim_ref[...], ldt_ref[...])
    mask = _slab_mask()
    bre, bim = bre_ref[...], bim_ref[...]
    we_r = _block_diag(bre * zr - bim * zi, mask).astype(BF16)
    we_i = _block_diag(bre * zi + bim * zr, mask).astype(BF16)
    u = u_ref[...]
    ub = u.astype(BF16)
    h0r, h0i = h0r_ref[...], h0i_ref[...]
    hr = ar * h0r - ai * h0i + _dot(ub, we_r)
    him = ar * h0i + ai * h0r + _dot(ub, we_i)
    hr_ref[...] = hr
    hi_ref[...] = him
    cr_bd = _block_diag(cre_ref[...], mask).astype(BF16)
    ci_bd = _block_diag(cim_ref[...], mask).astype(BF16)
    y_ref[...] = (_dot_nt(hr.astype(BF16), cr_bd) - _dot_nt(him.astype(BF16), ci_bd)
                  + d_ref[...] * u)


def _ssm_step(zs, u_col0, h0r, h0i, sp):
    n = zs.shape[0]
    lanes = SLAB_G * SSM_GROUP
    ns = sp[0].shape[0]
    ub = u_col0 // lanes

    def pspec(r, c):
        return pl.BlockSpec((None, r, c), lambda s: (s, 0, 0))

    hspec = pl.BlockSpec((None, n, SLAB_S), lambda s: (s, 0, 0))
    return pl.pallas_call(
        _ssm_step_kernel,
        grid=(ns,),
        in_specs=[pl.BlockSpec((n, lanes), lambda s: (0, ub + s)), hspec, hspec,
                  pspec(1, SLAB_S), pspec(1, SLAB_S), pspec(1, SLAB_S),
                  pspec(SSM_GROUP, SLAB_S), pspec(SSM_GROUP, SLAB_S),
                  pspec(SSM_GROUP, SLAB_S), pspec(SSM_GROUP, SLAB_S), pspec(1, lanes)],
        out_specs=[pl.BlockSpec((n, lanes), lambda s: (0, s)), hspec, hspec],
        out_shape=[jax.ShapeDtypeStruct((n, ns * lanes), F32),
                   jax.ShapeDtypeStruct((ns, n, SLAB_S), F32),
                   jax.ShapeDtypeStruct((ns, n, SLAB_S), F32)],
        compiler_params=_cp("parallel"),
        name="ssm_step",
    )(zs, h0r, h0i, *sp)


def _gelu_tanh(x):
    return 0.5 * x * (1.0 + jnp.tanh(math.sqrt(2.0 / math.pi) * (x + 0.044715 * (x * x * x))))


def _ssm_gate_kernel(y_ref, w_ref, b_ref, o_ref):
    yg = _gelu_tanh(y_ref[...])
    gate = _dot(yg.astype(BF16), w_ref[...]) + b_ref[...]
    o_ref[...] = (yg * _sigmoid(gate)).astype(o_ref.dtype)


def _ssm_gate(y, w, b, tm):
    m, d = y.shape
    return pl.pallas_call(
        _ssm_gate_kernel,
        grid=(m // tm,),
        in_specs=[pl.BlockSpec((tm, d), lambda i: (i, 0)),
                  pl.BlockSpec((d, d), lambda i: (0, 0)),
                  pl.BlockSpec((1, d), lambda i: (0, 0))],
        out_specs=pl.BlockSpec((tm, d), lambda i: (i, 0)),
        out_shape=jax.ShapeDtypeStruct((m, d), BF16),
        compiler_params=_cp("parallel"),
        name="ssm_gate",
    )(y, w, b.reshape(1, d).astype(F32))


def _ssm_gate_step_kernel(y_ref, yc_ref, w_ref, b_ref, o_ref, wb_ref):
    yg = _gelu_tanh(y_ref[...])
    gate = _dot(yg.astype(BF16), _use_w(w_ref, wb_ref)) + b_ref[...]
    o_ref[...] = (_gelu_tanh(yc_ref[...]) * _sigmoid(gate)).astype(o_ref.dtype)


def _ssm_gate_step(y, w, layer, b, tn):
    m, d = y.shape
    return pl.pallas_call(
        _ssm_gate_step_kernel,
        grid=(d // tn,),
        in_specs=[pl.BlockSpec((m, d), lambda j: (0, 0)),
                  pl.BlockSpec((m, tn), lambda j: (0, j)),
                  _wspec(w, layer, d, tn, 0, lambda j: j),
                  pl.BlockSpec((1, tn), lambda j: (0, j))],
        out_specs=[pl.BlockSpec((m, tn), lambda j: (0, j)),
                   pl.BlockSpec((d, tn), lambda j: (0, j))],
        out_shape=[jax.ShapeDtypeStruct((m, d), BF16), jax.ShapeDtypeStruct((d, d), BF16)],
        compiler_params=_cp("arbitrary"),
        name="ssm_gate_step",
    )(y, y, w, b.reshape(1, d).astype(F32))


def _outproj_kernel(a_ref, s_ref, wa_ref, ws_ref, x_ref, o_ref, *wb):
    wab_ref, wsb_ref = wb if wb else (None, None)
    o_ref[...] = (x_ref[...] + _dot(a_ref[...], _use_w(wa_ref, wab_ref))
                  + _dot(s_ref[...], _use_w(ws_ref, wsb_ref)))


def _outproj(attn, ssm, wa, ws, layer, x, tm, tn):
    m, ka = attn.shape
    n = wa.shape[-1]
    emit = layer is not None
    out_specs = [pl.BlockSpec((tm, tn), lambda i, j: (i, j))]
    out_shape = [jax.ShapeDtypeStruct((m, n), F32)]
    if emit:
        out_specs += [pl.BlockSpec((ka, tn), lambda i, j: (0, j))] * 2
        out_shape += [jax.ShapeDtypeStruct((ka, n), BF16)] * 2
    res = pl.pallas_call(
        _outproj_kernel,
        grid=(m // tm, n // tn),
        in_specs=[pl.BlockSpec((tm, ka), lambda i, j: (i, 0)),
                  pl.BlockSpec((tm, ka), lambda i, j: (i, 0)),
                  _wspec(wa, layer, ka, tn, 0, lambda i, j: j),
                  _wspec(ws, layer, ka, tn, 1 if emit else 0, lambda i, j: j),
                  pl.BlockSpec((tm, tn), lambda i, j: (i, j))],
        out_specs=out_specs,
        out_shape=out_shape,
        compiler_params=_cp("parallel", "arbitrary"),
        name="outproj",
    )(attn, ssm, wa, ws, x)
    return res if emit else res[0]


def _mm_res_kernel(a_ref, w_ref, b_ref, x_ref, o_ref, *wb):
    wb_ref = wb[0] if wb else None
    o_ref[...] = x_ref[...] + (_dot(a_ref[...], _use_w(w_ref, wb_ref)) + b_ref[...])


def _mm_res(a, w, layer, b, x, tm, tn, name):
    m, k = a.shape
    n = w.shape[-1]
    emit = layer is not None
    out_specs = [pl.BlockSpec((tm, tn), lambda i, j: (i, j))]
    out_shape = [jax.ShapeDtypeStruct((m, n), F32)]
    if emit:
        out_specs.append(pl.BlockSpec((k, tn), lambda i, j: (0, j)))
        out_shape.append(jax.ShapeDtypeStruct((k, n), BF16))
    res = pl.pallas_call(
        _mm_res_kernel,
        grid=(m // tm, n // tn),
        in_specs=[pl.BlockSpec((tm, k), lambda i, j: (i, 0)),
                  _wspec(w, layer, k, tn, 0, lambda i, j: j),
                  pl.BlockSpec((1, tn), lambda i, j: (0, j)),
                  pl.BlockSpec((tm, tn), lambda i, j: (i, j))],
        out_specs=out_specs,
        out_shape=out_shape,
        compiler_params=_cp("parallel", "arbitrary"),
        name=name,
    )(a, w, b.reshape(1, n).astype(F32), x)
    return res if emit else res[0]


def _pw1_kernel(x_ref, wa_ref, wg_ref, ba_ref, bg_ref, o_ref, *wb):
    wab_ref, wgb_ref = wb if wb else (None, None)
    x = x_ref[...]
    a = _dot(x, _use_w(wa_ref, wab_ref)) + ba_ref[...]
    gate = _dot(x, _use_w(wg_ref, wgb_ref)) + bg_ref[...]
    o_ref[...] = a * _sigmoid(gate)


def _pw1(hn, wa, wg, layer, b, tm, tn):
    m, k = hn.shape
    n = b.shape[0] // 2
    nj = n // tn
    emit = layer is not None
    b2 = b.reshape(1, 2 * n).astype(F32)
    out_specs = [pl.BlockSpec((tm, tn), lambda i, j: (i, j))]
    out_shape = [jax.ShapeDtypeStruct((m, n), F32)]
    if emit:
        out_specs += [pl.BlockSpec((k, tn), lambda i, j: (0, j))] * 2
        out_shape += [jax.ShapeDtypeStruct((k, n), BF16)] * 2
    goff = nj if emit else 0
    res = pl.pallas_call(
        _pw1_kernel,
        grid=(m // tm, nj),
        in_specs=[pl.BlockSpec((tm, k), lambda i, j: (i, 0)),
                  _wspec(wa, layer, k, tn, 0, lambda i, j: j),
                  _wspec(wg, layer, k, tn, 0, lambda i, j: j + goff),
                  pl.BlockSpec((1, tn), lambda i, j: (0, j)),
                  pl.BlockSpec((1, tn), lambda i, j: (0, j + nj))],
        out_specs=out_specs,
        out_shape=out_shape,
        compiler_params=_cp("parallel", "arbitrary"),
        name="conv_pw1_glu",
    )(hn, wa, wg, b2, b2)
    return res if emit else res[0]


def _ln_swish(y, g, b):
    mu = jnp.mean(y, axis=-1, keepdims=True)
    yc = y - mu
    var = jnp.mean(yc * yc, axis=-1, keepdims=True)
    yn = yc * lax.rsqrt(var + LN_EPS) * g + b
    return yn * _sigmoid(yn)


def _conv31_kernel(u_ref, halo_ref, w_ref, b_ref, g_ref, bb_ref, o_ref, ext_ref, y_ref,
                   *, tm, tiles_per_seq, rw):
    seq_start = (pl.program_id(0) % tiles_per_seq) == 0
    nc = u_ref.shape[1] // HEAD_DIM
    for c in range(nc):
        cs = slice(c * HEAD_DIM, (c + 1) * HEAD_DIM)
        ext_ref[0:CONV_HALO, c, :] = jnp.where(seq_start, 0.0, halo_ref[:, cs])
        ext_ref[CONV_HALO:, c, :] = u_ref[:, cs]
    off = CONV_HALO - (CONV_WIDTH - 1)

    def body(i, carry):
        r0 = pl.multiple_of(i * rw, rw)
        acc = w_ref[0] * ext_ref[pl.ds(r0 + off, rw)]
        for j in range(1, CONV_WIDTH):
            acc = acc + w_ref[j] * ext_ref[pl.ds(r0 + off + j, rw)]
        y = acc + b_ref[...]
        mu = jnp.mean(y, axis=(1, 2), keepdims=True)
        yc = y - mu
        var = jnp.mean(yc * yc, axis=(1, 2), keepdims=True)
        yn = yc * lax.rsqrt(var + LN_EPS) * g_ref[...] + bb_ref[...]
        y_ref[pl.ds(r0, rw)] = yn * _sigmoid(yn)
        return carry

    lax.fori_loop(0, tm // rw, body, 0)
    for c in range(nc):
        o_ref[:, c * HEAD_DIM:(c + 1) * HEAD_DIM] = y_ref[:, c, :].astype(o_ref.dtype)


def _conv31(u, t, w, b, g, bb, tm):
    m, d = u.shape
    nc = d // HEAD_DIM
    slab = lambda v: v.reshape(nc, HEAD_DIM).astype(F32)
    pspec = pl.BlockSpec((nc, HEAD_DIM), lambda i: (0, 0))
    hb = tm // CONV_HALO
    return pl.pallas_call(
        functools.partial(_conv31_kernel, tm=tm, tiles_per_seq=t // tm, rw=8),
        grid=(m // tm,),
        in_specs=[pl.BlockSpec((tm, d), lambda i: (i, 0)),
                  pl.BlockSpec((CONV_HALO, d), lambda i: (jnp.maximum(i * hb - 1, 0), 0)),
                  pl.BlockSpec((CONV_WIDTH, nc, HEAD_DIM), lambda i: (0, 0, 0)),
                  pspec, pspec, pspec],
        out_specs=pl.BlockSpec((tm, d), lambda i: (i, 0)),
        out_shape=jax.ShapeDtypeStruct((m, d), BF16),
        scratch_shapes=[pltpu.VMEM((tm + CONV_HALO, nc, HEAD_DIM), F32),
                        pltpu.VMEM((tm, nc, HEAD_DIM), F32)],
        compiler_params=_cp("parallel"),
        name="conv31_ln_swish",
    )(u, u, w.astype(F32).reshape(CONV_WIDTH, nc, HEAD_DIM), slab(b), slab(g), slab(bb))


def _conv31_step_kernel(u_ref, st_ref, w_ref, b_ref, g_ref, bb_ref, o_ref):
    acc = w_ref[0:1, :] * st_ref[0]
    for j in range(1, CONV_WIDTH - 1):
        acc = acc + w_ref[j:j + 1, :] * st_ref[j]
    acc = acc + w_ref[CONV_WIDTH - 1:CONV_WIDTH, :] * u_ref[...] + b_ref[...]
    o_ref[...] = _ln_swish(acc, g_ref[...], bb_ref[...]).astype(o_ref.dtype)


def _conv31_step(u, state_t, w, b, g, bb):
    n, d = u.shape
    row = lambda v: v.reshape(1, d).astype(F32)
    return pl.pallas_call(
        _conv31_step_kernel,
        out_shape=jax.ShapeDtypeStruct((n, d), BF16),
        compiler_params=pltpu.CompilerParams(vmem_limit_bytes=VMEM_LIMIT_V7X),
        name="conv31_step",
    )(u, state_t, w.astype(F32), row(b), row(g), row(bb))


def _ffn_up_kernel(x_ref, halo_ref, wg_ref, wv_ref, cwg_ref, cwv_ref, cbg_ref, cbv_ref,
                   act_ref, zlg_ref, zlv_ref, xs_ref, *, tm, tiles_per_seq, parts):
    i = pl.program_id(0)

    @pl.when(pl.program_id(1) == 0)
    def _():
        xs_ref[0:FFN_HALO, :] = halo_ref[...]
        xs_ref[FFN_HALO:, :] = x_ref[...]

    seq_start = (i % tiles_per_seq) == 0
    rows = tm // parts
    for part in range(parts):
        r0 = part * rows
        xs = xs_ref[r0:r0 + FFN_HALO + rows, :]

        def half(w_ref, cw_ref, cb_ref, zl_ref):
            z = _dot(xs, w_ref[...])
            if part == 0:
                row = lax.broadcasted_iota(jnp.int32, (FFN_HALO + rows, 1), 0)
                z = z * jnp.where(jnp.logical_and(row < FFN_HALO, seq_start), 0.0, 1.0)
            if part == parts - 1:
                zl_ref[...] = z[FFN_HALO + rows - 8:, :]
            zc = (cw_ref[2:3, :] * z + cw_ref[1:2, :] * pltpu.roll(z, 1, 0)
                  + cw_ref[0:1, :] * pltpu.roll(z, 2, 0) + cb_ref[...])
            return zc[FFN_HALO:, :]

        gate = half(wg_ref, cwg_ref, cbg_ref, zlg_ref)
        val = half(wv_ref, cwv_ref, cbv_ref, zlv_ref)
        act_ref[r0:r0 + rows, :] = (gate * _sigmoid(gate) * val).astype(act_ref.dtype)


def _ffn_up(hn, t, wg, wv, cw, cb, tm, tn):
    m, k = hn.shape
    dff = wg.shape[1]
    nj = dff // tn
    cw = cw.astype(F32)
    cb2 = cb.reshape(1, 2 * dff).astype(F32)
    hb = tm // FFN_HALO
    zl = jax.ShapeDtypeStruct((m // tm, 8, dff), F32)
    return pl.pallas_call(
        functools.partial(_ffn_up_kernel, tm=tm, tiles_per_seq=t // tm, parts=2),
        grid=(m // tm, nj),
        in_specs=[pl.BlockSpec((tm, k), lambda i, j: (i, 0)),
                  pl.BlockSpec((FFN_HALO, k), lambda i, j: (jnp.maximum(i * hb - 1, 0), 0)),
                  pl.BlockSpec((k, tn), lambda i, j: (0, j)),
                  pl.BlockSpec((k, tn), lambda i, j: (0, j)),
                  pl.BlockSpec((3, tn), lambda i, j: (0, j)),
                  pl.BlockSpec((3, tn), lambda i, j: (0, j + nj)),
                  pl.BlockSpec((1, tn), lambda i, j: (0, j)),
                  pl.BlockSpec((1, tn), lambda i, j: (0, j + nj))],
        out_specs=[pl.BlockSpec((tm, tn), lambda i, j: (i, j)),
                   pl.BlockSpec((None, 8, tn), lambda i, j: (i, 0, j)),
                   pl.BlockSpec((None, 8, tn), lambda i, j: (i, 0, j))],
        out_shape=[jax.ShapeDtypeStruct((m, dff), BF16), zl, zl],
        scratch_shapes=[pltpu.VMEM((tm + FFN_HALO, k), BF16)],
        compiler_params=_cp("parallel", "arbitrary"),
        name="ffn_up_conv_glu",
    )(hn, hn, wg, wv, cw, cw, cb2, cb2)


def _ffn_up_step_kernel(x_ref, wg_ref, wv_ref, sg_ref, sv_ref, cwg_ref, cwv_ref, cbg_ref, cbv_ref,
                        act_ref, zg_ref, zv_ref, wgb_ref, wvb_ref):
    x = x_ref[...]

    def half(w_ref, wb_ref, s_ref, cw_ref, cb_ref, z_ref):
        z = _dot(x, _use_w(w_ref, wb_ref))
        z_ref[...] = z
        return cw_ref[2:3, :] * z + cw_ref[1:2, :] * s_ref[1] + cw_ref[0:1, :] * s_ref[0] + cb_ref[...]

    gate = half(wg_ref, wgb_ref, sg_ref, cwg_ref, cbg_ref, zg_ref)
    val = half(wv_ref, wvb_ref, sv_ref, cwv_ref, cbv_ref, zv_ref)
    act_ref[...] = (gate * _sigmoid(gate) * val).astype(act_ref.dtype)


def _ffn_up_step(hn, state_t, w, layer, cw, cb, tn):
    n, k = hn.shape
    dff = w.shape[-1] // 2
    nj = dff // tn
    cw = cw.astype(F32)
    cb2 = cb.reshape(1, 2 * dff).astype(F32)
    zs = jax.ShapeDtypeStruct((n, dff), F32)
    wb = jax.ShapeDtypeStruct((k, dff), BF16)
    return pl.pallas_call(
        _ffn_up_step_kernel,
        grid=(nj,),
        in_specs=[pl.BlockSpec((n, k), lambda j: (0, 0)),
                  _wspec(w, layer, k, tn, 0, lambda j: j),
                  _wspec(w, layer, k, tn, 0, lambda j: j + nj),
                  pl.BlockSpec((2, n, tn), lambda j: (0, 0, j)),
                  pl.BlockSpec((2, n, tn), lambda j: (0, 0, j + nj)),
                  pl.BlockSpec((3, tn), lambda j: (0, j)),
                  pl.BlockSpec((3, tn), lambda j: (0, j + nj)),
                  pl.BlockSpec((1, tn), lambda j: (0, j)),
                  pl.BlockSpec((1, tn), lambda j: (0, j + nj))],
        out_specs=[pl.BlockSpec((n, tn), lambda j: (0, j)),
                   pl.BlockSpec((n, tn), lambda j: (0, j)),
                   pl.BlockSpec((n, tn), lambda j: (0, j)),
                   pl.BlockSpec((k, tn), lambda j: (0, j)),
                   pl.BlockSpec((k, tn), lambda j: (0, j))],
        out_shape=[jax.ShapeDtypeStruct((n, dff), BF16), zs, zs, wb, wb],
        compiler_params=_cp("arbitrary"),
        name="ffn_up_step",
    )(hn, w, w, state_t, state_t, cw, cw, cb2, cb2)


def _tile_plan(m):
    return 1024 if m >= 1024 else m


def _ffn_prompt(x, t, norm_g, wg, wv, cw, cb, w_down, tm):
    hn = _rmsnorm(x, norm_g, BF16)
    act, zlg, zlv = _ffn_up(hn, t, wg, wv, cw, cb, tm, 256)
    per = t // tm
    last = jnp.concatenate([zlg[per - 1::per, 6:8], zlv[per - 1::per, 6:8]], axis=-1)
    xo = _mm_res(act, w_down, None, jnp.zeros((w_down.shape[1],), F32), x, 512, 512, "ffn_down")
    return xo, last


def _ffn_step(x, prev_state_t, norm_g, w_up, w_down, layer, cw, cb):
    m = x.shape[0]
    hn = _rmsnorm(x, norm_g, BF16)
    act, zg, zv, wg_b, wv_b = _ffn_up_step(hn, prev_state_t, w_up, layer, cw, cb, 256)
    znew = jnp.concatenate([zg, zv], axis=-1)
    last = jnp.concatenate([prev_state_t[1][:, None], znew[:, None]], axis=1)
    xo, wd_b = _mm_res(act, w_down, layer, jnp.zeros((w_down.shape[-1],), F32), x, m, 256, "ffn_down_step")
    return xo, last, (wg_b, wv_b, wd_b)


def kernel(x_prompt, x_sample, cache_a0, cache_a1, cache_a2, state_ssm, state_conv, state_ffn, norm_mix, norm_ffn, norm_final, w_in_even, w_out_even, ssm_a_re, ssm_a_im, ssm_b_re, ssm_b_im, ssm_c_re, ssm_c_im, ssm_d, ssm_log_dt, w_glu, b_glu, w_pw1, b_pw1, w_dw, b_dw, ln_g, ln_b, w_pw2, b_pw2, w_up, ffn_dw, ffn_dw_b, w_down):
    n_p, t_p, d_model = x_prompt.shape
    n_s = x_sample.shape[0]
    caches = (cache_a0, cache_a1, cache_a2)
    u_col0 = N_DIL * QKV_W
    sp = _ssm_params(ssm_a_re[0], ssm_a_im[0], ssm_b_re[0], ssm_b_im[0], ssm_c_re[0], ssm_c_im[0],
                     ssm_d[0], ssm_log_dt[0])
    ns = sp[0].shape[0]

    def ssm_state(hr, him, n):
        def unslab(h):
            return h.reshape(ns, n, SLAB_G, SSM_STATE).transpose(1, 0, 2, 3).reshape(n, ns * SLAB_G, SSM_STATE)
        return jnp.stack([unslab(hr), unslab(him)], axis=-1)[None]

    xs = x_sample.reshape(n_s, d_model)
    cos_s, sin_s = _rope_tables(n_s, PAST_LEN, 0)
    hn = _rmsnorm(xs, norm_mix[0], BF16)
    zs, w_in_b = _inproj(hn, w_in_even, 0, cos_s, sin_s, n_s, 512)
    attn_s = _step_attn(zs, [c[0] for c in caches])

    def slab(h):
        return h.reshape(n_s, ns, SLAB_S).transpose(1, 0, 2)

    y_s, hr_s, hi_s = _ssm_step(zs, u_col0, slab(state_ssm[0, ..., 0]), slab(state_ssm[0, ..., 1]), sp)
    gate_s, w_glu_b = _ssm_gate_step(y_s, w_glu, 0, b_glu[0], 512)
    xs, w_oa_b, w_os_b = _outproj(attn_s, gate_s, w_out_even, w_out_even, 0, xs, n_s, 512)
    xs, ffn0_s, wf0 = _ffn_step(xs, state_ffn[0].transpose(1, 0, 2), norm_ffn[0], w_up, w_down, 0,
                                ffn_dw[0], ffn_dw_b[0])
    hn = _rmsnorm(xs, norm_mix[1], BF16)
    us, w_pa_b, w_pg_b = _pw1(hn, w_pw1, w_pw1, 0, b_pw1[0], n_s, 256)
    cs = _conv31_step(us, state_conv[0].transpose(1, 0, 2), w_dw[0], b_dw[0], ln_g[0], ln_b[0])
    conv_s = jnp.concatenate([state_conv[0][:, 1:], us[:, None]], axis=1)
    xs, w_pw2_b = _mm_res(cs, w_pw2, 0, b_pw2[0], xs, n_s, 512, "conv_pw2_step")
    xs, ffn1_s, wf1 = _ffn_step(xs, state_ffn[1].transpose(1, 0, 2), norm_ffn[1], w_up, w_down, 1,
                                ffn_dw[1], ffn_dw_b[1])
    y_sample = _rmsnorm(xs, norm_final, F32).reshape(n_s, 1, d_model)

    a_s = []
    for g in range(N_DIL):
        kv = zs[:, g * QKV_W + D_A:(g + 1) * QKV_W].reshape(n_s, 1, 2, A_HEADS, HEAD_DIM)
        a_s.append(_cache_shift(caches[g][0], kv)[None])
    ssm_s = ssm_state(hr_s, hi_s, n_s)
    conv_s = conv_s[None]
    ffn_s = jnp.stack([ffn0_s, ffn1_s])

    mp = n_p * t_p
    tm = _tile_plan(mp)
    x = x_prompt.reshape(mp, d_model)
    cos, sin = _rope_tables(t_p, 0, 1)
    hn = _rmsnorm(x, norm_mix[0], BF16)
    z = _inproj(hn, w_in_b, None, cos, sin, tm, 1024)
    outs, lses = [], []
    for g in range(N_DIL):
        o, lse = _band_attn(z, n_p, t_p, g)
        outs.append(o)
        lses.append(lse)
    attn = _combine(outs, lses)
    y_ssm, hlr, hli = _ssm_prompt(z, n_p, t_p, u_col0, sp)
    gate_p = _ssm_gate(y_ssm, w_glu_b, b_glu[0], 512)
    x = _outproj(attn, gate_p, w_oa_b, w_os_b, None, x, tm, 1024)
    x, ffn0_p = _ffn_prompt(x, t_p, norm_ffn[0], wf0[0], wf0[1], ffn_dw[0], ffn_dw_b[0], wf0[2], tm)
    hn = _rmsnorm(x, norm_mix[1], BF16)
    u = _pw1(hn, w_pa_b, w_pg_b, None, b_pw1[0], tm, 512)
    c = _conv31(u, t_p, w_dw[0], b_dw[0], ln_g[0], ln_b[0], 256)
    conv_p = u.reshape(n_p, t_p, -1)[None, :, t_p - (CONV_WIDTH - 1):]
    x = _mm_res(c, w_pw2_b, None, b_pw2[0], x, tm, 1024, "conv_pw2")
    x, ffn1_p = _ffn_prompt(x, t_p, norm_ffn[1], wf1[0], wf1[1], ffn_dw[1], ffn_dw_b[1], wf1[2], tm)
    y_prompt = _rmsnorm(x, norm_final, F32).reshape(n_p, t_p, d_model)

    a_p = []
    for g in range(N_DIL):
        a_p.append(_kv_rows(z, n_p, t_p, g, min(A_WINDOWS[g], PAST_LEN))[None])
    ssm_p = ssm_state(hlr, hli, n_p)
    ffn_p = jnp.stack([ffn0_p, ffn1_p])

    return (y_prompt, y_sample, a_p[0], a_s[0], a_p[1], a_s[1], a_p[2], a_s[2],
            ssm_p, ssm_s, conv_p, conv_s, ffn_p, ffn_s)
```

```python
import functools
import math

import jax
import jax.numpy as jnp
from jax import lax
from jax.experimental import pallas as pl
from jax.experimental.pallas import tpu as pltpu

F32 = jnp.float32
BF16 = jnp.bfloat16

HEAD_DIM = 128
A_HEADS = 16
D_A = A_HEADS * HEAD_DIM
A_WINDOWS = (128, 512, 2048)
A_DILATIONS = (1, 4, 16)
BAND = 128
N_DIL = 3
QKV_W = 3 * D_A
ROPE_THETA = 10000.0
PAST_LEN = 16384
SSM_GROUP = 16
SSM_STATE = 64
SLAB_G = 8
SLAB_S = SLAB_G * SSM_STATE
CHUNK = 8
CONV_WIDTH = 31
CONV_HALO = 32
FFN_HALO = 16
RMS_EPS = 1e-6
LN_EPS = 1e-5
NEG_BIG = -1e30
VMEM_LIMIT_V7X = 56 * 1024 * 1024


def _cp(*sem):
    return pltpu.CompilerParams(dimension_semantics=sem, vmem_limit_bytes=VMEM_LIMIT_V7X)


def _dot(a, b, **kw):
    return jnp.dot(a, b, preferred_element_type=F32, **kw)


def _dot_nt(a, b, **kw):
    return lax.dot_general(a, b, (((1,), (1,)), ((), ())), preferred_element_type=F32, **kw)


def _sigmoid(x):
    return 1.0 / (1.0 + jnp.exp(-x))


def _rmsnorm_kernel(x_ref, g_ref, o_ref):
    x = x_ref[...]
    ms = jnp.mean(x * x, axis=-1, keepdims=True)
    o_ref[...] = (x * lax.rsqrt(ms + RMS_EPS) * g_ref[...]).astype(o_ref.dtype)


def _rmsnorm(x, g, out_dtype):
    m, d = x.shape
    tm = min(m, 256)
    return pl.pallas_call(
        _rmsnorm_kernel,
        grid=(m // tm,),
        in_specs=[pl.BlockSpec((tm, d), lambda i: (i, 0)),
                  pl.BlockSpec((1, d), lambda i: (0, 0))],
        out_specs=pl.BlockSpec((tm, d), lambda i: (i, 0)),
        out_shape=jax.ShapeDtypeStruct((m, d), out_dtype),
        compiler_params=_cp("parallel"),
        name="rmsnorm",
    )(x, g.reshape(1, d))


def _rope_table_kernel(invf_ref, cos_ref, sin_ref, *, pos0, step):
    rows = cos_ref.shape[0]
    row = lax.broadcasted_iota(jnp.int32, (rows, HEAD_DIM), 0)
    lane = lax.broadcasted_iota(jnp.int32, (rows, HEAD_DIM), 1)
    pos = (row * step + pos0).astype(F32)
    ang = pos * invf_ref[...]
    cos_ref[...] = jnp.cos(ang)
    s = jnp.sin(ang)
    sin_ref[...] = jnp.where(lane < HEAD_DIM // 2, -s, s)


def _rope_tables(rows, pos0, step):
    half = HEAD_DIM // 2
    inv_freq = ROPE_THETA ** (-jnp.arange(half, dtype=F32) / half)
    invf = jnp.concatenate([inv_freq, inv_freq]).reshape(1, HEAD_DIM)
    shp = jax.ShapeDtypeStruct((rows, HEAD_DIM), F32)
    return pl.pallas_call(
        functools.partial(_rope_table_kernel, pos0=pos0, step=step),
        out_shape=(shp, shp),
        name="rope_tables",
    )(invf)


def _wspec(w, layer, kb, tn, kblk, jmap):
    if layer is None:
        return pl.BlockSpec((kb, tn), lambda *g: (kblk, jmap(*g)))
    return pl.BlockSpec((None, kb, tn), lambda *g: (layer, kblk, jmap(*g)))


def _use_w(w_ref, wb_ref, cols=slice(None)):
    w = w_ref[:, cols]
    if wb_ref is None:
        return w
    wb = w.astype(BF16)
    wb_ref[:, cols] = wb
    return wb


def _inproj_kernel(x_ref, w_ref, cos_ref, sin_ref, o_ref, *wb, tn, part):
    wb_ref = wb[0] if wb else None
    seg = (pl.program_id(1) * tn) // D_A
    is_rope = jnp.logical_and(seg < 3 * N_DIL, seg % 3 != 2)
    cos = cos_ref[...]
    sin = sin_ref[...]
    x = x_ref[...]
    for c0 in range(0, tn, part):
        acc = _dot(x, _use_w(w_ref, wb_ref, slice(c0, c0 + part)))
        for h in range(part // HEAD_DIM):
            xh = acc[:, h * HEAD_DIM:(h + 1) * HEAD_DIM]
            roped = xh * cos + pltpu.roll(xh, HEAD_DIM // 2, 1) * sin
            o_ref[:, c0 + h * HEAD_DIM:c0 + (h + 1) * HEAD_DIM] = jnp.where(is_rope, roped, xh)


def _inproj(hn, w, layer, cos, sin, tm, tn):
    m, k = hn.shape
    n = w.shape[-1]
    ct = cos.shape[0] // tm
    emit = layer is not None
    out_specs = [pl.BlockSpec((tm, tn), lambda i, j: (i, j))]
    out_shape = [jax.ShapeDtypeStruct((m, n), F32)]
    if emit:
        out_specs.append(pl.BlockSpec((k, tn), lambda i, j: (0, j)))
        out_shape.append(jax.ShapeDtypeStruct((k, n), BF16))
    res = pl.pallas_call(
        functools.partial(_inproj_kernel, tn=tn, part=min(tn, 256)),
        grid=(m // tm, n // tn),
        in_specs=[pl.BlockSpec((tm, k), lambda i, j: (i, 0)),
                  _wspec(w, layer, k, tn, 0, lambda i, j: j),
                  pl.BlockSpec((tm, HEAD_DIM), lambda i, j: (i % ct, 0)),
                  pl.BlockSpec((tm, HEAD_DIM), lambda i, j: (i % ct, 0))],
        out_specs=out_specs,
        out_shape=out_shape,
        compiler_params=_cp("parallel", "arbitrary"),
        name="inproj_rope",
    )(hn, w, cos, sin)
    return res if emit else res[0]


def _band_attn_kernel(q_ref, kp_ref, kc_ref, vp_ref, vc_ref, o_ref, lse_ref):
    has_prev = pl.program_id(2) > 0
    qi = lax.broadcasted_iota(jnp.int32, (BAND, 2 * BAND), 0)
    ki = lax.broadcasted_iota(jnp.int32, (BAND, 2 * BAND), 1)
    valid = jnp.logical_and(jnp.logical_and(ki >= qi, ki <= qi + BAND),
                            jnp.logical_or(ki >= BAND, has_prev))
    lane = lax.broadcasted_iota(jnp.int32, (BAND, HEAD_DIM), 1)
    lse_tile = jnp.zeros((BAND, HEAD_DIM), F32)
    scale = HEAD_DIM ** -0.5
    for h in range(A_HEADS):
        sl = slice(h * HEAD_DIM, (h + 1) * HEAD_DIM)
        q = q_ref[:, sl].astype(BF16)
        k = jnp.concatenate([kp_ref[:, sl], kc_ref[:, sl]], axis=0).astype(BF16)
        v = jnp.concatenate([vp_ref[:, sl], vc_ref[:, sl]], axis=0).astype(BF16)
        s = _dot_nt(q, k) * scale
        s = jnp.where(valid, s, NEG_BIG)
        m = jnp.max(s, axis=-1, keepdims=True)
        p = jnp.exp(s - m)
        l = jnp.sum(p, axis=-1, keepdims=True)
        o_ref[:, sl] = _dot(p.astype(BF16), v) / l
        lse_tile = jnp.where(lane == h, m + jnp.log(l), lse_tile)
    lse_ref[...] = lse_tile


def _band_attn_dil_kernel(*refs, d, hpb, use_prev):
    if use_prev:
        q_ref, kp_ref, kc_ref, vp_ref, vc_ref, o_ref, lse_ref, qs, ks, vs, os_, ls = refs
    else:
        q_ref, kc_ref, vc_ref, o_ref, lse_ref, qs, ks, vs, os_, ls = refs
    rows = BAND * d
    hb = pl.program_id(2)
    has_prev = pl.program_id(1) > 0
    for h in range(hpb):
        sl = slice(h * HEAD_DIM, (h + 1) * HEAD_DIM)
        qs[h] = q_ref[:, sl]
        if use_prev:
            ks[h, 0:rows] = kp_ref[:, sl]
            ks[h, rows:2 * rows] = kc_ref[:, sl]
            vs[h, 0:rows] = vp_ref[:, sl]
            vs[h, rows:2 * rows] = vc_ref[:, sl]
        else:
            ks[h] = kc_ref[:, sl]
            vs[h] = vc_ref[:, sl]

    @pl.when(hb == 0)
    def _():
        lse_ref[...] = jnp.zeros(lse_ref.shape, F32)

    nk = 2 * BAND if use_prev else BAND
    qi = lax.broadcasted_iota(jnp.int32, (1, BAND, nk), 1)
    ki = lax.broadcasted_iota(jnp.int32, (1, BAND, nk), 2)
    if use_prev:
        valid = jnp.logical_and(jnp.logical_and(ki >= qi, ki <= qi + BAND),
                                jnp.logical_or(ki >= BAND, has_prev))
    else:
        valid = ki <= qi
    scale = HEAD_DIM ** -0.5

    def residues(ref, h, base):
        return [ref[h, pl.ds(base + r, BAND, stride=d), :] for r in range(d)]

    for h in range(hpb):
        q = jnp.stack(residues(qs, h, 0)).astype(BF16)
        if use_prev:
            k = jnp.stack([jnp.concatenate([a, b], axis=0)
                           for a, b in zip(residues(ks, h, 0), residues(ks, h, rows))]).astype(BF16)
            v = jnp.stack([jnp.concatenate([a, b], axis=0)
                           for a, b in zip(residues(vs, h, 0), residues(vs, h, rows))]).astype(BF16)
        else:
            k = jnp.stack(residues(ks, h, 0)).astype(BF16)
            v = jnp.stack(residues(vs, h, 0)).astype(BF16)
        s = jnp.einsum('rqd,rkd->rqk', q, k, preferred_element_type=F32) * scale
        s = jnp.where(valid, s, NEG_BIG)
        m = jnp.max(s, axis=-1, keepdims=True)
        p = jnp.exp(s - m)
        l = jnp.sum(p, axis=-1, keepdims=True)
        o = jnp.einsum('rqk,rkd->rqd', p.astype(BF16), v, preferred_element_type=F32) / l
        lse_h = m + jnp.log(l)
        for r in range(d):
            os_[h, pl.ds(r, BAND, stride=d), :] = o[r]
            ls[h, pl.ds(r, BAND, stride=d), :] = jnp.broadcast_to(lse_h[r], (BAND, HEAD_DIM))
    lse = lse_ref[...]
    lane = lax.broadcasted_iota(jnp.int32, lse.shape, 1)
    for h in range(hpb):
        o_ref[:, h * HEAD_DIM:(h + 1) * HEAD_DIM] = os_[h]
        lse = jnp.where(lane == hb * hpb + h, ls[h], lse)
    lse_ref[...] = lse


def _band_attn(z, n_seq, t, g):
    d = A_DILATIONS[g]
    width = z.shape[1]
    zv = z.reshape(n_seq, t, width)
    if d == 1:
        qc = 3 * g

        def cur(off):
            return pl.BlockSpec((None, BAND, D_A), lambda n, r, b: (n, b, qc + off))

        def prev(off):
            return pl.BlockSpec((None, BAND, D_A), lambda n, r, b: (n, jnp.maximum(b - 1, 0), qc + off))

        o, lse = pl.pallas_call(
            _band_attn_kernel,
            grid=(n_seq, 1, t // BAND),
            in_specs=[cur(0), prev(1), cur(1), prev(2), cur(2)],
            out_specs=[pl.BlockSpec((None, BAND, D_A), lambda n, r, b: (n, b, 0)),
                       pl.BlockSpec((None, BAND, HEAD_DIM), lambda n, r, b: (n, b, 0))],
            out_shape=[jax.ShapeDtypeStruct((n_seq, t, D_A), F32),
                       jax.ShapeDtypeStruct((n_seq, t, HEAD_DIM), F32)],
            compiler_params=_cp("parallel", "parallel", "arbitrary"),
            name="band_attn_d1",
        )(zv, zv, zv, zv, zv)
        return o.reshape(n_seq * t, D_A), lse.reshape(n_seq * t, HEAD_DIM)

    rows = BAND * d
    nb = t // rows
    use_prev = nb > 1
    hpb = 8 if d <= 4 else 2
    cols = hpb * HEAD_DIM
    cpq = D_A // cols
    qc = 3 * g * cpq

    def cur(off):
        return pl.BlockSpec((None, rows, cols), lambda n, b, hb: (n, b, qc + off * cpq + hb))

    def prev(off):
        return pl.BlockSpec((None, rows, cols),
                            lambda n, b, hb: (n, jnp.maximum(b - 1, 0), qc + off * cpq + hb))

    in_specs = [cur(0), prev(1), cur(1), prev(2), cur(2)] if use_prev else [cur(0), cur(1), cur(2)]
    kv_rows = 2 * rows if use_prev else rows
    o, lse = pl.pallas_call(
        functools.partial(_band_attn_dil_kernel, d=d, hpb=hpb, use_prev=use_prev),
        grid=(n_seq, nb, cpq),
        in_specs=in_specs,
        out_specs=[pl.BlockSpec((None, rows, cols), lambda n, b, hb: (n, b, hb)),
                   pl.BlockSpec((None, rows, HEAD_DIM), lambda n, b, hb: (n, b, 0))],
        out_shape=[jax.ShapeDtypeStruct((n_seq, t, D_A), F32),
                   jax.ShapeDtypeStruct((n_seq, t, HEAD_DIM), F32)],
        scratch_shapes=[pltpu.VMEM((hpb, rows, HEAD_DIM), F32),
                        pltpu.VMEM((hpb, kv_rows, HEAD_DIM), F32),
                        pltpu.VMEM((hpb, kv_rows, HEAD_DIM), F32),
                        pltpu.VMEM((hpb, rows, HEAD_DIM), F32),
                        pltpu.VMEM((hpb, rows, HEAD_DIM), F32)],
        compiler_params=_cp("parallel", "parallel", "arbitrary"),
        name=f"band_attn_d{d}",
    )(*([zv] * len(in_specs)))
    return o.reshape(n_seq * t, D_A), lse.reshape(n_seq * t, HEAD_DIM)


def _combine_kernel(o0_ref, o1_ref, o2_ref, l0_ref, l1_ref, l2_ref, out_ref):
    a0, a1, a2 = l0_ref[...], l1_ref[...], l2_ref[...]
    m = jnp.maximum(jnp.maximum(a0, a1), a2)
    e0, e1, e2 = jnp.exp(a0 - m), jnp.exp(a1 - m), jnp.exp(a2 - m)
    inv = 1.0 / (e0 + e1 + e2)
    w0, w1, w2 = e0 * inv, e1 * inv, e2 * inv
    for h in range(A_HEADS):
        sl = slice(h * HEAD_DIM, (h + 1) * HEAD_DIM)
        out_ref[:, sl] = (w0[:, h:h + 1] * o0_ref[:, sl] + w1[:, h:h + 1] * o1_ref[:, sl]
                          + w2[:, h:h + 1] * o2_ref[:, sl]).astype(out_ref.dtype)


def _combine(os_, lses):
    m = os_[0].shape[0]
    tm = 256
    ob = pl.BlockSpec((tm, D_A), lambda i: (i, 0))
    lb = pl.BlockSpec((tm, HEAD_DIM), lambda i: (i, 0))
    return pl.pallas_call(
        _combine_kernel,
        grid=(m // tm,),
        in_specs=[ob, ob, ob, lb, lb, lb],
        out_specs=ob,
        out_shape=jax.ShapeDtypeStruct((m, D_A), BF16),
        compiler_params=_cp("parallel"),
        name="attn_combine",
    )(*os_, *lses)


def _step_attn_kernel(z_ref, c0_ref, c1_ref, c2_ref, o_ref):
    scale = HEAD_DIM ** -0.5
    ms, ls, accs = [], [], []
    for g, c_ref in enumerate((c0_ref, c1_ref, c2_ref)):
        base = 3 * g * A_HEADS
        q = z_ref[base:base + A_HEADS, :]
        kn = z_ref[base + A_HEADS:base + 2 * A_HEADS, :]
        vn = z_ref[base + 2 * A_HEADS:base + 3 * A_HEADS, :]
        s = jnp.sum(c_ref[:, 0] * q[None], axis=-1, keepdims=True) * scale
        s_new = jnp.sum(kn * q, axis=-1, keepdims=True) * scale
        m = jnp.maximum(jnp.max(s, axis=0), s_new)
        p = jnp.exp(s - m[None])
        p_new = jnp.exp(s_new - m)
        ls.append(jnp.sum(p, axis=0) + p_new)
        accs.append(jnp.sum(p * c_ref[:, 1], axis=0) + p_new * vn)
        ms.append(m)
    mm = jnp.maximum(jnp.maximum(ms[0], ms[1]), ms[2])
    es = [jnp.exp(m - mm) for m in ms]
    den = es[0] * ls[0] + es[1] * ls[1] + es[2] * ls[2]
    num = es[0] * accs[0] + es[1] * accs[1] + es[2] * accs[2]
    o_ref[...] = (num / den).astype(o_ref.dtype)


def _step_attn(zs, caches):
    n, width = zs.shape
    cviews, cspecs = [], []
    for g, c in enumerate(caches):
        d = A_DILATIONS[g]
        cviews.append(c.reshape(n, c.shape[1] // d, d, 2, A_HEADS, HEAD_DIM))
        cspecs.append(pl.BlockSpec((None, BAND, None, 2, A_HEADS, HEAD_DIM),
                                   lambda i: (i, 0, 0, 0, 0, 0)))
    zrows = width // HEAD_DIM
    out = pl.pallas_call(
        _step_attn_kernel,
        grid=(n,),
        in_specs=[pl.BlockSpec((None, zrows, HEAD_DIM), lambda i: (i, 0, 0))] + cspecs,
        out_specs=pl.BlockSpec((None, A_HEADS, HEAD_DIM), lambda i: (i, 0, 0)),
        out_shape=jax.ShapeDtypeStruct((n, A_HEADS, HEAD_DIM), BF16),
        compiler_params=_cp("parallel"),
        name="step_attn",
    )(zs.reshape(n, zrows, HEAD_DIM), *cviews)
    return out.reshape(n, D_A)


def _cache_shift_kernel(c_ref, nxt_ref, new_ref, o_ref, *, lb):
    last = pl.program_id(1) == pl.num_programs(1) - 1
    o_ref[0:lb - 1] = c_ref[1:lb]

    @pl.when(last)
    def _():
        o_ref[lb - 1] = new_ref[0]

    @pl.when(jnp.logical_not(last))
    def _():
        o_ref[lb - 1] = nxt_ref[0]


def _cache_shift(cache, new_kv):
    n, length = cache.shape[:2]
    lb = BAND
    nb = length // lb
    tail = cache.shape[2:]
    zeros = (0,) * len(tail)
    return pl.pallas_call(
        functools.partial(_cache_shift_kernel, lb=lb),
        grid=(n, nb),
        in_specs=[pl.BlockSpec((None, lb) + tail, lambda i, b: (i, b) + zeros),
                  pl.BlockSpec((None, 1) + tail, lambda i, b: (i, jnp.minimum(b + 1, nb - 1) * lb) + zeros),
                  pl.BlockSpec((None, 1) + tail, lambda i, b: (i, 0) + zeros)],
        out_specs=pl.BlockSpec((None, lb) + tail, lambda i, b: (i, b) + zeros),
        out_shape=jax.ShapeDtypeStruct(cache.shape, cache.dtype),
        compiler_params=_cp("parallel", "parallel"),
        name="cache_shift",
    )(cache, cache, new_kv)


def _kv_rows_kernel(k_ref, v_ref, o_ref, *, tm):
    per = 2 * A_HEADS
    for kv, x_ref in enumerate((k_ref, v_ref)):
        for h in range(A_HEADS):
            o_ref[pl.ds(kv * A_HEADS + h, tm, stride=per), :] = x_ref[:, h * HEAD_DIM:(h + 1) * HEAD_DIM]


def _kv_rows(z, n_seq, t, g, length):
    width = z.shape[1]
    zv = z.reshape(n_seq, t, width)
    tm = BAND
    r0 = (t - length) // tm
    c0 = 3 * g + 1
    per = 2 * A_HEADS
    out = pl.pallas_call(
        functools.partial(_kv_rows_kernel, tm=tm),
        grid=(n_seq, length // tm),
        in_specs=[pl.BlockSpec((None, tm, D_A), lambda n, i: (n, r0 + i, c0)),
                  pl.BlockSpec((None, tm, D_A), lambda n, i: (n, r0 + i, c0 + 1))],
        out_specs=pl.BlockSpec((None, tm * per, HEAD_DIM), lambda n, i: (n, i, 0)),
        out_shape=jax.ShapeDtypeStruct((n_seq, length * per, HEAD_DIM), F32),
        compiler_params=_cp("parallel", "parallel"),
        name="kv_rows",
    )(zv, zv)
    return out.reshape(n_seq, length, 2, A_HEADS, HEAD_DIM)


def _ssm_discretise(lre, lim, ldt):
    dt = jnp.exp(ldt)
    mag = jnp.exp(lre * dt)
    ar = mag * jnp.cos(lim * dt)
    ai = mag * jnp.sin(lim * dt)
    den = lre * lre + lim * lim
    zr = ((ar - 1.0) * lre + ai * lim) / den
    zi = (ai * lre - (ar - 1.0) * lim) / den
    return ar, ai, zr, zi


def _slab_mask():
    r = lax.broadcasted_iota(jnp.int32, (SLAB_G * SSM_GROUP, SLAB_S), 0) // SSM_GROUP
    c = lax.broadcasted_iota(jnp.int32, (SLAB_G * SSM_GROUP, SLAB_S), 1) // SSM_STATE
    return (r == c).astype(F32)


def _block_diag(x16, mask):
    return jnp.concatenate([x16] * SLAB_G, axis=0) * mask


def _ssm_prompt_kernel(u_ref, lre_ref, lim_ref, ldt_ref, bre_ref, bim_ref, cre_ref, cim_ref, d_ref,
                       y_ref, hlr_ref, hli_ref,
                       wer_ref, wei_ref, tz_ref, bdr_ref, bdi_ref, er_ref, ei_ref, *, n_seq, n_chunk):
    hi = lax.Precision.HIGHEST
    rows = n_seq * n_chunk

    def step_rows(j):
        return pl.ds(j, rows, stride=CHUNK)

    ar, ai, zr, zi = _ssm_discretise(lre_ref[...], lim_ref[...], ldt_ref[...])
    mask = _slab_mask()
    bre, bim = bre_ref[...], bim_ref[...]
    cre, cim = cre_ref[...], cim_ref[...]
    cre_bd = _block_diag(cre, mask)
    cim_bd = _block_diag(cim, mask)
    pr, pim = jnp.ones_like(ar), jnp.zeros_like(ar)
    for m in range(CHUNK + 1):
        if m < CHUNK:
            wr = zr * pr - zi * pim
            wi = zr * pim + zi * pr
            we_r = _block_diag(bre * wr - bim * wi, mask)
            we_i = _block_diag(bre * wi + bim * wr, mask)
            wer_ref[m] = we_r.astype(BF16)
            wei_ref[m] = we_i.astype(BF16)
            tz = _dot_nt(we_r, cre_bd, precision=hi) - _dot_nt(we_i, cim_bd, precision=hi)
            tz_ref[m] = tz.astype(BF16)
        if m >= 1:
            bdr_ref[m - 1] = _block_diag(cre * pr - cim * pim, mask).astype(BF16)
            bdi_ref[m - 1] = _block_diag(-(cre * pim + cim * pr), mask).astype(BF16)
        if m == CHUNK:
            a8r, a8i = pr, pim
        pr, pim = pr * ar - pim * ai, pr * ai + pim * ar

    er = jnp.zeros(er_ref.shape, F32)
    ei = jnp.zeros(ei_ref.shape, F32)
    for j in range(CHUNK):
        xb = u_ref[step_rows(j), :].astype(BF16)
        er = er + _dot(xb, wer_ref[CHUNK - 1 - j])
        ei = ei + _dot(xb, wei_ref[CHUNK - 1 - j])
    er_ref[...] = er
    ei_ref[...] = ei

    def body(k, carry):
        out = []
        for n in range(n_seq):
            hr, him = carry[2 * n], carry[2 * n + 1]
            row = n * n_chunk + k
            e_r = er_ref[pl.ds(row, 1), :]
            e_i = ei_ref[pl.ds(row, 1), :]
            er_ref[pl.ds(row, 1), :] = hr
            ei_ref[pl.ds(row, 1), :] = him
            out.append(a8r * hr - a8i * him + e_r)
            out.append(a8r * him + a8i * hr + e_i)
        return tuple(out)

    zero = jnp.zeros((1, SLAB_S), F32)
    fin = lax.fori_loop(0, n_chunk, body, (zero,) * (2 * n_seq))
    for n in range(n_seq):
        hlr_ref[pl.ds(n, 1), :] = fin[2 * n]
        hli_ref[pl.ds(n, 1), :] = fin[2 * n + 1]

    hrb = er_ref[...].astype(BF16)
    hib = ei_ref[...].astype(BF16)
    for jp in range(CHUNK):
        y = _dot_nt(hrb, bdr_ref[jp]) + _dot_nt(hib, bdi_ref[jp])
        for j in range(jp + 1):
            y = y + _dot(u_ref[step_rows(j), :].astype(BF16), tz_ref[jp - j])
        y_ref[step_rows(jp), :] = y + d_ref[...] * u_ref[step_rows(jp), :]


def _ssm_params(a_re, a_im, b_re, b_im, c_re, c_im, d_skip, log_dt):
    g = a_re.shape[0]
    ns = g // SLAB_G

    def row(x):
        return x.astype(F32).reshape(ns, 1, SLAB_S)

    def chan_rows(x):
        return x.astype(F32).reshape(ns, SLAB_G, SSM_GROUP, SSM_STATE).transpose(0, 2, 1, 3) \
            .reshape(ns, SSM_GROUP, SLAB_S)

    ldt = jnp.broadcast_to(log_dt.astype(F32)[:, None], (g, SSM_STATE))
    return (row(a_re), row(a_im), row(ldt),
            chan_rows(b_re.transpose(0, 2, 1)), chan_rows(b_im.transpose(0, 2, 1)),
            chan_rows(c_re), chan_rows(c_im),
            d_skip.astype(F32).reshape(ns, 1, SLAB_G * SSM_GROUP))


def _ssm_prompt(z, n_seq, t, u_col0, sp):
    m, width = z.shape
    rows = m // CHUNK
    n_chunk = t // CHUNK
    lanes = SLAB_G * SSM_GROUP
    ns = sp[0].shape[0]
    ub = u_col0 // lanes

    def pspec(r, c):
        return pl.BlockSpec((None, r, c), lambda s: (s, 0, 0))

    y, hlr, hli = pl.pallas_call(
        functools.partial(_ssm_prompt_kernel, n_seq=n_seq, n_chunk=n_chunk),
        grid=(ns,),
        in_specs=[pl.BlockSpec((m, lanes), lambda s: (0, ub + s)),
                  pspec(1, SLAB_S), pspec(1, SLAB_S), pspec(1, SLAB_S),
                  pspec(SSM_GROUP, SLAB_S), pspec(SSM_GROUP, SLAB_S),
                  pspec(SSM_GROUP, SLAB_S), pspec(SSM_GROUP, SLAB_S), pspec(1, lanes)],
        out_specs=[pl.BlockSpec((m, lanes), lambda s: (0, s)),
                   pl.BlockSpec((None, n_seq, SLAB_S), lambda s: (s, 0, 0)),
                   pl.BlockSpec((None, n_seq, SLAB_S), lambda s: (s, 0, 0))],
        out_shape=[jax.ShapeDtypeStruct((m, ns * lanes), F32),
                   jax.ShapeDtypeStruct((ns, n_seq, SLAB_S), F32),
                   jax.ShapeDtypeStruct((ns, n_seq, SLAB_S), F32)],
        scratch_shapes=[pltpu.VMEM((CHUNK, lanes, SLAB_S), BF16),
                        pltpu.VMEM((CHUNK, lanes, SLAB_S), BF16),
                        pltpu.VMEM((CHUNK, lanes, lanes), BF16),
                        pltpu.VMEM((CHUNK, lanes, SLAB_S), BF16),
                        pltpu.VMEM((CHUNK, lanes, SLAB_S), BF16),
                        pltpu.VMEM((rows, SLAB_S), F32),
                        pltpu.VMEM((rows, SLAB_S), F32)],
        compiler_params=_cp("parallel"),
        name="ssm_prompt",
    )(z, *sp)
    return y, hlr, hli


def _ssm_step_kernel(u_ref, h0r_ref, h0i_ref, lre_ref, lim_ref, ldt_ref, bre_ref, bim_ref,
                     cre_ref, cim_ref, d_ref, y_ref, hr_ref, hi_ref):
    ar, ai, zr, zi = _ssm_discretise(lre_ref[...], lim_ref[...], ldt_ref[...])
    mask = _slab_mask()
    bre, bim = bre_ref[...], bim_ref[...]
    we_r = _block_diag(bre * zr - bim * zi, mask).astype(BF16)
    we_i = _block_diag(bre * zi + bim * zr, mask).astype(BF16)
    u = u_ref[...]
    ub = u.astype(BF16)
    h0r, h0i = h0r_ref[...], h0i_ref[...]
    hr = ar * h0r - ai * h0i + _dot(ub, we_r)
    him = ar * h0i + ai * h0r + _dot(ub, we_i)
    hr_ref[...] = hr
    hi_ref[...] = him
    cr_bd = _block_diag(cre_ref[...], mask).astype(BF16)
    ci_bd = _block_diag(cim_ref[...], mask).astype(BF16)
    y_ref[...] = (_dot_nt(hr.astype(BF16), cr_bd) - _dot_nt(him.astype(BF16), ci_bd)
                  + d_ref[...] * u)


def _ssm_step(zs, u_col0, h0r, h0i, sp):
    n = zs.shape[0]
    lanes = SLAB_G * SSM_GROUP
    ns = sp[0].shape[0]
    ub = u_col0 // lanes

    def pspec(r, c):
        return pl.BlockSpec((None, r, c), lambda s: (s, 0, 0))

    hspec = pl.BlockSpec((None, n, SLAB_S), lambda s: (s, 0, 0))
    return pl.pallas_call(
        _ssm_step_kernel,
        grid=(ns,),
        in_specs=[pl.BlockSpec((n, lanes), lambda s: (0, ub + s)), hspec, hspec,
                  pspec(1, SLAB_S), pspec(1, SLAB_S), pspec(1, SLAB_S),
                  pspec(SSM_GROUP, SLAB_S), pspec(SSM_GROUP, SLAB_S),
                  pspec(SSM_GROUP, SLAB_S), pspec(SSM_GROUP, SLAB_S), pspec(1, lanes)],
        out_specs=[pl.BlockSpec((n, lanes), lambda s: (0, s)), hspec, hspec],
        out_shape=[jax.ShapeDtypeStruct((n, ns * lanes), F32),
                   jax.ShapeDtypeStruct((ns, n, SLAB_S), F32),
                   jax.ShapeDtypeStruct((ns, n, SLAB_S), F32)],
        compiler_params=_cp("parallel"),
        name="ssm_step",
    )(zs, h0r, h0i, *sp)


def _gelu_tanh(x):
    return 0.5 * x * (1.0 + jnp.tanh(math.sqrt(2.0 / math.pi) * (x + 0.044715 * (x * x * x))))


def _ssm_gate_kernel(y_ref, w_ref, b_ref, o_ref):
    yg = _gelu_tanh(y_ref[...])
    gate = _dot(yg.astype(BF16), w_ref[...]) + b_ref[...]
    o_ref[...] = (yg * _sigmoid(gate)).astype(o_ref.dtype)


def _ssm_gate(y, w, b, tm):
    m, d = y.shape
    return pl.pallas_call(
        _ssm_gate_kernel,
        grid=(m // tm,),
        in_specs=[pl.BlockSpec((tm, d), lambda i: (i, 0)),
                  pl.BlockSpec((d, d), lambda i: (0, 0)),
                  pl.BlockSpec((1, d), lambda i: (0, 0))],
        out_specs=pl.BlockSpec((tm, d), lambda i: (i, 0)),
        out_shape=jax.ShapeDtypeStruct((m, d), BF16),
        compiler_params=_cp("parallel"),
        name="ssm_gate",
    )(y, w, b.reshape(1, d).astype(F32))


def _ssm_gate_step_kernel(y_ref, yc_ref, w_ref, b_ref, o_ref, wb_ref):
    yg = _gelu_tanh(y_ref[...])
    gate = _dot(yg.astype(BF16), _use_w(w_ref, wb_ref)) + b_ref[...]
    o_ref[...] = (_gelu_tanh(yc_ref[...]) * _sigmoid(gate)).astype(o_ref.dtype)


def _ssm_gate_step(y, w, layer, b, tn):
    m, d = y.shape
    return pl.pallas_call(
        _ssm_gate_step_kernel,
        grid=(d // tn,),
        in_specs=[pl.BlockSpec((m, d), lambda j: (0, 0)),
                  pl.BlockSpec((m, tn), lambda j: (0, j)),
                  _wspec(w, layer, d, tn, 0, lambda j: j),
                  pl.BlockSpec((1, tn), lambda j: (0, j))],
        out_specs=[pl.BlockSpec((m, tn), lambda j: (0, j)),
                   pl.BlockSpec((d, tn), lambda j: (0, j))],
        out_shape=[jax.ShapeDtypeStruct((m, d), BF16), jax.ShapeDtypeStruct((d, d), BF16)],
        compiler_params=_cp("arbitrary"),
        name="ssm_gate_step",
    )(y, y, w, b.reshape(1, d).astype(F32))


def _outproj_kernel(a_ref, s_ref, wa_ref, ws_ref, x_ref, o_ref, *wb):
    wab_ref, wsb_ref = wb if wb else (None, None)
    o_ref[...] = (x_ref[...] + _dot(a_ref[...], _use_w(wa_ref, wab_ref))
                  + _dot(s_ref[...], _use_w(ws_ref, wsb_ref)))


def _outproj(attn, ssm, wa, ws, layer, x, tm, tn):
    m, ka = attn.shape
    n = wa.shape[-1]
    emit = layer is not None
    out_specs = [pl.BlockSpec((tm, tn), lambda i, j: (i, j))]
    out_shape = [jax.ShapeDtypeStruct((m, n), F32)]
    if emit:
        out_specs += [pl.BlockSpec((ka, tn), lambda i, j: (0, j))] * 2
        out_shape += [jax.ShapeDtypeStruct((ka, n), BF16)] * 2
    res = pl.pallas_call(
        _outproj_kernel,
        grid=(m // tm, n // tn),
        in_specs=[pl.BlockSpec((tm, ka), lambda i, j: (i, 0)),
                  pl.BlockSpec((tm, ka), lambda i, j: (i, 0)),
                  _wspec(wa, layer, ka, tn, 0, lambda i, j: j),
                  _wspec(ws, layer, ka, tn, 1 if emit else 0, lambda i, j: j),
                  pl.BlockSpec((tm, tn), lambda i, j: (i, j))],
        out_specs=out_specs,
        out_shape=out_shape,
        compiler_params=_cp("parallel", "arbitrary"),
        name="outproj",
    )(attn, ssm, wa, ws, x)
    return res if emit else res[0]


def _mm_res_kernel(a_ref, w_ref, b_ref, x_ref, o_ref, *wb):
    wb_ref = wb[0] if wb else None
    o_ref[...] = x_ref[...] + (_dot(a_ref[...], _use_w(w_ref, wb_ref)) + b_ref[...])


def _mm_res(a, w, layer, b, x, tm, tn, name):
    m, k = a.shape
    n = w.shape[-1]
    emit = layer is not None
    out_specs = [pl.BlockSpec((tm, tn), lambda i, j: (i, j))]
    out_shape = [jax.ShapeDtypeStruct((m, n), F32)]
    if emit:
        out_specs.append(pl.BlockSpec((k, tn), lambda i, j: (0, j)))
        out_shape.append(jax.ShapeDtypeStruct((k, n), BF16))
    res = pl.pallas_call(
        _mm_res_kernel,
        grid=(m // tm, n // tn),
        in_specs=[pl.BlockSpec((tm, k), lambda i, j: (i, 0)),
                  _wspec(w, layer, k, tn, 0, lambda i, j: j),
                  pl.BlockSpec((1, tn), lambda i, j: (0, j)),
                  pl.BlockSpec((tm, tn), lambda i, j: (i, j))],
        out_specs=out_specs,
        out_shape=out_shape,
        compiler_params=_cp("parallel", "arbitrary"),
        name=name,
    )(a, w, b.reshape(1, n).astype(F32), x)
    return res if emit else res[0]


def _pw1_kernel(x_ref, wa_ref, wg_ref, ba_ref, bg_ref, o_ref, *wb):
    wab_ref, wgb_ref = wb if wb else (None, None)
    x = x_ref[...]
    a = _dot(x, _use_w(wa_ref, wab_ref)) + ba_ref[...]
    gate = _dot(x, _use_w(wg_ref, wgb_ref)) + bg_ref[...]
    o_ref[...] = a * _sigmoid(gate)


def _pw1(hn, wa, wg, layer, b, tm, tn):
    m, k = hn.shape
    n = b.shape[0] // 2
    nj = n // tn
    emit = layer is not None
    b2 = b.reshape(1, 2 * n).astype(F32)
    out_specs = [pl.BlockSpec((tm, tn), lambda i, j: (i, j))]
    out_shape = [jax.ShapeDtypeStruct((m, n), F32)]
    if emit:
        out_specs += [pl.BlockSpec((k, tn), lambda i, j: (0, j))] * 2
        out_shape += [jax.ShapeDtypeStruct((k, n), BF16)] * 2
    goff = nj if emit else 0
    res = pl.pallas_call(
        _pw1_kernel,
        grid=(m // tm, nj),
        in_specs=[pl.BlockSpec((tm, k), lambda i, j: (i, 0)),
                  _wspec(wa, layer, k, tn, 0, lambda i, j: j),
                  _wspec(wg, layer, k, tn, 0, lambda i, j: j + goff),
                  pl.BlockSpec((1, tn), lambda i, j: (0, j)),
                  pl.BlockSpec((1, tn), lambda i, j: (0, j + nj))],
        out_specs=out_specs,
        out_shape=out_shape,
        compiler_params=_cp("parallel", "arbitrary"),
        name="conv_pw1_glu",
    )(hn, wa, wg, b2, b2)
    return res if emit else res[0]


def _ln_swish(y, g, b):
    mu = jnp.mean(y, axis=-1, keepdims=True)
    yc = y - mu
    var = jnp.mean(yc * yc, axis=-1, keepdims=True)
    yn = yc * lax.rsqrt(var + LN_EPS) * g + b
    return yn * _sigmoid(yn)


def _conv31_kernel(u_ref, halo_ref, w_ref, b_ref, g_ref, bb_ref, o_ref, ext_ref, y3_ref, y_ref,
                   *, tm, tiles_per_seq, slab):
    seq_start = (pl.program_id(0) % tiles_per_seq) == 0
    nc = u_ref.shape[1] // HEAD_DIM
    for c in range(nc):
        cs = slice(c * HEAD_DIM, (c + 1) * HEAD_DIM)
        ext_ref[pl.ds(c, CONV_HALO, stride=slab), :] = jnp.where(seq_start, 0.0, halo_ref[:, cs])
        ext_ref[pl.ds(CONV_HALO * slab + c, tm, stride=slab), :] = u_ref[:, cs]
    off = CONV_HALO - (CONV_WIDTH - 1)
    rw = 8

    def body(i, carry):
        for ct in range(nc // 8):
            base = pl.multiple_of((i * rw + off) * slab, 8) + ct * 8
            win = [ext_ref[pl.ds(base + s * slab, 8), :] for s in range(rw + CONV_WIDTH - 1)]
            acc = [None] * rw
            for j in range(CONV_WIDTH):
                wj = w_ref[j, ct * 8:(ct + 1) * 8, :]
                for tt in range(rw):
                    term = wj * win[tt + j]
                    acc[tt] = term if j == 0 else acc[tt] + term
            bias = b_ref[ct * 8:(ct + 1) * 8, :]
            obase = pl.multiple_of(i * rw * slab, 8) + ct * 8
            for tt in range(rw):
                y3_ref[pl.ds(obase + tt * slab, 8), :] = acc[tt] + bias
        return carry

    lax.fori_loop(0, tm // rw, body, 0)
    for c in range(nc):
        y_ref[:, c * HEAD_DIM:(c + 1) * HEAD_DIM] = y3_ref[pl.ds(c, tm, stride=slab), :]
    o_ref[...] = _ln_swish(y_ref[...], g_ref[...], bb_ref[...]).astype(o_ref.dtype)


def _conv31(u, t, w, b, g, bb, tm):
    m, d = u.shape
    nc = d // HEAD_DIM
    slab = nc + 8
    row = lambda v: v.reshape(1, d).astype(F32)
    rspec = pl.BlockSpec((1, d), lambda i: (0, 0))
    hb = tm // CONV_HALO
    return pl.pallas_call(
        functools.partial(_conv31_kernel, tm=tm, tiles_per_seq=t // tm, slab=slab),
        grid=(m // tm,),
        in_specs=[pl.BlockSpec((tm, d), lambda i: (i, 0)),
                  pl.BlockSpec((CONV_HALO, d), lambda i: (jnp.maximum(i * hb - 1, 0), 0)),
                  pl.BlockSpec((CONV_WIDTH, nc, HEAD_DIM), lambda i: (0, 0, 0)),
                  pl.BlockSpec((nc, HEAD_DIM), lambda i: (0, 0)), rspec, rspec],
        out_specs=pl.BlockSpec((tm, d), lambda i: (i, 0)),
        out_shape=jax.ShapeDtypeStruct((m, d), BF16),
        scratch_shapes=[pltpu.VMEM(((tm + CONV_HALO) * slab, HEAD_DIM), F32),
                        pltpu.VMEM((tm * slab, HEAD_DIM), F32),
                        pltpu.VMEM((tm, d), F32)],
        compiler_params=_cp("parallel"),
        name="conv31_ln_swish",
    )(u, u, w.astype(F32).reshape(CONV_WIDTH, nc, HEAD_DIM), b.reshape(nc, HEAD_DIM).astype(F32),
      row(g), row(bb))


def _conv31_step_kernel(u_ref, st_ref, w_ref, b_ref, g_ref, bb_ref, o_ref):
    acc = w_ref[0:1, :] * st_ref[0]
    for j in range(1, CONV_WIDTH - 1):
        acc = acc + w_ref[j:j + 1, :] * st_ref[j]
    acc = acc + w_ref[CONV_WIDTH - 1:CONV_WIDTH, :] * u_ref[...] + b_ref[...]
    o_ref[...] = _ln_swish(acc, g_ref[...], bb_ref[...]).astype(o_ref.dtype)


def _conv31_step(u, state_t, w, b, g, bb):
    n, d = u.shape
    row = lambda v: v.reshape(1, d).astype(F32)
    return pl.pallas_call(
        _conv31_step_kernel,
        out_shape=jax.ShapeDtypeStruct((n, d), BF16),
        compiler_params=pltpu.CompilerParams(vmem_limit_bytes=VMEM_LIMIT_V7X),
        name="conv31_step",
    )(u, state_t, w.astype(F32), row(b), row(g), row(bb))


def _ffn_up_kernel(x_ref, halo_ref, w_ref, cwg_ref, cwv_ref, cbg_ref, cbv_ref,
                   act_ref, zlg_ref, zlv_ref, xs_ref, *, tm, tn, tiles_per_seq, parts):
    i = pl.program_id(0)

    @pl.when(pl.program_id(1) == 0)
    def _():
        xs_ref[0:FFN_HALO, :] = halo_ref[...]
        xs_ref[FFN_HALO:, :] = x_ref[...]

    seq_start = (i % tiles_per_seq) == 0
    rows = tm // parts
    w = w_ref[...]
    cw = jnp.concatenate([cwg_ref[...], cwv_ref[...]], axis=1)
    cb = jnp.concatenate([cbg_ref[...], cbv_ref[...]], axis=1)
    prev = None
    for part in range(parts):
        if part == 0:
            z = _dot(xs_ref[0:FFN_HALO + rows, :], w)
            row = lax.broadcasted_iota(jnp.int32, (FFN_HALO + rows, 1), 0)
            z = z * jnp.where(jnp.logical_and(row < FFN_HALO, seq_start), 0.0, 1.0)
            lead = FFN_HALO
        else:
            r0 = FFN_HALO + part * rows
            z = jnp.concatenate([prev, _dot(xs_ref[r0:r0 + rows, :], w)], axis=0)
            lead = 8
        prev = z[lead + rows - 8:, :]
        zc = (cw[2:3, :] * z + cw[1:2, :] * pltpu.roll(z, 1, 0)
              + cw[0:1, :] * pltpu.roll(z, 2, 0) + cb)[lead:, :]
        gate = zc[:, :tn]
        act_ref[part * rows:(part + 1) * rows, :] = (gate * _sigmoid(gate) * zc[:, tn:]).astype(act_ref.dtype)
    zlg_ref[...] = prev[:, :tn]
    zlv_ref[...] = prev[:, tn:]


def _ffn_up(hn, t, w, cw, cb, tm, tn):
    m, k = hn.shape
    dff = w.shape[1] // 2
    nj = dff // tn
    cw = cw.astype(F32)
    cb2 = cb.reshape(1, 2 * dff).astype(F32)
    hb = tm // FFN_HALO
    zl = jax.ShapeDtypeStruct((m // tm, 8, dff), F32)
    return pl.pallas_call(
        functools.partial(_ffn_up_kernel, tm=tm, tn=tn, tiles_per_seq=t // tm, parts=4),
        grid=(m // tm, nj),
        in_specs=[pl.BlockSpec((tm, k), lambda i, j: (i, 0)),
                  pl.BlockSpec((FFN_HALO, k), lambda i, j: (jnp.maximum(i * hb - 1, 0), 0)),
                  pl.BlockSpec((k, 2 * tn), lambda i, j: (0, j)),
                  pl.BlockSpec((3, tn), lambda i, j: (0, j)),
                  pl.BlockSpec((3, tn), lambda i, j: (0, j + nj)),
                  pl.BlockSpec((1, tn), lambda i, j: (0, j)),
                  pl.BlockSpec((1, tn), lambda i, j: (0, j + nj))],
        out_specs=[pl.BlockSpec((tm, tn), lambda i, j: (i, j)),
                   pl.BlockSpec((None, 8, tn), lambda i, j: (i, 0, j)),
                   pl.BlockSpec((None, 8, tn), lambda i, j: (i, 0, j))],
        out_shape=[jax.ShapeDtypeStruct((m, dff), BF16), zl, zl],
        scratch_shapes=[pltpu.VMEM((tm + FFN_HALO, k), BF16)],
        compiler_params=_cp("parallel", "arbitrary"),
        name="ffn_up_conv_glu",
    )(hn, hn, w, cw, cw, cb2, cb2)


def _ffn_up_step_kernel(x_ref, wg_ref, wv_ref, sg_ref, sv_ref, cwg_ref, cwv_ref, cbg_ref, cbv_ref,
                        act_ref, zg_ref, zv_ref, wb_ref):
    x = x_ref[...]
    tn = wg_ref.shape[1]

    def half(w_ref, c0, s_ref, cw_ref, cb_ref, z_ref):
        wb = w_ref[...].astype(BF16)
        wb_ref[:, c0:c0 + tn] = wb
        z = _dot(x, wb)
        z_ref[...] = z
        return cw_ref[2:3, :] * z + cw_ref[1:2, :] * s_ref[1] + cw_ref[0:1, :] * s_ref[0] + cb_ref[...]

    gate = half(wg_ref, 0, sg_ref, cwg_ref, cbg_ref, zg_ref)
    val = half(wv_ref, tn, sv_ref, cwv_ref, cbv_ref, zv_ref)
    act_ref[...] = (gate * _sigmoid(gate) * val).astype(act_ref.dtype)


def _ffn_up_step(hn, state_t, w, layer, cw, cb, tn):
    n, k = hn.shape
    dff = w.shape[-1] // 2
    nj = dff // tn
    cw = cw.astype(F32)
    cb2 = cb.reshape(1, 2 * dff).astype(F32)
    zs = jax.ShapeDtypeStruct((n, dff), F32)
    return pl.pallas_call(
        _ffn_up_step_kernel,
        grid=(nj,),
        in_specs=[pl.BlockSpec((n, k), lambda j: (0, 0)),
                  _wspec(w, layer, k, tn, 0, lambda j: j),
                  _wspec(w, layer, k, tn, 0, lambda j: j + nj),
                  pl.BlockSpec((2, n, tn), lambda j: (0, 0, j)),
                  pl.BlockSpec((2, n, tn), lambda j: (0, 0, j + nj)),
                  pl.BlockSpec((3, tn), lambda j: (0, j)),
                  pl.BlockSpec((3, tn), lambda j: (0, j + nj)),
                  pl.BlockSpec((1, tn), lambda j: (0, j)),
                  pl.BlockSpec((1, tn), lambda j: (0, j + nj))],
        out_specs=[pl.BlockSpec((n, tn), lambda j: (0, j)),
                   pl.BlockSpec((n, tn), lambda j: (0, j)),
                   pl.BlockSpec((n, tn), lambda j: (0, j)),
                   pl.BlockSpec((k, 2 * tn), lambda j: (0, j))],
        out_shape=[jax.ShapeDtypeStruct((n, dff), BF16), zs, zs,
                   jax.ShapeDtypeStruct((k, 2 * dff), BF16)],
        compiler_params=_cp("arbitrary"),
        name="ffn_up_step",
    )(hn, w, w, state_t, state_t, cw, cw, cb2, cb2)


def _tile_plan(m):
    return 1024 if m >= 1024 else m


def _ffn_prompt(x, t, norm_g, w_up, cw, cb, w_down, tm):
    hn = _rmsnorm(x, norm_g, BF16)
    act, zlg, zlv = _ffn_up(hn, t, w_up, cw, cb, tm, 256)
    per = t // tm
    last = jnp.concatenate([zlg[per - 1::per, 6:8], zlv[per - 1::per, 6:8]], axis=-1)
    xo = _mm_res(act, w_down, None, jnp.zeros((w_down.shape[1],), F32), x, 512, 512, "ffn_down")
    return xo, last


def _ffn_step(x, prev_state_t, norm_g, w_up, w_down, layer, cw, cb):
    m = x.shape[0]
    hn = _rmsnorm(x, norm_g, BF16)
    act, zg, zv, wu_b = _ffn_up_step(hn, prev_state_t, w_up, layer, cw, cb, 256)
    znew = jnp.concatenate([zg, zv], axis=-1)
    last = jnp.concatenate([prev_state_t[1][:, None], znew[:, None]], axis=1)
    xo, wd_b = _mm_res(act, w_down, layer, jnp.zeros((w_down.shape[-1],), F32), x, m, 256, "ffn_down_step")
    return xo, last, (wu_b, wd_b)


def kernel(x_prompt, x_sample, cache_a0, cache_a1, cache_a2, state_ssm, state_conv, state_ffn, norm_mix, norm_ffn, norm_final, w_in_even, w_out_even, ssm_a_re, ssm_a_im, ssm_b_re, ssm_b_im, ssm_c_re, ssm_c_im, ssm_d, ssm_log_dt, w_glu, b_glu, w_pw1, b_pw1, w_dw, b_dw, ln_g, ln_b, w_pw2, b_pw2, w_up, ffn_dw, ffn_dw_b, w_down):
    n_p, t_p, d_model = x_prompt.shape
    n_s = x_sample.shape[0]
    caches = (cache_a0, cache_a1, cache_a2)
    u_col0 = N_DIL * QKV_W
    sp = _ssm_params(ssm_a_re[0], ssm_a_im[0], ssm_b_re[0], ssm_b_im[0], ssm_c_re[0], ssm_c_im[0],
                     ssm_d[0], ssm_log_dt[0])
    ns = sp[0].shape[0]

    def ssm_state(hr, him, n):
        def unslab(h):
            return h.reshape(ns, n, SLAB_G, SSM_STATE).transpose(1, 0, 2, 3).reshape(n, ns * SLAB_G, SSM_STATE)
        return jnp.stack([unslab(hr), unslab(him)], axis=-1)[None]

    xs = x_sample.reshape(n_s, d_model)
    cos_s, sin_s = _rope_tables(n_s, PAST_LEN, 0)
    hn = _rmsnorm(xs, norm_mix[0], BF16)
    zs, w_in_b = _inproj(hn, w_in_even, 0, cos_s, sin_s, n_s, 512)
    attn_s = _step_attn(zs, [c[0] for c in caches])

    def slab(h):
        return h.reshape(n_s, ns, SLAB_S).transpose(1, 0, 2)

    y_s, hr_s, hi_s = _ssm_step(zs, u_col0, slab(state_ssm[0, ..., 0]), slab(state_ssm[0, ..., 1]), sp)
    gate_s, w_glu_b = _ssm_gate_step(y_s, w_glu, 0, b_glu[0], 512)
    xs, w_oa_b, w_os_b = _outproj(attn_s, gate_s, w_out_even, w_out_even, 0, xs, n_s, 512)
    xs, ffn0_s, wf0 = _ffn_step(xs, state_ffn[0].transpose(1, 0, 2), norm_ffn[0], w_up, w_down, 0,
                                ffn_dw[0], ffn_dw_b[0])
    hn = _rmsnorm(xs, norm_mix[1], BF16)
    us, w_pa_b, w_pg_b = _pw1(hn, w_pw1, w_pw1, 0, b_pw1[0], n_s, 256)
    cs = _conv31_step(us, state_conv[0].transpose(1, 0, 2), w_dw[0], b_dw[0], ln_g[0], ln_b[0])
    conv_s = jnp.concatenate([state_conv[0][:, 1:], us[:, None]], axis=1)
    xs, w_pw2_b = _mm_res(cs, w_pw2, 0, b_pw2[0], xs, n_s, 512, "conv_pw2_step")
    xs, ffn1_s, wf1 = _ffn_step(xs, state_ffn[1].transpose(1, 0, 2), norm_ffn[1], w_up, w_down, 1,
                                ffn_dw[1], ffn_dw_b[1])
    y_sample = _rmsnorm(xs, norm_final, F32).reshape(n_s, 1, d_model)

    a_s = []
    for g in range(N_DIL):
        kv = zs[:, g * QKV_W + D_A:(g + 1) * QKV_W].reshape(n_s, 1, 2, A_HEADS, HEAD_DIM)
        a_s.append(_cache_shift(caches[g][0], kv)[None])
    ssm_s = ssm_state(hr_s, hi_s, n_s)
    conv_s = conv_s[None]
    ffn_s = jnp.stack([ffn0_s, ffn1_s])

    mp = n_p * t_p
    tm = _tile_plan(mp)
    x = x_prompt.reshape(mp, d_model)
    cos, sin = _rope_tables(t_p, 0, 1)
    hn = _rmsnorm(x, norm_mix[0], BF16)
    z = _inproj(hn, w_in_b, None, cos, sin, tm, 1024)
    outs, lses = [], []
    for g in range(N_DIL):
        o, lse = _band_attn(z, n_p, t_p, g)
        outs.append(o)
        lses.append(lse)
    attn = _combine(outs, lses)
    y_ssm, hlr, hli = _ssm_prompt(z, n_p, t_p, u_col0, sp)
    gate_p = _ssm_gate(y_ssm, w_glu_b, b_glu[0], 512)
    x = _outproj(attn, gate_p, w_oa_b, w_os_b, None, x, tm, 1024)
    x, ffn0_p = _ffn_prompt(x, t_p, norm_ffn[0], wf0[0], ffn_dw[0], ffn_dw_b[0], wf0[1], tm)
    hn = _rmsnorm(x, norm_mix[1], BF16)
    u = _pw1(hn, w_pa_b, w_pg_b, None, b_pw1[0], tm, 512)
    c = _conv31(u, t_p, w_dw[0], b_dw[0], ln_g[0], ln_b[0], 256)
    conv_p = u.reshape(n_p, t_p, -1)[None, :, t_p - (CONV_WIDTH - 1):]
    x = _mm_res(c, w_pw2_b, None, b_pw2[0], x, tm, 1024, "conv_pw2")
    x, ffn1_p = _ffn_prompt(x, t_p, norm_ffn[1], wf1[0], ffn_dw[1], ffn_dw_b[1], wf1[1], tm)
    y_prompt = _rmsnorm(x, norm_final, F32).reshape(n_p, t_p, d_model)

    a_p = []
    for g in range(N_DIL):
        a_p.append(_kv_rows(z, n_p, t_p, g, min(A_WINDOWS[g], PAST_LEN))[None])
    ssm_p = ssm_state(hlr, hli, n_p)
    ffn_p = jnp.stack([ffn0_p, ffn1_p])

    return (y_prompt, y_sample, a_p[0], a_s[0], a_p[1], a_s[1], a_p[2], a_s[2],
            ssm_p, ssm_s, conv_p, conv_s, ffn_p, ffn_s)
```

```python
import functools
import math

import jax
import jax.numpy as jnp
from jax import lax
from jax.experimental import pallas as pl
from jax.experimental.pallas import tpu as pltpu

F32 = jnp.float32
BF16 = jnp.bfloat16

HEAD_DIM = 128
A_HEADS = 16
D_A = A_HEADS * HEAD_DIM
A_WINDOWS = (128, 512, 2048)
A_DILATIONS = (1, 4, 16)
BAND = 128
N_DIL = 3
QKV_W = 3 * D_A
ROPE_THETA = 10000.0
PAST_LEN = 16384
SSM_GROUP = 16
SSM_STATE = 64
SLAB_G = 8
SLAB_S = SLAB_G * SSM_STATE
CHUNK = 8
CONV_WIDTH = 31
CONV_HALO = 32
RMS_EPS = 1e-6
LN_EPS = 1e-5
NEG_BIG = -1e30
VMEM_LIMIT_V7X = 56 * 1024 * 1024


def _cp(*sem):
    return pltpu.CompilerParams(dimension_semantics=sem, vmem_limit_bytes=VMEM_LIMIT_V7X)


def _dot(a, b, **kw):
    return jnp.dot(a, b, preferred_element_type=F32, **kw)


def _dot_nt(a, b, **kw):
    return lax.dot_general(a, b, (((1,), (1,)), ((), ())), preferred_element_type=F32, **kw)


def _sigmoid(x):
    return 1.0 / (1.0 + jnp.exp(-x))


def _rmsnorm_kernel(x_ref, g_ref, o_ref):
    x = x_ref[...]
    ms = jnp.mean(x * x, axis=-1, keepdims=True)
    o_ref[...] = (x * lax.rsqrt(ms + RMS_EPS) * g_ref[...]).astype(o_ref.dtype)


def _rmsnorm(x, g, out_dtype):
    m, d = x.shape
    tm = min(m, 256)
    return pl.pallas_call(
        _rmsnorm_kernel,
        grid=(m // tm,),
        in_specs=[pl.BlockSpec((tm, d), lambda i: (i, 0)),
                  pl.BlockSpec((1, d), lambda i: (0, 0))],
        out_specs=pl.BlockSpec((tm, d), lambda i: (i, 0)),
        out_shape=jax.ShapeDtypeStruct((m, d), out_dtype),
        compiler_params=_cp("parallel"),
        name="rmsnorm",
    )(x, g.reshape(1, d))


def _rope_table_kernel(invf_ref, cos_ref, sin_ref, *, pos0, step):
    rows = cos_ref.shape[0]
    row = lax.broadcasted_iota(jnp.int32, (rows, HEAD_DIM), 0)
    lane = lax.broadcasted_iota(jnp.int32, (rows, HEAD_DIM), 1)
    pos = (row * step + pos0).astype(F32)
    ang = pos * invf_ref[...]
    cos_ref[...] = jnp.cos(ang)
    s = jnp.sin(ang)
    sin_ref[...] = jnp.where(lane < HEAD_DIM // 2, -s, s)


def _rope_tables(rows, pos0, step):
    half = HEAD_DIM // 2
    inv_freq = ROPE_THETA ** (-jnp.arange(half, dtype=F32) / half)
    invf = jnp.concatenate([inv_freq, inv_freq]).reshape(1, HEAD_DIM)
    shp = jax.ShapeDtypeStruct((rows, HEAD_DIM), F32)
    return pl.pallas_call(
        functools.partial(_rope_table_kernel, pos0=pos0, step=step),
        out_shape=(shp, shp),
        name="rope_tables",
    )(invf)


def _wspec(w, layer, kb, tn, kblk, jmap):
    if layer is None:
        return pl.BlockSpec((kb, tn), lambda *g: (kblk, jmap(*g)))
    return pl.BlockSpec((None, kb, tn), lambda *g: (layer, kblk, jmap(*g)))


def _use_w(w_ref, wb_ref, cols=slice(None)):
    w = w_ref[:, cols]
    if wb_ref is None:
        return w
    wb = w.astype(BF16)
    wb_ref[:, cols] = wb
    return wb


def _inproj_kernel(x_ref, w_ref, cos_ref, sin_ref, o_ref, *wb, tn, part):
    wb_ref = wb[0] if wb else None
    seg = (pl.program_id(1) * tn) // D_A
    is_rope = jnp.logical_and(seg < 3 * N_DIL, seg % 3 != 2)
    cos = cos_ref[...]
    sin = sin_ref[...]
    x = x_ref[...]
    for c0 in range(0, tn, part):
        acc = _dot(x, _use_w(w_ref, wb_ref, slice(c0, c0 + part)))
        for h in range(part // HEAD_DIM):
            xh = acc[:, h * HEAD_DIM:(h + 1) * HEAD_DIM]
            roped = xh * cos + pltpu.roll(xh, HEAD_DIM // 2, 1) * sin
            o_ref[:, c0 + h * HEAD_DIM:c0 + (h + 1) * HEAD_DIM] = jnp.where(is_rope, roped, xh)


def _inproj(hn, w, layer, cos, sin, tm, tn):
    m, k = hn.shape
    n = w.shape[-1]
    ct = cos.shape[0] // tm
    emit = layer is not None
    out_specs = [pl.BlockSpec((tm, tn), lambda i, j: (i, j))]
    out_shape = [jax.ShapeDtypeStruct((m, n), F32)]
    if emit:
        out_specs.append(pl.BlockSpec((k, tn), lambda i, j: (0, j)))
        out_shape.append(jax.ShapeDtypeStruct((k, n), BF16))
    res = pl.pallas_call(
        functools.partial(_inproj_kernel, tn=tn, part=min(tn, 256)),
        grid=(m // tm, n // tn),
        in_specs=[pl.BlockSpec((tm, k), lambda i, j: (i, 0)),
                  _wspec(w, layer, k, tn, 0, lambda i, j: j),
                  pl.BlockSpec((tm, HEAD_DIM), lambda i, j: (i % ct, 0)),
                  pl.BlockSpec((tm, HEAD_DIM), lambda i, j: (i % ct, 0))],
        out_specs=out_specs,
        out_shape=out_shape,
        compiler_params=_cp("parallel", "arbitrary"),
        name="inproj_rope",
    )(hn, w, cos, sin)
    return res if emit else res[0]


def _band_attn_kernel(q_ref, kp_ref, kc_ref, vp_ref, vc_ref, o_ref, lse_ref):
    has_prev = pl.program_id(2) > 0
    qi = lax.broadcasted_iota(jnp.int32, (BAND, 2 * BAND), 0)
    ki = lax.broadcasted_iota(jnp.int32, (BAND, 2 * BAND), 1)
    valid = jnp.logical_and(jnp.logical_and(ki >= qi, ki <= qi + BAND),
                            jnp.logical_or(ki >= BAND, has_prev))
    lane = lax.broadcasted_iota(jnp.int32, (BAND, HEAD_DIM), 1)
    lse_tile = jnp.zeros((BAND, HEAD_DIM), F32)
    scale = HEAD_DIM ** -0.5
    for h in range(A_HEADS):
        sl = slice(h * HEAD_DIM, (h + 1) * HEAD_DIM)
        q = q_ref[:, sl].astype(BF16)
        k = jnp.concatenate([kp_ref[:, sl], kc_ref[:, sl]], axis=0).astype(BF16)
        v = jnp.concatenate([vp_ref[:, sl], vc_ref[:, sl]], axis=0).astype(BF16)
        s = _dot_nt(q, k) * scale
        s = jnp.where(valid, s, NEG_BIG)
        m = jnp.max(s, axis=-1, keepdims=True)
        p = jnp.exp(s - m)
        l = jnp.sum(p, axis=-1, keepdims=True)
        o_ref[:, sl] = _dot(p.astype(BF16), v) / l
        lse_tile = jnp.where(lane == h, m + jnp.log(l), lse_tile)
    lse_ref[...] = lse_tile


def _band_attn_dil_kernel(*refs, d, hpb, use_prev):
    if use_prev:
        q_ref, kp_ref, kc_ref, vp_ref, vc_ref, o_ref, lse_ref, qs, ks, vs, os_, ls = refs
    else:
        q_ref, kc_ref, vc_ref, o_ref, lse_ref, qs, ks, vs, os_, ls = refs
    rows = BAND * d
    hb = pl.program_id(2)
    has_prev = pl.program_id(1) > 0
    for h in range(hpb):
        sl = slice(h * HEAD_DIM, (h + 1) * HEAD_DIM)
        qs[h] = q_ref[:, sl]
        if use_prev:
            ks[h, 0:rows] = kp_ref[:, sl]
            ks[h, rows:2 * rows] = kc_ref[:, sl]
            vs[h, 0:rows] = vp_ref[:, sl]
            vs[h, rows:2 * rows] = vc_ref[:, sl]
        else:
            ks[h] = kc_ref[:, sl]
            vs[h] = vc_ref[:, sl]

    @pl.when(hb == 0)
    def _():
        lse_ref[...] = jnp.zeros(lse_ref.shape, F32)

    nk = 2 * BAND if use_prev else BAND
    qi = lax.broadcasted_iota(jnp.int32, (1, BAND, nk), 1)
    ki = lax.broadcasted_iota(jnp.int32, (1, BAND, nk), 2)
    if use_prev:
        valid = jnp.logical_and(jnp.logical_and(ki >= qi, ki <= qi + BAND),
                                jnp.logical_or(ki >= BAND, has_prev))
    else:
        valid = ki <= qi
    scale = HEAD_DIM ** -0.5

    def residues(ref, h, base):
        return [ref[h, pl.ds(base + r, BAND, stride=d), :] for r in range(d)]

    for h in range(hpb):
        q = jnp.stack(residues(qs, h, 0)).astype(BF16)
        if use_prev:
            k = jnp.stack([jnp.concatenate([a, b], axis=0)
                           for a, b in zip(residues(ks, h, 0), residues(ks, h, rows))]).astype(BF16)
            v = jnp.stack([jnp.concatenate([a, b], axis=0)
                           for a, b in zip(residues(vs, h, 0), residues(vs, h, rows))]).astype(BF16)
        else:
            k = jnp.stack(residues(ks, h, 0)).astype(BF16)
            v = jnp.stack(residues(vs, h, 0)).astype(BF16)
        s = jnp.einsum('rqd,rkd->rqk', q, k, preferred_element_type=F32) * scale
        s = jnp.where(valid, s, NEG_BIG)
        m = jnp.max(s, axis=-1, keepdims=True)
        p = jnp.exp(s - m)
        l = jnp.sum(p, axis=-1, keepdims=True)
        o = jnp.einsum('rqk,rkd->rqd', p.astype(BF16), v, preferred_element_type=F32) / l
        lse_h = m + jnp.log(l)
        for r in range(d):
            os_[h, pl.ds(r, BAND, stride=d), :] = o[r]
            ls[h, pl.ds(r, BAND, stride=d), :] = jnp.broadcast_to(lse_h[r], (BAND, HEAD_DIM))
    lse = lse_ref[...]
    lane = lax.broadcasted_iota(jnp.int32, lse.shape, 1)
    for h in range(hpb):
        o_ref[:, h * HEAD_DIM:(h + 1) * HEAD_DIM] = os_[h]
        lse = jnp.where(lane == hb * hpb + h, ls[h], lse)
    lse_ref[...] = lse


def _band_attn(z, n_seq, t, g):
    d = A_DILATIONS[g]
    width = z.shape[1]
    zv = z.reshape(n_seq, t, width)
    if d == 1:
        qc = 3 * g

        def cur(off):
            return pl.BlockSpec((None, BAND, D_A), lambda n, r, b: (n, b, qc + off))

        def prev(off):
            return pl.BlockSpec((None, BAND, D_A), lambda n, r, b: (n, jnp.maximum(b - 1, 0), qc + off))

        o, lse = pl.pallas_call(
            _band_attn_kernel,
            grid=(n_seq, 1, t // BAND),
            in_specs=[cur(0), prev(1), cur(1), prev(2), cur(2)],
            out_specs=[pl.BlockSpec((None, BAND, D_A), lambda n, r, b: (n, b, 0)),
                       pl.BlockSpec((None, BAND, HEAD_DIM), lambda n, r, b: (n, b, 0))],
            out_shape=[jax.ShapeDtypeStruct((n_seq, t, D_A), F32),
                       jax.ShapeDtypeStruct((n_seq, t, HEAD_DIM), F32)],
            compiler_params=_cp("parallel", "parallel", "arbitrary"),
            name="band_attn_d1",
        )(zv, zv, zv, zv, zv)
        return o.reshape(n_seq * t, D_A), lse.reshape(n_seq * t, HEAD_DIM)

    rows = BAND * d
    nb = t // rows
    use_prev = nb > 1
    hpb = 8 if d <= 4 else 2
    cols = hpb * HEAD_DIM
    cpq = D_A // cols
    qc = 3 * g * cpq

    def cur(off):
        return pl.BlockSpec((None, rows, cols), lambda n, b, hb: (n, b, qc + off * cpq + hb))

    def prev(off):
        return pl.BlockSpec((None, rows, cols),
                            lambda n, b, hb: (n, jnp.maximum(b - 1, 0), qc + off * cpq + hb))

    in_specs = [cur(0), prev(1), cur(1), prev(2), cur(2)] if use_prev else [cur(0), cur(1), cur(2)]
    kv_rows = 2 * rows if use_prev else rows
    o, lse = pl.pallas_call(
        functools.partial(_band_attn_dil_kernel, d=d, hpb=hpb, use_prev=use_prev),
        grid=(n_seq, nb, cpq),
        in_specs=in_specs,
        out_specs=[pl.BlockSpec((None, rows, cols), lambda n, b, hb: (n, b, hb)),
                   pl.BlockSpec((None, rows, HEAD_DIM), lambda n, b, hb: (n, b, 0))],
        out_shape=[jax.ShapeDtypeStruct((n_seq, t, D_A), F32),
                   jax.ShapeDtypeStruct((n_seq, t, HEAD_DIM), F32)],
        scratch_shapes=[pltpu.VMEM((hpb, rows, HEAD_DIM), F32),
                        pltpu.VMEM((hpb, kv_rows, HEAD_DIM), F32),
                        pltpu.VMEM((hpb, kv_rows, HEAD_DIM), F32),
                        pltpu.VMEM((hpb, rows, HEAD_DIM), F32),
                        pltpu.VMEM((hpb, rows, HEAD_DIM), F32)],
        compiler_params=_cp("parallel", "parallel", "arbitrary"),
        name=f"band_attn_d{d}",
    )(*([zv] * len(in_specs)))
    return o.reshape(n_seq * t, D_A), lse.reshape(n_seq * t, HEAD_DIM)


def _combine_kernel(o0_ref, o1_ref, o2_ref, l0_ref, l1_ref, l2_ref, out_ref):
    a0, a1, a2 = l0_ref[...], l1_ref[...], l2_ref[...]
    m = jnp.maximum(jnp.maximum(a0, a1), a2)
    e0, e1, e2 = jnp.exp(a0 - m), jnp.exp(a1 - m), jnp.exp(a2 - m)
    inv = 1.0 / (e0 + e1 + e2)
    w0, w1, w2 = e0 * inv, e1 * inv, e2 * inv
    for h in range(A_HEADS):
        sl = slice(h * HEAD_DIM, (h + 1) * HEAD_DIM)
        out_ref[:, sl] = (w0[:, h:h + 1] * o0_ref[:, sl] + w1[:, h:h + 1] * o1_ref[:, sl]
                          + w2[:, h:h + 1] * o2_ref[:, sl]).astype(out_ref.dtype)


def _combine(os_, lses):
    m = os_[0].shape[0]
    tm = 256
    ob = pl.BlockSpec((tm, D_A), lambda i: (i, 0))
    lb = pl.BlockSpec((tm, HEAD_DIM), lambda i: (i, 0))
    return pl.pallas_call(
        _combine_kernel,
        grid=(m // tm,),
        in_specs=[ob, ob, ob, lb, lb, lb],
        out_specs=ob,
        out_shape=jax.ShapeDtypeStruct((m, D_A), BF16),
        compiler_params=_cp("parallel"),
        name="attn_combine",
    )(*os_, *lses)


def _step_attn_kernel(z_ref, c0_ref, c1_ref, c2_ref, o_ref):
    scale = HEAD_DIM ** -0.5
    ms, ls, accs = [], [], []
    for g, c_ref in enumerate((c0_ref, c1_ref, c2_ref)):
        base = 3 * g * A_HEADS
        q = z_ref[base:base + A_HEADS, :]
        kn = z_ref[base + A_HEADS:base + 2 * A_HEADS, :]
        vn = z_ref[base + 2 * A_HEADS:base + 3 * A_HEADS, :]
        s = jnp.sum(c_ref[:, 0] * q[None], axis=-1, keepdims=True) * scale
        s_new = jnp.sum(kn * q, axis=-1, keepdims=True) * scale
        m = jnp.maximum(jnp.max(s, axis=0), s_new)
        p = jnp.exp(s - m[None])
        p_new = jnp.exp(s_new - m)
        ls.append(jnp.sum(p, axis=0) + p_new)
        accs.append(jnp.sum(p * c_ref[:, 1], axis=0) + p_new * vn)
        ms.append(m)
    mm = jnp.maximum(jnp.maximum(ms[0], ms[1]), ms[2])
    es = [jnp.exp(m - mm) for m in ms]
    den = es[0] * ls[0] + es[1] * ls[1] + es[2] * ls[2]
    num = es[0] * accs[0] + es[1] * accs[1] + es[2] * accs[2]
    o_ref[...] = (num / den).astype(o_ref.dtype)


def _step_attn(zs, caches):
    n, width = zs.shape
    cviews, cspecs = [], []
    for g, c in enumerate(caches):
        d = A_DILATIONS[g]
        cviews.append(c.reshape(n, c.shape[1] // d, d, 2, A_HEADS, HEAD_DIM))
        cspecs.append(pl.BlockSpec((None, BAND, None, 2, A_HEADS, HEAD_DIM),
                                   lambda i: (i, 0, 0, 0, 0, 0)))
    zrows = width // HEAD_DIM
    out = pl.pallas_call(
        _step_attn_kernel,
        grid=(n,),
        in_specs=[pl.BlockSpec((None, zrows, HEAD_DIM), lambda i: (i, 0, 0))] + cspecs,
        out_specs=pl.BlockSpec((None, A_HEADS, HEAD_DIM), lambda i: (i, 0, 0)),
        out_shape=jax.ShapeDtypeStruct((n, A_HEADS, HEAD_DIM), BF16),
        compiler_params=_cp("parallel"),
        name="step_attn",
    )(zs.reshape(n, zrows, HEAD_DIM), *cviews)
    return out.reshape(n, D_A)


def _cache_shift_kernel(c_ref, nxt_ref, new_ref, o_ref, *, lb):
    last = pl.program_id(1) == pl.num_programs(1) - 1
    o_ref[0:lb - 1] = c_ref[1:lb]

    @pl.when(last)
    def _():
        o_ref[lb - 1] = new_ref[0]

    @pl.when(jnp.logical_not(last))
    def _():
        o_ref[lb - 1] = nxt_ref[0]


def _cache_shift(cache, new_kv):
    n, length = cache.shape[:2]
    lb = min(length, 4 * BAND)
    nb = length // lb
    tail = cache.shape[2:]
    zeros = (0,) * len(tail)
    return pl.pallas_call(
        functools.partial(_cache_shift_kernel, lb=lb),
        grid=(n, nb),
        in_specs=[pl.BlockSpec((None, lb) + tail, lambda i, b: (i, b) + zeros),
                  pl.BlockSpec((None, 1) + tail, lambda i, b: (i, jnp.minimum(b + 1, nb - 1) * lb) + zeros),
                  pl.BlockSpec((None, 1) + tail, lambda i, b: (i, 0) + zeros)],
        out_specs=pl.BlockSpec((None, lb) + tail, lambda i, b: (i, b) + zeros),
        out_shape=jax.ShapeDtypeStruct(cache.shape, cache.dtype),
        compiler_params=_cp("parallel", "parallel"),
        name="cache_shift",
    )(cache, cache, new_kv)


def _kv_rows_kernel(k_ref, v_ref, o_ref, *, tm):
    per = 2 * A_HEADS
    for kv, x_ref in enumerate((k_ref, v_ref)):
        for h in range(A_HEADS):
            o_ref[pl.ds(kv * A_HEADS + h, tm, stride=per), :] = x_ref[:, h * HEAD_DIM:(h + 1) * HEAD_DIM]


def _kv_rows(z, n_seq, t, g, length):
    width = z.shape[1]
    zv = z.reshape(n_seq, t, width)
    tm = min(length, 2 * BAND)
    r0 = (t - length) // tm
    c0 = 3 * g + 1
    per = 2 * A_HEADS
    out = pl.pallas_call(
        functools.partial(_kv_rows_kernel, tm=tm),
        grid=(n_seq, length // tm),
        in_specs=[pl.BlockSpec((None, tm, D_A), lambda n, i: (n, r0 + i, c0)),
                  pl.BlockSpec((None, tm, D_A), lambda n, i: (n, r0 + i, c0 + 1))],
        out_specs=pl.BlockSpec((None, tm * per, HEAD_DIM), lambda n, i: (n, i, 0)),
        out_shape=jax.ShapeDtypeStruct((n_seq, length * per, HEAD_DIM), F32),
        compiler_params=_cp("parallel", "parallel"),
        name="kv_rows",
    )(zv, zv)
    return out.reshape(n_seq, length, 2, A_HEADS, HEAD_DIM)


def _ssm_discretise(lre, lim, ldt):
    dt = jnp.exp(ldt)
    mag = jnp.exp(lre * dt)
    ar = mag * jnp.cos(lim * dt)
    ai = mag * jnp.sin(lim * dt)
    den = lre * lre + lim * lim
    zr = ((ar - 1.0) * lre + ai * lim) / den
    zi = (ai * lre - (ar - 1.0) * lim) / den
    return ar, ai, zr, zi


def _slab_mask():
    r = lax.broadcasted_iota(jnp.int32, (SLAB_G * SSM_GROUP, SLAB_S), 0) // SSM_GROUP
    c = lax.broadcasted_iota(jnp.int32, (SLAB_G * SSM_GROUP, SLAB_S), 1) // SSM_STATE
    return (r == c).astype(F32)


def _block_diag(x16, mask):
    return jnp.concatenate([x16] * SLAB_G, axis=0) * mask


def _ssm_prompt_kernel(u_ref, lre_ref, lim_ref, ldt_ref, bre_ref, bim_ref, cre_ref, cim_ref, d_ref,
                       y_ref, hlr_ref, hli_ref,
                       wer_ref, wei_ref, tz_ref, bdr_ref, bdi_ref, er_ref, ei_ref, *, n_seq, n_chunk):
    def split(a):
        ah = a.astype(BF16)
        return ah, (a - ah.astype(F32)).astype(BF16)

    def dot_nt_x3(a, b_split):
        ah, al = split(a)
        bh, bl = b_split
        return _dot_nt(ah, bh) + _dot_nt(ah, bl) + _dot_nt(al, bh)

    rows = n_seq * n_chunk

    def step_rows(j):
        return pl.ds(j, rows, stride=CHUNK)

    ar, ai, zr, zi = _ssm_discretise(lre_ref[...], lim_ref[...], ldt_ref[...])
    mask = _slab_mask()
    bre, bim = bre_ref[...], bim_ref[...]
    cre, cim = cre_ref[...], cim_ref[...]
    cre_bd = split(_block_diag(cre, mask))
    cim_bd = split(_block_diag(cim, mask))
    lanes = SLAB_G * SSM_GROUP
    tz_ref[...] = jnp.zeros(tz_ref.shape, BF16)
    pr, pim = jnp.ones_like(ar), jnp.zeros_like(ar)
    for m in range(CHUNK + 1):
        if m < CHUNK:
            wr = zr * pr - zi * pim
            wi = zr * pim + zi * pr
            we_r = _block_diag(bre * wr - bim * wi, mask)
            we_i = _block_diag(bre * wi + bim * wr, mask)
            j = CHUNK - 1 - m
            wer_ref[j * lanes:(j + 1) * lanes, :] = we_r.astype(BF16)
            wei_ref[j * lanes:(j + 1) * lanes, :] = we_i.astype(BF16)
            tz = (dot_nt_x3(we_r, cre_bd) - dot_nt_x3(we_i, cim_bd)).astype(BF16)
            for j in range(CHUNK - m):
                tz_ref[j * lanes:(j + 1) * lanes, (j + m) * lanes:(j + m + 1) * lanes] = tz
        if m >= 1:
            bdr_ref[(m - 1) * lanes:m * lanes, :] = _block_diag(cre * pr - cim * pim, mask).astype(BF16)
            bdi_ref[(m - 1) * lanes:m * lanes, :] = _block_diag(-(cre * pim + cim * pr), mask).astype(BF16)
        if m == CHUNK:
            a8r, a8i = pr, pim
        pr, pim = pr * ar - pim * ai, pr * ai + pim * ar

    xcat = jnp.concatenate([u_ref[step_rows(j), :].astype(BF16) for j in range(CHUNK)], axis=1)

    er_ref[...] = _dot(xcat, wer_ref[...])
    ei_ref[...] = _dot(xcat, wei_ref[...])

    def body(k, carry):
        out = []
        for n in range(n_seq):
            hr, him = carry[2 * n], carry[2 * n + 1]
            row = n * n_chunk + k
            e_r = er_ref[pl.ds(row, 1), :]
            e_i = ei_ref[pl.ds(row, 1), :]
            er_ref[pl.ds(row, 1), :] = hr
            ei_ref[pl.ds(row, 1), :] = him
            out.append(a8r * hr - a8i * him + e_r)
            out.append(a8r * him + a8i * hr + e_i)
        return tuple(out)

    zero = jnp.zeros((1, SLAB_S), F32)
    fin = lax.fori_loop(0, n_chunk, body, (zero,) * (2 * n_seq))
    for n in range(n_seq):
        hlr_ref[pl.ds(n, 1), :] = fin[2 * n]
        hli_ref[pl.ds(n, 1), :] = fin[2 * n + 1]

    y = (_dot(xcat, tz_ref[...]) + _dot_nt(er_ref[...].astype(BF16), bdr_ref[...])
         + _dot_nt(ei_ref[...].astype(BF16), bdi_ref[...]))
    for jp in range(CHUNK):
        y_ref[step_rows(jp), :] = y[:, jp * lanes:(jp + 1) * lanes] + d_ref[...] * u_ref[step_rows(jp), :]


def _ssm_params(a_re, a_im, b_re, b_im, c_re, c_im, d_skip, log_dt):
    g = a_re.shape[0]
    ns = g // SLAB_G

    def row(x):
        return x.astype(F32).reshape(ns, 1, SLAB_S)

    def chan_rows(x):
        return x.astype(F32).reshape(ns, SLAB_G, SSM_GROUP, SSM_STATE).transpose(0, 2, 1, 3) \
            .reshape(ns, SSM_GROUP, SLAB_S)

    ldt = jnp.broadcast_to(log_dt.astype(F32)[:, None], (g, SSM_STATE))
    return (row(a_re), row(a_im), row(ldt),
            chan_rows(b_re.transpose(0, 2, 1)), chan_rows(b_im.transpose(0, 2, 1)),
            chan_rows(c_re), chan_rows(c_im),
            d_skip.astype(F32).reshape(ns, 1, SLAB_G * SSM_GROUP))


def _ssm_prompt(z, n_seq, t, u_col0, sp):
    m, width = z.shape
    rows = m // CHUNK
    n_chunk = t // CHUNK
    lanes = SLAB_G * SSM_GROUP
    ns = sp[0].shape[0]
    ub = u_col0 // lanes

    def pspec(r, c):
        return pl.BlockSpec((None, r, c), lambda s: (s, 0, 0))

    y, hlr, hli = pl.pallas_call(
        functools.partial(_ssm_prompt_kernel, n_seq=n_seq, n_chunk=n_chunk),
        grid=(ns,),
        in_specs=[pl.BlockSpec((m, lanes), lambda s: (0, ub + s)),
                  pspec(1, SLAB_S), pspec(1, SLAB_S), pspec(1, SLAB_S),
                  pspec(SSM_GROUP, SLAB_S), pspec(SSM_GROUP, SLAB_S),
                  pspec(SSM_GROUP, SLAB_S), pspec(SSM_GROUP, SLAB_S), pspec(1, lanes)],
        out_specs=[pl.BlockSpec((m, lanes), lambda s: (0, s)),
                   pl.BlockSpec((None, n_seq, SLAB_S), lambda s: (s, 0, 0)),
                   pl.BlockSpec((None, n_seq, SLAB_S), lambda s: (s, 0, 0))],
        out_shape=[jax.ShapeDtypeStruct((m, ns * lanes), F32),
                   jax.ShapeDtypeStruct((ns, n_seq, SLAB_S), F32),
                   jax.ShapeDtypeStruct((ns, n_seq, SLAB_S), F32)],
        scratch_shapes=[pltpu.VMEM((CHUNK * lanes, SLAB_S), BF16),
                        pltpu.VMEM((CHUNK * lanes, SLAB_S), BF16),
                        pltpu.VMEM((CHUNK * lanes, CHUNK * lanes), BF16),
                        pltpu.VMEM((CHUNK * lanes, SLAB_S), BF16),
                        pltpu.VMEM((CHUNK * lanes, SLAB_S), BF16),
                        pltpu.VMEM((rows, SLAB_S), F32),
                        pltpu.VMEM((rows, SLAB_S), F32)],
        compiler_params=_cp("parallel"),
        name="ssm_prompt",
    )(z, *sp)
    return y, hlr, hli


def _ssm_step_kernel(u_ref, h0r_ref, h0i_ref, lre_ref, lim_ref, ldt_ref, bre_ref, bim_ref,
                     cre_ref, cim_ref, d_ref, y_ref, hr_ref, hi_ref):
    ar, ai, zr, zi = _ssm_discretise(lre_ref[...], lim_ref[...], ldt_ref[...])
    mask = _slab_mask()
    bre, bim = bre_ref[...], bim_ref[...]
    we_r = _block_diag(bre * zr - bim * zi, mask).astype(BF16)
    we_i = _block_diag(bre * zi + bim * zr, mask).astype(BF16)
    u = u_ref[...]
    ub = u.astype(BF16)
    h0r, h0i = h0r_ref[...], h0i_ref[...]
    hr = ar * h0r - ai * h0i + _dot(ub, we_r)
    him = ar * h0i + ai * h0r + _dot(ub, we_i)
    hr_ref[...] = hr
    hi_ref[...] = him
    cr_bd = _block_diag(cre_ref[...], mask).astype(BF16)
    ci_bd = _block_diag(cim_ref[...], mask).astype(BF16)
    y_ref[...] = (_dot_nt(hr.astype(BF16), cr_bd) - _dot_nt(him.astype(BF16), ci_bd)
                  + d_ref[...] * u)


def _ssm_step(zs, u_col0, h0r, h0i, sp):
    n = zs.shape[0]
    lanes = SLAB_G * SSM_GROUP
    ns = sp[0].shape[0]
    ub = u_col0 // lanes

    def pspec(r, c):
        return pl.BlockSpec((None, r, c), lambda s: (s, 0, 0))

    hspec = pl.BlockSpec((None, n, SLAB_S), lambda s: (s, 0, 0))
    return pl.pallas_call(
        _ssm_step_kernel,
        grid=(ns,),
        in_specs=[pl.BlockSpec((n, lanes), lambda s: (0, ub + s)), hspec, hspec,
                  pspec(1, SLAB_S), pspec(1, SLAB_S), pspec(1, SLAB_S),
                  pspec(SSM_GROUP, SLAB_S), pspec(SSM_GROUP, SLAB_S),
                  pspec(SSM_GROUP, SLAB_S), pspec(SSM_GROUP, SLAB_S), pspec(1, lanes)],
        out_specs=[pl.BlockSpec((n, lanes), lambda s: (0, s)), hspec, hspec],
        out_shape=[jax.ShapeDtypeStruct((n, ns * lanes), F32),
                   jax.ShapeDtypeStruct((ns, n, SLAB_S), F32),
                   jax.ShapeDtypeStruct((ns, n, SLAB_S), F32)],
        compiler_params=_cp("parallel"),
        name="ssm_step",
    )(zs, h0r, h0i, *sp)


def _gelu_tanh(x):
    return 0.5 * x * (1.0 + jnp.tanh(math.sqrt(2.0 / math.pi) * (x + 0.044715 * (x * x * x))))


def _ssm_gate_kernel(y_ref, w_ref, b_ref, o_ref):
    yg = _gelu_tanh(y_ref[...])
    gate = _dot(yg.astype(BF16), w_ref[...]) + b_ref[...]
    o_ref[...] = (yg * _sigmoid(gate)).astype(o_ref.dtype)


def _ssm_gate(y, w, b, tm):
    m, d = y.shape
    return pl.pallas_call(
        _ssm_gate_kernel,
        grid=(m // tm,),
        in_specs=[pl.BlockSpec((tm, d), lambda i: (i, 0)),
                  pl.BlockSpec((d, d), lambda i: (0, 0)),
                  pl.BlockSpec((1, d), lambda i: (0, 0))],
        out_specs=pl.BlockSpec((tm, d), lambda i: (i, 0)),
        out_shape=jax.ShapeDtypeStruct((m, d), BF16),
        compiler_params=_cp("parallel"),
        name="ssm_gate",
    )(y, w, b.reshape(1, d).astype(F32))


def _ssm_gate_step_kernel(y_ref, yc_ref, w_ref, b_ref, o_ref, wb_ref):
    yg = _gelu_tanh(y_ref[...])
    gate = _dot(yg.astype(BF16), _use_w(w_ref, wb_ref)) + b_ref[...]
    o_ref[...] = (_gelu_tanh(yc_ref[...]) * _sigmoid(gate)).astype(o_ref.dtype)


def _ssm_gate_step(y, w, layer, b, tn):
    m, d = y.shape
    return pl.pallas_call(
        _ssm_gate_step_kernel,
        grid=(d // tn,),
        in_specs=[pl.BlockSpec((m, d), lambda j: (0, 0)),
                  pl.BlockSpec((m, tn), lambda j: (0, j)),
                  _wspec(w, layer, d, tn, 0, lambda j: j),
                  pl.BlockSpec((1, tn), lambda j: (0, j))],
        out_specs=[pl.BlockSpec((m, tn), lambda j: (0, j)),
                   pl.BlockSpec((d, tn), lambda j: (0, j))],
        out_shape=[jax.ShapeDtypeStruct((m, d), BF16), jax.ShapeDtypeStruct((d, d), BF16)],
        compiler_params=_cp("arbitrary"),
        name="ssm_gate_step",
    )(y, y, w, b.reshape(1, d).astype(F32))


def _outproj_kernel(a_ref, s_ref, wa_ref, ws_ref, x_ref, o_ref, *wb):
    wab_ref, wsb_ref = wb if wb else (None, None)
    o_ref[...] = (x_ref[...] + _dot(a_ref[...], _use_w(wa_ref, wab_ref))
                  + _dot(s_ref[...], _use_w(ws_ref, wsb_ref)))


def _outproj(attn, ssm, wa, ws, layer, x, tm, tn):
    m, ka = attn.shape
    n = wa.shape[-1]
    emit = layer is not None
    out_specs = [pl.BlockSpec((tm, tn), lambda i, j: (i, j))]
    out_shape = [jax.ShapeDtypeStruct((m, n), F32)]
    if emit:
        out_specs += [pl.BlockSpec((ka, tn), lambda i, j: (0, j))] * 2
        out_shape += [jax.ShapeDtypeStruct((ka, n), BF16)] * 2
    res = pl.pallas_call(
        _outproj_kernel,
        grid=(m // tm, n // tn),
        in_specs=[pl.BlockSpec((tm, ka), lambda i, j: (i, 0)),
                  pl.BlockSpec((tm, ka), lambda i, j: (i, 0)),
                  _wspec(wa, layer, ka, tn, 0, lambda i, j: j),
                  _wspec(ws, layer, ka, tn, 1 if emit else 0, lambda i, j: j),
                  pl.BlockSpec((tm, tn), lambda i, j: (i, j))],
        out_specs=out_specs,
        out_shape=out_shape,
        compiler_params=_cp("parallel", "arbitrary"),
        name="outproj",
    )(attn, ssm, wa, ws, x)
    return res if emit else res[0]


def _mm_res_kernel(a_ref, w_ref, b_ref, x_ref, o_ref, *wb):
    wb_ref = wb[0] if wb else None
    o_ref[...] = x_ref[...] + (_dot(a_ref[...], _use_w(w_ref, wb_ref)) + b_ref[...])


def _mm_res(a, w, layer, b, x, tm, tn, name):
    m, k = a.shape
    n = w.shape[-1]
    emit = layer is not None
    out_specs = [pl.BlockSpec((tm, tn), lambda i, j: (i, j))]
    out_shape = [jax.ShapeDtypeStruct((m, n), F32)]
    if emit:
        out_specs.append(pl.BlockSpec((k, tn), lambda i, j: (0, j)))
        out_shape.append(jax.ShapeDtypeStruct((k, n), BF16))
    res = pl.pallas_call(
        _mm_res_kernel,
        grid=(m // tm, n // tn),
        in_specs=[pl.BlockSpec((tm, k), lambda i, j: (i, 0)),
                  _wspec(w, layer, k, tn, 0, lambda i, j: j),
                  pl.BlockSpec((1, tn), lambda i, j: (0, j)),
                  pl.BlockSpec((tm, tn), lambda i, j: (i, j))],
        out_specs=out_specs,
        out_shape=out_shape,
        compiler_params=_cp("parallel", "arbitrary"),
        name=name,
    )(a, w, b.reshape(1, n).astype(F32), x)
    return res if emit else res[0]


def _pw1_kernel(x_ref, wa_ref, wg_ref, ba_ref, bg_ref, o_ref, *wb):
    wab_ref, wgb_ref = wb if wb else (None, None)
    x = x_ref[...]
    a = _dot(x, _use_w(wa_ref, wab_ref)) + ba_ref[...]
    gate = _dot(x, _use_w(wg_ref, wgb_ref)) + bg_ref[...]
    o_ref[...] = a * _sigmoid(gate)


def _pw1(hn, wa, wg, layer, b, tm, tn):
    m, k = hn.shape
    n = b.shape[0] // 2
    nj = n // tn
    emit = layer is not None
    b2 = b.reshape(1, 2 * n).astype(F32)
    out_specs = [pl.BlockSpec((tm, tn), lambda i, j: (i, j))]
    out_shape = [jax.ShapeDtypeStruct((m, n), F32)]
    if emit:
        out_specs += [pl.BlockSpec((k, tn), lambda i, j: (0, j))] * 2
        out_shape += [jax.ShapeDtypeStruct((k, n), BF16)] * 2
    goff = nj if emit else 0
    res = pl.pallas_call(
        _pw1_kernel,
        grid=(m // tm, nj),
        in_specs=[pl.BlockSpec((tm, k), lambda i, j: (i, 0)),
                  _wspec(wa, layer, k, tn, 0, lambda i, j: j),
                  _wspec(wg, layer, k, tn, 0, lambda i, j: j + goff),
                  pl.BlockSpec((1, tn), lambda i, j: (0, j)),
                  pl.BlockSpec((1, tn), lambda i, j: (0, j + nj))],
        out_specs=out_specs,
        out_shape=out_shape,
        compiler_params=_cp("parallel", "arbitrary"),
        name="conv_pw1_glu",
    )(hn, wa, wg, b2, b2)
    return res if emit else res[0]


def _ln_swish(y, g, b):
    mu = jnp.mean(y, axis=-1, keepdims=True)
    yc = y - mu
    var = jnp.mean(yc * yc, axis=-1, keepdims=True)
    yn = yc * lax.rsqrt(var + LN_EPS) * g + b
    return yn * _sigmoid(yn)


def _conv31_kernel(u_ref, halo_ref, w_ref, b_ref, g_ref, bb_ref, o_ref, ext_ref, y3_ref, y_ref,
                   *, tm, tiles_per_seq, slab):
    seq_start = (pl.program_id(0) % tiles_per_seq) == 0
    nc = u_ref.shape[1] // HEAD_DIM
    for c in range(nc):
        cs = slice(c * HEAD_DIM, (c + 1) * HEAD_DIM)
        ext_ref[pl.ds(c, CONV_HALO, stride=slab), :] = jnp.where(seq_start, 0.0, halo_ref[:, cs])
        ext_ref[pl.ds(CONV_HALO * slab + c, tm, stride=slab), :] = u_ref[:, cs]
    off = CONV_HALO - (CONV_WIDTH - 1)
    rw = 8

    def body(i, carry):
        for ct in range(nc // 8):
            base = pl.multiple_of((i * rw + off) * slab, 8) + ct * 8
            win = [ext_ref[pl.ds(base + s * slab, 8), :] for s in range(rw + CONV_WIDTH - 1)]
            acc = [None] * rw
            for j in range(CONV_WIDTH):
                wj = w_ref[j, ct * 8:(ct + 1) * 8, :]
                for tt in range(rw):
                    term = wj * win[tt + j]
                    acc[tt] = term if j == 0 else acc[tt] + term
            bias = b_ref[ct * 8:(ct + 1) * 8, :]
            obase = pl.multiple_of(i * rw * slab, 8) + ct * 8
            for tt in range(rw):
                y3_ref[pl.ds(obase + tt * slab, 8), :] = acc[tt] + bias
        return carry

    lax.fori_loop(0, tm // rw, body, 0)
    for c in range(nc):
        y_ref[:, c * HEAD_DIM:(c + 1) * HEAD_DIM] = y3_ref[pl.ds(c, tm, stride=slab), :]
    o_ref[...] = _ln_swish(y_ref[...], g_ref[...], bb_ref[...]).astype(o_ref.dtype)


def _conv31(u, t, w, b, g, bb, tm):
    m, d = u.shape
    nc = d // HEAD_DIM
    slab = nc + 8
    row = lambda v: v.reshape(1, d).astype(F32)
    rspec = pl.BlockSpec((1, d), lambda i: (0, 0))
    hb = tm // CONV_HALO
    return pl.pallas_call(
        functools.partial(_conv31_kernel, tm=tm, tiles_per_seq=t // tm, slab=slab),
        grid=(m // tm,),
        in_specs=[pl.BlockSpec((tm, d), lambda i: (i, 0)),
                  pl.BlockSpec((CONV_HALO, d), lambda i: (jnp.maximum(i * hb - 1, 0), 0)),
                  pl.BlockSpec((CONV_WIDTH, nc, HEAD_DIM), lambda i: (0, 0, 0)),
                  pl.BlockSpec((nc, HEAD_DIM), lambda i: (0, 0)), rspec, rspec],
        out_specs=pl.BlockSpec((tm, d), lambda i: (i, 0)),
        out_shape=jax.ShapeDtypeStruct((m, d), BF16),
        scratch_shapes=[pltpu.VMEM(((tm + CONV_HALO) * slab, HEAD_DIM), F32),
                        pltpu.VMEM((tm * slab, HEAD_DIM), F32),
                        pltpu.VMEM((tm, d), F32)],
        compiler_params=_cp("parallel"),
        name="conv31_ln_swish",
    )(u, u, w.astype(F32).reshape(CONV_WIDTH, nc, HEAD_DIM), b.reshape(nc, HEAD_DIM).astype(F32),
      row(g), row(bb))


def _conv31_step_kernel(u_ref, st_ref, w_ref, b_ref, g_ref, bb_ref, o_ref):
    acc = w_ref[0:1, :] * st_ref[0]
    for j in range(1, CONV_WIDTH - 1):
        acc = acc + w_ref[j:j + 1, :] * st_ref[j]
    acc = acc + w_ref[CONV_WIDTH - 1:CONV_WIDTH, :] * u_ref[...] + b_ref[...]
    o_ref[...] = _ln_swish(acc, g_ref[...], bb_ref[...]).astype(o_ref.dtype)


def _conv31_step(u, state_t, w, b, g, bb):
    n, d = u.shape
    row = lambda v: v.reshape(1, d).astype(F32)
    return pl.pallas_call(
        _conv31_step_kernel,
        out_shape=jax.ShapeDtypeStruct((n, d), BF16),
        compiler_params=pltpu.CompilerParams(vmem_limit_bytes=VMEM_LIMIT_V7X),
        name="conv31_step",
    )(u, state_t, w.astype(F32), row(b), row(g), row(bb))


def _ffn_up_kernel(x_ref, w_ref, cwg_ref, cwv_ref, cbg_ref, cbv_ref,
                   act_ref, zlg_ref, zlv_ref, *, t, tn, rows):
    prev_g = prev_v = jnp.zeros((8, tn), F32)
    for r0 in range(0, t, rows):
        x = x_ref[r0:r0 + rows, :]

        def conv(zn, prev, cw_ref, cb_ref):
            z = jnp.concatenate([prev, zn], axis=0)
            zc = (cw_ref[2:3, :] * z + cw_ref[1:2, :] * pltpu.roll(z, 1, 0)
                  + cw_ref[0:1, :] * pltpu.roll(z, 2, 0) + cb_ref[...])
            return zc[8:, :], z[rows:, :]

        gate, prev_g = conv(_dot(x, w_ref[:, :tn]), prev_g, cwg_ref, cbg_ref)
        val, prev_v = conv(_dot(x, w_ref[:, tn:]), prev_v, cwv_ref, cbv_ref)
        act_ref[r0:r0 + rows, :] = (gate * _sigmoid(gate) * val).astype(act_ref.dtype)
    zlg_ref[...] = prev_g
    zlv_ref[...] = prev_v


def _ffn_up(hn, t, w, cw, cb, tn):
    m, k = hn.shape
    dff = w.shape[1] // 2
    nj = dff // tn
    cw = cw.astype(F32)
    cb2 = cb.reshape(1, 2 * dff).astype(F32)
    zl = jax.ShapeDtypeStruct((m // t, 8, dff), F32)
    return pl.pallas_call(
        functools.partial(_ffn_up_kernel, t=t, tn=tn, rows=1024),
        grid=(m // t, nj),
        in_specs=[pl.BlockSpec((t, k), lambda i, j: (i, 0)),
                  pl.BlockSpec((k, 2 * tn), lambda i, j: (0, j)),
                  pl.BlockSpec((3, tn), lambda i, j: (0, j)),
                  pl.BlockSpec((3, tn), lambda i, j: (0, j + nj)),
                  pl.BlockSpec((1, tn), lambda i, j: (0, j)),
                  pl.BlockSpec((1, tn), lambda i, j: (0, j + nj))],
        out_specs=[pl.BlockSpec((t, tn), lambda i, j: (i, j)),
                   pl.BlockSpec((None, 8, tn), lambda i, j: (i, 0, j)),
                   pl.BlockSpec((None, 8, tn), lambda i, j: (i, 0, j))],
        out_shape=[jax.ShapeDtypeStruct((m, dff), BF16), zl, zl],
        compiler_params=_cp("parallel", "arbitrary"),
        name="ffn_up_conv_glu",
    )(hn, w, cw, cw, cb2, cb2)


def _ffn_up_step_kernel(x_ref, wg_ref, wv_ref, sg_ref, sv_ref, cwg_ref, cwv_ref, cbg_ref, cbv_ref,
                        act_ref, zg_ref, zv_ref, wb_ref):
    x = x_ref[...]
    tn = wg_ref.shape[1]

    def half(w_ref, c0, s_ref, cw_ref, cb_ref, z_ref):
        wb = w_ref[...].astype(BF16)
        wb_ref[:, c0:c0 + tn] = wb
        z = _dot(x, wb)
        z_ref[...] = z
        return cw_ref[2:3, :] * z + cw_ref[1:2, :] * s_ref[1] + cw_ref[0:1, :] * s_ref[0] + cb_ref[...]

    gate = half(wg_ref, 0, sg_ref, cwg_ref, cbg_ref, zg_ref)
    val = half(wv_ref, tn, sv_ref, cwv_ref, cbv_ref, zv_ref)
    act_ref[...] = (gate * _sigmoid(gate) * val).astype(act_ref.dtype)


def _ffn_up_step(hn, state_t, w, layer, cw, cb, tn):
    n, k = hn.shape
    dff = w.shape[-1] // 2
    nj = dff // tn
    cw = cw.astype(F32)
    cb2 = cb.reshape(1, 2 * dff).astype(F32)
    zs = jax.ShapeDtypeStruct((n, dff), F32)
    return pl.pallas_call(
        _ffn_up_step_kernel,
        grid=(nj,),
        in_specs=[pl.BlockSpec((n, k), lambda j: (0, 0)),
                  _wspec(w, layer, k, tn, 0, lambda j: j),
                  _wspec(w, layer, k, tn, 0, lambda j: j + nj),
                  pl.BlockSpec((2, n, tn), lambda j: (0, 0, j)),
                  pl.BlockSpec((2, n, tn), lambda j: (0, 0, j + nj)),
                  pl.BlockSpec((3, tn), lambda j: (0, j)),
                  pl.BlockSpec((3, tn), lambda j: (0, j + nj)),
                  pl.BlockSpec((1, tn), lambda j: (0, j)),
                  pl.BlockSpec((1, tn), lambda j: (0, j + nj))],
        out_specs=[pl.BlockSpec((n, tn), lambda j: (0, j)),
                   pl.BlockSpec((n, tn), lambda j: (0, j)),
                   pl.BlockSpec((n, tn), lambda j: (0, j)),
                   pl.BlockSpec((k, 2 * tn), lambda j: (0, j))],
        out_shape=[jax.ShapeDtypeStruct((n, dff), BF16), zs, zs,
                   jax.ShapeDtypeStruct((k, 2 * dff), BF16)],
        compiler_params=_cp("arbitrary"),
        name="ffn_up_step",
    )(hn, w, w, state_t, state_t, cw, cw, cb2, cb2)


def _tile_plan(m):
    return 1024 if m >= 1024 else m


def _ffn_prompt(x, t, norm_g, w_up, cw, cb, w_down, tm):
    hn = _rmsnorm(x, norm_g, BF16)
    act, zlg, zlv = _ffn_up(hn, t, w_up, cw, cb, 256)
    last = jnp.concatenate([zlg[:, 6:8], zlv[:, 6:8]], axis=-1)
    xo = _mm_res(act, w_down, None, jnp.zeros((w_down.shape[1],), F32), x, 512, 512, "ffn_down")
    return xo, last


def _ffn_step(x, prev_state_t, norm_g, w_up, w_down, layer, cw, cb):
    m = x.shape[0]
    hn = _rmsnorm(x, norm_g, BF16)
    act, zg, zv, wu_b = _ffn_up_step(hn, prev_state_t, w_up, layer, cw, cb, 256)
    znew = jnp.concatenate([zg, zv], axis=-1)
    last = jnp.concatenate([prev_state_t[1][:, None], znew[:, None]], axis=1)
    xo, wd_b = _mm_res(act, w_down, layer, jnp.zeros((w_down.shape[-1],), F32), x, m, 256, "ffn_down_step")
    return xo, last, (wu_b, wd_b)


def kernel(x_prompt, x_sample, cache_a0, cache_a1, cache_a2, state_ssm, state_conv, state_ffn, norm_mix, norm_ffn, norm_final, w_in_even, w_out_even, ssm_a_re, ssm_a_im, ssm_b_re, ssm_b_im, ssm_c_re, ssm_c_im, ssm_d, ssm_log_dt, w_glu, b_glu, w_pw1, b_pw1, w_dw, b_dw, ln_g, ln_b, w_pw2, b_pw2, w_up, ffn_dw, ffn_dw_b, w_down):
    n_p, t_p, d_model = x_prompt.shape
    n_s = x_sample.shape[0]
    caches = (cache_a0, cache_a1, cache_a2)
    u_col0 = N_DIL * QKV_W
    sp = _ssm_params(ssm_a_re[0], ssm_a_im[0], ssm_b_re[0], ssm_b_im[0], ssm_c_re[0], ssm_c_im[0],
                     ssm_d[0], ssm_log_dt[0])
    ns = sp[0].shape[0]

    def ssm_state(hr, him, n):
        def unslab(h):
            return h.reshape(ns, n, SLAB_G, SSM_STATE).transpose(1, 0, 2, 3).reshape(n, ns * SLAB_G, SSM_STATE)
        return jnp.stack([unslab(hr), unslab(him)], axis=-1)[None]

    xs = x_sample.reshape(n_s, d_model)
    cos_s, sin_s = _rope_tables(n_s, PAST_LEN, 0)
    hn = _rmsnorm(xs, norm_mix[0], BF16)
    zs, w_in_b = _inproj(hn, w_in_even, 0, cos_s, sin_s, n_s, 512)
    attn_s = _step_attn(zs, [c[0] for c in caches])

    def slab(h):
        return h.reshape(n_s, ns, SLAB_S).transpose(1, 0, 2)

    y_s, hr_s, hi_s = _ssm_step(zs, u_col0, slab(state_ssm[0, ..., 0]), slab(state_ssm[0, ..., 1]), sp)
    gate_s, w_glu_b = _ssm_gate_step(y_s, w_glu, 0, b_glu[0], 512)
    xs, w_oa_b, w_os_b = _outproj(attn_s, gate_s, w_out_even, w_out_even, 0, xs, n_s, 512)
    xs, ffn0_s, wf0 = _ffn_step(xs, state_ffn[0].transpose(1, 0, 2), norm_ffn[0], w_up, w_down, 0,
                                ffn_dw[0], ffn_dw_b[0])
    hn = _rmsnorm(xs, norm_mix[1], BF16)
    us, w_pa_b, w_pg_b = _pw1(hn, w_pw1, w_pw1, 0, b_pw1[0], n_s, 256)
    cs = _conv31_step(us, state_conv[0].transpose(1, 0, 2), w_dw[0], b_dw[0], ln_g[0], ln_b[0])
    conv_s = jnp.concatenate([state_conv[0][:, 1:], us[:, None]], axis=1)
    xs, w_pw2_b = _mm_res(cs, w_pw2, 0, b_pw2[0], xs, n_s, 512, "conv_pw2_step")
    xs, ffn1_s, wf1 = _ffn_step(xs, state_ffn[1].transpose(1, 0, 2), norm_ffn[1], w_up, w_down, 1,
                                ffn_dw[1], ffn_dw_b[1])
    y_sample = _rmsnorm(xs, norm_final, F32).reshape(n_s, 1, d_model)

    a_s = []
    for g in range(N_DIL):
        kv = zs[:, g * QKV_W + D_A:(g + 1) * QKV_W].reshape(n_s, 1, 2, A_HEADS, HEAD_DIM)
        a_s.append(_cache_shift(caches[g][0], kv)[None])
    ssm_s = ssm_state(hr_s, hi_s, n_s)
    conv_s = conv_s[None]
    ffn_s = jnp.stack([ffn0_s, ffn1_s])

    mp = n_p * t_p
    tm = _tile_plan(mp)
    x = x_prompt.reshape(mp, d_model)
    cos, sin = _rope_tables(t_p, 0, 1)
    hn = _rmsnorm(x, norm_mix[0], BF16)
    z = _inproj(hn, w_in_b, None, cos, sin, tm, 1024)
    outs, lses = [], []
    for g in range(N_DIL):
        o, lse = _band_attn(z, n_p, t_p, g)
        outs.append(o)
        lses.append(lse)
    attn = _combine(outs, lses)
    y_ssm, hlr, hli = _ssm_prompt(z, n_p, t_p, u_col0, sp)
    gate_p = _ssm_gate(y_ssm, w_glu_b, b_glu[0], 512)
    x = _outproj(attn, gate_p, w_oa_b, w_os_b, None, x, tm, 1024)
    x, ffn0_p = _ffn_prompt(x, t_p, norm_ffn[0], wf0[0], ffn_dw[0], ffn_dw_b[0], wf0[1], tm)
    hn = _rmsnorm(x, norm_mix[1], BF16)
    u = _pw1(hn, w_pa_b, w_pg_b, None, b_pw1[0], tm, 512)
    c = _conv31(u, t_p, w_dw[0], b_dw[0], ln_g[0], ln_b[0], 256)
    conv_p = u.reshape(n_p, t_p, -1)[None, :, t_p - (CONV_WIDTH - 1):]
    x = _mm_res(c, w_pw2_b, None, b_pw2[0], x, tm, 1024, "conv_pw2")
    x, ffn1_p = _ffn_prompt(x, t_p, norm_ffn[1], wf1[0], ffn_dw[1], ffn_dw_b[1], wf1[1], tm)
    y_prompt = _rmsnorm(x, norm_final, F32).reshape(n_p, t_p, d_model)

    a_p = []
    for g in range(N_DIL):
        a_p.append(_kv_rows(z, n_p, t_p, g, min(A_WINDOWS[g], PAST_LEN))[None])
    ssm_p = ssm_state(hlr, hli, n_p)
    ffn_p = jnp.stack([ffn0_p, ffn1_p])

    return (y_prompt, y_sample, a_p[0], a_s[0], a_p[1], a_s[1], a_p[2], a_s[2],
            ssm_p, ssm_s, conv_p, conv_s, ffn_p, ffn_s)
```

```python
import functools
import math

import jax
import jax.numpy as jnp
from jax import lax
from jax.experimental import pallas as pl
from jax.experimental.pallas import tpu as pltpu

F32 = jnp.float32
BF16 = jnp.bfloat16

HEAD_DIM = 128
A_HEADS = 16
D_A = A_HEADS * HEAD_DIM
A_WINDOWS = (128, 512, 2048)
A_DILATIONS = (1, 4, 16)
BAND = 128
N_DIL = 3
QKV_W = 3 * D_A
ROPE_THETA = 10000.0
PAST_LEN = 16384
SSM_GROUP = 16
SSM_STATE = 64
SLAB_G = 8
SLAB_S = SLAB_G * SSM_STATE
CHUNK = 8
CONV_WIDTH = 31
CONV_HALO = 32
RMS_EPS = 1e-6
LN_EPS = 1e-5
NEG_BIG = -1e30
VMEM_LIMIT_V7X = 56 * 1024 * 1024


def _cp(*sem):
    return pltpu.CompilerParams(dimension_semantics=sem, vmem_limit_bytes=VMEM_LIMIT_V7X)


def _dot(a, b, **kw):
    return jnp.dot(a, b, preferred_element_type=F32, **kw)


def _dot_nt(a, b, **kw):
    return lax.dot_general(a, b, (((1,), (1,)), ((), ())), preferred_element_type=F32, **kw)


def _sigmoid(x):
    return 1.0 / (1.0 + jnp.exp(-x))


def _rmsnorm_kernel(x_ref, g_ref, o_ref):
    x = x_ref[...]
    ms = jnp.mean(x * x, axis=-1, keepdims=True)
    o_ref[...] = (x * lax.rsqrt(ms + RMS_EPS) * g_ref[...]).astype(o_ref.dtype)


def _rmsnorm(x, g, out_dtype):
    m, d = x.shape
    tm = min(m, 256)
    return pl.pallas_call(
        _rmsnorm_kernel,
        grid=(m // tm,),
        in_specs=[pl.BlockSpec((tm, d), lambda i: (i, 0)),
                  pl.BlockSpec((1, d), lambda i: (0, 0))],
        out_specs=pl.BlockSpec((tm, d), lambda i: (i, 0)),
        out_shape=jax.ShapeDtypeStruct((m, d), out_dtype),
        compiler_params=_cp("parallel"),
        name="rmsnorm",
    )(x, g.reshape(1, d))


def _rope_table_kernel(invf_ref, cos_ref, sin_ref, *, pos0, step):
    rows = cos_ref.shape[0]
    row = lax.broadcasted_iota(jnp.int32, (rows, HEAD_DIM), 0)
    lane = lax.broadcasted_iota(jnp.int32, (rows, HEAD_DIM), 1)
    pos = (row * step + pos0).astype(F32)
    ang = pos * invf_ref[...]
    cos_ref[...] = jnp.cos(ang)
    s = jnp.sin(ang)
    sin_ref[...] = jnp.where(lane < HEAD_DIM // 2, -s, s)


def _rope_tables(rows, pos0, step):
    half = HEAD_DIM // 2
    inv_freq = ROPE_THETA ** (-jnp.arange(half, dtype=F32) / half)
    invf = jnp.concatenate([inv_freq, inv_freq]).reshape(1, HEAD_DIM)
    shp = jax.ShapeDtypeStruct((rows, HEAD_DIM), F32)
    return pl.pallas_call(
        functools.partial(_rope_table_kernel, pos0=pos0, step=step),
        out_shape=(shp, shp),
        name="rope_tables",
    )(invf)


def _wspec(w, layer, kb, tn, kblk, jmap):
    if layer is None:
        return pl.BlockSpec((kb, tn), lambda *g: (kblk, jmap(*g)))
    return pl.BlockSpec((None, kb, tn), lambda *g: (layer, kblk, jmap(*g)))


def _use_w(w_ref, wb_ref, cols=slice(None)):
    w = w_ref[:, cols]
    if wb_ref is None:
        return w
    wb = w.astype(BF16)
    wb_ref[:, cols] = wb
    return wb


def _inproj_kernel(x_ref, w_ref, cos_ref, sin_ref, o_ref, *wb, tn, part):
    wb_ref = wb[0] if wb else None
    seg = (pl.program_id(1) * tn) // D_A
    is_rope = jnp.logical_and(seg < 3 * N_DIL, seg % 3 != 2)
    cos = cos_ref[...]
    sin = sin_ref[...]
    x = x_ref[...]
    for c0 in range(0, tn, part):
        acc = _dot(x, _use_w(w_ref, wb_ref, slice(c0, c0 + part)))
        for h in range(part // HEAD_DIM):
            xh = acc[:, h * HEAD_DIM:(h + 1) * HEAD_DIM]
            roped = xh * cos + pltpu.roll(xh, HEAD_DIM // 2, 1) * sin
            o_ref[:, c0 + h * HEAD_DIM:c0 + (h + 1) * HEAD_DIM] = jnp.where(is_rope, roped, xh)


def _inproj(hn, w, layer, cos, sin, tm, tn):
    m, k = hn.shape
    n = w.shape[-1]
    ct = cos.shape[0] // tm
    emit = layer is not None
    out_specs = [pl.BlockSpec((tm, tn), lambda i, j: (i, j))]
    out_shape = [jax.ShapeDtypeStruct((m, n), F32)]
    if emit:
        out_specs.append(pl.BlockSpec((k, tn), lambda i, j: (0, j)))
        out_shape.append(jax.ShapeDtypeStruct((k, n), BF16))
    res = pl.pallas_call(
        functools.partial(_inproj_kernel, tn=tn, part=min(tn, 256)),
        grid=(m // tm, n // tn),
        in_specs=[pl.BlockSpec((tm, k), lambda i, j: (i, 0)),
                  _wspec(w, layer, k, tn, 0, lambda i, j: j),
                  pl.BlockSpec((tm, HEAD_DIM), lambda i, j: (i % ct, 0)),
                  pl.BlockSpec((tm, HEAD_DIM), lambda i, j: (i % ct, 0))],
        out_specs=out_specs,
        out_shape=out_shape,
        compiler_params=_cp("parallel", "arbitrary"),
        name="inproj_rope",
    )(hn, w, cos, sin)
    return res if emit else res[0]


def _band_attn_kernel(q_ref, kp_ref, kc_ref, vp_ref, vc_ref, o_ref, lse_ref):
    has_prev = pl.program_id(2) > 0
    qi = lax.broadcasted_iota(jnp.int32, (BAND, 2 * BAND), 0)
    ki = lax.broadcasted_iota(jnp.int32, (BAND, 2 * BAND), 1)
    valid = jnp.logical_and(jnp.logical_and(ki >= qi, ki <= qi + BAND),
                            jnp.logical_or(ki >= BAND, has_prev))
    lane = lax.broadcasted_iota(jnp.int32, (BAND, HEAD_DIM), 1)
    lse_tile = jnp.zeros((BAND, HEAD_DIM), F32)
    scale = HEAD_DIM ** -0.5
    for h in range(A_HEADS):
        sl = slice(h * HEAD_DIM, (h + 1) * HEAD_DIM)
        q = q_ref[:, sl].astype(BF16)
        k = jnp.concatenate([kp_ref[:, sl], kc_ref[:, sl]], axis=0).astype(BF16)
        v = jnp.concatenate([vp_ref[:, sl], vc_ref[:, sl]], axis=0).astype(BF16)
        s = _dot_nt(q, k) * scale
        s = jnp.where(valid, s, NEG_BIG)
        m = jnp.max(s, axis=-1, keepdims=True)
        p = jnp.exp(s - m)
        l = jnp.sum(p, axis=-1, keepdims=True)
        o_ref[:, sl] = _dot(p.astype(BF16), v) / l
        lse_tile = jnp.where(lane == h, m + jnp.log(l), lse_tile)
    lse_ref[...] = lse_tile


def _band_attn_dil_kernel(*refs, d, hpb, use_prev):
    if use_prev:
        q_ref, kp_ref, kc_ref, vp_ref, vc_ref, o_ref, lse_ref, qs, ks, vs, os_, ls = refs
    else:
        q_ref, kc_ref, vc_ref, o_ref, lse_ref, qs, ks, vs, os_, ls = refs
    rows = BAND * d
    hb = pl.program_id(2)
    has_prev = pl.program_id(1) > 0
    for h in range(hpb):
        sl = slice(h * HEAD_DIM, (h + 1) * HEAD_DIM)
        qs[h] = q_ref[:, sl]
        if use_prev:
            ks[h, 0:rows] = kp_ref[:, sl]
            ks[h, rows:2 * rows] = kc_ref[:, sl]
            vs[h, 0:rows] = vp_ref[:, sl]
            vs[h, rows:2 * rows] = vc_ref[:, sl]
        else:
            ks[h] = kc_ref[:, sl]
            vs[h] = vc_ref[:, sl]

    @pl.when(hb == 0)
    def _():
        lse_ref[...] = jnp.zeros(lse_ref.shape, F32)

    nk = 2 * BAND if use_prev else BAND
    qi = lax.broadcasted_iota(jnp.int32, (1, BAND, nk), 1)
    ki = lax.broadcasted_iota(jnp.int32, (1, BAND, nk), 2)
    if use_prev:
        valid = jnp.logical_and(jnp.logical_and(ki >= qi, ki <= qi + BAND),
                                jnp.logical_or(ki >= BAND, has_prev))
    else:
        valid = ki <= qi
    scale = HEAD_DIM ** -0.5

    def residues(ref, h, base):
        return [ref[h, pl.ds(base + r, BAND, stride=d), :] for r in range(d)]

    for h in range(hpb):
        q = jnp.stack(residues(qs, h, 0)).astype(BF16)
        if use_prev:
            k = jnp.stack([jnp.concatenate([a, b], axis=0)
                           for a, b in zip(residues(ks, h, 0), residues(ks, h, rows))]).astype(BF16)
            v = jnp.stack([jnp.concatenate([a, b], axis=0)
                           for a, b in zip(residues(vs, h, 0), residues(vs, h, rows))]).astype(BF16)
        else:
            k = jnp.stack(residues(ks, h, 0)).astype(BF16)
            v = jnp.stack(residues(vs, h, 0)).astype(BF16)
        s = jnp.einsum('rqd,rkd->rqk', q, k, preferred_element_type=F32) * scale
        s = jnp.where(valid, s, NEG_BIG)
        m = jnp.max(s, axis=-1, keepdims=True)
        p = jnp.exp(s - m)
        l = jnp.sum(p, axis=-1, keepdims=True)
        o = jnp.einsum('rqk,rkd->rqd', p.astype(BF16), v, preferred_element_type=F32) / l
        lse_h = m + jnp.log(l)
        for r in range(d):
            os_[h, pl.ds(r, BAND, stride=d), :] = o[r]
            ls[h, pl.ds(r, BAND, stride=d), :] = jnp.broadcast_to(lse_h[r], (BAND, HEAD_DIM))
    lse = lse_ref[...]
    lane = lax.broadcasted_iota(jnp.int32, lse.shape, 1)
    for h in range(hpb):
        o_ref[:, h * HEAD_DIM:(h + 1) * HEAD_DIM] = os_[h]
        lse = jnp.where(lane == hb * hpb + h, ls[h], lse)
    lse_ref[...] = lse


def _band_attn(z, n_seq, t, g):
    d = A_DILATIONS[g]
    width = z.shape[1]
    zv = z.reshape(n_seq, t, width)
    if d == 1:
        qc = 3 * g

        def cur(off):
            return pl.BlockSpec((None, BAND, D_A), lambda n, r, b: (n, b, qc + off))

        def prev(off):
            return pl.BlockSpec((None, BAND, D_A), lambda n, r, b: (n, jnp.maximum(b - 1, 0), qc + off))

        o, lse = pl.pallas_call(
            _band_attn_kernel,
            grid=(n_seq, 1, t // BAND),
            in_specs=[cur(0), prev(1), cur(1), prev(2), cur(2)],
            out_specs=[pl.BlockSpec((None, BAND, D_A), lambda n, r, b: (n, b, 0)),
                       pl.BlockSpec((None, BAND, HEAD_DIM), lambda n, r, b: (n, b, 0))],
            out_shape=[jax.ShapeDtypeStruct((n_seq, t, D_A), F32),
                       jax.ShapeDtypeStruct((n_seq, t, HEAD_DIM), F32)],
            compiler_params=_cp("parallel", "parallel", "arbitrary"),
            name="band_attn_d1",
        )(zv, zv, zv, zv, zv)
        return o.reshape(n_seq * t, D_A), lse.reshape(n_seq * t, HEAD_DIM)

    rows = BAND * d
    nb = t // rows
    use_prev = nb > 1
    hpb = 8 if d <= 4 else 2
    cols = hpb * HEAD_DIM
    cpq = D_A // cols
    qc = 3 * g * cpq

    def cur(off):
        return pl.BlockSpec((None, rows, cols), lambda n, b, hb: (n, b, qc + off * cpq + hb))

    def prev(off):
        return pl.BlockSpec((None, rows, cols),
                            lambda n, b, hb: (n, jnp.maximum(b - 1, 0), qc + off * cpq + hb))

    in_specs = [cur(0), prev(1), cur(1), prev(2), cur(2)] if use_prev else [cur(0), cur(1), cur(2)]
    kv_rows = 2 * rows if use_prev else rows
    o, lse = pl.pallas_call(
        functools.partial(_band_attn_dil_kernel, d=d, hpb=hpb, use_prev=use_prev),
        grid=(n_seq, nb, cpq),
        in_specs=in_specs,
        out_specs=[pl.BlockSpec((None, rows, cols), lambda n, b, hb: (n, b, hb)),
                   pl.BlockSpec((None, rows, HEAD_DIM), lambda n, b, hb: (n, b, 0))],
        out_shape=[jax.ShapeDtypeStruct((n_seq, t, D_A), F32),
                   jax.ShapeDtypeStruct((n_seq, t, HEAD_DIM), F32)],
        scratch_shapes=[pltpu.VMEM((hpb, rows, HEAD_DIM), F32),
                        pltpu.VMEM((hpb, kv_rows, HEAD_DIM), F32),
                        pltpu.VMEM((hpb, kv_rows, HEAD_DIM), F32),
                        pltpu.VMEM((hpb, rows, HEAD_DIM), F32),
                        pltpu.VMEM((hpb, rows, HEAD_DIM), F32)],
        compiler_params=_cp("parallel", "parallel", "arbitrary"),
        name=f"band_attn_d{d}",
    )(*([zv] * len(in_specs)))
    return o.reshape(n_seq * t, D_A), lse.reshape(n_seq * t, HEAD_DIM)


def _combine_kernel(o0_ref, o1_ref, o2_ref, l0_ref, l1_ref, l2_ref, out_ref):
    a0, a1, a2 = l0_ref[...], l1_ref[...], l2_ref[...]
    m = jnp.maximum(jnp.maximum(a0, a1), a2)
    e0, e1, e2 = jnp.exp(a0 - m), jnp.exp(a1 - m), jnp.exp(a2 - m)
    inv = 1.0 / (e0 + e1 + e2)
    w0, w1, w2 = e0 * inv, e1 * inv, e2 * inv
    for h in range(A_HEADS):
        sl = slice(h * HEAD_DIM, (h + 1) * HEAD_DIM)
        out_ref[:, sl] = (w0[:, h:h + 1] * o0_ref[:, sl] + w1[:, h:h + 1] * o1_ref[:, sl]
                          + w2[:, h:h + 1] * o2_ref[:, sl]).astype(out_ref.dtype)


def _combine(os_, lses):
    m = os_[0].shape[0]
    tm = 256
    ob = pl.BlockSpec((tm, D_A), lambda i: (i, 0))
    lb = pl.BlockSpec((tm, HEAD_DIM), lambda i: (i, 0))
    return pl.pallas_call(
        _combine_kernel,
        grid=(m // tm,),
        in_specs=[ob, ob, ob, lb, lb, lb],
        out_specs=ob,
        out_shape=jax.ShapeDtypeStruct((m, D_A), BF16),
        compiler_params=_cp("parallel"),
        name="attn_combine",
    )(*os_, *lses)


def _step_attn_kernel(z_ref, c0_ref, c1_ref, c2_ref, o_ref):
    scale = HEAD_DIM ** -0.5
    ms, ls, accs = [], [], []
    for g, c_ref in enumerate((c0_ref, c1_ref, c2_ref)):
        base = 3 * g * A_HEADS
        q = z_ref[base:base + A_HEADS, :]
        kn = z_ref[base + A_HEADS:base + 2 * A_HEADS, :]
        vn = z_ref[base + 2 * A_HEADS:base + 3 * A_HEADS, :]
        s = jnp.sum(c_ref[:, 0] * q[None], axis=-1, keepdims=True) * scale
        s_new = jnp.sum(kn * q, axis=-1, keepdims=True) * scale
        m = jnp.maximum(jnp.max(s, axis=0), s_new)
        p = jnp.exp(s - m[None])
        p_new = jnp.exp(s_new - m)
        ls.append(jnp.sum(p, axis=0) + p_new)
        accs.append(jnp.sum(p * c_ref[:, 1], axis=0) + p_new * vn)
        ms.append(m)
    mm = jnp.maximum(jnp.maximum(ms[0], ms[1]), ms[2])
    es = [jnp.exp(m - mm) for m in ms]
    den = es[0] * ls[0] + es[1] * ls[1] + es[2] * ls[2]
    num = es[0] * accs[0] + es[1] * accs[1] + es[2] * accs[2]
    o_ref[...] = (num / den).astype(o_ref.dtype)


def _step_attn(zs, caches):
    n, width = zs.shape
    cviews, cspecs = [], []
    for g, c in enumerate(caches):
        d = A_DILATIONS[g]
        cviews.append(c.reshape(n, c.shape[1] // d, d, 2, A_HEADS, HEAD_DIM))
        cspecs.append(pl.BlockSpec((None, BAND, None, 2, A_HEADS, HEAD_DIM),
                                   lambda i: (i, 0, 0, 0, 0, 0)))
    zrows = width // HEAD_DIM
    out = pl.pallas_call(
        _step_attn_kernel,
        grid=(n,),
        in_specs=[pl.BlockSpec((None, zrows, HEAD_DIM), lambda i: (i, 0, 0))] + cspecs,
        out_specs=pl.BlockSpec((None, A_HEADS, HEAD_DIM), lambda i: (i, 0, 0)),
        out_shape=jax.ShapeDtypeStruct((n, A_HEADS, HEAD_DIM), BF16),
        compiler_params=_cp("parallel"),
        name="step_attn",
    )(zs.reshape(n, zrows, HEAD_DIM), *cviews)
    return out.reshape(n, D_A)


def _cache_shift_kernel(c_ref, nxt_ref, new_ref, o_ref, *, lb):
    last = pl.program_id(1) == pl.num_programs(1) - 1
    o_ref[0:lb - 1] = c_ref[1:lb]

    @pl.when(last)
    def _():
        o_ref[lb - 1] = new_ref[0]

    @pl.when(jnp.logical_not(last))
    def _():
        o_ref[lb - 1] = nxt_ref[0]


def _cache_shift(cache, new_kv):
    n, length = cache.shape[:2]
    lb = min(length, 4 * BAND)
    nb = length // lb
    tail = cache.shape[2:]
    zeros = (0,) * len(tail)
    return pl.pallas_call(
        functools.partial(_cache_shift_kernel, lb=lb),
        grid=(n, nb),
        in_specs=[pl.BlockSpec((None, lb) + tail, lambda i, b: (i, b) + zeros),
                  pl.BlockSpec((None, 1) + tail, lambda i, b: (i, jnp.minimum(b + 1, nb - 1) * lb) + zeros),
                  pl.BlockSpec((None, 1) + tail, lambda i, b: (i, 0) + zeros)],
        out_specs=pl.BlockSpec((None, lb) + tail, lambda i, b: (i, b) + zeros),
        out_shape=jax.ShapeDtypeStruct(cache.shape, cache.dtype),
        compiler_params=_cp("parallel", "parallel"),
        name="cache_shift",
    )(cache, cache, new_kv)


def _kv_rows_kernel(k_ref, v_ref, o_ref, *, tm):
    per = 2 * A_HEADS
    for kv, x_ref in enumerate((k_ref, v_ref)):
        for h in range(A_HEADS):
            o_ref[pl.ds(kv * A_HEADS + h, tm, stride=per), :] = x_ref[:, h * HEAD_DIM:(h + 1) * HEAD_DIM]


def _kv_rows(z, n_seq, t, g, length):
    width = z.shape[1]
    zv = z.reshape(n_seq, t, width)
    tm = min(length, 2 * BAND)
    r0 = (t - length) // tm
    c0 = 3 * g + 1
    per = 2 * A_HEADS
    out = pl.pallas_call(
        functools.partial(_kv_rows_kernel, tm=tm),
        grid=(n_seq, length // tm),
        in_specs=[pl.BlockSpec((None, tm, D_A), lambda n, i: (n, r0 + i, c0)),
                  pl.BlockSpec((None, tm, D_A), lambda n, i: (n, r0 + i, c0 + 1))],
        out_specs=pl.BlockSpec((None, tm * per, HEAD_DIM), lambda n, i: (n, i, 0)),
        out_shape=jax.ShapeDtypeStruct((n_seq, length * per, HEAD_DIM), F32),
        compiler_params=_cp("parallel", "parallel"),
        name="kv_rows",
    )(zv, zv)
    return out.reshape(n_seq, length, 2, A_HEADS, HEAD_DIM)


def _ssm_discretise(lre, lim, ldt):
    dt = jnp.exp(ldt)
    mag = jnp.exp(lre * dt)
    ar = mag * jnp.cos(lim * dt)
    ai = mag * jnp.sin(lim * dt)
    den = lre * lre + lim * lim
    zr = ((ar - 1.0) * lre + ai * lim) / den
    zi = (ai * lre - (ar - 1.0) * lim) / den
    return ar, ai, zr, zi


def _slab_mask():
    r = lax.broadcasted_iota(jnp.int32, (SLAB_G * SSM_GROUP, SLAB_S), 0) // SSM_GROUP
    c = lax.broadcasted_iota(jnp.int32, (SLAB_G * SSM_GROUP, SLAB_S), 1) // SSM_STATE
    return (r == c).astype(F32)


def _block_diag(x16, mask):
    return jnp.concatenate([x16] * SLAB_G, axis=0) * mask


def _ssm_prompt_kernel(u_ref, lre_ref, lim_ref, ldt_ref, bre_ref, bim_ref, cre_ref, cim_ref, d_ref,
                       y_ref, hlr_ref, hli_ref,
                       wer_ref, wei_ref, tz_ref, bdr_ref, bdi_ref, er_ref, ei_ref, *, n_seq, n_chunk):
    def split(a):
        ah = a.astype(BF16)
        return ah, (a - ah.astype(F32)).astype(BF16)

    def dot_nt_x3(a, b_split):
        ah, al = split(a)
        bh, bl = b_split
        return _dot_nt(ah, bh) + _dot_nt(ah, bl) + _dot_nt(al, bh)

    rows = n_seq * n_chunk

    def step_rows(j):
        return pl.ds(j, rows, stride=CHUNK)

    ar, ai, zr, zi = _ssm_discretise(lre_ref[...], lim_ref[...], ldt_ref[...])
    mask = _slab_mask()
    bre, bim = bre_ref[...], bim_ref[...]
    cre, cim = cre_ref[...], cim_ref[...]
    cre_bd = split(_block_diag(cre, mask))
    cim_bd = split(_block_diag(cim, mask))
    lanes = SLAB_G * SSM_GROUP
    tz_ref[...] = jnp.zeros(tz_ref.shape, BF16)
    pr, pim = jnp.ones_like(ar), jnp.zeros_like(ar)
    for m in range(CHUNK + 1):
        if m < CHUNK:
            wr = zr * pr - zi * pim
            wi = zr * pim + zi * pr
            we_r = _block_diag(bre * wr - bim * wi, mask)
            we_i = _block_diag(bre * wi + bim * wr, mask)
            j = CHUNK - 1 - m
            wer_ref[j * lanes:(j + 1) * lanes, :] = we_r.astype(BF16)
            wei_ref[j * lanes:(j + 1) * lanes, :] = we_i.astype(BF16)
            tz = (dot_nt_x3(we_r, cre_bd) - dot_nt_x3(we_i, cim_bd)).astype(BF16)
            for j in range(CHUNK - m):
                tz_ref[j * lanes:(j + 1) * lanes, (j + m) * lanes:(j + m + 1) * lanes] = tz
        if m >= 1:
            bdr_ref[(m - 1) * lanes:m * lanes, :] = _block_diag(cre * pr - cim * pim, mask).astype(BF16)
            bdi_ref[(m - 1) * lanes:m * lanes, :] = _block_diag(-(cre * pim + cim * pr), mask).astype(BF16)
        if m == CHUNK:
            a8r, a8i = pr, pim
        pr, pim = pr * ar - pim * ai, pr * ai + pim * ar

    xcat = jnp.concatenate([u_ref[step_rows(j), :].astype(BF16) for j in range(CHUNK)], axis=1)

    er_ref[...] = _dot(xcat, wer_ref[...])
    ei_ref[...] = _dot(xcat, wei_ref[...])

    def body(k, carry):
        out = []
        for n in range(n_seq):
            hr, him = carry[2 * n], carry[2 * n + 1]
            row = n * n_chunk + k
            e_r = er_ref[pl.ds(row, 1), :]
            e_i = ei_ref[pl.ds(row, 1), :]
            er_ref[pl.ds(row, 1), :] = hr
            ei_ref[pl.ds(row, 1), :] = him
            out.append(a8r * hr - a8i * him + e_r)
            out.append(a8r * him + a8i * hr + e_i)
        return tuple(out)

    zero = jnp.zeros((1, SLAB_S), F32)
    fin = lax.fori_loop(0, n_chunk, body, (zero,) * (2 * n_seq))
    for n in range(n_seq):
        hlr_ref[pl.ds(n, 1), :] = fin[2 * n]
        hli_ref[pl.ds(n, 1), :] = fin[2 * n + 1]

    y = (_dot(xcat, tz_ref[...]) + _dot_nt(er_ref[...].astype(BF16), bdr_ref[...])
         + _dot_nt(ei_ref[...].astype(BF16), bdi_ref[...]))
    for jp in range(CHUNK):
        y_ref[step_rows(jp), :] = y[:, jp * lanes:(jp + 1) * lanes] + d_ref[...] * u_ref[step_rows(jp), :]


def _ssm_params(a_re, a_im, b_re, b_im, c_re, c_im, d_skip, log_dt):
    g = a_re.shape[0]
    ns = g // SLAB_G

    def row(x):
        return x.astype(F32).reshape(ns, 1, SLAB_S)

    def chan_rows(x):
        return x.astype(F32).reshape(ns, SLAB_G, SSM_GROUP, SSM_STATE).transpose(0, 2, 1, 3) \
            .reshape(ns, SSM_GROUP, SLAB_S)

    ldt = jnp.broadcast_to(log_dt.astype(F32)[:, None], (g, SSM_STATE))
    return (row(a_re), row(a_im), row(ldt),
            chan_rows(b_re.transpose(0, 2, 1)), chan_rows(b_im.transpose(0, 2, 1)),
            chan_rows(c_re), chan_rows(c_im),
            d_skip.astype(F32).reshape(ns, 1, SLAB_G * SSM_GROUP))


def _ssm_prompt(z, n_seq, t, u_col0, sp):
    m, width = z.shape
    rows = m // CHUNK
    n_chunk = t // CHUNK
    lanes = SLAB_G * SSM_GROUP
    ns = sp[0].shape[0]
    ub = u_col0 // lanes

    def pspec(r, c):
        return pl.BlockSpec((None, r, c), lambda s: (s, 0, 0))

    y, hlr, hli = pl.pallas_call(
        functools.partial(_ssm_prompt_kernel, n_seq=n_seq, n_chunk=n_chunk),
        grid=(ns,),
        in_specs=[pl.BlockSpec((m, lanes), lambda s: (0, ub + s)),
                  pspec(1, SLAB_S), pspec(1, SLAB_S), pspec(1, SLAB_S),
                  pspec(SSM_GROUP, SLAB_S), pspec(SSM_GROUP, SLAB_S),
                  pspec(SSM_GROUP, SLAB_S), pspec(SSM_GROUP, SLAB_S), pspec(1, lanes)],
        out_specs=[pl.BlockSpec((m, lanes), lambda s: (0, s)),
                   pl.BlockSpec((None, n_seq, SLAB_S), lambda s: (s, 0, 0)),
                   pl.BlockSpec((None, n_seq, SLAB_S), lambda s: (s, 0, 0))],
        out_shape=[jax.ShapeDtypeStruct((m, ns * lanes), F32),
                   jax.ShapeDtypeStruct((ns, n_seq, SLAB_S), F32),
                   jax.ShapeDtypeStruct((ns, n_seq, SLAB_S), F32)],
        scratch_shapes=[pltpu.VMEM((CHUNK * lanes, SLAB_S), BF16),
                        pltpu.VMEM((CHUNK * lanes, SLAB_S), BF16),
                        pltpu.VMEM((CHUNK * lanes, CHUNK * lanes), BF16),
                        pltpu.VMEM((CHUNK * lanes, SLAB_S), BF16),
                        pltpu.VMEM((CHUNK * lanes, SLAB_S), BF16),
                        pltpu.VMEM((rows, SLAB_S), F32),
                        pltpu.VMEM((rows, SLAB_S), F32)],
        compiler_params=_cp("parallel"),
        name="ssm_prompt",
    )(z, *sp)
    return y, hlr, hli


def _ssm_step_kernel(u_ref, h0r_ref, h0i_ref, lre_ref, lim_ref, ldt_ref, bre_ref, bim_ref,
                     cre_ref, cim_ref, d_ref, y_ref, hr_ref, hi_ref):
    ar, ai, zr, zi = _ssm_discretise(lre_ref[...], lim_ref[...], ldt_ref[...])
    mask = _slab_mask()
    bre, bim = bre_ref[...], bim_ref[...]
    we_r = _block_diag(bre * zr - bim * zi, mask).astype(BF16)
    we_i = _block_diag(bre * zi + bim * zr, mask).astype(BF16)
    u = u_ref[...]
    ub = u.astype(BF16)
    h0r, h0i = h0r_ref[...], h0i_ref[...]
    hr = ar * h0r - ai * h0i + _dot(ub, we_r)
    him = ar * h0i + ai * h0r + _dot(ub, we_i)
    hr_ref[...] = hr
    hi_ref[...] = him
    cr_bd = _block_diag(cre_ref[...], mask).astype(BF16)
    ci_bd = _block_diag(cim_ref[...], mask).astype(BF16)
    y_ref[...] = (_dot_nt(hr.astype(BF16), cr_bd) - _dot_nt(him.astype(BF16), ci_bd)
                  + d_ref[...] * u)


def _ssm_step(zs, u_col0, h0r, h0i, sp):
    n = zs.shape[0]
    lanes = SLAB_G * SSM_GROUP
    ns = sp[0].shape[0]
    ub = u_col0 // lanes

    def pspec(r, c):
        return pl.BlockSpec((None, r, c), lambda s: (s, 0, 0))

    hspec = pl.BlockSpec((None, n, SLAB_S), lambda s: (s, 0, 0))
    return pl.pallas_call(
        _ssm_step_kernel,
        grid=(ns,),
        in_specs=[pl.BlockSpec((n, lanes), lambda s: (0, ub + s)), hspec, hspec,
                  pspec(1, SLAB_S), pspec(1, SLAB_S), pspec(1, SLAB_S),
                  pspec(SSM_GROUP, SLAB_S), pspec(SSM_GROUP, SLAB_S),
                  pspec(SSM_GROUP, SLAB_S), pspec(SSM_GROUP, SLAB_S), pspec(1, lanes)],
        out_specs=[pl.BlockSpec((n, lanes), lambda s: (0, s)), hspec, hspec],
        out_shape=[jax.ShapeDtypeStruct((n, ns * lanes), F32),
                   jax.ShapeDtypeStruct((ns, n, SLAB_S), F32),
                   jax.ShapeDtypeStruct((ns, n, SLAB_S), F32)],
        compiler_params=_cp("parallel"),
        name="ssm_step",
    )(zs, h0r, h0i, *sp)


def _gelu_tanh(x):
    return 0.5 * x * (1.0 + jnp.tanh(math.sqrt(2.0 / math.pi) * (x + 0.044715 * (x * x * x))))


def _ssm_gate_kernel(y_ref, w_ref, b_ref, o_ref):
    yg = _gelu_tanh(y_ref[...])
    gate = _dot(yg.astype(BF16), w_ref[...]) + b_ref[...]
    o_ref[...] = (yg * _sigmoid(gate)).astype(o_ref.dtype)


def _ssm_gate(y, w, b, tm):
    m, d = y.shape
    return pl.pallas_call(
        _ssm_gate_kernel,
        grid=(m // tm,),
        in_specs=[pl.BlockSpec((tm, d), lambda i: (i, 0)),
                  pl.BlockSpec((d, d), lambda i: (0, 0)),
                  pl.BlockSpec((1, d), lambda i: (0, 0))],
        out_specs=pl.BlockSpec((tm, d), lambda i: (i, 0)),
        out_shape=jax.ShapeDtypeStruct((m, d), BF16),
        compiler_params=_cp("parallel"),
        name="ssm_gate",
    )(y, w, b.reshape(1, d).astype(F32))


def _ssm_gate_step_kernel(y_ref, yc_ref, w_ref, b_ref, o_ref, wb_ref):
    yg = _gelu_tanh(y_ref[...])
    gate = _dot(yg.astype(BF16), _use_w(w_ref, wb_ref)) + b_ref[...]
    o_ref[...] = (_gelu_tanh(yc_ref[...]) * _sigmoid(gate)).astype(o_ref.dtype)


def _ssm_gate_step(y, w, layer, b, tn):
    m, d = y.shape
    return pl.pallas_call(
        _ssm_gate_step_kernel,
        grid=(d // tn,),
        in_specs=[pl.BlockSpec((m, d), lambda j: (0, 0)),
                  pl.BlockSpec((m, tn), lambda j: (0, j)),
                  _wspec(w, layer, d, tn, 0, lambda j: j),
                  pl.BlockSpec((1, tn), lambda j: (0, j))],
        out_specs=[pl.BlockSpec((m, tn), lambda j: (0, j)),
                   pl.BlockSpec((d, tn), lambda j: (0, j))],
        out_shape=[jax.ShapeDtypeStruct((m, d), BF16), jax.ShapeDtypeStruct((d, d), BF16)],
        compiler_params=_cp("arbitrary"),
        name="ssm_gate_step",
    )(y, y, w, b.reshape(1, d).astype(F32))


def _outproj_kernel(a_ref, s_ref, wa_ref, ws_ref, x_ref, o_ref, *wb):
    wab_ref, wsb_ref = wb if wb else (None, None)
    o_ref[...] = (x_ref[...] + _dot(a_ref[...], _use_w(wa_ref, wab_ref))
                  + _dot(s_ref[...], _use_w(ws_ref, wsb_ref)))


def _outproj(attn, ssm, wa, ws, layer, x, tm, tn):
    m, ka = attn.shape
    n = wa.shape[-1]
    emit = layer is not None
    out_specs = [pl.BlockSpec((tm, tn), lambda i, j: (i, j))]
    out_shape = [jax.ShapeDtypeStruct((m, n), F32)]
    if emit:
        out_specs += [pl.BlockSpec((ka, tn), lambda i, j: (0, j))] * 2
        out_shape += [jax.ShapeDtypeStruct((ka, n), BF16)] * 2
    res = pl.pallas_call(
        _outproj_kernel,
        grid=(m // tm, n // tn),
        in_specs=[pl.BlockSpec((tm, ka), lambda i, j: (i, 0)),
                  pl.BlockSpec((tm, ka), lambda i, j: (i, 0)),
                  _wspec(wa, layer, ka, tn, 0, lambda i, j: j),
                  _wspec(ws, layer, ka, tn, 1 if emit else 0, lambda i, j: j),
                  pl.BlockSpec((tm, tn), lambda i, j: (i, j))],
        out_specs=out_specs,
        out_shape=out_shape,
        compiler_params=_cp("parallel", "arbitrary"),
        name="outproj",
    )(attn, ssm, wa, ws, x)
    return res if emit else res[0]


def _mm_res_kernel(a_ref, w_ref, b_ref, x_ref, o_ref, *wb):
    wb_ref = wb[0] if wb else None
    o_ref[...] = x_ref[...] + (_dot(a_ref[...], _use_w(w_ref, wb_ref)) + b_ref[...])


def _mm_res(a, w, layer, b, x, tm, tn, name):
    m, k = a.shape
    n = w.shape[-1]
    emit = layer is not None
    out_specs = [pl.BlockSpec((tm, tn), lambda i, j: (i, j))]
    out_shape = [jax.ShapeDtypeStruct((m, n), F32)]
    if emit:
        out_specs.append(pl.BlockSpec((k, tn), lambda i, j: (0, j)))
        out_shape.append(jax.ShapeDtypeStruct((k, n), BF16))
    res = pl.pallas_call(
        _mm_res_kernel,
        grid=(m // tm, n // tn),
        in_specs=[pl.BlockSpec((tm, k), lambda i, j: (i, 0)),
                  _wspec(w, layer, k, tn, 0, lambda i, j: j),
                  pl.BlockSpec((1, tn), lambda i, j: (0, j)),
                  pl.BlockSpec((tm, tn), lambda i, j: (i, j))],
        out_specs=out_specs,
        out_shape=out_shape,
        compiler_params=_cp("parallel", "arbitrary"),
        name=name,
    )(a, w, b.reshape(1, n).astype(F32), x)
    return res if emit else res[0]


def _pw1_kernel(x_ref, wa_ref, wg_ref, ba_ref, bg_ref, o_ref, *wb):
    wab_ref, wgb_ref = wb if wb else (None, None)
    x = x_ref[...]
    a = _dot(x, _use_w(wa_ref, wab_ref)) + ba_ref[...]
    gate = _dot(x, _use_w(wg_ref, wgb_ref)) + bg_ref[...]
    o_ref[...] = a * _sigmoid(gate)


def _pw1(hn, wa, wg, layer, b, tm, tn):
    m, k = hn.shape
    n = b.shape[0] // 2
    nj = n // tn
    emit = layer is not None
    b2 = b.reshape(1, 2 * n).astype(F32)
    out_specs = [pl.BlockSpec((tm, tn), lambda i, j: (i, j))]
    out_shape = [jax.ShapeDtypeStruct((m, n), F32)]
    if emit:
        out_specs += [pl.BlockSpec((k, tn), lambda i, j: (0, j))] * 2
        out_shape += [jax.ShapeDtypeStruct((k, n), BF16)] * 2
    goff = nj if emit else 0
    res = pl.pallas_call(
        _pw1_kernel,
        grid=(m // tm, nj),
        in_specs=[pl.BlockSpec((tm, k), lambda i, j: (i, 0)),
                  _wspec(wa, layer, k, tn, 0, lambda i, j: j),
                  _wspec(wg, layer, k, tn, 0, lambda i, j: j + goff),
                  pl.BlockSpec((1, tn), lambda i, j: (0, j)),
                  pl.BlockSpec((1, tn), lambda i, j: (0, j + nj))],
        out_specs=out_specs,
        out_shape=out_shape,
        compiler_params=_cp("parallel", "arbitrary"),
        name="conv_pw1_glu",
    )(hn, wa, wg, b2, b2)
    return res if emit else res[0]


def _ln_swish(y, g, b):
    mu = jnp.mean(y, axis=-1, keepdims=True)
    yc = y - mu
    var = jnp.mean(yc * yc, axis=-1, keepdims=True)
    yn = yc * lax.rsqrt(var + LN_EPS) * g + b
    return yn * _sigmoid(yn)


def _conv31_kernel(u_ref, halo_ref, w_ref, b_ref, g_ref, bb_ref, o_ref, ext_ref, y3_ref, y_ref,
                   *, tm, tiles_per_seq, slab):
    seq_start = (pl.program_id(0) % tiles_per_seq) == 0
    nc = u_ref.shape[1] // HEAD_DIM
    for c in range(nc):
        cs = slice(c * HEAD_DIM, (c + 1) * HEAD_DIM)
        ext_ref[pl.ds(c, CONV_HALO, stride=slab), :] = jnp.where(seq_start, 0.0, halo_ref[:, cs])
        ext_ref[pl.ds(CONV_HALO * slab + c, tm, stride=slab), :] = u_ref[:, cs]
    off = CONV_HALO - (CONV_WIDTH - 1)
    rw = 8

    def body(i, carry):
        for ct in range(nc // 8):
            base = pl.multiple_of((i * rw + off) * slab, 8) + ct * 8
            win = [ext_ref[pl.ds(base + s * slab, 8), :] for s in range(rw + CONV_WIDTH - 1)]
            acc = [None] * rw
            for j in range(CONV_WIDTH):
                wj = w_ref[j, ct * 8:(ct + 1) * 8, :]
                for tt in range(rw):
                    term = wj * win[tt + j]
                    acc[tt] = term if j == 0 else acc[tt] + term
            bias = b_ref[ct * 8:(ct + 1) * 8, :]
            obase = pl.multiple_of(i * rw * slab, 8) + ct * 8
            for tt in range(rw):
                y3_ref[pl.ds(obase + tt * slab, 8), :] = acc[tt] + bias
        return carry

    lax.fori_loop(0, tm // rw, body, 0)
    for c in range(nc):
        y_ref[:, c * HEAD_DIM:(c + 1) * HEAD_DIM] = y3_ref[pl.ds(c, tm, stride=slab), :]
    o_ref[...] = _ln_swish(y_ref[...], g_ref[...], bb_ref[...]).astype(o_ref.dtype)


def _conv31(u, t, w, b, g, bb, tm):
    m, d = u.shape
    nc = d // HEAD_DIM
    slab = nc + 8
    row = lambda v: v.reshape(1, d).astype(F32)
    rspec = pl.BlockSpec((1, d), lambda i: (0, 0))
    hb = tm // CONV_HALO
    return pl.pallas_call(
        functools.partial(_conv31_kernel, tm=tm, tiles_per_seq=t // tm, slab=slab),
        grid=(m // tm,),
        in_specs=[pl.BlockSpec((tm, d), lambda i: (i, 0)),
                  pl.BlockSpec((CONV_HALO, d), lambda i: (jnp.maximum(i * hb - 1, 0), 0)),
                  pl.BlockSpec((CONV_WIDTH, nc, HEAD_DIM), lambda i: (0, 0, 0)),
                  pl.BlockSpec((nc, HEAD_DIM), lambda i: (0, 0)), rspec, rspec],
        out_specs=pl.BlockSpec((tm, d), lambda i: (i, 0)),
        out_shape=jax.ShapeDtypeStruct((m, d), BF16),
        scratch_shapes=[pltpu.VMEM(((tm + CONV_HALO) * slab, HEAD_DIM), F32),
                        pltpu.VMEM((tm * slab, HEAD_DIM), F32),
                        pltpu.VMEM((tm, d), F32)],
        compiler_params=_cp("parallel"),
        name="conv31_ln_swish",
    )(u, u, w.astype(F32).reshape(CONV_WIDTH, nc, HEAD_DIM), b.reshape(nc, HEAD_DIM).astype(F32),
      row(g), row(bb))


def _conv31_step_kernel(u_ref, st_ref, w_ref, b_ref, g_ref, bb_ref, o_ref):
    acc = w_ref[0:1, :] * st_ref[0]
    for j in range(1, CONV_WIDTH - 1):
        acc = acc + w_ref[j:j + 1, :] * st_ref[j]
    acc = acc + w_ref[CONV_WIDTH - 1:CONV_WIDTH, :] * u_ref[...] + b_ref[...]
    o_ref[...] = _ln_swish(acc, g_ref[...], bb_ref[...]).astype(o_ref.dtype)


def _conv31_step(u, state_t, w, b, g, bb):
    n, d = u.shape
    row = lambda v: v.reshape(1, d).astype(F32)
    return pl.pallas_call(
        _conv31_step_kernel,
        out_shape=jax.ShapeDtypeStruct((n, d), BF16),
        compiler_params=pltpu.CompilerParams(vmem_limit_bytes=VMEM_LIMIT_V7X),
        name="conv31_step",
    )(u, state_t, w.astype(F32), row(b), row(g), row(bb))


def _ffn_up_kernel(x_ref, w_ref, cwg_ref, cwv_ref, cbg_ref, cbv_ref,
                   act_ref, zlg_ref, zlv_ref, z_scr, *, t, tn, rows):
    nc = tn // HEAD_DIM
    cw = (cwg_ref, cwv_ref)
    cb = (cbg_ref, cbv_ref)
    hist = [jnp.zeros((8, HEAD_DIM), F32)] * (2 * nc)
    for p, r0 in enumerate(range(0, t, rows)):
        z = _dot(x_ref[r0:r0 + rows, :], w_ref[...])
        zc = []
        for c in range(2 * nc):
            zs = z_scr.at[p, c]
            zn = z[:, c * HEAD_DIM:(c + 1) * HEAD_DIM]
            zs[0:8, :] = hist[c]
            zs[8:, :] = zn
            hist[c] = zn[rows - 8:, :]
            half, cc = divmod(c, nc)
            ls = slice(cc * HEAD_DIM, (cc + 1) * HEAD_DIM)
            zc.append(cw[half][2:3, ls] * zn + cw[half][1:2, ls] * zs[pl.ds(7, rows, stride=1), :]
                      + cw[half][0:1, ls] * zs[pl.ds(6, rows, stride=1), :] + cb[half][:, ls])
        for c in range(nc):
            gate = zc[c]
            act_ref[r0:r0 + rows, c * HEAD_DIM:(c + 1) * HEAD_DIM] = (
                gate * _sigmoid(gate) * zc[nc + c]).astype(act_ref.dtype)
    for c in range(nc):
        zlg_ref[:, c * HEAD_DIM:(c + 1) * HEAD_DIM] = hist[c]
        zlv_ref[:, c * HEAD_DIM:(c + 1) * HEAD_DIM] = hist[nc + c]


def _ffn_up(hn, t, w, cw, cb, tn):
    m, k = hn.shape
    dff = w.shape[1] // 2
    nj = dff // tn
    cw = cw.astype(F32)
    cb2 = cb.reshape(1, 2 * dff).astype(F32)
    zl = jax.ShapeDtypeStruct((m // t, 8, dff), F32)
    rows = min(t, 1024)
    return pl.pallas_call(
        functools.partial(_ffn_up_kernel, t=t, tn=tn, rows=rows),
        grid=(m // t, nj),
        in_specs=[pl.BlockSpec((t, k), lambda i, j: (i, 0)),
                  pl.BlockSpec((k, 2 * tn), lambda i, j: (0, j)),
                  pl.BlockSpec((3, tn), lambda i, j: (0, j)),
                  pl.BlockSpec((3, tn), lambda i, j: (0, j + nj)),
                  pl.BlockSpec((1, tn), lambda i, j: (0, j)),
                  pl.BlockSpec((1, tn), lambda i, j: (0, j + nj))],
        out_specs=[pl.BlockSpec((t, tn), lambda i, j: (i, j)),
                   pl.BlockSpec((None, 8, tn), lambda i, j: (i, 0, j)),
                   pl.BlockSpec((None, 8, tn), lambda i, j: (i, 0, j))],
        out_shape=[jax.ShapeDtypeStruct((m, dff), BF16), zl, zl],
        scratch_shapes=[pltpu.VMEM((t // rows, 2 * tn // HEAD_DIM, 8 + rows, HEAD_DIM), F32)],
        compiler_params=_cp("parallel", "arbitrary"),
        name="ffn_up_conv_glu",
    )(hn, w, cw, cw, cb2, cb2)


def _ffn_up_step_kernel(x_ref, wg_ref, wv_ref, sg_ref, sv_ref, cwg_ref, cwv_ref, cbg_ref, cbv_ref,
                        act_ref, zg_ref, zv_ref, wb_ref):
    x = x_ref[...]
    tn = wg_ref.shape[1]

    def half(w_ref, c0, s_ref, cw_ref, cb_ref, z_ref):
        wb = w_ref[...].astype(BF16)
        wb_ref[:, c0:c0 + tn] = wb
        z = _dot(x, wb)
        z_ref[...] = z
        return cw_ref[2:3, :] * z + cw_ref[1:2, :] * s_ref[1] + cw_ref[0:1, :] * s_ref[0] + cb_ref[...]

    gate = half(wg_ref, 0, sg_ref, cwg_ref, cbg_ref, zg_ref)
    val = half(wv_ref, tn, sv_ref, cwv_ref, cbv_ref, zv_ref)
    act_ref[...] = (gate * _sigmoid(gate) * val).astype(act_ref.dtype)


def _ffn_up_step(hn, state_t, w, layer, cw, cb, tn):
    n, k = hn.shape
    dff = w.shape[-1] // 2
    nj = dff // tn
    cw = cw.astype(F32)
    cb2 = cb.reshape(1, 2 * dff).astype(F32)
    zs = jax.ShapeDtypeStruct((n, dff), F32)
    return pl.pallas_call(
        _ffn_up_step_kernel,
        grid=(nj,),
        in_specs=[pl.BlockSpec((n, k), lambda j: (0, 0)),
                  _wspec(w, layer, k, tn, 0, lambda j: j),
                  _wspec(w, layer, k, tn, 0, lambda j: j + nj),
                  pl.BlockSpec((2, n, tn), lambda j: (0, 0, j)),
                  pl.BlockSpec((2, n, tn), lambda j: (0, 0, j + nj)),
                  pl.BlockSpec((3, tn), lambda j: (0, j)),
                  pl.BlockSpec((3, tn), lambda j: (0, j + nj)),
                  pl.BlockSpec((1, tn), lambda j: (0, j)),
                  pl.BlockSpec((1, tn), lambda j: (0, j + nj))],
        out_specs=[pl.BlockSpec((n, tn), lambda j: (0, j)),
                   pl.BlockSpec((n, tn), lambda j: (0, j)),
                   pl.BlockSpec((n, tn), lambda j: (0, j)),
                   pl.BlockSpec((k, 2 * tn), lambda j: (0, j))],
        out_shape=[jax.ShapeDtypeStruct((n, dff), BF16), zs, zs,
                   jax.ShapeDtypeStruct((k, 2 * dff), BF16)],
        compiler_params=_cp("arbitrary"),
        name="ffn_up_step",
    )(hn, w, w, state_t, state_t, cw, cw, cb2, cb2)


def _tile_plan(m):
    return 1024 if m >= 1024 else m


def _ffn_prompt(x, t, norm_g, w_up, cw, cb, w_down, tm):
    hn = _rmsnorm(x, norm_g, BF16)
    act, zlg, zlv = _ffn_up(hn, t, w_up, cw, cb, 256)
    last = jnp.concatenate([zlg[:, 6:8], zlv[:, 6:8]], axis=-1)
    xo = _mm_res(act, w_down, None, jnp.zeros((w_down.shape[1],), F32), x, 512, 512, "ffn_down")
    return xo, last


def _ffn_step(x, prev_state_t, norm_g, w_up, w_down, layer, cw, cb):
    m = x.shape[0]
    hn = _rmsnorm(x, norm_g, BF16)
    act, zg, zv, wu_b = _ffn_up_step(hn, prev_state_t, w_up, layer, cw, cb, 256)
    znew = jnp.concatenate([zg, zv], axis=-1)
    last = jnp.concatenate([prev_state_t[1][:, None], znew[:, None]], axis=1)
    xo, wd_b = _mm_res(act, w_down, layer, jnp.zeros((w_down.shape[-1],), F32), x, m, 256, "ffn_down_step")
    return xo, last, (wu_b, wd_b)


def kernel(x_prompt, x_sample, cache_a0, cache_a1, cache_a2, state_ssm, state_conv, state_ffn, norm_mix, norm_ffn, norm_final, w_in_even, w_out_even, ssm_a_re, ssm_a_im, ssm_b_re, ssm_b_im, ssm_c_re, ssm_c_im, ssm_d, ssm_log_dt, w_glu, b_glu, w_pw1, b_pw1, w_dw, b_dw, ln_g, ln_b, w_pw2, b_pw2, w_up, ffn_dw, ffn_dw_b, w_down):
    n_p, t_p, d_model = x_prompt.shape
    n_s = x_sample.shape[0]
    caches = (cache_a0, cache_a1, cache_a2)
    u_col0 = N_DIL * QKV_W
    sp = _ssm_params(ssm_a_re[0], ssm_a_im[0], ssm_b_re[0], ssm_b_im[0], ssm_c_re[0], ssm_c_im[0],
                     ssm_d[0], ssm_log_dt[0])
    ns = sp[0].shape[0]

    def ssm_state(hr, him, n):
        def unslab(h):
            return h.reshape(ns, n, SLAB_G, SSM_STATE).transpose(1, 0, 2, 3).reshape(n, ns * SLAB_G, SSM_STATE)
        return jnp.stack([unslab(hr), unslab(him)], axis=-1)[None]

    xs = x_sample.reshape(n_s, d_model)
    cos_s, sin_s = _rope_tables(n_s, PAST_LEN, 0)
    hn = _rmsnorm(xs, norm_mix[0], BF16)
    zs, w_in_b = _inproj(hn, w_in_even, 0, cos_s, sin_s, n_s, 512)
    attn_s = _step_attn(zs, [c[0] for c in caches])

    def slab(h):
        return h.reshape(n_s, ns, SLAB_S).transpose(1, 0, 2)

    y_s, hr_s, hi_s = _ssm_step(zs, u_col0, slab(state_ssm[0, ..., 0]), slab(state_ssm[0, ..., 1]), sp)
    gate_s, w_glu_b = _ssm_gate_step(y_s, w_glu, 0, b_glu[0], 512)
    xs, w_oa_b, w_os_b = _outproj(attn_s, gate_s, w_out_even, w_out_even, 0, xs, n_s, 512)
    xs, ffn0_s, wf0 = _ffn_step(xs, state_ffn[0].transpose(1, 0, 2), norm_ffn[0], w_up, w_down, 0,
                                ffn_dw[0], ffn_dw_b[0])
    hn = _rmsnorm(xs, norm_mix[1], BF16)
    us, w_pa_b, w_pg_b = _pw1(hn, w_pw1, w_pw1, 0, b_pw1[0], n_s, 256)
    cs = _conv31_step(us, state_conv[0].transpose(1, 0, 2), w_dw[0], b_dw[0], ln_g[0], ln_b[0])
    conv_s = jnp.concatenate([state_conv[0][:, 1:], us[:, None]], axis=1)
    xs, w_pw2_b = _mm_res(cs, w_pw2, 0, b_pw2[0], xs, n_s, 512, "conv_pw2_step")
    xs, ffn1_s, wf1 = _ffn_step(xs, state_ffn[1].transpose(1, 0, 2), norm_ffn[1], w_up, w_down, 1,
                                ffn_dw[1], ffn_dw_b[1])
    y_sample = _rmsnorm(xs, norm_final, F32).reshape(n_s, 1, d_model)

    a_s = []
    for g in range(N_DIL):
        kv = zs[:, g * QKV_W + D_A:(g + 1) * QKV_W].reshape(n_s, 1, 2, A_HEADS, HEAD_DIM)
        a_s.append(_cache_shift(caches[g][0], kv)[None])
    ssm_s = ssm_state(hr_s, hi_s, n_s)
    conv_s = conv_s[None]
    ffn_s = jnp.stack([ffn0_s, ffn1_s])

    mp = n_p * t_p
    tm = _tile_plan(mp)
    x = x_prompt.reshape(mp, d_model)
    cos, sin = _rope_tables(t_p, 0, 1)
    hn = _rmsnorm(x, norm_mix[0], BF16)
    z = _inproj(hn, w_in_b, None, cos, sin, tm, 1024)
    outs, lses = [], []
    for g in range(N_DIL):
        o, lse = _band_attn(z, n_p, t_p, g)
        outs.append(o)
        lses.append(lse)
    attn = _combine(outs, lses)
    y_ssm, hlr, hli = _ssm_prompt(z, n_p, t_p, u_col0, sp)
    gate_p = _ssm_gate(y_ssm, w_glu_b, b_glu[0], 512)
    x = _outproj(attn, gate_p, w_oa_b, w_os_b, None, x, tm, 1024)
    x, ffn0_p = _ffn_prompt(x, t_p, norm_ffn[0], wf0[0], ffn_dw[0], ffn_dw_b[0], wf0[1], tm)
    hn = _rmsnorm(x, norm_mix[1], BF16)
    u = _pw1(hn, w_pa_b, w_pg_b, None, b_pw1[0], tm, 512)
    c = _conv31(u, t_p, w_dw[0], b_dw[0], ln_g[0], ln_b[0], 256)
    conv_p = u.reshape(n_p, t_p, -1)[None, :, t_p - (CONV_WIDTH - 1):]
    x = _mm_res(c, w_pw2_b, None, b_pw2[0], x, tm, 1024, "conv_pw2")
    x, ffn1_p = _ffn_prompt(x, t_p, norm_ffn[1], wf1[0], ffn_dw[1], ffn_dw_b[1], wf1[1], tm)
    y_prompt = _rmsnorm(x, norm_final, F32).reshape(n_p, t_p, d_model)

    a_p = []
    for g in range(N_DIL):
        a_p.append(_kv_rows(z, n_p, t_p, g, min(A_WINDOWS[g], PAST_LEN))[None])
    ssm_p = ssm_state(hlr, hli, n_p)
    ffn_p = jnp.stack([ffn0_p, ffn1_p])

    return (y_prompt, y_sample, a_p[0], a_s[0], a_p[1], a_s[1], a_p[2], a_s[2],
            ssm_p, ssm_s, conv_p, conv_s, ffn_p, ffn_s)
```

```python
import functools
import math

import jax
import jax.numpy as jnp
from jax import lax
from jax.experimental import pallas as pl
from jax.experimental.pallas import tpu as pltpu

F32 = jnp.float32
BF16 = jnp.bfloat16

HEAD_DIM = 128
A_HEADS = 16
D_A = A_HEADS * HEAD_DIM
A_WINDOWS = (128, 512, 2048)
A_DILATIONS = (1, 4, 16)
BAND = 128
N_DIL = 3
QKV_W = 3 * D_A
ROPE_THETA = 10000.0
PAST_LEN = 16384
SSM_GROUP = 16
SSM_STATE = 64
SLAB_G = 8
SLAB_S = SLAB_G * SSM_STATE
CHUNK = 8
CONV_WIDTH = 31
CONV_HALO = 32
RMS_EPS = 1e-6
LN_EPS = 1e-5
NEG_BIG = -1e30
VMEM_LIMIT_V7X = 56 * 1024 * 1024


def _cp(*sem):
    return pltpu.CompilerParams(dimension_semantics=sem, vmem_limit_bytes=VMEM_LIMIT_V7X)


def _dot(a, b, **kw):
    return jnp.dot(a, b, preferred_element_type=F32, **kw)


def _dot_nt(a, b, **kw):
    return lax.dot_general(a, b, (((1,), (1,)), ((), ())), preferred_element_type=F32, **kw)


def _sigmoid(x):
    return 1.0 / (1.0 + jnp.exp(-x))


def _rmsnorm_kernel(x_ref, g_ref, o_ref):
    x = x_ref[...]
    ms = jnp.mean(x * x, axis=-1, keepdims=True)
    o_ref[...] = (x * lax.rsqrt(ms + RMS_EPS) * g_ref[...]).astype(o_ref.dtype)


def _rmsnorm(x, g, out_dtype):
    m, d = x.shape
    tm = min(m, 256)
    return pl.pallas_call(
        _rmsnorm_kernel,
        grid=(m // tm,),
        in_specs=[pl.BlockSpec((tm, d), lambda i: (i, 0)),
                  pl.BlockSpec((1, d), lambda i: (0, 0))],
        out_specs=pl.BlockSpec((tm, d), lambda i: (i, 0)),
        out_shape=jax.ShapeDtypeStruct((m, d), out_dtype),
        compiler_params=_cp("parallel"),
        name="rmsnorm",
    )(x, g.reshape(1, d))


def _rope_table_kernel(invf_ref, cos_ref, sin_ref, *, pos0, step):
    rows = cos_ref.shape[0]
    row = lax.broadcasted_iota(jnp.int32, (rows, HEAD_DIM), 0)
    lane = lax.broadcasted_iota(jnp.int32, (rows, HEAD_DIM), 1)
    pos = (row * step + pos0).astype(F32)
    ang = pos * invf_ref[...]
    cos_ref[...] = jnp.cos(ang)
    s = jnp.sin(ang)
    sin_ref[...] = jnp.where(lane < HEAD_DIM // 2, -s, s)


def _rope_tables(rows, pos0, step):
    half = HEAD_DIM // 2
    inv_freq = ROPE_THETA ** (-jnp.arange(half, dtype=F32) / half)
    invf = jnp.concatenate([inv_freq, inv_freq]).reshape(1, HEAD_DIM)
    shp = jax.ShapeDtypeStruct((rows, HEAD_DIM), F32)
    return pl.pallas_call(
        functools.partial(_rope_table_kernel, pos0=pos0, step=step),
        out_shape=(shp, shp),
        name="rope_tables",
    )(invf)


def _wspec(w, layer, kb, tn, kblk, jmap):
    if layer is None:
        return pl.BlockSpec((kb, tn), lambda *g: (kblk, jmap(*g)))
    return pl.BlockSpec((None, kb, tn), lambda *g: (layer, kblk, jmap(*g)))


def _use_w(w_ref, wb_ref, cols=slice(None), gain_ref=None):
    w = w_ref[:, cols]
    if wb_ref is None:
        return w
    wb = w.astype(BF16)
    wb_ref[:, cols] = wb if gain_ref is None else (w * gain_ref[...]).astype(BF16)
    return wb


def _emit_norm_parts(xn, xb_ref, ssq_ref):
    xb_ref[...] = xn.astype(BF16)
    part = jnp.broadcast_to(jnp.sum(xn * xn, axis=-1, keepdims=True), ssq_ref.shape)

    @pl.when(pl.program_id(1) == 0)
    def _():
        ssq_ref[...] = part

    @pl.when(pl.program_id(1) != 0)
    def _():
        ssq_ref[...] += part


def _row_rms_scale(ssq_ref, rows, d):
    return lax.rsqrt(ssq_ref[rows, 0:1] * (1.0 / d) + RMS_EPS)


def _norm_outs(m, n, tm, tn):
    specs = [pl.BlockSpec((tm, tn), lambda i, j: (i, j)), pl.BlockSpec((tm, HEAD_DIM), lambda i, j: (i, 0))]
    shapes = [jax.ShapeDtypeStruct((m, n), BF16), jax.ShapeDtypeStruct((m, HEAD_DIM), F32)]
    return specs, shapes


def _inproj_kernel(x_ref, w_ref, cos_ref, sin_ref, o_ref, *wb, tn, part):
    wb_ref = wb[0] if wb else None
    seg = (pl.program_id(1) * tn) // D_A
    is_rope = jnp.logical_and(seg < 3 * N_DIL, seg % 3 != 2)
    cos = cos_ref[...]
    sin = sin_ref[...]
    x = x_ref[...]
    for c0 in range(0, tn, part):
        acc = _dot(x, _use_w(w_ref, wb_ref, slice(c0, c0 + part)))
        for h in range(part // HEAD_DIM):
            xh = acc[:, h * HEAD_DIM:(h + 1) * HEAD_DIM]
            roped = xh * cos + pltpu.roll(xh, HEAD_DIM // 2, 1) * sin
            o_ref[:, c0 + h * HEAD_DIM:c0 + (h + 1) * HEAD_DIM] = jnp.where(is_rope, roped, xh)


def _inproj(hn, w, layer, cos, sin, tm, tn):
    m, k = hn.shape
    n = w.shape[-1]
    ct = cos.shape[0] // tm
    emit = layer is not None
    out_specs = [pl.BlockSpec((tm, tn), lambda i, j: (i, j))]
    out_shape = [jax.ShapeDtypeStruct((m, n), F32)]
    if emit:
        out_specs.append(pl.BlockSpec((k, tn), lambda i, j: (0, j)))
        out_shape.append(jax.ShapeDtypeStruct((k, n), BF16))
    res = pl.pallas_call(
        functools.partial(_inproj_kernel, tn=tn, part=min(tn, 256)),
        grid=(m // tm, n // tn),
        in_specs=[pl.BlockSpec((tm, k), lambda i, j: (i, 0)),
                  _wspec(w, layer, k, tn, 0, lambda i, j: j),
                  pl.BlockSpec((tm, HEAD_DIM), lambda i, j: (i % ct, 0)),
                  pl.BlockSpec((tm, HEAD_DIM), lambda i, j: (i % ct, 0))],
        out_specs=out_specs,
        out_shape=out_shape,
        compiler_params=_cp("parallel", "arbitrary"),
        name="inproj_rope",
    )(hn, w, cos, sin)
    return res if emit else res[0]


def _band_attn_kernel(q_ref, kp_ref, kc_ref, vp_ref, vc_ref, o_ref, lse_ref):
    has_prev = pl.program_id(2) > 0
    qi = lax.broadcasted_iota(jnp.int32, (BAND, 2 * BAND), 0)
    ki = lax.broadcasted_iota(jnp.int32, (BAND, 2 * BAND), 1)
    valid = jnp.logical_and(jnp.logical_and(ki >= qi, ki <= qi + BAND),
                            jnp.logical_or(ki >= BAND, has_prev))
    lane = lax.broadcasted_iota(jnp.int32, (BAND, HEAD_DIM), 1)
    lse_tile = jnp.zeros((BAND, HEAD_DIM), F32)
    scale = HEAD_DIM ** -0.5
    for h in range(A_HEADS):
        sl = slice(h * HEAD_DIM, (h + 1) * HEAD_DIM)
        q = q_ref[:, sl].astype(BF16)
        k = jnp.concatenate([kp_ref[:, sl], kc_ref[:, sl]], axis=0).astype(BF16)
        v = jnp.concatenate([vp_ref[:, sl], vc_ref[:, sl]], axis=0).astype(BF16)
        s = _dot_nt(q, k) * scale
        s = jnp.where(valid, s, NEG_BIG)
        m = jnp.max(s, axis=-1, keepdims=True)
        p = jnp.exp(s - m)
        l = jnp.sum(p, axis=-1, keepdims=True)
        o_ref[:, sl] = _dot(p.astype(BF16), v) / l
        lse_tile = jnp.where(lane == h, m + jnp.log(l), lse_tile)
    lse_ref[...] = lse_tile


def _band_attn_dil_kernel(*refs, d, hpb, use_prev):
    if use_prev:
        q_ref, kp_ref, kc_ref, vp_ref, vc_ref, o_ref, lse_ref, qs, ks, vs, os_, ls = refs
    else:
        q_ref, kc_ref, vc_ref, o_ref, lse_ref, qs, ks, vs, os_, ls = refs
    rows = BAND * d
    hb = pl.program_id(2)
    has_prev = pl.program_id(1) > 0
    for h in range(hpb):
        sl = slice(h * HEAD_DIM, (h + 1) * HEAD_DIM)
        qs[h] = q_ref[:, sl]
        if use_prev:
            ks[h, 0:rows] = kp_ref[:, sl]
            ks[h, rows:2 * rows] = kc_ref[:, sl]
            vs[h, 0:rows] = vp_ref[:, sl]
            vs[h, rows:2 * rows] = vc_ref[:, sl]
        else:
            ks[h] = kc_ref[:, sl]
            vs[h] = vc_ref[:, sl]

    @pl.when(hb == 0)
    def _():
        lse_ref[...] = jnp.zeros(lse_ref.shape, F32)

    nk = 2 * BAND if use_prev else BAND
    qi = lax.broadcasted_iota(jnp.int32, (1, BAND, nk), 1)
    ki = lax.broadcasted_iota(jnp.int32, (1, BAND, nk), 2)
    if use_prev:
        valid = jnp.logical_and(jnp.logical_and(ki >= qi, ki <= qi + BAND),
                                jnp.logical_or(ki >= BAND, has_prev))
    else:
        valid = ki <= qi
    scale = HEAD_DIM ** -0.5

    def residues(ref, h, base):
        return [ref[h, pl.ds(base + r, BAND, stride=d), :] for r in range(d)]

    for h in range(hpb):
        q = jnp.stack(residues(qs, h, 0)).astype(BF16)
        if use_prev:
            k = jnp.stack([jnp.concatenate([a, b], axis=0)
                           for a, b in zip(residues(ks, h, 0), residues(ks, h, rows))]).astype(BF16)
            v = jnp.stack([jnp.concatenate([a, b], axis=0)
                           for a, b in zip(residues(vs, h, 0), residues(vs, h, rows))]).astype(BF16)
        else:
            k = jnp.stack(residues(ks, h, 0)).astype(BF16)
            v = jnp.stack(residues(vs, h, 0)).astype(BF16)
        s = jnp.einsum('rqd,rkd->rqk', q, k, preferred_element_type=F32) * scale
        s = jnp.where(valid, s, NEG_BIG)
        m = jnp.max(s, axis=-1, keepdims=True)
        p = jnp.exp(s - m)
        l = jnp.sum(p, axis=-1, keepdims=True)
        o = jnp.einsum('rqk,rkd->rqd', p.astype(BF16), v, preferred_element_type=F32) / l
        lse_h = m + jnp.log(l)
        for r in range(d):
            os_[h, pl.ds(r, BAND, stride=d), :] = o[r]
            ls[h, pl.ds(r, BAND, stride=d), :] = jnp.broadcast_to(lse_h[r], (BAND, HEAD_DIM))
    lse = lse_ref[...]
    lane = lax.broadcasted_iota(jnp.int32, lse.shape, 1)
    for h in range(hpb):
        o_ref[:, h * HEAD_DIM:(h + 1) * HEAD_DIM] = os_[h]
        lse = jnp.where(lane == hb * hpb + h, ls[h], lse)
    lse_ref[...] = lse


def _band_attn(z, n_seq, t, g):
    d = A_DILATIONS[g]
    width = z.shape[1]
    zv = z.reshape(n_seq, t, width)
    if d == 1:
        qc = 3 * g

        def cur(off):
            return pl.BlockSpec((None, BAND, D_A), lambda n, r, b: (n, b, qc + off))

        def prev(off):
            return pl.BlockSpec((None, BAND, D_A), lambda n, r, b: (n, jnp.maximum(b - 1, 0), qc + off))

        o, lse = pl.pallas_call(
            _band_attn_kernel,
            grid=(n_seq, 1, t // BAND),
            in_specs=[cur(0), prev(1), cur(1), prev(2), cur(2)],
            out_specs=[pl.BlockSpec((None, BAND, D_A), lambda n, r, b: (n, b, 0)),
                       pl.BlockSpec((None, BAND, HEAD_DIM), lambda n, r, b: (n, b, 0))],
            out_shape=[jax.ShapeDtypeStruct((n_seq, t, D_A), F32),
                       jax.ShapeDtypeStruct((n_seq, t, HEAD_DIM), F32)],
            compiler_params=_cp("parallel", "parallel", "arbitrary"),
            name="band_attn_d1",
        )(zv, zv, zv, zv, zv)
        return o.reshape(n_seq * t, D_A), lse.reshape(n_seq * t, HEAD_DIM)

    rows = BAND * d
    nb = t // rows
    use_prev = nb > 1
    hpb = 8 if d <= 4 else 2
    cols = hpb * HEAD_DIM
    cpq = D_A // cols
    qc = 3 * g * cpq

    def cur(off):
        return pl.BlockSpec((None, rows, cols), lambda n, b, hb: (n, b, qc + off * cpq + hb))

    def prev(off):
        return pl.BlockSpec((None, rows, cols),
                            lambda n, b, hb: (n, jnp.maximum(b - 1, 0), qc + off * cpq + hb))

    in_specs = [cur(0), prev(1), cur(1), prev(2), cur(2)] if use_prev else [cur(0), cur(1), cur(2)]
    kv_rows = 2 * rows if use_prev else rows
    o, lse = pl.pallas_call(
        functools.partial(_band_attn_dil_kernel, d=d, hpb=hpb, use_prev=use_prev),
        grid=(n_seq, nb, cpq),
        in_specs=in_specs,
        out_specs=[pl.BlockSpec((None, rows, cols), lambda n, b, hb: (n, b, hb)),
                   pl.BlockSpec((None, rows, HEAD_DIM), lambda n, b, hb: (n, b, 0))],
        out_shape=[jax.ShapeDtypeStruct((n_seq, t, D_A), F32),
                   jax.ShapeDtypeStruct((n_seq, t, HEAD_DIM), F32)],
        scratch_shapes=[pltpu.VMEM((hpb, rows, HEAD_DIM), F32),
                        pltpu.VMEM((hpb, kv_rows, HEAD_DIM), F32),
                        pltpu.VMEM((hpb, kv_rows, HEAD_DIM), F32),
                        pltpu.VMEM((hpb, rows, HEAD_DIM), F32),
                        pltpu.VMEM((hpb, rows, HEAD_DIM), F32)],
        compiler_params=_cp("parallel", "parallel", "arbitrary"),
        name=f"band_attn_d{d}",
    )(*([zv] * len(in_specs)))
    return o.reshape(n_seq * t, D_A), lse.reshape(n_seq * t, HEAD_DIM)


def _combine_kernel(o0_ref, o1_ref, o2_ref, l0_ref, l1_ref, l2_ref, out_ref):
    a0, a1, a2 = l0_ref[...], l1_ref[...], l2_ref[...]
    m = jnp.maximum(jnp.maximum(a0, a1), a2)
    e0, e1, e2 = jnp.exp(a0 - m), jnp.exp(a1 - m), jnp.exp(a2 - m)
    inv = 1.0 / (e0 + e1 + e2)
    w0, w1, w2 = e0 * inv, e1 * inv, e2 * inv
    for h in range(A_HEADS):
        sl = slice(h * HEAD_DIM, (h + 1) * HEAD_DIM)
        out_ref[:, sl] = (w0[:, h:h + 1] * o0_ref[:, sl] + w1[:, h:h + 1] * o1_ref[:, sl]
                          + w2[:, h:h + 1] * o2_ref[:, sl]).astype(out_ref.dtype)


def _combine(os_, lses):
    m = os_[0].shape[0]
    tm = 256
    ob = pl.BlockSpec((tm, D_A), lambda i: (i, 0))
    lb = pl.BlockSpec((tm, HEAD_DIM), lambda i: (i, 0))
    return pl.pallas_call(
        _combine_kernel,
        grid=(m // tm,),
        in_specs=[ob, ob, ob, lb, lb, lb],
        out_specs=ob,
        out_shape=jax.ShapeDtypeStruct((m, D_A), BF16),
        compiler_params=_cp("parallel"),
        name="attn_combine",
    )(*os_, *lses)


def _step_attn_kernel(z_ref, c0_ref, c1_ref, c2_ref, o_ref):
    scale = HEAD_DIM ** -0.5
    ms, ls, accs = [], [], []
    for g, c_ref in enumerate((c0_ref, c1_ref, c2_ref)):
        base = 3 * g * A_HEADS
        q = z_ref[base:base + A_HEADS, :]
        kn = z_ref[base + A_HEADS:base + 2 * A_HEADS, :]
        vn = z_ref[base + 2 * A_HEADS:base + 3 * A_HEADS, :]
        s = jnp.sum(c_ref[:, 0] * q[None], axis=-1, keepdims=True) * scale
        s_new = jnp.sum(kn * q, axis=-1, keepdims=True) * scale
        m = jnp.maximum(jnp.max(s, axis=0), s_new)
        p = jnp.exp(s - m[None])
        p_new = jnp.exp(s_new - m)
        ls.append(jnp.sum(p, axis=0) + p_new)
        accs.append(jnp.sum(p * c_ref[:, 1], axis=0) + p_new * vn)
        ms.append(m)
    mm = jnp.maximum(jnp.maximum(ms[0], ms[1]), ms[2])
    es = [jnp.exp(m - mm) for m in ms]
    den = es[0] * ls[0] + es[1] * ls[1] + es[2] * ls[2]
    num = es[0] * accs[0] + es[1] * accs[1] + es[2] * accs[2]
    o_ref[...] = (num / den).astype(o_ref.dtype)


def _step_attn(zs, caches):
    n, width = zs.shape
    cviews, cspecs = [], []
    for g, c in enumerate(caches):
        d = A_DILATIONS[g]
        cviews.append(c.reshape(n, c.shape[1] // d, d, 2, A_HEADS, HEAD_DIM))
        cspecs.append(pl.BlockSpec((None, BAND, None, 2, A_HEADS, HEAD_DIM),
                                   lambda i: (i, 0, 0, 0, 0, 0)))
    zrows = width // HEAD_DIM
    out = pl.pallas_call(
        _step_attn_kernel,
        grid=(n,),
        in_specs=[pl.BlockSpec((None, zrows, HEAD_DIM), lambda i: (i, 0, 0))] + cspecs,
        out_specs=pl.BlockSpec((None, A_HEADS, HEAD_DIM), lambda i: (i, 0, 0)),
        out_shape=jax.ShapeDtypeStruct((n, A_HEADS, HEAD_DIM), BF16),
        compiler_params=_cp("parallel"),
        name="step_attn",
    )(zs.reshape(n, zrows, HEAD_DIM), *cviews)
    return out.reshape(n, D_A)


def _cache_shift_kernel(c_ref, nxt_ref, new_ref, o_ref, *, lb):
    last = pl.program_id(1) == pl.num_programs(1) - 1
    o_ref[0:lb - 1] = c_ref[1:lb]

    @pl.when(last)
    def _():
        o_ref[lb - 1] = new_ref[0]

    @pl.when(jnp.logical_not(last))
    def _():
        o_ref[lb - 1] = nxt_ref[0]


def _cache_shift(cache, new_kv):
    n, length = cache.shape[:2]
    lb = min(length, 4 * BAND)
    nb = length // lb
    tail = cache.shape[2:]
    zeros = (0,) * len(tail)
    return pl.pallas_call(
        functools.partial(_cache_shift_kernel, lb=lb),
        grid=(n, nb),
        in_specs=[pl.BlockSpec((None, lb) + tail, lambda i, b: (i, b) + zeros),
                  pl.BlockSpec((None, 1) + tail, lambda i, b: (i, jnp.minimum(b + 1, nb - 1) * lb) + zeros),
                  pl.BlockSpec((None, 1) + tail, lambda i, b: (i, 0) + zeros)],
        out_specs=pl.BlockSpec((None, lb) + tail, lambda i, b: (i, b) + zeros),
        out_shape=jax.ShapeDtypeStruct(cache.shape, cache.dtype),
        compiler_params=_cp("parallel", "parallel"),
        name="cache_shift",
    )(cache, cache, new_kv)


def _kv_rows_kernel(k_ref, v_ref, o_ref, *, tm):
    per = 2 * A_HEADS
    for kv, x_ref in enumerate((k_ref, v_ref)):
        for h in range(A_HEADS):
            o_ref[pl.ds(kv * A_HEADS + h, tm, stride=per), :] = x_ref[:, h * HEAD_DIM:(h + 1) * HEAD_DIM]


def _kv_rows(z, n_seq, t, g, length):
    width = z.shape[1]
    zv = z.reshape(n_seq, t, width)
    tm = min(length, 2 * BAND)
    r0 = (t - length) // tm
    c0 = 3 * g + 1
    per = 2 * A_HEADS
    out = pl.pallas_call(
        functools.partial(_kv_rows_kernel, tm=tm),
        grid=(n_seq, length // tm),
        in_specs=[pl.BlockSpec((None, tm, D_A), lambda n, i: (n, r0 + i, c0)),
                  pl.BlockSpec((None, tm, D_A), lambda n, i: (n, r0 + i, c0 + 1))],
        out_specs=pl.BlockSpec((None, tm * per, HEAD_DIM), lambda n, i: (n, i, 0)),
        out_shape=jax.ShapeDtypeStruct((n_seq, length * per, HEAD_DIM), F32),
        compiler_params=_cp("parallel", "parallel"),
        name="kv_rows",
    )(zv, zv)
    return out.reshape(n_seq, length, 2, A_HEADS, HEAD_DIM)


def _ssm_discretise(lre, lim, ldt):
    dt = jnp.exp(ldt)
    mag = jnp.exp(lre * dt)
    ar = mag * jnp.cos(lim * dt)
    ai = mag * jnp.sin(lim * dt)
    den = lre * lre + lim * lim
    zr = ((ar - 1.0) * lre + ai * lim) / den
    zi = (ai * lre - (ar - 1.0) * lim) / den
    return ar, ai, zr, zi


def _slab_mask():
    r = lax.broadcasted_iota(jnp.int32, (SLAB_G * SSM_GROUP, SLAB_S), 0) // SSM_GROUP
    c = lax.broadcasted_iota(jnp.int32, (SLAB_G * SSM_GROUP, SLAB_S), 1) // SSM_STATE
    return (r == c).astype(F32)


def _block_diag(x16, mask):
    return jnp.concatenate([x16] * SLAB_G, axis=0) * mask


def _ssm_prompt_kernel(u_ref, lre_ref, lim_ref, ldt_ref, bre_ref, bim_ref, cre_ref, cim_ref, d_ref,
                       y_ref, hlr_ref, hli_ref,
                       wer_ref, wei_ref, tz_ref, bdr_ref, bdi_ref, er_ref, ei_ref, *, n_seq, n_chunk):
    def split(a):
        ah = a.astype(BF16)
        return ah, (a - ah.astype(F32)).astype(BF16)

    def dot_nt_x3(a, b_split):
        ah, al = split(a)
        bh, bl = b_split
        return _dot_nt(ah, bh) + _dot_nt(ah, bl) + _dot_nt(al, bh)

    rows = n_seq * n_chunk

    def step_rows(j):
        return pl.ds(j, rows, stride=CHUNK)

    ar, ai, zr, zi = _ssm_discretise(lre_ref[...], lim_ref[...], ldt_ref[...])
    mask = _slab_mask()
    bre, bim = bre_ref[...], bim_ref[...]
    cre, cim = cre_ref[...], cim_ref[...]
    cre_bd = split(_block_diag(cre, mask))
    cim_bd = split(_block_diag(cim, mask))
    lanes = SLAB_G * SSM_GROUP
    tz_ref[...] = jnp.zeros(tz_ref.shape, BF16)
    pr, pim = jnp.ones_like(ar), jnp.zeros_like(ar)
    for m in range(CHUNK + 1):
        if m < CHUNK:
            wr = zr * pr - zi * pim
            wi = zr * pim + zi * pr
            we_r = _block_diag(bre * wr - bim * wi, mask)
            we_i = _block_diag(bre * wi + bim * wr, mask)
            j = CHUNK - 1 - m
            wer_ref[j * lanes:(j + 1) * lanes, :] = we_r.astype(BF16)
            wei_ref[j * lanes:(j + 1) * lanes, :] = we_i.astype(BF16)
            tz = (dot_nt_x3(we_r, cre_bd) - dot_nt_x3(we_i, cim_bd)).astype(BF16)
            for j in range(CHUNK - m):
                tz_ref[j * lanes:(j + 1) * lanes, (j + m) * lanes:(j + m + 1) * lanes] = tz
        if m >= 1:
            bdr_ref[(m - 1) * lanes:m * lanes, :] = _block_diag(cre * pr - cim * pim, mask).astype(BF16)
            bdi_ref[(m - 1) * lanes:m * lanes, :] = _block_diag(-(cre * pim + cim * pr), mask).astype(BF16)
        if m == CHUNK:
            a8r, a8i = pr, pim
        pr, pim = pr * ar - pim * ai, pr * ai + pim * ar

    xcat = jnp.concatenate([u_ref[step_rows(j), :].astype(BF16) for j in range(CHUNK)], axis=1)

    er_ref[...] = _dot(xcat, wer_ref[...])
    ei_ref[...] = _dot(xcat, wei_ref[...])

    def body(k, carry):
        out = []
        for n in range(n_seq):
            hr, him = carry[2 * n], carry[2 * n + 1]
            row = n * n_chunk + k
            e_r = er_ref[pl.ds(row, 1), :]
            e_i = ei_ref[pl.ds(row, 1), :]
            er_ref[pl.ds(row, 1), :] = hr
            ei_ref[pl.ds(row, 1), :] = him
            out.append(a8r * hr - a8i * him + e_r)
            out.append(a8r * him + a8i * hr + e_i)
        return tuple(out)

    zero = jnp.zeros((1, SLAB_S), F32)
    fin = lax.fori_loop(0, n_chunk, body, (zero,) * (2 * n_seq))
    for n in range(n_seq):
        hlr_ref[pl.ds(n, 1), :] = fin[2 * n]
        hli_ref[pl.ds(n, 1), :] = fin[2 * n + 1]

    y = (_dot(xcat, tz_ref[...]) + _dot_nt(er_ref[...].astype(BF16), bdr_ref[...])
         + _dot_nt(ei_ref[...].astype(BF16), bdi_ref[...]))
    for jp in range(CHUNK):
        y_ref[step_rows(jp), :] = y[:, jp * lanes:(jp + 1) * lanes] + d_ref[...] * u_ref[step_rows(jp), :]


def _ssm_params(a_re, a_im, b_re, b_im, c_re, c_im, d_skip, log_dt):
    g = a_re.shape[0]
    ns = g // SLAB_G

    def row(x):
        return x.astype(F32).reshape(ns, 1, SLAB_S)

    def chan_rows(x):
        return x.astype(F32).reshape(ns, SLAB_G, SSM_GROUP, SSM_STATE).transpose(0, 2, 1, 3) \
            .reshape(ns, SSM_GROUP, SLAB_S)

    ldt = jnp.broadcast_to(log_dt.astype(F32)[:, None], (g, SSM_STATE))
    return (row(a_re), row(a_im), row(ldt),
            chan_rows(b_re.transpose(0, 2, 1)), chan_rows(b_im.transpose(0, 2, 1)),
            chan_rows(c_re), chan_rows(c_im),
            d_skip.astype(F32).reshape(ns, 1, SLAB_G * SSM_GROUP))


def _ssm_prompt(z, n_seq, t, u_col0, sp):
    m, width = z.shape
    rows = m // CHUNK
    n_chunk = t // CHUNK
    lanes = SLAB_G * SSM_GROUP
    ns = sp[0].shape[0]
    ub = u_col0 // lanes

    def pspec(r, c):
        return pl.BlockSpec((None, r, c), lambda s: (s, 0, 0))

    y, hlr, hli = pl.pallas_call(
        functools.partial(_ssm_prompt_kernel, n_seq=n_seq, n_chunk=n_chunk),
        grid=(ns,),
        in_specs=[pl.BlockSpec((m, lanes), lambda s: (0, ub + s)),
                  pspec(1, SLAB_S), pspec(1, SLAB_S), pspec(1, SLAB_S),
                  pspec(SSM_GROUP, SLAB_S), pspec(SSM_GROUP, SLAB_S),
                  pspec(SSM_GROUP, SLAB_S), pspec(SSM_GROUP, SLAB_S), pspec(1, lanes)],
        out_specs=[pl.BlockSpec((m, lanes), lambda s: (0, s)),
                   pl.BlockSpec((None, n_seq, SLAB_S), lambda s: (s, 0, 0)),
                   pl.BlockSpec((None, n_seq, SLAB_S), lambda s: (s, 0, 0))],
        out_shape=[jax.ShapeDtypeStruct((m, ns * lanes), F32),
                   jax.ShapeDtypeStruct((ns, n_seq, SLAB_S), F32),
                   jax.ShapeDtypeStruct((ns, n_seq, SLAB_S), F32)],
        scratch_shapes=[pltpu.VMEM((CHUNK * lanes, SLAB_S), BF16),
                        pltpu.VMEM((CHUNK * lanes, SLAB_S), BF16),
                        pltpu.VMEM((CHUNK * lanes, CHUNK * lanes), BF16),
                        pltpu.VMEM((CHUNK * lanes, SLAB_S), BF16),
                        pltpu.VMEM((CHUNK * lanes, SLAB_S), BF16),
                        pltpu.VMEM((rows, SLAB_S), F32),
                        pltpu.VMEM((rows, SLAB_S), F32)],
        compiler_params=_cp("parallel"),
        name="ssm_prompt",
    )(z, *sp)
    return y, hlr, hli


def _ssm_step_kernel(u_ref, h0r_ref, h0i_ref, lre_ref, lim_ref, ldt_ref, bre_ref, bim_ref,
                     cre_ref, cim_ref, d_ref, y_ref, hr_ref, hi_ref):
    ar, ai, zr, zi = _ssm_discretise(lre_ref[...], lim_ref[...], ldt_ref[...])
    mask = _slab_mask()
    bre, bim = bre_ref[...], bim_ref[...]
    we_r = _block_diag(bre * zr - bim * zi, mask).astype(BF16)
    we_i = _block_diag(bre * zi + bim * zr, mask).astype(BF16)
    u = u_ref[...]
    ub = u.astype(BF16)
    h0r, h0i = h0r_ref[...], h0i_ref[...]
    hr = ar * h0r - ai * h0i + _dot(ub, we_r)
    him = ar * h0i + ai * h0r + _dot(ub, we_i)
    hr_ref[...] = hr
    hi_ref[...] = him
    cr_bd = _block_diag(cre_ref[...], mask).astype(BF16)
    ci_bd = _block_diag(cim_ref[...], mask).astype(BF16)
    y_ref[...] = (_dot_nt(hr.astype(BF16), cr_bd) - _dot_nt(him.astype(BF16), ci_bd)
                  + d_ref[...] * u)


def _ssm_step(zs, u_col0, h0r, h0i, sp):
    n = zs.shape[0]
    lanes = SLAB_G * SSM_GROUP
    ns = sp[0].shape[0]
    ub = u_col0 // lanes

    def pspec(r, c):
        return pl.BlockSpec((None, r, c), lambda s: (s, 0, 0))

    hspec = pl.BlockSpec((None, n, SLAB_S), lambda s: (s, 0, 0))
    return pl.pallas_call(
        _ssm_step_kernel,
        grid=(ns,),
        in_specs=[pl.BlockSpec((n, lanes), lambda s: (0, ub + s)), hspec, hspec,
                  pspec(1, SLAB_S), pspec(1, SLAB_S), pspec(1, SLAB_S),
                  pspec(SSM_GROUP, SLAB_S), pspec(SSM_GROUP, SLAB_S),
                  pspec(SSM_GROUP, SLAB_S), pspec(SSM_GROUP, SLAB_S), pspec(1, lanes)],
        out_specs=[pl.BlockSpec((n, lanes), lambda s: (0, s)), hspec, hspec],
        out_shape=[jax.ShapeDtypeStruct((n, ns * lanes), F32),
                   jax.ShapeDtypeStruct((ns, n, SLAB_S), F32),
                   jax.ShapeDtypeStruct((ns, n, SLAB_S), F32)],
        compiler_params=_cp("parallel"),
        name="ssm_step",
    )(zs, h0r, h0i, *sp)


def _gelu_tanh(x):
    return 0.5 * x * (1.0 + jnp.tanh(math.sqrt(2.0 / math.pi) * (x + 0.044715 * (x * x * x))))


def _ssm_gate_kernel(y_ref, w_ref, b_ref, o_ref):
    yg = _gelu_tanh(y_ref[...])
    gate = _dot(yg.astype(BF16), w_ref[...]) + b_ref[...]
    o_ref[...] = (yg * _sigmoid(gate)).astype(o_ref.dtype)


def _ssm_gate(y, w, b, tm):
    m, d = y.shape
    return pl.pallas_call(
        _ssm_gate_kernel,
        grid=(m // tm,),
        in_specs=[pl.BlockSpec((tm, d), lambda i: (i, 0)),
                  pl.BlockSpec((d, d), lambda i: (0, 0)),
                  pl.BlockSpec((1, d), lambda i: (0, 0))],
        out_specs=pl.BlockSpec((tm, d), lambda i: (i, 0)),
        out_shape=jax.ShapeDtypeStruct((m, d), BF16),
        compiler_params=_cp("parallel"),
        name="ssm_gate",
    )(y, w, b.reshape(1, d).astype(F32))


def _ssm_gate_step_kernel(y_ref, yc_ref, w_ref, b_ref, o_ref, wb_ref):
    yg = _gelu_tanh(y_ref[...])
    gate = _dot(yg.astype(BF16), _use_w(w_ref, wb_ref)) + b_ref[...]
    o_ref[...] = (_gelu_tanh(yc_ref[...]) * _sigmoid(gate)).astype(o_ref.dtype)


def _ssm_gate_step(y, w, layer, b, tn):
    m, d = y.shape
    return pl.pallas_call(
        _ssm_gate_step_kernel,
        grid=(d // tn,),
        in_specs=[pl.BlockSpec((m, d), lambda j: (0, 0)),
                  pl.BlockSpec((m, tn), lambda j: (0, j)),
                  _wspec(w, layer, d, tn, 0, lambda j: j),
                  pl.BlockSpec((1, tn), lambda j: (0, j))],
        out_specs=[pl.BlockSpec((m, tn), lambda j: (0, j)),
                   pl.BlockSpec((d, tn), lambda j: (0, j))],
        out_shape=[jax.ShapeDtypeStruct((m, d), BF16), jax.ShapeDtypeStruct((d, d), BF16)],
        compiler_params=_cp("arbitrary"),
        name="ssm_gate_step",
    )(y, y, w, b.reshape(1, d).astype(F32))


def _outproj_kernel(a_ref, s_ref, wa_ref, ws_ref, x_ref, o_ref, *extra, emit):
    wab_ref, wsb_ref = extra if emit else (None, None)
    xn = (x_ref[...] + _dot(a_ref[...], _use_w(wa_ref, wab_ref))
          + _dot(s_ref[...], _use_w(ws_ref, wsb_ref)))
    o_ref[...] = xn
    if not emit:
        _emit_norm_parts(xn, *extra)


def _outproj(attn, ssm, wa, ws, layer, x, tm, tn):
    m, ka = attn.shape
    n = wa.shape[-1]
    emit = layer is not None
    out_specs = [pl.BlockSpec((tm, tn), lambda i, j: (i, j))]
    out_shape = [jax.ShapeDtypeStruct((m, n), F32)]
    if emit:
        out_specs += [pl.BlockSpec((ka, tn), lambda i, j: (0, j))] * 2
        out_shape += [jax.ShapeDtypeStruct((ka, n), BF16)] * 2
    else:
        nspecs, nshapes = _norm_outs(m, n, tm, tn)
        out_specs += nspecs
        out_shape += nshapes
    res = pl.pallas_call(
        functools.partial(_outproj_kernel, emit=emit),
        grid=(m // tm, n // tn),
        in_specs=[pl.BlockSpec((tm, ka), lambda i, j: (i, 0)),
                  pl.BlockSpec((tm, ka), lambda i, j: (i, 0)),
                  _wspec(wa, layer, ka, tn, 0, lambda i, j: j),
                  _wspec(ws, layer, ka, tn, 1 if emit else 0, lambda i, j: j),
                  pl.BlockSpec((tm, tn), lambda i, j: (i, j))],
        out_specs=out_specs,
        out_shape=out_shape,
        compiler_params=_cp("parallel", "arbitrary"),
        name="outproj",
    )(attn, ssm, wa, ws, x)
    return res


def _mm_res_kernel(a_ref, w_ref, b_ref, x_ref, o_ref, *extra, emit, norm):
    wb_ref = extra[0] if emit else None
    xn = x_ref[...] + (_dot(a_ref[...], _use_w(w_ref, wb_ref)) + b_ref[...])
    o_ref[...] = xn
    if norm:
        _emit_norm_parts(xn, *extra)


def _mm_res(a, w, layer, b, x, tm, tn, name, norm=False):
    m, k = a.shape
    n = w.shape[-1]
    emit = layer is not None
    out_specs = [pl.BlockSpec((tm, tn), lambda i, j: (i, j))]
    out_shape = [jax.ShapeDtypeStruct((m, n), F32)]
    if emit:
        out_specs.append(pl.BlockSpec((k, tn), lambda i, j: (0, j)))
        out_shape.append(jax.ShapeDtypeStruct((k, n), BF16))
    if norm:
        nspecs, nshapes = _norm_outs(m, n, tm, tn)
        out_specs += nspecs
        out_shape += nshapes
    res = pl.pallas_call(
        functools.partial(_mm_res_kernel, emit=emit, norm=norm),
        grid=(m // tm, n // tn),
        in_specs=[pl.BlockSpec((tm, k), lambda i, j: (i, 0)),
                  _wspec(w, layer, k, tn, 0, lambda i, j: j),
                  pl.BlockSpec((1, tn), lambda i, j: (0, j)),
                  pl.BlockSpec((tm, tn), lambda i, j: (i, j))],
        out_specs=out_specs,
        out_shape=out_shape,
        compiler_params=_cp("parallel", "arbitrary"),
        name=name,
    )(a, w, b.reshape(1, n).astype(F32), x)
    return res if (emit or norm) else res[0]


def _pw1_kernel(x_ref, aux_ref, wa_ref, wg_ref, ba_ref, bg_ref, o_ref, *wb, emit):
    x = x_ref[...]
    if emit:
        wab_ref, wgb_ref = wb
        a = _dot(x, _use_w(wa_ref, wab_ref, gain_ref=aux_ref))
        gate = _dot(x, _use_w(wg_ref, wgb_ref, gain_ref=aux_ref))
    else:
        r = _row_rms_scale(aux_ref, slice(None), x.shape[1])
        a = _dot(x, wa_ref[...]) * r
        gate = _dot(x, wg_ref[...]) * r
    o_ref[...] = (a + ba_ref[...]) * _sigmoid(gate + bg_ref[...])


def _pw1(x, aux, wa, wg, layer, b, tm, tn):
    m, k = x.shape
    n = b.shape[0] // 2
    nj = n // tn
    emit = layer is not None
    b2 = b.reshape(1, 2 * n).astype(F32)
    out_specs = [pl.BlockSpec((tm, tn), lambda i, j: (i, j))]
    out_shape = [jax.ShapeDtypeStruct((m, n), F32)]
    if emit:
        out_specs += [pl.BlockSpec((k, tn), lambda i, j: (0, j))] * 2
        out_shape += [jax.ShapeDtypeStruct((k, n), BF16)] * 2
        aux = aux.reshape(k, 1).astype(F32)
        aux_spec = pl.BlockSpec((k, 1), lambda i, j: (0, 0))
    else:
        aux_spec = pl.BlockSpec((tm, HEAD_DIM), lambda i, j: (i, 0))
    goff = nj if emit else 0
    res = pl.pallas_call(
        functools.partial(_pw1_kernel, emit=emit),
        grid=(m // tm, nj),
        in_specs=[pl.BlockSpec((tm, k), lambda i, j: (i, 0)),
                  aux_spec,
                  _wspec(wa, layer, k, tn, 0, lambda i, j: j),
                  _wspec(wg, layer, k, tn, 0, lambda i, j: j + goff),
                  pl.BlockSpec((1, tn), lambda i, j: (0, j)),
                  pl.BlockSpec((1, tn), lambda i, j: (0, j + nj))],
        out_specs=out_specs,
        out_shape=out_shape,
        compiler_params=_cp("parallel", "arbitrary"),
        name="conv_pw1_glu",
    )(x, aux, wa, wg, b2, b2)
    return res if emit else res[0]


def _ln_swish(y, g, b):
    mu = jnp.mean(y, axis=-1, keepdims=True)
    yc = y - mu
    var = jnp.mean(yc * yc, axis=-1, keepdims=True)
    yn = yc * lax.rsqrt(var + LN_EPS) * g + b
    return yn * _sigmoid(yn)


def _conv31_kernel(u_ref, halo_ref, w_ref, b_ref, g_ref, bb_ref, o_ref, ext_ref, y3_ref, y_ref,
                   *, tm, tiles_per_seq, slab):
    seq_start = (pl.program_id(0) % tiles_per_seq) == 0
    nc = u_ref.shape[1] // HEAD_DIM
    for c in range(nc):
        cs = slice(c * HEAD_DIM, (c + 1) * HEAD_DIM)
        ext_ref[pl.ds(c, CONV_HALO, stride=slab), :] = jnp.where(seq_start, 0.0, halo_ref[:, cs])
        ext_ref[pl.ds(CONV_HALO * slab + c, tm, stride=slab), :] = u_ref[:, cs]
    off = CONV_HALO - (CONV_WIDTH - 1)
    rw = 8

    def body(i, carry):
        for ct in range(nc // 8):
            base = pl.multiple_of((i * rw + off) * slab, 8) + ct * 8
            win = [ext_ref[pl.ds(base + s * slab, 8), :] for s in range(rw + CONV_WIDTH - 1)]
            acc = [None] * rw
            for j in range(CONV_WIDTH):
                wj = w_ref[j, ct * 8:(ct + 1) * 8, :]
                for tt in range(rw):
                    term = wj * win[tt + j]
                    acc[tt] = term if j == 0 else acc[tt] + term
            bias = b_ref[ct * 8:(ct + 1) * 8, :]
            obase = pl.multiple_of(i * rw * slab, 8) + ct * 8
            for tt in range(rw):
                y3_ref[pl.ds(obase + tt * slab, 8), :] = acc[tt] + bias
        return carry

    lax.fori_loop(0, tm // rw, body, 0)
    for c in range(nc):
        y_ref[:, c * HEAD_DIM:(c + 1) * HEAD_DIM] = y3_ref[pl.ds(c, tm, stride=slab), :]
    o_ref[...] = _ln_swish(y_ref[...], g_ref[...], bb_ref[...]).astype(o_ref.dtype)


def _conv31(u, t, w, b, g, bb, tm):
    m, d = u.shape
    nc = d // HEAD_DIM
    slab = nc + 8
    row = lambda v: v.reshape(1, d).astype(F32)
    rspec = pl.BlockSpec((1, d), lambda i: (0, 0))
    hb = tm // CONV_HALO
    return pl.pallas_call(
        functools.partial(_conv31_kernel, tm=tm, tiles_per_seq=t // tm, slab=slab),
        grid=(m // tm,),
        in_specs=[pl.BlockSpec((tm, d), lambda i: (i, 0)),
                  pl.BlockSpec((CONV_HALO, d), lambda i: (jnp.maximum(i * hb - 1, 0), 0)),
                  pl.BlockSpec((CONV_WIDTH, nc, HEAD_DIM), lambda i: (0, 0, 0)),
                  pl.BlockSpec((nc, HEAD_DIM), lambda i: (0, 0)), rspec, rspec],
        out_specs=pl.BlockSpec((tm, d), lambda i: (i, 0)),
        out_shape=jax.ShapeDtypeStruct((m, d), BF16),
        scratch_shapes=[pltpu.VMEM(((tm + CONV_HALO) * slab, HEAD_DIM), F32),
                        pltpu.VMEM((tm * slab, HEAD_DIM), F32),
                        pltpu.VMEM((tm, d), F32)],
        compiler_params=_cp("parallel"),
        name="conv31_ln_swish",
    )(u, u, w.astype(F32).reshape(CONV_WIDTH, nc, HEAD_DIM), b.reshape(nc, HEAD_DIM).astype(F32),
      row(g), row(bb))


def _conv31_step_kernel(u_ref, st_ref, w_ref, b_ref, g_ref, bb_ref, o_ref):
    acc = w_ref[0:1, :] * st_ref[0]
    for j in range(1, CONV_WIDTH - 1):
        acc = acc + w_ref[j:j + 1, :] * st_ref[j]
    acc = acc + w_ref[CONV_WIDTH - 1:CONV_WIDTH, :] * u_ref[...] + b_ref[...]
    o_ref[...] = _ln_swish(acc, g_ref[...], bb_ref[...]).astype(o_ref.dtype)


def _conv31_step(u, state_t, w, b, g, bb):
    n, d = u.shape
    row = lambda v: v.reshape(1, d).astype(F32)
    return pl.pallas_call(
        _conv31_step_kernel,
        out_shape=jax.ShapeDtypeStruct((n, d), BF16),
        compiler_params=pltpu.CompilerParams(vmem_limit_bytes=VMEM_LIMIT_V7X),
        name="conv31_step",
    )(u, state_t, w.astype(F32), row(b), row(g), row(bb))


def _ffn_up_kernel(x_ref, ssq_ref, w_ref, cwg_ref, cwv_ref, cbg_ref, cbv_ref,
                   act_ref, zlg_ref, zlv_ref, z_scr, *, t, tn, rows):
    nc = tn // HEAD_DIM
    cw = (cwg_ref, cwv_ref)
    cb = (cbg_ref, cbv_ref)
    hist = [jnp.zeros((8, HEAD_DIM), F32)] * (2 * nc)
    for p, r0 in enumerate(range(0, t, rows)):
        z = (_dot(x_ref[r0:r0 + rows, :], w_ref[...])
             * _row_rms_scale(ssq_ref, slice(r0, r0 + rows), x_ref.shape[1]))
        zc = []
        for c in range(2 * nc):
            zs = z_scr.at[p, c]
            zn = z[:, c * HEAD_DIM:(c + 1) * HEAD_DIM]
            zs[0:8, :] = hist[c]
            zs[8:, :] = zn
            hist[c] = zn[rows - 8:, :]
            half, cc = divmod(c, nc)
            ls = slice(cc * HEAD_DIM, (cc + 1) * HEAD_DIM)
            zc.append(cw[half][2:3, ls] * zn + cw[half][1:2, ls] * zs[pl.ds(7, rows, stride=1), :]
                      + cw[half][0:1, ls] * zs[pl.ds(6, rows, stride=1), :] + cb[half][:, ls])
        for c in range(nc):
            gate = zc[c]
            act_ref[r0:r0 + rows, c * HEAD_DIM:(c + 1) * HEAD_DIM] = (
                gate * _sigmoid(gate) * zc[nc + c]).astype(act_ref.dtype)
    for c in range(nc):
        zlg_ref[:, c * HEAD_DIM:(c + 1) * HEAD_DIM] = hist[c]
        zlv_ref[:, c * HEAD_DIM:(c + 1) * HEAD_DIM] = hist[nc + c]


def _ffn_up(xb, ssq, t, w, cw, cb, tn):
    m, k = xb.shape
    dff = w.shape[1] // 2
    nj = dff // tn
    cw = cw.astype(F32)
    cb2 = cb.reshape(1, 2 * dff).astype(F32)
    zl = jax.ShapeDtypeStruct((m // t, 8, dff), F32)
    rows = min(t, 1024)
    return pl.pallas_call(
        functools.partial(_ffn_up_kernel, t=t, tn=tn, rows=rows),
        grid=(m // t, nj),
        in_specs=[pl.BlockSpec((t, k), lambda i, j: (i, 0)),
                  pl.BlockSpec((t, HEAD_DIM), lambda i, j: (i, 0)),
                  pl.BlockSpec((k, 2 * tn), lambda i, j: (0, j)),
                  pl.BlockSpec((3, tn), lambda i, j: (0, j)),
                  pl.BlockSpec((3, tn), lambda i, j: (0, j + nj)),
                  pl.BlockSpec((1, tn), lambda i, j: (0, j)),
                  pl.BlockSpec((1, tn), lambda i, j: (0, j + nj))],
        out_specs=[pl.BlockSpec((t, tn), lambda i, j: (i, j)),
                   pl.BlockSpec((None, 8, tn), lambda i, j: (i, 0, j)),
                   pl.BlockSpec((None, 8, tn), lambda i, j: (i, 0, j))],
        out_shape=[jax.ShapeDtypeStruct((m, dff), BF16), zl, zl],
        scratch_shapes=[pltpu.VMEM((t // rows, 2 * tn // HEAD_DIM, 8 + rows, HEAD_DIM), F32)],
        compiler_params=_cp("parallel", "arbitrary"),
        name="ffn_up_conv_glu",
    )(xb, ssq, w, cw, cw, cb2, cb2)


def _ffn_up_step_kernel(x_ref, gain_ref, wg_ref, wv_ref, sg_ref, sv_ref, cwg_ref, cwv_ref, cbg_ref, cbv_ref,
                        act_ref, zg_ref, zv_ref, wb_ref):
    x = x_ref[...]
    tn = wg_ref.shape[1]

    def half(w_ref, c0, s_ref, cw_ref, cb_ref, z_ref):
        w = w_ref[...]
        wb_ref[:, c0:c0 + tn] = (w * gain_ref[...]).astype(BF16)
        z = _dot(x, w.astype(BF16))
        z_ref[...] = z
        return cw_ref[2:3, :] * z + cw_ref[1:2, :] * s_ref[1] + cw_ref[0:1, :] * s_ref[0] + cb_ref[...]

    gate = half(wg_ref, 0, sg_ref, cwg_ref, cbg_ref, zg_ref)
    val = half(wv_ref, tn, sv_ref, cwv_ref, cbv_ref, zv_ref)
    act_ref[...] = (gate * _sigmoid(gate) * val).astype(act_ref.dtype)


def _ffn_up_step(hn, gain, state_t, w, layer, cw, cb, tn):
    n, k = hn.shape
    dff = w.shape[-1] // 2
    nj = dff // tn
    cw = cw.astype(F32)
    cb2 = cb.reshape(1, 2 * dff).astype(F32)
    zs = jax.ShapeDtypeStruct((n, dff), F32)
    return pl.pallas_call(
        _ffn_up_step_kernel,
        grid=(nj,),
        in_specs=[pl.BlockSpec((n, k), lambda j: (0, 0)),
                  pl.BlockSpec((k, 1), lambda j: (0, 0)),
                  _wspec(w, layer, k, tn, 0, lambda j: j),
                  _wspec(w, layer, k, tn, 0, lambda j: j + nj),
                  pl.BlockSpec((2, n, tn), lambda j: (0, 0, j)),
                  pl.BlockSpec((2, n, tn), lambda j: (0, 0, j + nj)),
                  pl.BlockSpec((3, tn), lambda j: (0, j)),
                  pl.BlockSpec((3, tn), lambda j: (0, j + nj)),
                  pl.BlockSpec((1, tn), lambda j: (0, j)),
                  pl.BlockSpec((1, tn), lambda j: (0, j + nj))],
        out_specs=[pl.BlockSpec((n, tn), lambda j: (0, j)),
                   pl.BlockSpec((n, tn), lambda j: (0, j)),
                   pl.BlockSpec((n, tn), lambda j: (0, j)),
                   pl.BlockSpec((k, 2 * tn), lambda j: (0, j))],
        out_shape=[jax.ShapeDtypeStruct((n, dff), BF16), zs, zs,
                   jax.ShapeDtypeStruct((k, 2 * dff), BF16)],
        compiler_params=_cp("arbitrary"),
        name="ffn_up_step",
    )(hn, gain.reshape(k, 1).astype(F32), w, w, state_t, state_t, cw, cw, cb2, cb2)


def _tile_plan(m):
    return 1024 if m >= 1024 else m


def _ffn_prompt(x, xb, ssq, t, w_up, cw, cb, w_down, norm_out):
    act, zlg, zlv = _ffn_up(xb, ssq, t, w_up, cw, cb, 256)
    last = jnp.concatenate([zlg[:, 6:8], zlv[:, 6:8]], axis=-1)
    res = _mm_res(act, w_down, None, jnp.zeros((w_down.shape[1],), F32), x, 512, 512, "ffn_down", norm=norm_out)
    return res, last


def _ffn_step(x, prev_state_t, norm_g, w_up, w_down, layer, cw, cb):
    m = x.shape[0]
    hn = _rmsnorm(x, norm_g, BF16)
    act, zg, zv, wu_b = _ffn_up_step(hn, norm_g, prev_state_t, w_up, layer, cw, cb, 256)
    znew = jnp.concatenate([zg, zv], axis=-1)
    last = jnp.concatenate([prev_state_t[1][:, None], znew[:, None]], axis=1)
    xo, wd_b = _mm_res(act, w_down, layer, jnp.zeros((w_down.shape[-1],), F32), x, m, 256, "ffn_down_step")
    return xo, last, (wu_b, wd_b)


def kernel(x_prompt, x_sample, cache_a0, cache_a1, cache_a2, state_ssm, state_conv, state_ffn, norm_mix, norm_ffn, norm_final, w_in_even, w_out_even, ssm_a_re, ssm_a_im, ssm_b_re, ssm_b_im, ssm_c_re, ssm_c_im, ssm_d, ssm_log_dt, w_glu, b_glu, w_pw1, b_pw1, w_dw, b_dw, ln_g, ln_b, w_pw2, b_pw2, w_up, ffn_dw, ffn_dw_b, w_down):
    n_p, t_p, d_model = x_prompt.shape
    n_s = x_sample.shape[0]
    caches = (cache_a0, cache_a1, cache_a2)
    u_col0 = N_DIL * QKV_W
    sp = _ssm_params(ssm_a_re[0], ssm_a_im[0], ssm_b_re[0], ssm_b_im[0], ssm_c_re[0], ssm_c_im[0],
                     ssm_d[0], ssm_log_dt[0])
    ns = sp[0].shape[0]

    def ssm_state(hr, him, n):
        def unslab(h):
            return h.reshape(ns, n, SLAB_G, SSM_STATE).transpose(1, 0, 2, 3).reshape(n, ns * SLAB_G, SSM_STATE)
        return jnp.stack([unslab(hr), unslab(him)], axis=-1)[None]

    xs = x_sample.reshape(n_s, d_model)
    cos_s, sin_s = _rope_tables(n_s, PAST_LEN, 0)
    hn = _rmsnorm(xs, norm_mix[0], BF16)
    zs, w_in_b = _inproj(hn, w_in_even, 0, cos_s, sin_s, n_s, 512)
    attn_s = _step_attn(zs, [c[0] for c in caches])

    def slab(h):
        return h.reshape(n_s, ns, SLAB_S).transpose(1, 0, 2)

    y_s, hr_s, hi_s = _ssm_step(zs, u_col0, slab(state_ssm[0, ..., 0]), slab(state_ssm[0, ..., 1]), sp)
    gate_s, w_glu_b = _ssm_gate_step(y_s, w_glu, 0, b_glu[0], 512)
    xs, w_oa_b, w_os_b = _outproj(attn_s, gate_s, w_out_even, w_out_even, 0, xs, n_s, 512)
    xs, ffn0_s, wf0 = _ffn_step(xs, state_ffn[0].transpose(1, 0, 2), norm_ffn[0], w_up, w_down, 0,
                                ffn_dw[0], ffn_dw_b[0])
    hn = _rmsnorm(xs, norm_mix[1], BF16)
    us, w_pa_b, w_pg_b = _pw1(hn, norm_mix[1], w_pw1, w_pw1, 0, b_pw1[0], n_s, 256)
    cs = _conv31_step(us, state_conv[0].transpose(1, 0, 2), w_dw[0], b_dw[0], ln_g[0], ln_b[0])
    conv_s = jnp.concatenate([state_conv[0][:, 1:], us[:, None]], axis=1)
    xs, w_pw2_b = _mm_res(cs, w_pw2, 0, b_pw2[0], xs, n_s, 512, "conv_pw2_step")
    xs, ffn1_s, wf1 = _ffn_step(xs, state_ffn[1].transpose(1, 0, 2), norm_ffn[1], w_up, w_down, 1,
                                ffn_dw[1], ffn_dw_b[1])
    y_sample = _rmsnorm(xs, norm_final, F32).reshape(n_s, 1, d_model)

    a_s = []
    for g in range(N_DIL):
        kv = zs[:, g * QKV_W + D_A:(g + 1) * QKV_W].reshape(n_s, 1, 2, A_HEADS, HEAD_DIM)
        a_s.append(_cache_shift(caches[g][0], kv)[None])
    ssm_s = ssm_state(hr_s, hi_s, n_s)
    conv_s = conv_s[None]
    ffn_s = jnp.stack([ffn0_s, ffn1_s])

    mp = n_p * t_p
    tm = _tile_plan(mp)
    x = x_prompt.reshape(mp, d_model)
    cos, sin = _rope_tables(t_p, 0, 1)
    hn = _rmsnorm(x, norm_mix[0], BF16)
    z = _inproj(hn, w_in_b, None, cos, sin, tm, 1024)
    outs, lses = [], []
    for g in range(N_DIL):
        o, lse = _band_attn(z, n_p, t_p, g)
        outs.append(o)
        lses.append(lse)
    attn = _combine(outs, lses)
    y_ssm, hlr, hli = _ssm_prompt(z, n_p, t_p, u_col0, sp)
    gate_p = _ssm_gate(y_ssm, w_glu_b, b_glu[0], 512)
    x, xb, ssq = _outproj(attn, gate_p, w_oa_b, w_os_b, None, x, tm, 512)
    (x, xb, ssq), ffn0_p = _ffn_prompt(x, xb, ssq, t_p, wf0[0], ffn_dw[0], ffn_dw_b[0], wf0[1], True)
    u = _pw1(xb, ssq, w_pa_b, w_pg_b, None, b_pw1[0], tm, 512)
    c = _conv31(u, t_p, w_dw[0], b_dw[0], ln_g[0], ln_b[0], 256)
    conv_p = u.reshape(n_p, t_p, -1)[None, :, t_p - (CONV_WIDTH - 1):]
    x, xb, ssq = _mm_res(c, w_pw2_b, None, b_pw2[0], x, tm, 512, "conv_pw2", norm=True)
    x, ffn1_p = _ffn_prompt(x, xb, ssq, t_p, wf1[0], ffn_dw[1], ffn_dw_b[1], wf1[1], False)
    y_prompt = _rmsnorm(x, norm_final, F32).reshape(n_p, t_p, d_model)

    a_p = []
    for g in range(N_DIL):
        a_p.append(_kv_rows(z, n_p, t_p, g, min(A_WINDOWS[g], PAST_LEN))[None])
    ssm_p = ssm_state(hlr, hli, n_p)
    ffn_p = jnp.stack([ffn0_p, ffn1_p])

    return (y_prompt, y_sample, a_p[0], a_s[0], a_p[1], a_s[1], a_p[2], a_s[2],
            ssm_p, ssm_s, conv_p, conv_s, ffn_p, ffn_s)
```

```python
import functools
import math

import jax
import jax.numpy as jnp
from jax import lax
from jax.experimental import pallas as pl
from jax.experimental.pallas import tpu as pltpu

F32 = jnp.float32
BF16 = jnp.bfloat16

HEAD_DIM = 128
A_HEADS = 16
D_A = A_HEADS * HEAD_DIM
A_WINDOWS = (128, 512, 2048)
A_DILATIONS = (1, 4, 16)
BAND = 128
N_DIL = 3
QKV_W = 3 * D_A
ROPE_THETA = 10000.0
PAST_LEN = 16384
SSM_GROUP = 16
SSM_STATE = 64
SLAB_G = 8
SLAB_S = SLAB_G * SSM_STATE
CHUNK = 8
CONV_WIDTH = 31
CONV_HALO = 32
RMS_EPS = 1e-6
LN_EPS = 1e-5
NEG_BIG = -1e30
VMEM_LIMIT_V7X = 56 * 1024 * 1024

PROMPT_TM = 1024
INPROJ_TN = 1024
OUTPROJ_TN = 1024
PW1_TN = 512
PW2_TN = 1024
FFN_TN = 256
FFN_PART_ROWS = 1024
FFN_DOWN_TM = 512
FFN_DOWN_TN = 512
GATE_TM = 512
CONV_TM = 256
NORM_TM = 512
COMBINE_TM = 512
STEP_INPROJ_TN = 512
STEP_GATE_TN = 512
STEP_OUTPROJ_TN = 512
STEP_PW1_TN = 256
STEP_PW2_TN = 512
STEP_FFN_DOWN_TN = 256


def _cp(*sem):
    return pltpu.CompilerParams(dimension_semantics=sem, vmem_limit_bytes=VMEM_LIMIT_V7X)


def _dot(a, b, **kw):
    return jnp.dot(a, b, preferred_element_type=F32, **kw)


def _dot_nt(a, b, **kw):
    return lax.dot_general(a, b, (((1,), (1,)), ((), ())), preferred_element_type=F32, **kw)


def _sigmoid(x):
    return 1.0 / (1.0 + jnp.exp(-x))


def _rmsnorm_kernel(x_ref, g_ref, o_ref):
    x = x_ref[...]
    ms = jnp.mean(x * x, axis=-1, keepdims=True)
    o_ref[...] = (x * lax.rsqrt(ms + RMS_EPS) * g_ref[...]).astype(o_ref.dtype)


def _rmsnorm(x, g, out_dtype):
    m, d = x.shape
    tm = min(m, NORM_TM)
    return pl.pallas_call(
        _rmsnorm_kernel,
        grid=(m // tm,),
        in_specs=[pl.BlockSpec((tm, d), lambda i: (i, 0)),
                  pl.BlockSpec((1, d), lambda i: (0, 0))],
        out_specs=pl.BlockSpec((tm, d), lambda i: (i, 0)),
        out_shape=jax.ShapeDtypeStruct((m, d), out_dtype),
        compiler_params=_cp("parallel"),
        name="rmsnorm",
    )(x, g.reshape(1, d))


def _rope_table_kernel(invf_ref, cos_ref, sin_ref, *, pos0, step):
    rows = cos_ref.shape[0]
    row = lax.broadcasted_iota(jnp.int32, (rows, HEAD_DIM), 0)
    lane = lax.broadcasted_iota(jnp.int32, (rows, HEAD_DIM), 1)
    pos = (row * step + pos0).astype(F32)
    ang = pos * invf_ref[...]
    cos_ref[...] = jnp.cos(ang)
    s = jnp.sin(ang)
    sin_ref[...] = jnp.where(lane < HEAD_DIM // 2, -s, s)


def _rope_tables(rows, pos0, step):
    half = HEAD_DIM // 2
    inv_freq = ROPE_THETA ** (-jnp.arange(half, dtype=F32) / half)
    invf = jnp.concatenate([inv_freq, inv_freq]).reshape(1, HEAD_DIM)
    shp = jax.ShapeDtypeStruct((rows, HEAD_DIM), F32)
    return pl.pallas_call(
        functools.partial(_rope_table_kernel, pos0=pos0, step=step),
        out_shape=(shp, shp),
        name="rope_tables",
    )(invf)


def _wspec(w, layer, kb, tn, kblk, jmap):
    if layer is None:
        return pl.BlockSpec((kb, tn), lambda *g: (kblk, jmap(*g)))
    return pl.BlockSpec((None, kb, tn), lambda *g: (layer, kblk, jmap(*g)))


def _use_w(w_ref, wb_ref, cols=slice(None)):
    w = w_ref[:, cols]
    if wb_ref is None:
        return w
    wb = w.astype(BF16)
    wb_ref[:, cols] = wb
    return wb


def _inproj_kernel(x_ref, w_ref, cos_ref, sin_ref, o_ref, *wb, tn, part):
    wb_ref = wb[0] if wb else None
    seg = (pl.program_id(1) * tn) // D_A
    is_rope = jnp.logical_and(seg < 3 * N_DIL, seg % 3 != 2)
    cos = cos_ref[...]
    sin = sin_ref[...]
    x = x_ref[...]
    for c0 in range(0, tn, part):
        acc = _dot(x, _use_w(w_ref, wb_ref, slice(c0, c0 + part)))
        for h in range(part // HEAD_DIM):
            xh = acc[:, h * HEAD_DIM:(h + 1) * HEAD_DIM]
            roped = xh * cos + pltpu.roll(xh, HEAD_DIM // 2, 1) * sin
            o_ref[:, c0 + h * HEAD_DIM:c0 + (h + 1) * HEAD_DIM] = jnp.where(is_rope, roped, xh)


def _inproj(hn, w, layer, cos, sin, tm, tn):
    m, k = hn.shape
    n = w.shape[-1]
    ct = cos.shape[0] // tm
    emit = layer is not None
    out_specs = [pl.BlockSpec((tm, tn), lambda i, j: (i, j))]
    out_shape = [jax.ShapeDtypeStruct((m, n), F32)]
    if emit:
        out_specs.append(pl.BlockSpec((k, tn), lambda i, j: (0, j)))
        out_shape.append(jax.ShapeDtypeStruct((k, n), BF16))
    res = pl.pallas_call(
        functools.partial(_inproj_kernel, tn=tn, part=min(tn, 256)),
        grid=(m // tm, n // tn),
        in_specs=[pl.BlockSpec((tm, k), lambda i, j: (i, 0)),
                  _wspec(w, layer, k, tn, 0, lambda i, j: j),
                  pl.BlockSpec((tm, HEAD_DIM), lambda i, j: (i % ct, 0)),
                  pl.BlockSpec((tm, HEAD_DIM), lambda i, j: (i % ct, 0))],
        out_specs=out_specs,
        out_shape=out_shape,
        compiler_params=_cp("parallel", "arbitrary"),
        name="inproj_rope",
    )(hn, w, cos, sin)
    return res if emit else res[0]


def _band_attn_kernel(q_ref, kp_ref, kc_ref, vp_ref, vc_ref, o_ref, lse_ref):
    has_prev = pl.program_id(2) > 0
    qi = lax.broadcasted_iota(jnp.int32, (BAND, 2 * BAND), 0)
    ki = lax.broadcasted_iota(jnp.int32, (BAND, 2 * BAND), 1)
    valid = jnp.logical_and(jnp.logical_and(ki >= qi, ki <= qi + BAND),
                            jnp.logical_or(ki >= BAND, has_prev))
    lane = lax.broadcasted_iota(jnp.int32, (BAND, HEAD_DIM), 1)
    lse_tile = jnp.zeros((BAND, HEAD_DIM), F32)
    scale = HEAD_DIM ** -0.5
    for h in range(A_HEADS):
        sl = slice(h * HEAD_DIM, (h + 1) * HEAD_DIM)
        q = q_ref[:, sl].astype(BF16)
        k = jnp.concatenate([kp_ref[:, sl], kc_ref[:, sl]], axis=0).astype(BF16)
        v = jnp.concatenate([vp_ref[:, sl], vc_ref[:, sl]], axis=0).astype(BF16)
        s = _dot_nt(q, k) * scale
        s = jnp.where(valid, s, NEG_BIG)
        m = jnp.max(s, axis=-1, keepdims=True)
        p = jnp.exp(s - m)
        l = jnp.sum(p, axis=-1, keepdims=True)
        o_ref[:, sl] = _dot(p.astype(BF16), v) / l
        lse_tile = jnp.where(lane == h, m + jnp.log(l), lse_tile)
    lse_ref[...] = lse_tile


def _band_attn_dil_kernel(*refs, d, hpb, use_prev):
    if use_prev:
        q_ref, kp_ref, kc_ref, vp_ref, vc_ref, o_ref, lse_ref, qs, ks, vs, os_, ls = refs
    else:
        q_ref, kc_ref, vc_ref, o_ref, lse_ref, qs, ks, vs, os_, ls = refs
    rows = BAND * d
    hb = pl.program_id(2)
    has_prev = pl.program_id(1) > 0
    for h in range(hpb):
        sl = slice(h * HEAD_DIM, (h + 1) * HEAD_DIM)
        qs[h] = q_ref[:, sl]
        if use_prev:
            ks[h, 0:rows] = kp_ref[:, sl]
            ks[h, rows:2 * rows] = kc_ref[:, sl]
            vs[h, 0:rows] = vp_ref[:, sl]
            vs[h, rows:2 * rows] = vc_ref[:, sl]
        else:
            ks[h] = kc_ref[:, sl]
            vs[h] = vc_ref[:, sl]

    @pl.when(hb == 0)
    def _():
        lse_ref[...] = jnp.zeros(lse_ref.shape, F32)

    nk = 2 * BAND if use_prev else BAND
    qi = lax.broadcasted_iota(jnp.int32, (1, BAND, nk), 1)
    ki = lax.broadcasted_iota(jnp.int32, (1, BAND, nk), 2)
    if use_prev:
        valid = jnp.logical_and(jnp.logical_and(ki >= qi, ki <= qi + BAND),
                                jnp.logical_or(ki >= BAND, has_prev))
    else:
        valid = ki <= qi
    scale = HEAD_DIM ** -0.5

    def residues(ref, h, base):
        return [ref[h, pl.ds(base + r, BAND, stride=d), :] for r in range(d)]

    for h in range(hpb):
        q = jnp.stack(residues(qs, h, 0)).astype(BF16)
        if use_prev:
            k = jnp.stack([jnp.concatenate([a, b], axis=0)
                           for a, b in zip(residues(ks, h, 0), residues(ks, h, rows))]).astype(BF16)
            v = jnp.stack([jnp.concatenate([a, b], axis=0)
                           for a, b in zip(residues(vs, h, 0), residues(vs, h, rows))]).astype(BF16)
        else:
            k = jnp.stack(residues(ks, h, 0)).astype(BF16)
            v = jnp.stack(residues(vs, h, 0)).astype(BF16)
        s = jnp.einsum('rqd,rkd->rqk', q, k, preferred_element_type=F32) * scale
        s = jnp.where(valid, s, NEG_BIG)
        m = jnp.max(s, axis=-1, keepdims=True)
        p = jnp.exp(s - m)
        l = jnp.sum(p, axis=-1, keepdims=True)
        o = jnp.einsum('rqk,rkd->rqd', p.astype(BF16), v, preferred_element_type=F32) / l
        lse_h = m + jnp.log(l)
        for r in range(d):
            os_[h, pl.ds(r, BAND, stride=d), :] = o[r]
            ls[h, pl.ds(r, BAND, stride=d), :] = jnp.broadcast_to(lse_h[r], (BAND, HEAD_DIM))
    lse = lse_ref[...]
    lane = lax.broadcasted_iota(jnp.int32, lse.shape, 1)
    for h in range(hpb):
        o_ref[:, h * HEAD_DIM:(h + 1) * HEAD_DIM] = os_[h]
        lse = jnp.where(lane == hb * hpb + h, ls[h], lse)
    lse_ref[...] = lse


def _band_attn(z, n_seq, t, g):
    d = A_DILATIONS[g]
    width = z.shape[1]
    zv = z.reshape(n_seq, t, width)
    if d == 1:
        qc = 3 * g

        def cur(off):
            return pl.BlockSpec((None, BAND, D_A), lambda n, r, b: (n, b, qc + off))

        def prev(off):
            return pl.BlockSpec((None, BAND, D_A), lambda n, r, b: (n, jnp.maximum(b - 1, 0), qc + off))

        o, lse = pl.pallas_call(
            _band_attn_kernel,
            grid=(n_seq, 1, t // BAND),
            in_specs=[cur(0), prev(1), cur(1), prev(2), cur(2)],
            out_specs=[pl.BlockSpec((None, BAND, D_A), lambda n, r, b: (n, b, 0)),
                       pl.BlockSpec((None, BAND, HEAD_DIM), lambda n, r, b: (n, b, 0))],
            out_shape=[jax.ShapeDtypeStruct((n_seq, t, D_A), F32),
                       jax.ShapeDtypeStruct((n_seq, t, HEAD_DIM), F32)],
            compiler_params=_cp("parallel", "parallel", "arbitrary"),
            name="band_attn_d1",
        )(zv, zv, zv, zv, zv)
        return o.reshape(n_seq * t, D_A), lse.reshape(n_seq * t, HEAD_DIM)

    rows = BAND * d
    nb = t // rows
    use_prev = nb > 1
    hpb = 8 if d <= 4 else 2
    cols = hpb * HEAD_DIM
    cpq = D_A // cols
    qc = 3 * g * cpq

    def cur(off):
        return pl.BlockSpec((None, rows, cols), lambda n, b, hb: (n, b, qc + off * cpq + hb))

    def prev(off):
        return pl.BlockSpec((None, rows, cols),
                            lambda n, b, hb: (n, jnp.maximum(b - 1, 0), qc + off * cpq + hb))

    in_specs = [cur(0), prev(1), cur(1), prev(2), cur(2)] if use_prev else [cur(0), cur(1), cur(2)]
    kv_rows = 2 * rows if use_prev else rows
    o, lse = pl.pallas_call(
        functools.partial(_band_attn_dil_kernel, d=d, hpb=hpb, use_prev=use_prev),
        grid=(n_seq, nb, cpq),
        in_specs=in_specs,
        out_specs=[pl.BlockSpec((None, rows, cols), lambda n, b, hb: (n, b, hb)),
                   pl.BlockSpec((None, rows, HEAD_DIM), lambda n, b, hb: (n, b, 0))],
        out_shape=[jax.ShapeDtypeStruct((n_seq, t, D_A), F32),
                   jax.ShapeDtypeStruct((n_seq, t, HEAD_DIM), F32)],
        scratch_shapes=[pltpu.VMEM((hpb, rows, HEAD_DIM), F32),
                        pltpu.VMEM((hpb, kv_rows, HEAD_DIM), F32),
                        pltpu.VMEM((hpb, kv_rows, HEAD_DIM), F32),
                        pltpu.VMEM((hpb, rows, HEAD_DIM), F32),
                        pltpu.VMEM((hpb, rows, HEAD_DIM), F32)],
        compiler_params=_cp("parallel", "parallel", "arbitrary"),
        name=f"band_attn_d{d}",
    )(*([zv] * len(in_specs)))
    return o.reshape(n_seq * t, D_A), lse.reshape(n_seq * t, HEAD_DIM)


def _combine_kernel(o0_ref, o1_ref, o2_ref, l0_ref, l1_ref, l2_ref, out_ref):
    a0, a1, a2 = l0_ref[...], l1_ref[...], l2_ref[...]
    m = jnp.maximum(jnp.maximum(a0, a1), a2)
    e0, e1, e2 = jnp.exp(a0 - m), jnp.exp(a1 - m), jnp.exp(a2 - m)
    inv = 1.0 / (e0 + e1 + e2)
    w0, w1, w2 = e0 * inv, e1 * inv, e2 * inv
    for h in range(A_HEADS):
        sl = slice(h * HEAD_DIM, (h + 1) * HEAD_DIM)
        out_ref[:, sl] = (w0[:, h:h + 1] * o0_ref[:, sl] + w1[:, h:h + 1] * o1_ref[:, sl]
                          + w2[:, h:h + 1] * o2_ref[:, sl]).astype(out_ref.dtype)


def _combine(os_, lses):
    m = os_[0].shape[0]
    tm = COMBINE_TM
    ob = pl.BlockSpec((tm, D_A), lambda i: (i, 0))
    lb = pl.BlockSpec((tm, HEAD_DIM), lambda i: (i, 0))
    return pl.pallas_call(
        _combine_kernel,
        grid=(m // tm,),
        in_specs=[ob, ob, ob, lb, lb, lb],
        out_specs=ob,
        out_shape=jax.ShapeDtypeStruct((m, D_A), BF16),
        compiler_params=_cp("parallel"),
        name="attn_combine",
    )(*os_, *lses)


def _step_attn_kernel(z_ref, c0_ref, c1_ref, c2_ref, o_ref):
    scale = HEAD_DIM ** -0.5
    ms, ls, accs = [], [], []
    for g, c_ref in enumerate((c0_ref, c1_ref, c2_ref)):
        base = 3 * g * A_HEADS
        q = z_ref[base:base + A_HEADS, :]
        kn = z_ref[base + A_HEADS:base + 2 * A_HEADS, :]
        vn = z_ref[base + 2 * A_HEADS:base + 3 * A_HEADS, :]
        s = jnp.sum(c_ref[:, 0] * q[None], axis=-1, keepdims=True) * scale
        s_new = jnp.sum(kn * q, axis=-1, keepdims=True) * scale
        m = jnp.maximum(jnp.max(s, axis=0), s_new)
        p = jnp.exp(s - m[None])
        p_new = jnp.exp(s_new - m)
        ls.append(jnp.sum(p, axis=0) + p_new)
        accs.append(jnp.sum(p * c_ref[:, 1], axis=0) + p_new * vn)
        ms.append(m)
    mm = jnp.maximum(jnp.maximum(ms[0], ms[1]), ms[2])
    es = [jnp.exp(m - mm) for m in ms]
    den = es[0] * ls[0] + es[1] * ls[1] + es[2] * ls[2]
    num = es[0] * accs[0] + es[1] * accs[1] + es[2] * accs[2]
    o_ref[...] = (num / den).astype(o_ref.dtype)


def _step_attn(zs, caches):
    n, width = zs.shape
    cviews, cspecs = [], []
    for g, c in enumerate(caches):
        d = A_DILATIONS[g]
        cviews.append(c.reshape(n, c.shape[1] // d, d, 2, A_HEADS, HEAD_DIM))
        cspecs.append(pl.BlockSpec((None, BAND, None, 2, A_HEADS, HEAD_DIM),
                                   lambda i: (i, 0, 0, 0, 0, 0)))
    zrows = width // HEAD_DIM
    out = pl.pallas_call(
        _step_attn_kernel,
        grid=(n,),
        in_specs=[pl.BlockSpec((None, zrows, HEAD_DIM), lambda i: (i, 0, 0))] + cspecs,
        out_specs=pl.BlockSpec((None, A_HEADS, HEAD_DIM), lambda i: (i, 0, 0)),
        out_shape=jax.ShapeDtypeStruct((n, A_HEADS, HEAD_DIM), BF16),
        compiler_params=_cp("parallel"),
        name="step_attn",
    )(zs.reshape(n, zrows, HEAD_DIM), *cviews)
    return out.reshape(n, D_A)


def _cache_shift_kernel(c_ref, nxt_ref, new_ref, o_ref, *, lb):
    last = pl.program_id(1) == pl.num_programs(1) - 1
    o_ref[0:lb - 1] = c_ref[1:lb]

    @pl.when(last)
    def _():
        o_ref[lb - 1] = new_ref[0]

    @pl.when(jnp.logical_not(last))
    def _():
        o_ref[lb - 1] = nxt_ref[0]


def _cache_shift(cache, new_kv):
    n, length = cache.shape[:2]
    lb = min(length, 4 * BAND)
    nb = length // lb
    tail = cache.shape[2:]
    zeros = (0,) * len(tail)
    return pl.pallas_call(
        functools.partial(_cache_shift_kernel, lb=lb),
        grid=(n, nb),
        in_specs=[pl.BlockSpec((None, lb) + tail, lambda i, b: (i, b) + zeros),
                  pl.BlockSpec((None, 1) + tail, lambda i, b: (i, jnp.minimum(b + 1, nb - 1) * lb) + zeros),
                  pl.BlockSpec((None, 1) + tail, lambda i, b: (i, 0) + zeros)],
        out_specs=pl.BlockSpec((None, lb) + tail, lambda i, b: (i, b) + zeros),
        out_shape=jax.ShapeDtypeStruct(cache.shape, cache.dtype),
        compiler_params=_cp("parallel", "parallel"),
        name="cache_shift",
    )(cache, cache, new_kv)


def _kv_rows_kernel(k_ref, v_ref, o_ref, *, tm):
    per = 2 * A_HEADS
    for kv, x_ref in enumerate((k_ref, v_ref)):
        for h in range(A_HEADS):
            o_ref[pl.ds(kv * A_HEADS + h, tm, stride=per), :] = x_ref[:, h * HEAD_DIM:(h + 1) * HEAD_DIM]


def _kv_rows(z, n_seq, t, g, length):
    width = z.shape[1]
    zv = z.reshape(n_seq, t, width)
    tm = min(length, 2 * BAND)
    r0 = (t - length) // tm
    c0 = 3 * g + 1
    per = 2 * A_HEADS
    out = pl.pallas_call(
        functools.partial(_kv_rows_kernel, tm=tm),
        grid=(n_seq, length // tm),
        in_specs=[pl.BlockSpec((None, tm, D_A), lambda n, i: (n, r0 + i, c0)),
                  pl.BlockSpec((None, tm, D_A), lambda n, i: (n, r0 + i, c0 + 1))],
        out_specs=pl.BlockSpec((None, tm * per, HEAD_DIM), lambda n, i: (n, i, 0)),
        out_shape=jax.ShapeDtypeStruct((n_seq, length * per, HEAD_DIM), F32),
        compiler_params=_cp("parallel", "parallel"),
        name="kv_rows",
    )(zv, zv)
    return out.reshape(n_seq, length, 2, A_HEADS, HEAD_DIM)


def _ssm_discretise(lre, lim, ldt):
    dt = jnp.exp(ldt)
    mag = jnp.exp(lre * dt)
    ar = mag * jnp.cos(lim * dt)
    ai = mag * jnp.sin(lim * dt)
    den = lre * lre + lim * lim
    zr = ((ar - 1.0) * lre + ai * lim) / den
    zi = (ai * lre - (ar - 1.0) * lim) / den
    return ar, ai, zr, zi


def _slab_mask():
    r = lax.broadcasted_iota(jnp.int32, (SLAB_G * SSM_GROUP, SLAB_S), 0) // SSM_GROUP
    c = lax.broadcasted_iota(jnp.int32, (SLAB_G * SSM_GROUP, SLAB_S), 1) // SSM_STATE
    return (r == c).astype(F32)


def _block_diag(x16, mask):
    return jnp.concatenate([x16] * SLAB_G, axis=0) * mask


def _ssm_prompt_kernel(u_ref, lre_ref, lim_ref, ldt_ref, bre_ref, bim_ref, cre_ref, cim_ref, d_ref,
                       y_ref, hlr_ref, hli_ref,
                       wer_ref, wei_ref, tz_ref, bdr_ref, bdi_ref, er_ref, ei_ref, *, n_seq, n_chunk):
    def split(a):
        ah = a.astype(BF16)
        return ah, (a - ah.astype(F32)).astype(BF16)

    def dot_nt_x3(a, b_split):
        ah, al = split(a)
        bh, bl = b_split
        return _dot_nt(ah, bh) + _dot_nt(ah, bl) + _dot_nt(al, bh)

    rows = n_seq * n_chunk

    def step_rows(j):
        return pl.ds(j, rows, stride=CHUNK)

    ar, ai, zr, zi = _ssm_discretise(lre_ref[...], lim_ref[...], ldt_ref[...])
    mask = _slab_mask()
    bre, bim = bre_ref[...], bim_ref[...]
    cre, cim = cre_ref[...], cim_ref[...]
    cre_bd = split(_block_diag(cre, mask))
    cim_bd = split(_block_diag(cim, mask))
    lanes = SLAB_G * SSM_GROUP
    tz_ref[...] = jnp.zeros(tz_ref.shape, BF16)
    pr, pim = jnp.ones_like(ar), jnp.zeros_like(ar)
    for m in range(CHUNK + 1):
        if m < CHUNK:
            wr = zr * pr - zi * pim
            wi = zr * pim + zi * pr
            we_r = _block_diag(bre * wr - bim * wi, mask)
            we_i = _block_diag(bre * wi + bim * wr, mask)
            j = CHUNK - 1 - m
            wer_ref[j * lanes:(j + 1) * lanes, :] = we_r.astype(BF16)
            wei_ref[j * lanes:(j + 1) * lanes, :] = we_i.astype(BF16)
            tz = (dot_nt_x3(we_r, cre_bd) - dot_nt_x3(we_i, cim_bd)).astype(BF16)
            for j in range(CHUNK - m):
                tz_ref[j * lanes:(j + 1) * lanes, (j + m) * lanes:(j + m + 1) * lanes] = tz
        if m >= 1:
            bdr_ref[(m - 1) * lanes:m * lanes, :] = _block_diag(cre * pr - cim * pim, mask).astype(BF16)
            bdi_ref[(m - 1) * lanes:m * lanes, :] = _block_diag(-(cre * pim + cim * pr), mask).astype(BF16)
        if m == CHUNK:
            a8r, a8i = pr, pim
        pr, pim = pr * ar - pim * ai, pr * ai + pim * ar

    xcat = jnp.concatenate([u_ref[step_rows(j), :].astype(BF16) for j in range(CHUNK)], axis=1)

    er_ref[...] = _dot(xcat, wer_ref[...])
    ei_ref[...] = _dot(xcat, wei_ref[...])

    def body(k, carry):
        out = []
        for n in range(n_seq):
            hr, him = carry[2 * n], carry[2 * n + 1]
            row = n * n_chunk + k
            e_r = er_ref[pl.ds(row, 1), :]
            e_i = ei_ref[pl.ds(row, 1), :]
            er_ref[pl.ds(row, 1), :] = hr
            ei_ref[pl.ds(row, 1), :] = him
            out.append(a8r * hr - a8i * him + e_r)
            out.append(a8r * him + a8i * hr + e_i)
        return tuple(out)

    zero = jnp.zeros((1, SLAB_S), F32)
    fin = lax.fori_loop(0, n_chunk, body, (zero,) * (2 * n_seq), unroll=2)
    for n in range(n_seq):
        hlr_ref[pl.ds(n, 1), :] = fin[2 * n]
        hli_ref[pl.ds(n, 1), :] = fin[2 * n + 1]

    y = (_dot(xcat, tz_ref[...]) + _dot_nt(er_ref[...].astype(BF16), bdr_ref[...])
         + _dot_nt(ei_ref[...].astype(BF16), bdi_ref[...]))
    for jp in range(CHUNK):
        y_ref[step_rows(jp), :] = y[:, jp * lanes:(jp + 1) * lanes] + d_ref[...] * u_ref[step_rows(jp), :]


def _ssm_params(a_re, a_im, b_re, b_im, c_re, c_im, d_skip, log_dt):
    g = a_re.shape[0]
    ns = g // SLAB_G

    def row(x):
        return x.astype(F32).reshape(ns, 1, SLAB_S)

    def chan_rows(x):
        return x.astype(F32).reshape(ns, SLAB_G, SSM_GROUP, SSM_STATE).transpose(0, 2, 1, 3) \
            .reshape(ns, SSM_GROUP, SLAB_S)

    ldt = jnp.broadcast_to(log_dt.astype(F32)[:, None], (g, SSM_STATE))
    return (row(a_re), row(a_im), row(ldt),
            chan_rows(b_re.transpose(0, 2, 1)), chan_rows(b_im.transpose(0, 2, 1)),
            chan_rows(c_re), chan_rows(c_im),
            d_skip.astype(F32).reshape(ns, 1, SLAB_G * SSM_GROUP))


def _ssm_prompt(z, n_seq, t, u_col0, sp):
    m, width = z.shape
    rows = m // CHUNK
    n_chunk = t // CHUNK
    lanes = SLAB_G * SSM_GROUP
    ns = sp[0].shape[0]
    ub = u_col0 // lanes

    def pspec(r, c):
        return pl.BlockSpec((None, r, c), lambda s: (s, 0, 0))

    y, hlr, hli = pl.pallas_call(
        functools.partial(_ssm_prompt_kernel, n_seq=n_seq, n_chunk=n_chunk),
        grid=(ns,),
        in_specs=[pl.BlockSpec((m, lanes), lambda s: (0, ub + s)),
                  pspec(1, SLAB_S), pspec(1, SLAB_S), pspec(1, SLAB_S),
                  pspec(SSM_GROUP, SLAB_S), pspec(SSM_GROUP, SLAB_S),
                  pspec(SSM_GROUP, SLAB_S), pspec(SSM_GROUP, SLAB_S), pspec(1, lanes)],
        out_specs=[pl.BlockSpec((m, lanes), lambda s: (0, s)),
                   pl.BlockSpec((None, n_seq, SLAB_S), lambda s: (s, 0, 0)),
                   pl.BlockSpec((None, n_seq, SLAB_S), lambda s: (s, 0, 0))],
        out_shape=[jax.ShapeDtypeStruct((m, ns * lanes), F32),
                   jax.ShapeDtypeStruct((ns, n_seq, SLAB_S), F32),
                   jax.ShapeDtypeStruct((ns, n_seq, SLAB_S), F32)],
        scratch_shapes=[pltpu.VMEM((CHUNK * lanes, SLAB_S), BF16),
                        pltpu.VMEM((CHUNK * lanes, SLAB_S), BF16),
                        pltpu.VMEM((CHUNK * lanes, CHUNK * lanes), BF16),
                        pltpu.VMEM((CHUNK * lanes, SLAB_S), BF16),
                        pltpu.VMEM((CHUNK * lanes, SLAB_S), BF16),
                        pltpu.VMEM((rows, SLAB_S), F32),
                        pltpu.VMEM((rows, SLAB_S), F32)],
        compiler_params=_cp("parallel"),
        name="ssm_prompt",
    )(z, *sp)
    return y, hlr, hli


def _ssm_step_kernel(u_ref, h0r_ref, h0i_ref, lre_ref, lim_ref, ldt_ref, bre_ref, bim_ref,
                     cre_ref, cim_ref, d_ref, y_ref, hr_ref, hi_ref):
    ar, ai, zr, zi = _ssm_discretise(lre_ref[...], lim_ref[...], ldt_ref[...])
    mask = _slab_mask()
    bre, bim = bre_ref[...], bim_ref[...]
    we_r = _block_diag(bre * zr - bim * zi, mask).astype(BF16)
    we_i = _block_diag(bre * zi + bim * zr, mask).astype(BF16)
    u = u_ref[...]
    ub = u.astype(BF16)
    h0r, h0i = h0r_ref[...], h0i_ref[...]
    hr = ar * h0r - ai * h0i + _dot(ub, we_r)
    him = ar * h0i + ai * h0r + _dot(ub, we_i)
    hr_ref[...] = hr
    hi_ref[...] = him
    cr_bd = _block_diag(cre_ref[...], mask).astype(BF16)
    ci_bd = _block_diag(cim_ref[...], mask).astype(BF16)
    y_ref[...] = (_dot_nt(hr.astype(BF16), cr_bd) - _dot_nt(him.astype(BF16), ci_bd)
                  + d_ref[...] * u)


def _ssm_step(zs, u_col0, h0r, h0i, sp):
    n = zs.shape[0]
    lanes = SLAB_G * SSM_GROUP
    ns = sp[0].shape[0]
    ub = u_col0 // lanes

    def pspec(r, c):
        return pl.BlockSpec((None, r, c), lambda s: (s, 0, 0))

    hspec = pl.BlockSpec((None, n, SLAB_S), lambda s: (s, 0, 0))
    return pl.pallas_call(
        _ssm_step_kernel,
        grid=(ns,),
        in_specs=[pl.BlockSpec((n, lanes), lambda s: (0, ub + s)), hspec, hspec,
                  pspec(1, SLAB_S), pspec(1, SLAB_S), pspec(1, SLAB_S),
                  pspec(SSM_GROUP, SLAB_S), pspec(SSM_GROUP, SLAB_S),
                  pspec(SSM_GROUP, SLAB_S), pspec(SSM_GROUP, SLAB_S), pspec(1, lanes)],
        out_specs=[pl.BlockSpec((n, lanes), lambda s: (0, s)), hspec, hspec],
        out_shape=[jax.ShapeDtypeStruct((n, ns * lanes), F32),
                   jax.ShapeDtypeStruct((ns, n, SLAB_S), F32),
                   jax.ShapeDtypeStruct((ns, n, SLAB_S), F32)],
        compiler_params=_cp("parallel"),
        name="ssm_step",
    )(zs, h0r, h0i, *sp)


def _gelu_tanh(x):
    return 0.5 * x * (1.0 + jnp.tanh(math.sqrt(2.0 / math.pi) * (x + 0.044715 * (x * x * x))))


def _ssm_gate_kernel(y_ref, w_ref, b_ref, o_ref):
    yg = _gelu_tanh(y_ref[...])
    gate = _dot(yg.astype(BF16), w_ref[...]) + b_ref[...]
    o_ref[...] = (yg * _sigmoid(gate)).astype(o_ref.dtype)


def _ssm_gate(y, w, b, tm):
    m, d = y.shape
    return pl.pallas_call(
        _ssm_gate_kernel,
        grid=(m // tm,),
        in_specs=[pl.BlockSpec((tm, d), lambda i: (i, 0)),
                  pl.BlockSpec((d, d), lambda i: (0, 0)),
                  pl.BlockSpec((1, d), lambda i: (0, 0))],
        out_specs=pl.BlockSpec((tm, d), lambda i: (i, 0)),
        out_shape=jax.ShapeDtypeStruct((m, d), BF16),
        compiler_params=_cp("parallel"),
        name="ssm_gate",
    )(y, w, b.reshape(1, d).astype(F32))


def _ssm_gate_step_kernel(y_ref, yc_ref, w_ref, b_ref, o_ref, wb_ref):
    yg = _gelu_tanh(y_ref[...])
    gate = _dot(yg.astype(BF16), _use_w(w_ref, wb_ref)) + b_ref[...]
    o_ref[...] = (_gelu_tanh(yc_ref[...]) * _sigmoid(gate)).astype(o_ref.dtype)


def _ssm_gate_step(y, w, layer, b, tn):
    m, d = y.shape
    return pl.pallas_call(
        _ssm_gate_step_kernel,
        grid=(d // tn,),
        in_specs=[pl.BlockSpec((m, d), lambda j: (0, 0)),
                  pl.BlockSpec((m, tn), lambda j: (0, j)),
                  _wspec(w, layer, d, tn, 0, lambda j: j),
                  pl.BlockSpec((1, tn), lambda j: (0, j))],
        out_specs=[pl.BlockSpec((m, tn), lambda j: (0, j)),
                   pl.BlockSpec((d, tn), lambda j: (0, j))],
        out_shape=[jax.ShapeDtypeStruct((m, d), BF16), jax.ShapeDtypeStruct((d, d), BF16)],
        compiler_params=_cp("arbitrary"),
        name="ssm_gate_step",
    )(y, y, w, b.reshape(1, d).astype(F32))


def _outproj_kernel(a_ref, s_ref, wa_ref, ws_ref, x_ref, o_ref, *wb):
    wab_ref, wsb_ref = wb if wb else (None, None)
    o_ref[...] = (x_ref[...] + _dot(a_ref[...], _use_w(wa_ref, wab_ref))
                  + _dot(s_ref[...], _use_w(ws_ref, wsb_ref)))


def _outproj(attn, ssm, wa, ws, layer, x, tm, tn):
    m, ka = attn.shape
    n = wa.shape[-1]
    emit = layer is not None
    out_specs = [pl.BlockSpec((tm, tn), lambda i, j: (i, j))]
    out_shape = [jax.ShapeDtypeStruct((m, n), F32)]
    if emit:
        out_specs += [pl.BlockSpec((ka, tn), lambda i, j: (0, j))] * 2
        out_shape += [jax.ShapeDtypeStruct((ka, n), BF16)] * 2
    res = pl.pallas_call(
        _outproj_kernel,
        grid=(m // tm, n // tn),
        in_specs=[pl.BlockSpec((tm, ka), lambda i, j: (i, 0)),
                  pl.BlockSpec((tm, ka), lambda i, j: (i, 0)),
                  _wspec(wa, layer, ka, tn, 0, lambda i, j: j),
                  _wspec(ws, layer, ka, tn, 1 if emit else 0, lambda i, j: j),
                  pl.BlockSpec((tm, tn), lambda i, j: (i, j))],
        out_specs=out_specs,
        out_shape=out_shape,
        compiler_params=_cp("parallel", "arbitrary"),
        name="outproj",
    )(attn, ssm, wa, ws, x)
    return res if emit else res[0]


def _mm_res_kernel(a_ref, w_ref, b_ref, x_ref, o_ref, *wb):
    wb_ref = wb[0] if wb else None
    o_ref[...] = x_ref[...] + (_dot(a_ref[...], _use_w(w_ref, wb_ref)) + b_ref[...])


def _mm_res(a, w, layer, b, x, tm, tn, name):
    m, k = a.shape
    n = w.shape[-1]
    emit = layer is not None
    out_specs = [pl.BlockSpec((tm, tn), lambda i, j: (i, j))]
    out_shape = [jax.ShapeDtypeStruct((m, n), F32)]
    if emit:
        out_specs.append(pl.BlockSpec((k, tn), lambda i, j: (0, j)))
        out_shape.append(jax.ShapeDtypeStruct((k, n), BF16))
    res = pl.pallas_call(
        _mm_res_kernel,
        grid=(m // tm, n // tn),
        in_specs=[pl.BlockSpec((tm, k), lambda i, j: (i, 0)),
                  _wspec(w, layer, k, tn, 0, lambda i, j: j),
                  pl.BlockSpec((1, tn), lambda i, j: (0, j)),
                  pl.BlockSpec((tm, tn), lambda i, j: (i, j))],
        out_specs=out_specs,
        out_shape=out_shape,
        compiler_params=_cp("parallel", "arbitrary"),
        name=name,
    )(a, w, b.reshape(1, n).astype(F32), x)
    return res if emit else res[0]


def _pw1_kernel(x_ref, wa_ref, wg_ref, ba_ref, bg_ref, o_ref, *wb):
    wab_ref, wgb_ref = wb if wb else (None, None)
    x = x_ref[...]
    a = _dot(x, _use_w(wa_ref, wab_ref)) + ba_ref[...]
    gate = _dot(x, _use_w(wg_ref, wgb_ref)) + bg_ref[...]
    o_ref[...] = a * _sigmoid(gate)


def _pw1(hn, wa, wg, layer, b, tm, tn):
    m, k = hn.shape
    n = b.shape[0] // 2
    nj = n // tn
    emit = layer is not None
    b2 = b.reshape(1, 2 * n).astype(F32)
    out_specs = [pl.BlockSpec((tm, tn), lambda i, j: (i, j))]
    out_shape = [jax.ShapeDtypeStruct((m, n), F32)]
    if emit:
        out_specs += [pl.BlockSpec((k, tn), lambda i, j: (0, j))] * 2
        out_shape += [jax.ShapeDtypeStruct((k, n), BF16)] * 2
    goff = nj if emit else 0
    res = pl.pallas_call(
        _pw1_kernel,
        grid=(m // tm, nj),
        in_specs=[pl.BlockSpec((tm, k), lambda i, j: (i, 0)),
                  _wspec(wa, layer, k, tn, 0, lambda i, j: j),
                  _wspec(wg, layer, k, tn, 0, lambda i, j: j + goff),
                  pl.BlockSpec((1, tn), lambda i, j: (0, j)),
                  pl.BlockSpec((1, tn), lambda i, j: (0, j + nj))],
        out_specs=out_specs,
        out_shape=out_shape,
        compiler_params=_cp("parallel", "arbitrary"),
        name="conv_pw1_glu",
    )(hn, wa, wg, b2, b2)
    return res if emit else res[0]


def _ln_swish(y, g, b):
    mu = jnp.mean(y, axis=-1, keepdims=True)
    yc = y - mu
    var = jnp.mean(yc * yc, axis=-1, keepdims=True)
    yn = yc * lax.rsqrt(var + LN_EPS) * g + b
    return yn * _sigmoid(yn)


def _conv31_kernel(u_ref, halo_ref, w_ref, b_ref, g_ref, bb_ref, o_ref, ext_ref, y3_ref, y_ref,
                   *, tm, tiles_per_seq, slab):
    seq_start = (pl.program_id(0) % tiles_per_seq) == 0
    nc = u_ref.shape[1] // HEAD_DIM
    for c in range(nc):
        cs = slice(c * HEAD_DIM, (c + 1) * HEAD_DIM)
        ext_ref[pl.ds(c, CONV_HALO, stride=slab), :] = jnp.where(seq_start, 0.0, halo_ref[:, cs])
        ext_ref[pl.ds(CONV_HALO * slab + c, tm, stride=slab), :] = u_ref[:, cs]
    off = CONV_HALO - (CONV_WIDTH - 1)
    rw = 8

    def body(i, carry):
        for ct in range(nc // 8):
            base = pl.multiple_of((i * rw + off) * slab, 8) + ct * 8
            win = [ext_ref[pl.ds(base + s * slab, 8), :] for s in range(rw + CONV_WIDTH - 1)]
            acc = [None] * rw
            for j in range(CONV_WIDTH):
                wj = w_ref[j, ct * 8:(ct + 1) * 8, :]
                for tt in range(rw):
                    term = wj * win[tt + j]
                    acc[tt] = term if j == 0 else acc[tt] + term
            bias = b_ref[ct * 8:(ct + 1) * 8, :]
            obase = pl.multiple_of(i * rw * slab, 8) + ct * 8
            for tt in range(rw):
                y3_ref[pl.ds(obase + tt * slab, 8), :] = acc[tt] + bias
        return carry

    lax.fori_loop(0, tm // rw, body, 0)
    for c in range(nc):
        y_ref[:, c * HEAD_DIM:(c + 1) * HEAD_DIM] = y3_ref[pl.ds(c, tm, stride=slab), :]
    o_ref[...] = _ln_swish(y_ref[...], g_ref[...], bb_ref[...]).astype(o_ref.dtype)


def _conv31(u, t, w, b, g, bb, tm):
    m, d = u.shape
    nc = d // HEAD_DIM
    slab = nc + 8
    row = lambda v: v.reshape(1, d).astype(F32)
    rspec = pl.BlockSpec((1, d), lambda i: (0, 0))
    hb = tm // CONV_HALO
    return pl.pallas_call(
        functools.partial(_conv31_kernel, tm=tm, tiles_per_seq=t // tm, slab=slab),
        grid=(m // tm,),
        in_specs=[pl.BlockSpec((tm, d), lambda i: (i, 0)),
                  pl.BlockSpec((CONV_HALO, d), lambda i: (jnp.maximum(i * hb - 1, 0), 0)),
                  pl.BlockSpec((CONV_WIDTH, nc, HEAD_DIM), lambda i: (0, 0, 0)),
                  pl.BlockSpec((nc, HEAD_DIM), lambda i: (0, 0)), rspec, rspec],
        out_specs=pl.BlockSpec((tm, d), lambda i: (i, 0)),
        out_shape=jax.ShapeDtypeStruct((m, d), BF16),
        scratch_shapes=[pltpu.VMEM(((tm + CONV_HALO) * slab, HEAD_DIM), F32),
                        pltpu.VMEM((tm * slab, HEAD_DIM), F32),
                        pltpu.VMEM((tm, d), F32)],
        compiler_params=_cp("parallel"),
        name="conv31_ln_swish",
    )(u, u, w.astype(F32).reshape(CONV_WIDTH, nc, HEAD_DIM), b.reshape(nc, HEAD_DIM).astype(F32),
      row(g), row(bb))


def _conv31_step_kernel(u_ref, st_ref, w_ref, b_ref, g_ref, bb_ref, o_ref):
    acc = w_ref[0:1, :] * st_ref[0]
    for j in range(1, CONV_WIDTH - 1):
        acc = acc + w_ref[j:j + 1, :] * st_ref[j]
    acc = acc + w_ref[CONV_WIDTH - 1:CONV_WIDTH, :] * u_ref[...] + b_ref[...]
    o_ref[...] = _ln_swish(acc, g_ref[...], bb_ref[...]).astype(o_ref.dtype)


def _conv31_step(u, state_t, w, b, g, bb):
    n, d = u.shape
    row = lambda v: v.reshape(1, d).astype(F32)
    return pl.pallas_call(
        _conv31_step_kernel,
        out_shape=jax.ShapeDtypeStruct((n, d), BF16),
        compiler_params=pltpu.CompilerParams(vmem_limit_bytes=VMEM_LIMIT_V7X),
        name="conv31_step",
    )(u, state_t, w.astype(F32), row(b), row(g), row(bb))


def _ffn_up_kernel(x_ref, w_ref, cwg_ref, cwv_ref, cbg_ref, cbv_ref,
                   act_ref, zlg_ref, zlv_ref, z_scr, *, t, tn, rows):
    nc = tn // HEAD_DIM
    cw = (cwg_ref, cwv_ref)
    cb = (cbg_ref, cbv_ref)
    hist = [jnp.zeros((8, HEAD_DIM), F32)] * (2 * nc)
    for p, r0 in enumerate(range(0, t, rows)):
        z = _dot(x_ref[r0:r0 + rows, :], w_ref[...])
        zc = []
        for c in range(2 * nc):
            zs = z_scr.at[p, c]
            zn = z[:, c * HEAD_DIM:(c + 1) * HEAD_DIM]
            zs[0:8, :] = hist[c]
            zs[8:, :] = zn
            hist[c] = zn[rows - 8:, :]
            half, cc = divmod(c, nc)
            ls = slice(cc * HEAD_DIM, (cc + 1) * HEAD_DIM)
            zc.append(cw[half][2:3, ls] * zn + cw[half][1:2, ls] * zs[pl.ds(7, rows, stride=1), :]
                      + cw[half][0:1, ls] * zs[pl.ds(6, rows, stride=1), :] + cb[half][:, ls])
        for c in range(nc):
            gate = zc[c]
            act_ref[r0:r0 + rows, c * HEAD_DIM:(c + 1) * HEAD_DIM] = (
                gate * _sigmoid(gate) * zc[nc + c]).astype(act_ref.dtype)
    for c in range(nc):
        zlg_ref[:, c * HEAD_DIM:(c + 1) * HEAD_DIM] = hist[c]
        zlv_ref[:, c * HEAD_DIM:(c + 1) * HEAD_DIM] = hist[nc + c]


def _ffn_up(hn, t, w, cw, cb, tn):
    m, k = hn.shape
    dff = w.shape[1] // 2
    nj = dff // tn
    cw = cw.astype(F32)
    cb2 = cb.reshape(1, 2 * dff).astype(F32)
    zl = jax.ShapeDtypeStruct((m // t, 8, dff), F32)
    rows = min(t, FFN_PART_ROWS)
    return pl.pallas_call(
        functools.partial(_ffn_up_kernel, t=t, tn=tn, rows=rows),
        grid=(m // t, nj),
        in_specs=[pl.BlockSpec((t, k), lambda i, j: (i, 0)),
                  pl.BlockSpec((k, 2 * tn), lambda i, j: (0, j)),
                  pl.BlockSpec((3, tn), lambda i, j: (0, j)),
                  pl.BlockSpec((3, tn), lambda i, j: (0, j + nj)),
                  pl.BlockSpec((1, tn), lambda i, j: (0, j)),
                  pl.BlockSpec((1, tn), lambda i, j: (0, j + nj))],
        out_specs=[pl.BlockSpec((t, tn), lambda i, j: (i, j)),
                   pl.BlockSpec((None, 8, tn), lambda i, j: (i, 0, j)),
                   pl.BlockSpec((None, 8, tn), lambda i, j: (i, 0, j))],
        out_shape=[jax.ShapeDtypeStruct((m, dff), BF16), zl, zl],
        scratch_shapes=[pltpu.VMEM((t // rows, 2 * tn // HEAD_DIM, 8 + rows, HEAD_DIM), F32)],
        compiler_params=_cp("parallel", "arbitrary"),
        name="ffn_up_conv_glu",
    )(hn, w, cw, cw, cb2, cb2)


def _ffn_up_step_kernel(x_ref, wg_ref, wv_ref, sg_ref, sv_ref, cwg_ref, cwv_ref, cbg_ref, cbv_ref,
                        act_ref, zg_ref, zv_ref, wb_ref):
    x = x_ref[...]
    tn = wg_ref.shape[1]

    def half(w_ref, c0, s_ref, cw_ref, cb_ref, z_ref):
        wb = w_ref[...].astype(BF16)
        wb_ref[:, c0:c0 + tn] = wb
        z = _dot(x, wb)
        z_ref[...] = z
        return cw_ref[2:3, :] * z + cw_ref[1:2, :] * s_ref[1] + cw_ref[0:1, :] * s_ref[0] + cb_ref[...]

    gate = half(wg_ref, 0, sg_ref, cwg_ref, cbg_ref, zg_ref)
    val = half(wv_ref, tn, sv_ref, cwv_ref, cbv_ref, zv_ref)
    act_ref[...] = (gate * _sigmoid(gate) * val).astype(act_ref.dtype)


def _ffn_up_step(hn, state_t, w, layer, cw, cb, tn):
    n, k = hn.shape
    dff = w.shape[-1] // 2
    nj = dff // tn
    cw = cw.astype(F32)
    cb2 = cb.reshape(1, 2 * dff).astype(F32)
    zs = jax.ShapeDtypeStruct((n, dff), F32)
    return pl.pallas_call(
        _ffn_up_step_kernel,
        grid=(nj,),
        in_specs=[pl.BlockSpec((n, k), lambda j: (0, 0)),
                  _wspec(w, layer, k, tn, 0, lambda j: j),
                  _wspec(w, layer, k, tn, 0, lambda j: j + nj),
                  pl.BlockSpec((2, n, tn), lambda j: (0, 0, j)),
                  pl.BlockSpec((2, n, tn), lambda j: (0, 0, j + nj)),
                  pl.BlockSpec((3, tn), lambda j: (0, j)),
                  pl.BlockSpec((3, tn), lambda j: (0, j + nj)),
                  pl.BlockSpec((1, tn), lambda j: (0, j)),
                  pl.BlockSpec((1, tn), lambda j: (0, j + nj))],
        out_specs=[pl.BlockSpec((n, tn), lambda j: (0, j)),
                   pl.BlockSpec((n, tn), lambda j: (0, j)),
                   pl.BlockSpec((n, tn), lambda j: (0, j)),
                   pl.BlockSpec((k, 2 * tn), lambda j: (0, j))],
        out_shape=[jax.ShapeDtypeStruct((n, dff), BF16), zs, zs,
                   jax.ShapeDtypeStruct((k, 2 * dff), BF16)],
        compiler_params=_cp("arbitrary"),
        name="ffn_up_step",
    )(hn, w, w, state_t, state_t, cw, cw, cb2, cb2)


def _ffn_prompt(x, t, norm_g, w_up, cw, cb, w_down):
    hn = _rmsnorm(x, norm_g, BF16)
    act, zlg, zlv = _ffn_up(hn, t, w_up, cw, cb, FFN_TN)
    last = jnp.concatenate([zlg[:, 6:8], zlv[:, 6:8]], axis=-1)
    xo = _mm_res(act, w_down, None, jnp.zeros((w_down.shape[1],), F32), x, FFN_DOWN_TM, FFN_DOWN_TN, "ffn_down")
    return xo, last


def _ffn_step(x, prev_state_t, norm_g, w_up, w_down, layer, cw, cb):
    m = x.shape[0]
    hn = _rmsnorm(x, norm_g, BF16)
    act, zg, zv, wu_b = _ffn_up_step(hn, prev_state_t, w_up, layer, cw, cb, FFN_TN)
    znew = jnp.concatenate([zg, zv], axis=-1)
    last = jnp.concatenate([prev_state_t[1][:, None], znew[:, None]], axis=1)
    xo, wd_b = _mm_res(act, w_down, layer, jnp.zeros((w_down.shape[-1],), F32), x, m, STEP_FFN_DOWN_TN,
                       "ffn_down_step")
    return xo, last, (wu_b, wd_b)


def kernel(x_prompt, x_sample, cache_a0, cache_a1, cache_a2, state_ssm, state_conv, state_ffn, norm_mix, norm_ffn, norm_final, w_in_even, w_out_even, ssm_a_re, ssm_a_im, ssm_b_re, ssm_b_im, ssm_c_re, ssm_c_im, ssm_d, ssm_log_dt, w_glu, b_glu, w_pw1, b_pw1, w_dw, b_dw, ln_g, ln_b, w_pw2, b_pw2, w_up, ffn_dw, ffn_dw_b, w_down):
    n_p, t_p, d_model = x_prompt.shape
    n_s = x_sample.shape[0]
    caches = (cache_a0, cache_a1, cache_a2)
    u_col0 = N_DIL * QKV_W
    sp = _ssm_params(ssm_a_re[0], ssm_a_im[0], ssm_b_re[0], ssm_b_im[0], ssm_c_re[0], ssm_c_im[0],
                     ssm_d[0], ssm_log_dt[0])
    ns = sp[0].shape[0]

    def ssm_state(hr, him, n):
        def unslab(h):
            return h.reshape(ns, n, SLAB_G, SSM_STATE).transpose(1, 0, 2, 3).reshape(n, ns * SLAB_G, SSM_STATE)
        return jnp.stack([unslab(hr), unslab(him)], axis=-1)[None]

    xs = x_sample.reshape(n_s, d_model)
    cos_s, sin_s = _rope_tables(n_s, PAST_LEN, 0)
    hn = _rmsnorm(xs, norm_mix[0], BF16)
    zs, w_in_b = _inproj(hn, w_in_even, 0, cos_s, sin_s, n_s, STEP_INPROJ_TN)
    attn_s = _step_attn(zs, [c[0] for c in caches])

    def slab(h):
        return h.reshape(n_s, ns, SLAB_S).transpose(1, 0, 2)

    y_s, hr_s, hi_s = _ssm_step(zs, u_col0, slab(state_ssm[0, ..., 0]), slab(state_ssm[0, ..., 1]), sp)
    gate_s, w_glu_b = _ssm_gate_step(y_s, w_glu, 0, b_glu[0], STEP_GATE_TN)
    xs, w_oa_b, w_os_b = _outproj(attn_s, gate_s, w_out_even, w_out_even, 0, xs, n_s, STEP_OUTPROJ_TN)
    xs, ffn0_s, wf0 = _ffn_step(xs, state_ffn[0].transpose(1, 0, 2), norm_ffn[0], w_up, w_down, 0,
                                ffn_dw[0], ffn_dw_b[0])
    hn = _rmsnorm(xs, norm_mix[1], BF16)
    us, w_pa_b, w_pg_b = _pw1(hn, w_pw1, w_pw1, 0, b_pw1[0], n_s, STEP_PW1_TN)
    cs = _conv31_step(us, state_conv[0].transpose(1, 0, 2), w_dw[0], b_dw[0], ln_g[0], ln_b[0])
    conv_s = jnp.concatenate([state_conv[0][:, 1:], us[:, None]], axis=1)
    xs, w_pw2_b = _mm_res(cs, w_pw2, 0, b_pw2[0], xs, n_s, STEP_PW2_TN, "conv_pw2_step")
    xs, ffn1_s, wf1 = _ffn_step(xs, state_ffn[1].transpose(1, 0, 2), norm_ffn[1], w_up, w_down, 1,
                                ffn_dw[1], ffn_dw_b[1])
    y_sample = _rmsnorm(xs, norm_final, F32).reshape(n_s, 1, d_model)

    a_s = []
    for g in range(N_DIL):
        kv = zs[:, g * QKV_W + D_A:(g + 1) * QKV_W].reshape(n_s, 1, 2, A_HEADS, HEAD_DIM)
        a_s.append(_cache_shift(caches[g][0], kv)[None])
    ssm_s = ssm_state(hr_s, hi_s, n_s)
    conv_s = conv_s[None]
    ffn_s = jnp.stack([ffn0_s, ffn1_s])

    mp = n_p * t_p
    tm = PROMPT_TM
    x = x_prompt.reshape(mp, d_model)
    cos, sin = _rope_tables(t_p, 0, 1)
    hn = _rmsnorm(x, norm_mix[0], BF16)
    z = _inproj(hn, w_in_b, None, cos, sin, tm, INPROJ_TN)
    outs, lses = [], []
    for g in range(N_DIL):
        o, lse = _band_attn(z, n_p, t_p, g)
        outs.append(o)
        lses.append(lse)
    attn = _combine(outs, lses)
    y_ssm, hlr, hli = _ssm_prompt(z, n_p, t_p, u_col0, sp)
    gate_p = _ssm_gate(y_ssm, w_glu_b, b_glu[0], GATE_TM)
    x = _outproj(attn, gate_p, w_oa_b, w_os_b, None, x, tm, OUTPROJ_TN)
    x, ffn0_p = _ffn_prompt(x, t_p, norm_ffn[0], wf0[0], ffn_dw[0], ffn_dw_b[0], wf0[1])
    hn = _rmsnorm(x, norm_mix[1], BF16)
    u = _pw1(hn, w_pa_b, w_pg_b, None, b_pw1[0], tm, PW1_TN)
    c = _conv31(u, t_p, w_dw[0], b_dw[0], ln_g[0], ln_b[0], CONV_TM)
    conv_p = u.reshape(n_p, t_p, -1)[None, :, t_p - (CONV_WIDTH - 1):]
    x = _mm_res(c, w_pw2_b, None, b_pw2[0], x, tm, PW2_TN, "conv_pw2")
    x, ffn1_p = _ffn_prompt(x, t_p, norm_ffn[1], wf1[0], ffn_dw[1], ffn_dw_b[1], wf1[1])
    y_prompt = _rmsnorm(x, norm_final, F32).reshape(n_p, t_p, d_model)

    a_p = []
    for g in range(N_DIL):
        a_p.append(_kv_rows(z, n_p, t_p, g, min(A_WINDOWS[g], PAST_LEN))[None])
    ssm_p = ssm_state(hlr, hli, n_p)
    ffn_p = jnp.stack([ffn0_p, ffn1_p])

    return (y_prompt, y_sample, a_p[0], a_s[0], a_p[1], a_s[1], a_p[2], a_s[2],
            ssm_p, ssm_s, conv_p, conv_s, ffn_p, ffn_s)
```

```python
import functools
import math

import jax
import jax.numpy as jnp
from jax import lax
from jax.experimental import pallas as pl
from jax.experimental.pallas import tpu as pltpu

F32 = jnp.float32
BF16 = jnp.bfloat16

HEAD_DIM = 128
A_HEADS = 16
D_A = A_HEADS * HEAD_DIM
A_WINDOWS = (128, 512, 2048)
A_DILATIONS = (1, 4, 16)
BAND = 128
N_DIL = 3
QKV_W = 3 * D_A
ROPE_THETA = 10000.0
PAST_LEN = 16384
SSM_GROUP = 16
SSM_STATE = 64
SLAB_G = 8
SLAB_S = SLAB_G * SSM_STATE
CHUNK = 8
CONV_WIDTH = 31
CONV_HALO = 32
RMS_EPS = 1e-6
LN_EPS = 1e-5
NEG_BIG = -1e30
VMEM_LIMIT_V7X = 56 * 1024 * 1024

PROMPT_TM = 1024
INPROJ_TN = 1024
OUTPROJ_TN = 1024
PW1_TN = 512
PW2_TN = 1024
FFN_TN = 256
FFN_PART_ROWS = 1024
FFN_DOWN_TM = 512
FFN_DOWN_TN = 512
GATE_TM = 512
CONV_TM = 256
NORM_TM = 512
COMBINE_TM = 512
STEP_INPROJ_TN = 512
STEP_GATE_TN = 512
STEP_OUTPROJ_TN = 512
STEP_PW1_TN = 256
STEP_PW2_TN = 512
STEP_FFN_DOWN_TN = 256


def _cp(*sem):
    return pltpu.CompilerParams(dimension_semantics=sem, vmem_limit_bytes=VMEM_LIMIT_V7X)


def _dot(a, b, **kw):
    return jnp.dot(a, b, preferred_element_type=F32, **kw)


def _dot_nt(a, b, **kw):
    return lax.dot_general(a, b, (((1,), (1,)), ((), ())), preferred_element_type=F32, **kw)


def _sigmoid(x):
    return 1.0 / (1.0 + jnp.exp(-x))


def _rmsnorm_kernel(x_ref, g_ref, o_ref):
    x = x_ref[...]
    ms = jnp.mean(x * x, axis=-1, keepdims=True)
    o_ref[...] = (x * lax.rsqrt(ms + RMS_EPS) * g_ref[...]).astype(o_ref.dtype)


def _rmsnorm(x, g, out_dtype):
    m, d = x.shape
    tm = min(m, NORM_TM)
    return pl.pallas_call(
        _rmsnorm_kernel,
        grid=(m // tm,),
        in_specs=[pl.BlockSpec((tm, d), lambda i: (i, 0)),
                  pl.BlockSpec((1, d), lambda i: (0, 0))],
        out_specs=pl.BlockSpec((tm, d), lambda i: (i, 0)),
        out_shape=jax.ShapeDtypeStruct((m, d), out_dtype),
        compiler_params=_cp("parallel"),
        name="rmsnorm",
    )(x, g.reshape(1, d))


def _rope_table_kernel(invf_ref, cos_ref, sin_ref, *, pos0, step):
    rows = cos_ref.shape[0]
    row = lax.broadcasted_iota(jnp.int32, (rows, HEAD_DIM), 0)
    lane = lax.broadcasted_iota(jnp.int32, (rows, HEAD_DIM), 1)
    pos = (row * step + pos0).astype(F32)
    ang = pos * invf_ref[...]
    cos_ref[...] = jnp.cos(ang)
    s = jnp.sin(ang)
    sin_ref[...] = jnp.where(lane < HEAD_DIM // 2, -s, s)


def _rope_tables(rows, pos0, step):
    half = HEAD_DIM // 2
    inv_freq = ROPE_THETA ** (-jnp.arange(half, dtype=F32) / half)
    invf = jnp.concatenate([inv_freq, inv_freq]).reshape(1, HEAD_DIM)
    shp = jax.ShapeDtypeStruct((rows, HEAD_DIM), F32)
    return pl.pallas_call(
        functools.partial(_rope_table_kernel, pos0=pos0, step=step),
        out_shape=(shp, shp),
        name="rope_tables",
    )(invf)


def _wspec(w, layer, kb, tn, kblk, jmap):
    if layer is None:
        return pl.BlockSpec((kb, tn), lambda *g: (kblk, jmap(*g)))
    return pl.BlockSpec((None, kb, tn), lambda *g: (layer, kblk, jmap(*g)))


def _use_w(w_ref, wb_ref, cols=slice(None)):
    w = w_ref[:, cols]
    if wb_ref is None:
        return w
    wb = w.astype(BF16)
    wb_ref[:, cols] = wb
    return wb


def _inproj_kernel(x_ref, w_ref, cos_ref, sin_ref, o_ref, *wb, tn, part):
    wb_ref = wb[0] if wb else None
    seg = (pl.program_id(1) * tn) // D_A
    is_rope = jnp.logical_and(seg < 3 * N_DIL, seg % 3 != 2)
    def body(rope):
        x = x_ref[...]
        for c0 in range(0, tn, part):
            acc = _dot(x, _use_w(w_ref, wb_ref, slice(c0, c0 + part)))
            for h in range(part // HEAD_DIM):
                xh = acc[:, h * HEAD_DIM:(h + 1) * HEAD_DIM]
                if rope:
                    xh = xh * cos_ref[...] + pltpu.roll(xh, HEAD_DIM // 2, 1) * sin_ref[...]
                o_ref[:, c0 + h * HEAD_DIM:c0 + (h + 1) * HEAD_DIM] = xh

    @pl.when(is_rope)
    def _():
        body(True)

    @pl.when(jnp.logical_not(is_rope))
    def _():
        body(False)


def _inproj(hn, w, layer, cos, sin, tm, tn):
    m, k = hn.shape
    n = w.shape[-1]
    ct = cos.shape[0] // tm
    emit = layer is not None
    out_specs = [pl.BlockSpec((tm, tn), lambda i, j: (i, j))]
    out_shape = [jax.ShapeDtypeStruct((m, n), F32)]
    if emit:
        out_specs.append(pl.BlockSpec((k, tn), lambda i, j: (0, j)))
        out_shape.append(jax.ShapeDtypeStruct((k, n), BF16))
    res = pl.pallas_call(
        functools.partial(_inproj_kernel, tn=tn, part=min(tn, 256)),
        grid=(m // tm, n // tn),
        in_specs=[pl.BlockSpec((tm, k), lambda i, j: (i, 0)),
                  _wspec(w, layer, k, tn, 0, lambda i, j: j),
                  pl.BlockSpec((tm, HEAD_DIM), lambda i, j: (i % ct, 0)),
                  pl.BlockSpec((tm, HEAD_DIM), lambda i, j: (i % ct, 0))],
        out_specs=out_specs,
        out_shape=out_shape,
        compiler_params=_cp("parallel", "arbitrary"),
        name="inproj_rope",
    )(hn, w, cos, sin)
    return res if emit else res[0]


def _band_attn_kernel(q_ref, kp_ref, kc_ref, vp_ref, vc_ref, o_ref, lse_ref):
    has_prev = pl.program_id(2) > 0
    qi = lax.broadcasted_iota(jnp.int32, (BAND, 2 * BAND), 0)
    ki = lax.broadcasted_iota(jnp.int32, (BAND, 2 * BAND), 1)
    valid = jnp.logical_and(jnp.logical_and(ki >= qi, ki <= qi + BAND),
                            jnp.logical_or(ki >= BAND, has_prev))
    lane = lax.broadcasted_iota(jnp.int32, (BAND, HEAD_DIM), 1)
    lse_tile = jnp.zeros((BAND, HEAD_DIM), F32)
    scale = HEAD_DIM ** -0.5
    for h in range(A_HEADS):
        sl = slice(h * HEAD_DIM, (h + 1) * HEAD_DIM)
        q = q_ref[:, sl].astype(BF16)
        k = jnp.concatenate([kp_ref[:, sl], kc_ref[:, sl]], axis=0).astype(BF16)
        v = jnp.concatenate([vp_ref[:, sl], vc_ref[:, sl]], axis=0).astype(BF16)
        s = _dot_nt(q, k) * scale
        s = jnp.where(valid, s, NEG_BIG)
        m = jnp.max(s, axis=-1, keepdims=True)
        p = jnp.exp(s - m)
        l = jnp.sum(p, axis=-1, keepdims=True)
        o_ref[:, sl] = _dot(p.astype(BF16), v) / l
        lse_tile = jnp.where(lane == h, m + jnp.log(l), lse_tile)
    lse_ref[...] = lse_tile


def _band_attn_dil_kernel(*refs, d, hpb, use_prev):
    if use_prev:
        q_ref, kp_ref, kc_ref, vp_ref, vc_ref, o_ref, lse_ref, qs, ks, vs, os_, ls = refs
    else:
        q_ref, kc_ref, vc_ref, o_ref, lse_ref, qs, ks, vs, os_, ls = refs
    rows = BAND * d
    hb = pl.program_id(2)
    has_prev = pl.program_id(1) > 0
    for h in range(hpb):
        sl = slice(h * HEAD_DIM, (h + 1) * HEAD_DIM)
        qs[h] = q_ref[:, sl]
        if use_prev:
            ks[h, 0:rows] = kp_ref[:, sl]
            ks[h, rows:2 * rows] = kc_ref[:, sl]
            vs[h, 0:rows] = vp_ref[:, sl]
            vs[h, rows:2 * rows] = vc_ref[:, sl]
        else:
            ks[h] = kc_ref[:, sl]
            vs[h] = vc_ref[:, sl]

    @pl.when(hb == 0)
    def _():
        lse_ref[...] = jnp.zeros(lse_ref.shape, F32)

    nk = 2 * BAND if use_prev else BAND
    qi = lax.broadcasted_iota(jnp.int32, (1, BAND, nk), 1)
    ki = lax.broadcasted_iota(jnp.int32, (1, BAND, nk), 2)
    if use_prev:
        valid = jnp.logical_and(jnp.logical_and(ki >= qi, ki <= qi + BAND),
                                jnp.logical_or(ki >= BAND, has_prev))
    else:
        valid = ki <= qi
    scale = HEAD_DIM ** -0.5

    def residues(ref, h, base):
        return [ref[h, pl.ds(base + r, BAND, stride=d), :] for r in range(d)]

    for h in range(hpb):
        q = jnp.stack(residues(qs, h, 0)).astype(BF16)
        if use_prev:
            k = jnp.stack([jnp.concatenate([a, b], axis=0)
                           for a, b in zip(residues(ks, h, 0), residues(ks, h, rows))]).astype(BF16)
            v = jnp.stack([jnp.concatenate([a, b], axis=0)
                           for a, b in zip(residues(vs, h, 0), residues(vs, h, rows))]).astype(BF16)
        else:
            k = jnp.stack(residues(ks, h, 0)).astype(BF16)
            v = jnp.stack(residues(vs, h, 0)).astype(BF16)
        s = jnp.einsum('rqd,rkd->rqk', q, k, preferred_element_type=F32) * scale
        s = jnp.where(valid, s, NEG_BIG)
        m = jnp.max(s, axis=-1, keepdims=True)
        p = jnp.exp(s - m)
        l = jnp.sum(p, axis=-1, keepdims=True)
        o = jnp.einsum('rqk,rkd->rqd', p.astype(BF16), v, preferred_element_type=F32) / l
        lse_h = m + jnp.log(l)
        for r in range(d):
            os_[h, pl.ds(r, BAND, stride=d), :] = o[r]
            ls[h, pl.ds(r, BAND, stride=d), :] = jnp.broadcast_to(lse_h[r], (BAND, HEAD_DIM))
    lse = lse_ref[...]
    lane = lax.broadcasted_iota(jnp.int32, lse.shape, 1)
    for h in range(hpb):
        o_ref[:, h * HEAD_DIM:(h + 1) * HEAD_DIM] = os_[h]
        lse = jnp.where(lane == hb * hpb + h, ls[h], lse)
    lse_ref[...] = lse


def _band_attn(z, n_seq, t, g):
    d = A_DILATIONS[g]
    width = z.shape[1]
    zv = z.reshape(n_seq, t, width)
    if d == 1:
        qc = 3 * g

        def cur(off):
            return pl.BlockSpec((None, BAND, D_A), lambda n, r, b: (n, b, qc + off))

        def prev(off):
            return pl.BlockSpec((None, BAND, D_A), lambda n, r, b: (n, jnp.maximum(b - 1, 0), qc + off))

        o, lse = pl.pallas_call(
            _band_attn_kernel,
            grid=(n_seq, 1, t // BAND),
            in_specs=[cur(0), prev(1), cur(1), prev(2), cur(2)],
            out_specs=[pl.BlockSpec((None, BAND, D_A), lambda n, r, b: (n, b, 0)),
                       pl.BlockSpec((None, BAND, HEAD_DIM), lambda n, r, b: (n, b, 0))],
            out_shape=[jax.ShapeDtypeStruct((n_seq, t, D_A), F32),
                       jax.ShapeDtypeStruct((n_seq, t, HEAD_DIM), F32)],
            compiler_params=_cp("parallel", "parallel", "arbitrary"),
            name="band_attn_d1",
        )(zv, zv, zv, zv, zv)
        return o.reshape(n_seq * t, D_A), lse.reshape(n_seq * t, HEAD_DIM)

    rows = BAND * d
    nb = t // rows
    use_prev = nb > 1
    hpb = 8 if d <= 4 else 2
    cols = hpb * HEAD_DIM
    cpq = D_A // cols
    qc = 3 * g * cpq

    def cur(off):
        return pl.BlockSpec((None, rows, cols), lambda n, b, hb: (n, b, qc + off * cpq + hb))

    def prev(off):
        return pl.BlockSpec((None, rows, cols),
                            lambda n, b, hb: (n, jnp.maximum(b - 1, 0), qc + off * cpq + hb))

    in_specs = [cur(0), prev(1), cur(1), prev(2), cur(2)] if use_prev else [cur(0), cur(1), cur(2)]
    kv_rows = 2 * rows if use_prev else rows
    o, lse = pl.pallas_call(
        functools.partial(_band_attn_dil_kernel, d=d, hpb=hpb, use_prev=use_prev),
        grid=(n_seq, nb, cpq),
        in_specs=in_specs,
        out_specs=[pl.BlockSpec((None, rows, cols), lambda n, b, hb: (n, b, hb)),
                   pl.BlockSpec((None, rows, HEAD_DIM), lambda n, b, hb: (n, b, 0))],
        out_shape=[jax.ShapeDtypeStruct((n_seq, t, D_A), F32),
                   jax.ShapeDtypeStruct((n_seq, t, HEAD_DIM), F32)],
        scratch_shapes=[pltpu.VMEM((hpb, rows, HEAD_DIM), F32),
                        pltpu.VMEM((hpb, kv_rows, HEAD_DIM), F32),
                        pltpu.VMEM((hpb, kv_rows, HEAD_DIM), F32),
                        pltpu.VMEM((hpb, rows, HEAD_DIM), F32),
                        pltpu.VMEM((hpb, rows, HEAD_DIM), F32)],
        compiler_params=_cp("parallel", "parallel", "arbitrary"),
        name=f"band_attn_d{d}",
    )(*([zv] * len(in_specs)))
    return o.reshape(n_seq * t, D_A), lse.reshape(n_seq * t, HEAD_DIM)


def _combine_kernel(o0_ref, o1_ref, o2_ref, l0_ref, l1_ref, l2_ref, out_ref):
    a0, a1, a2 = l0_ref[...], l1_ref[...], l2_ref[...]
    m = jnp.maximum(jnp.maximum(a0, a1), a2)
    e0, e1, e2 = jnp.exp(a0 - m), jnp.exp(a1 - m), jnp.exp(a2 - m)
    inv = 1.0 / (e0 + e1 + e2)
    w0, w1, w2 = e0 * inv, e1 * inv, e2 * inv
    for h in range(A_HEADS):
        sl = slice(h * HEAD_DIM, (h + 1) * HEAD_DIM)
        out_ref[:, sl] = (w0[:, h:h + 1] * o0_ref[:, sl] + w1[:, h:h + 1] * o1_ref[:, sl]
                          + w2[:, h:h + 1] * o2_ref[:, sl]).astype(out_ref.dtype)


def _combine(os_, lses):
    m = os_[0].shape[0]
    tm = COMBINE_TM
    ob = pl.BlockSpec((tm, D_A), lambda i: (i, 0))
    lb = pl.BlockSpec((tm, HEAD_DIM), lambda i: (i, 0))
    return pl.pallas_call(
        _combine_kernel,
        grid=(m // tm,),
        in_specs=[ob, ob, ob, lb, lb, lb],
        out_specs=ob,
        out_shape=jax.ShapeDtypeStruct((m, D_A), BF16),
        compiler_params=_cp("parallel"),
        name="attn_combine",
    )(*os_, *lses)


def _step_attn_kernel(z_ref, c0_ref, c1_ref, c2_ref, o_ref):
    scale = HEAD_DIM ** -0.5
    ms, ls, accs = [], [], []
    for g, c_ref in enumerate((c0_ref, c1_ref, c2_ref)):
        base = 3 * g * A_HEADS
        q = z_ref[base:base + A_HEADS, :]
        kn = z_ref[base + A_HEADS:base + 2 * A_HEADS, :]
        vn = z_ref[base + 2 * A_HEADS:base + 3 * A_HEADS, :]
        s = jnp.sum(c_ref[:, 0] * q[None], axis=-1, keepdims=True) * scale
        s_new = jnp.sum(kn * q, axis=-1, keepdims=True) * scale
        m = jnp.maximum(jnp.max(s, axis=0), s_new)
        p = jnp.exp(s - m[None])
        p_new = jnp.exp(s_new - m)
        ls.append(jnp.sum(p, axis=0) + p_new)
        accs.append(jnp.sum(p * c_ref[:, 1], axis=0) + p_new * vn)
        ms.append(m)
    mm = jnp.maximum(jnp.maximum(ms[0], ms[1]), ms[2])
    es = [jnp.exp(m - mm) for m in ms]
    den = es[0] * ls[0] + es[1] * ls[1] + es[2] * ls[2]
    num = es[0] * accs[0] + es[1] * accs[1] + es[2] * accs[2]
    o_ref[...] = (num / den).astype(o_ref.dtype)


def _step_attn(zs, caches):
    n, width = zs.shape
    cviews, cspecs = [], []
    for g, c in enumerate(caches):
        d = A_DILATIONS[g]
        cviews.append(c.reshape(n, c.shape[1] // d, d, 2, A_HEADS, HEAD_DIM))
        cspecs.append(pl.BlockSpec((None, BAND, None, 2, A_HEADS, HEAD_DIM),
                                   lambda i: (i, 0, 0, 0, 0, 0)))
    zrows = width // HEAD_DIM
    out = pl.pallas_call(
        _step_attn_kernel,
        grid=(n,),
        in_specs=[pl.BlockSpec((None, zrows, HEAD_DIM), lambda i: (i, 0, 0))] + cspecs,
        out_specs=pl.BlockSpec((None, A_HEADS, HEAD_DIM), lambda i: (i, 0, 0)),
        out_shape=jax.ShapeDtypeStruct((n, A_HEADS, HEAD_DIM), BF16),
        compiler_params=_cp("parallel"),
        name="step_attn",
    )(zs.reshape(n, zrows, HEAD_DIM), *cviews)
    return out.reshape(n, D_A)


def _cache_shift_kernel(c_ref, nxt_ref, new_ref, o_ref, *, lb):
    last = pl.program_id(1) == pl.num_programs(1) - 1
    o_ref[0:lb - 1] = c_ref[1:lb]

    @pl.when(last)
    def _():
        o_ref[lb - 1] = new_ref[0]

    @pl.when(jnp.logical_not(last))
    def _():
        o_ref[lb - 1] = nxt_ref[0]


def _cache_shift(cache, new_kv):
    n, length = cache.shape[:2]
    lb = min(length, 4 * BAND)
    nb = length // lb
    tail = cache.shape[2:]
    zeros = (0,) * len(tail)
    return pl.pallas_call(
        functools.partial(_cache_shift_kernel, lb=lb),
        grid=(n, nb),
        in_specs=[pl.BlockSpec((None, lb) + tail, lambda i, b: (i, b) + zeros),
                  pl.BlockSpec((None, 1) + tail, lambda i, b: (i, jnp.minimum(b + 1, nb - 1) * lb) + zeros),
                  pl.BlockSpec((None, 1) + tail, lambda i, b: (i, 0) + zeros)],
        out_specs=pl.BlockSpec((None, lb) + tail, lambda i, b: (i, b) + zeros),
        out_shape=jax.ShapeDtypeStruct(cache.shape, cache.dtype),
        compiler_params=_cp("parallel", "parallel"),
        name="cache_shift",
    )(cache, cache, new_kv)


def _kv_rows_kernel(k_ref, v_ref, o_ref, *, tm):
    per = 2 * A_HEADS
    for kv, x_ref in enumerate((k_ref, v_ref)):
        for h in range(A_HEADS):
            o_ref[pl.ds(kv * A_HEADS + h, tm, stride=per), :] = x_ref[:, h * HEAD_DIM:(h + 1) * HEAD_DIM]


def _kv_rows(z, n_seq, t, g, length):
    width = z.shape[1]
    zv = z.reshape(n_seq, t, width)
    tm = min(length, 2 * BAND)
    r0 = (t - length) // tm
    c0 = 3 * g + 1
    per = 2 * A_HEADS
    out = pl.pallas_call(
        functools.partial(_kv_rows_kernel, tm=tm),
        grid=(n_seq, length // tm),
        in_specs=[pl.BlockSpec((None, tm, D_A), lambda n, i: (n, r0 + i, c0)),
                  pl.BlockSpec((None, tm, D_A), lambda n, i: (n, r0 + i, c0 + 1))],
        out_specs=pl.BlockSpec((None, tm * per, HEAD_DIM), lambda n, i: (n, i, 0)),
        out_shape=jax.ShapeDtypeStruct((n_seq, length * per, HEAD_DIM), F32),
        compiler_params=_cp("parallel", "parallel"),
        name="kv_rows",
    )(zv, zv)
    return out.reshape(n_seq, length, 2, A_HEADS, HEAD_DIM)


def _ssm_discretise(lre, lim, ldt):
    dt = jnp.exp(ldt)
    mag = jnp.exp(lre * dt)
    ar = mag * jnp.cos(lim * dt)
    ai = mag * jnp.sin(lim * dt)
    den = lre * lre + lim * lim
    zr = ((ar - 1.0) * lre + ai * lim) / den
    zi = (ai * lre - (ar - 1.0) * lim) / den
    return ar, ai, zr, zi


def _slab_mask():
    r = lax.broadcasted_iota(jnp.int32, (SLAB_G * SSM_GROUP, SLAB_S), 0) // SSM_GROUP
    c = lax.broadcasted_iota(jnp.int32, (SLAB_G * SSM_GROUP, SLAB_S), 1) // SSM_STATE
    return (r == c).astype(F32)


def _block_diag(x16, mask):
    return jnp.concatenate([x16] * SLAB_G, axis=0) * mask


def _ssm_prompt_kernel(u_ref, lre_ref, lim_ref, ldt_ref, bre_ref, bim_ref, cre_ref, cim_ref, d_ref,
                       y_ref, hlr_ref, hli_ref,
                       wer_ref, wei_ref, tz_ref, bdr_ref, bdi_ref, er_ref, ei_ref, *, n_seq, n_chunk):
    def split(a):
        ah = a.astype(BF16)
        return ah, (a - ah.astype(F32)).astype(BF16)

    def dot_nt_x3(a, b_split):
        ah, al = split(a)
        bh, bl = b_split
        return _dot_nt(ah, bh) + _dot_nt(ah, bl) + _dot_nt(al, bh)

    rows = n_seq * n_chunk

    def step_rows(j):
        return pl.ds(j, rows, stride=CHUNK)

    ar, ai, zr, zi = _ssm_discretise(lre_ref[...], lim_ref[...], ldt_ref[...])
    mask = _slab_mask()
    bre, bim = bre_ref[...], bim_ref[...]
    cre, cim = cre_ref[...], cim_ref[...]
    cre_bd = split(_block_diag(cre, mask))
    cim_bd = split(_block_diag(cim, mask))
    lanes = SLAB_G * SSM_GROUP
    tz_ref[...] = jnp.zeros(tz_ref.shape, BF16)
    pr, pim = jnp.ones_like(ar), jnp.zeros_like(ar)
    for m in range(CHUNK + 1):
        if m < CHUNK:
            wr = zr * pr - zi * pim
            wi = zr * pim + zi * pr
            we_r = _block_diag(bre * wr - bim * wi, mask)
            we_i = _block_diag(bre * wi + bim * wr, mask)
            j = CHUNK - 1 - m
            wer_ref[j * lanes:(j + 1) * lanes, :] = we_r.astype(BF16)
            wei_ref[j * lanes:(j + 1) * lanes, :] = we_i.astype(BF16)
            tz = (dot_nt_x3(we_r, cre_bd) - dot_nt_x3(we_i, cim_bd)).astype(BF16)
            for j in range(CHUNK - m):
                tz_ref[j * lanes:(j + 1) * lanes, (j + m) * lanes:(j + m + 1) * lanes] = tz
        if m >= 1:
            bdr_ref[(m - 1) * lanes:m * lanes, :] = _block_diag(cre * pr - cim * pim, mask).astype(BF16)
            bdi_ref[(m - 1) * lanes:m * lanes, :] = _block_diag(-(cre * pim + cim * pr), mask).astype(BF16)
        if m == CHUNK:
            a8r, a8i = pr, pim
        pr, pim = pr * ar - pim * ai, pr * ai + pim * ar

    xcat = jnp.concatenate([u_ref[step_rows(j), :].astype(BF16) for j in range(CHUNK)], axis=1)

    er_ref[...] = _dot(xcat, wer_ref[...])
    ei_ref[...] = _dot(xcat, wei_ref[...])

    def body(k, carry):
        out = []
        for n in range(n_seq):
            hr, him = carry[2 * n], carry[2 * n + 1]
            row = n * n_chunk + k
            e_r = er_ref[pl.ds(row, 1), :]
            e_i = ei_ref[pl.ds(row, 1), :]
            er_ref[pl.ds(row, 1), :] = hr
            ei_ref[pl.ds(row, 1), :] = him
            out.append(a8r * hr - a8i * him + e_r)
            out.append(a8r * him + a8i * hr + e_i)
        return tuple(out)

    zero = jnp.zeros((1, SLAB_S), F32)
    fin = lax.fori_loop(0, n_chunk, body, (zero,) * (2 * n_seq), unroll=2)
    for n in range(n_seq):
        hlr_ref[pl.ds(n, 1), :] = fin[2 * n]
        hli_ref[pl.ds(n, 1), :] = fin[2 * n + 1]

    y = (_dot(xcat, tz_ref[...]) + _dot_nt(er_ref[...].astype(BF16), bdr_ref[...])
         + _dot_nt(ei_ref[...].astype(BF16), bdi_ref[...]))
    for jp in range(CHUNK):
        y_ref[step_rows(jp), :] = y[:, jp * lanes:(jp + 1) * lanes] + d_ref[...] * u_ref[step_rows(jp), :]


def _ssm_params(a_re, a_im, b_re, b_im, c_re, c_im, d_skip, log_dt):
    g = a_re.shape[0]
    ns = g // SLAB_G

    def row(x):
        return x.astype(F32).reshape(ns, 1, SLAB_S)

    def chan_rows(x):
        return x.astype(F32).reshape(ns, SLAB_G, SSM_GROUP, SSM_STATE).transpose(0, 2, 1, 3) \
            .reshape(ns, SSM_GROUP, SLAB_S)

    ldt = jnp.broadcast_to(log_dt.astype(F32)[:, None], (g, SSM_STATE))
    return (row(a_re), row(a_im), row(ldt),
            chan_rows(b_re.transpose(0, 2, 1)), chan_rows(b_im.transpose(0, 2, 1)),
            chan_rows(c_re), chan_rows(c_im),
            d_skip.astype(F32).reshape(ns, 1, SLAB_G * SSM_GROUP))


def _ssm_prompt(z, n_seq, t, u_col0, sp):
    m, width = z.shape
    rows = m // CHUNK
    n_chunk = t // CHUNK
    lanes = SLAB_G * SSM_GROUP
    ns = sp[0].shape[0]
    ub = u_col0 // lanes

    def pspec(r, c):
        return pl.BlockSpec((None, r, c), lambda s: (s, 0, 0))

    y, hlr, hli = pl.pallas_call(
        functools.partial(_ssm_prompt_kernel, n_seq=n_seq, n_chunk=n_chunk),
        grid=(ns,),
        in_specs=[pl.BlockSpec((m, lanes), lambda s: (0, ub + s)),
                  pspec(1, SLAB_S), pspec(1, SLAB_S), pspec(1, SLAB_S),
                  pspec(SSM_GROUP, SLAB_S), pspec(SSM_GROUP, SLAB_S),
                  pspec(SSM_GROUP, SLAB_S), pspec(SSM_GROUP, SLAB_S), pspec(1, lanes)],
        out_specs=[pl.BlockSpec((m, lanes), lambda s: (0, s)),
                   pl.BlockSpec((None, n_seq, SLAB_S), lambda s: (s, 0, 0)),
                   pl.BlockSpec((None, n_seq, SLAB_S), lambda s: (s, 0, 0))],
        out_shape=[jax.ShapeDtypeStruct((m, ns * lanes), F32),
                   jax.ShapeDtypeStruct((ns, n_seq, SLAB_S), F32),
                   jax.ShapeDtypeStruct((ns, n_seq, SLAB_S), F32)],
        scratch_shapes=[pltpu.VMEM((CHUNK * lanes, SLAB_S), BF16),
                        pltpu.VMEM((CHUNK * lanes, SLAB_S), BF16),
                        pltpu.VMEM((CHUNK * lanes, CHUNK * lanes), BF16),
                        pltpu.VMEM((CHUNK * lanes, SLAB_S), BF16),
                        pltpu.VMEM((CHUNK * lanes, SLAB_S), BF16),
                        pltpu.VMEM((rows, SLAB_S), F32),
                        pltpu.VMEM((rows, SLAB_S), F32)],
        compiler_params=_cp("parallel"),
        name="ssm_prompt",
    )(z, *sp)
    return y, hlr, hli


def _ssm_step_kernel(u_ref, h0r_ref, h0i_ref, lre_ref, lim_ref, ldt_ref, bre_ref, bim_ref,
                     cre_ref, cim_ref, d_ref, y_ref, hr_ref, hi_ref):
    ar, ai, zr, zi = _ssm_discretise(lre_ref[...], lim_ref[...], ldt_ref[...])
    mask = _slab_mask()
    bre, bim = bre_ref[...], bim_ref[...]
    we_r = _block_diag(bre * zr - bim * zi, mask).astype(BF16)
    we_i = _block_diag(bre * zi + bim * zr, mask).astype(BF16)
    u = u_ref[...]
    ub = u.astype(BF16)
    h0r, h0i = h0r_ref[...], h0i_ref[...]
    hr = ar * h0r - ai * h0i + _dot(ub, we_r)
    him = ar * h0i + ai * h0r + _dot(ub, we_i)
    hr_ref[...] = hr
    hi_ref[...] = him
    cr_bd = _block_diag(cre_ref[...], mask).astype(BF16)
    ci_bd = _block_diag(cim_ref[...], mask).astype(BF16)
    y_ref[...] = (_dot_nt(hr.astype(BF16), cr_bd) - _dot_nt(him.astype(BF16), ci_bd)
                  + d_ref[...] * u)


def _ssm_step(zs, u_col0, h0r, h0i, sp):
    n = zs.shape[0]
    lanes = SLAB_G * SSM_GROUP
    ns = sp[0].shape[0]
    ub = u_col0 // lanes

    def pspec(r, c):
        return pl.BlockSpec((None, r, c), lambda s: (s, 0, 0))

    hspec = pl.BlockSpec((None, n, SLAB_S), lambda s: (s, 0, 0))
    return pl.pallas_call(
        _ssm_step_kernel,
        grid=(ns,),
        in_specs=[pl.BlockSpec((n, lanes), lambda s: (0, ub + s)), hspec, hspec,
                  pspec(1, SLAB_S), pspec(1, SLAB_S), pspec(1, SLAB_S),
                  pspec(SSM_GROUP, SLAB_S), pspec(SSM_GROUP, SLAB_S),
                  pspec(SSM_GROUP, SLAB_S), pspec(SSM_GROUP, SLAB_S), pspec(1, lanes)],
        out_specs=[pl.BlockSpec((n, lanes), lambda s: (0, s)), hspec, hspec],
        out_shape=[jax.ShapeDtypeStruct((n, ns * lanes), F32),
                   jax.ShapeDtypeStruct((ns, n, SLAB_S), F32),
                   jax.ShapeDtypeStruct((ns, n, SLAB_S), F32)],
        compiler_params=_cp("parallel"),
        name="ssm_step",
    )(zs, h0r, h0i, *sp)


def _gelu_tanh(x):
    return 0.5 * x * (1.0 + jnp.tanh(math.sqrt(2.0 / math.pi) * (x + 0.044715 * (x * x * x))))


def _ssm_gate_kernel(y_ref, w_ref, b_ref, o_ref):
    yg = _gelu_tanh(y_ref[...])
    gate = _dot(yg.astype(BF16), w_ref[...]) + b_ref[...]
    o_ref[...] = (yg * _sigmoid(gate)).astype(o_ref.dtype)


def _ssm_gate(y, w, b, tm):
    m, d = y.shape
    return pl.pallas_call(
        _ssm_gate_kernel,
        grid=(m // tm,),
        in_specs=[pl.BlockSpec((tm, d), lambda i: (i, 0)),
                  pl.BlockSpec((d, d), lambda i: (0, 0)),
                  pl.BlockSpec((1, d), lambda i: (0, 0))],
        out_specs=pl.BlockSpec((tm, d), lambda i: (i, 0)),
        out_shape=jax.ShapeDtypeStruct((m, d), BF16),
        compiler_params=_cp("parallel"),
        name="ssm_gate",
    )(y, w, b.reshape(1, d).astype(F32))


def _ssm_gate_step_kernel(y_ref, yc_ref, w_ref, b_ref, o_ref, wb_ref):
    yg = _gelu_tanh(y_ref[...])
    gate = _dot(yg.astype(BF16), _use_w(w_ref, wb_ref)) + b_ref[...]
    o_ref[...] = (_gelu_tanh(yc_ref[...]) * _sigmoid(gate)).astype(o_ref.dtype)


def _ssm_gate_step(y, w, layer, b, tn):
    m, d = y.shape
    return pl.pallas_call(
        _ssm_gate_step_kernel,
        grid=(d // tn,),
        in_specs=[pl.BlockSpec((m, d), lambda j: (0, 0)),
                  pl.BlockSpec((m, tn), lambda j: (0, j)),
                  _wspec(w, layer, d, tn, 0, lambda j: j),
                  pl.BlockSpec((1, tn), lambda j: (0, j))],
        out_specs=[pl.BlockSpec((m, tn), lambda j: (0, j)),
                   pl.BlockSpec((d, tn), lambda j: (0, j))],
        out_shape=[jax.ShapeDtypeStruct((m, d), BF16), jax.ShapeDtypeStruct((d, d), BF16)],
        compiler_params=_cp("arbitrary"),
        name="ssm_gate_step",
    )(y, y, w, b.reshape(1, d).astype(F32))


def _outproj_kernel(a_ref, s_ref, wa_ref, ws_ref, x_ref, o_ref, *wb):
    wab_ref, wsb_ref = wb if wb else (None, None)
    o_ref[...] = (x_ref[...] + _dot(a_ref[...], _use_w(wa_ref, wab_ref))
                  + _dot(s_ref[...], _use_w(ws_ref, wsb_ref)))


def _outproj(attn, ssm, wa, ws, layer, x, tm, tn):
    m, ka = attn.shape
    n = wa.shape[-1]
    emit = layer is not None
    out_specs = [pl.BlockSpec((tm, tn), lambda i, j: (i, j))]
    out_shape = [jax.ShapeDtypeStruct((m, n), F32)]
    if emit:
        out_specs += [pl.BlockSpec((ka, tn), lambda i, j: (0, j))] * 2
        out_shape += [jax.ShapeDtypeStruct((ka, n), BF16)] * 2
    res = pl.pallas_call(
        _outproj_kernel,
        grid=(m // tm, n // tn),
        in_specs=[pl.BlockSpec((tm, ka), lambda i, j: (i, 0)),
                  pl.BlockSpec((tm, ka), lambda i, j: (i, 0)),
                  _wspec(wa, layer, ka, tn, 0, lambda i, j: j),
                  _wspec(ws, layer, ka, tn, 1 if emit else 0, lambda i, j: j),
                  pl.BlockSpec((tm, tn), lambda i, j: (i, j))],
        out_specs=out_specs,
        out_shape=out_shape,
        compiler_params=_cp("parallel", "arbitrary"),
        name="outproj",
    )(attn, ssm, wa, ws, x)
    return res if emit else res[0]


def _mm_res_kernel(a_ref, w_ref, b_ref, x_ref, o_ref, *wb):
    wb_ref = wb[0] if wb else None
    o_ref[...] = x_ref[...] + (_dot(a_ref[...], _use_w(w_ref, wb_ref)) + b_ref[...])


def _mm_res(a, w, layer, b, x, tm, tn, name):
    m, k = a.shape
    n = w.shape[-1]
    emit = layer is not None
    out_specs = [pl.BlockSpec((tm, tn), lambda i, j: (i, j))]
    out_shape = [jax.ShapeDtypeStruct((m, n), F32)]
    if emit:
        out_specs.append(pl.BlockSpec((k, tn), lambda i, j: (0, j)))
        out_shape.append(jax.ShapeDtypeStruct((k, n), BF16))
    res = pl.pallas_call(
        _mm_res_kernel,
        grid=(m // tm, n // tn),
        in_specs=[pl.BlockSpec((tm, k), lambda i, j: (i, 0)),
                  _wspec(w, layer, k, tn, 0, lambda i, j: j),
                  pl.BlockSpec((1, tn), lambda i, j: (0, j)),
                  pl.BlockSpec((tm, tn), lambda i, j: (i, j))],
        out_specs=out_specs,
        out_shape=out_shape,
        compiler_params=_cp("parallel", "arbitrary"),
        name=name,
    )(a, w, b.reshape(1, n).astype(F32), x)
    return res if emit else res[0]


def _pw1_kernel(x_ref, wa_ref, wg_ref, ba_ref, bg_ref, o_ref, *wb):
    wab_ref, wgb_ref = wb if wb else (None, None)
    x = x_ref[...]
    a = _dot(x, _use_w(wa_ref, wab_ref)) + ba_ref[...]
    gate = _dot(x, _use_w(wg_ref, wgb_ref)) + bg_ref[...]
    o_ref[...] = a * _sigmoid(gate)


def _pw1(hn, wa, wg, layer, b, tm, tn):
    m, k = hn.shape
    n = b.shape[0] // 2
    nj = n // tn
    emit = layer is not None
    b2 = b.reshape(1, 2 * n).astype(F32)
    out_specs = [pl.BlockSpec((tm, tn), lambda i, j: (i, j))]
    out_shape = [jax.ShapeDtypeStruct((m, n), F32)]
    if emit:
        out_specs += [pl.BlockSpec((k, tn), lambda i, j: (0, j))] * 2
        out_shape += [jax.ShapeDtypeStruct((k, n), BF16)] * 2
    goff = nj if emit else 0
    res = pl.pallas_call(
        _pw1_kernel,
        grid=(m // tm, nj),
        in_specs=[pl.BlockSpec((tm, k), lambda i, j: (i, 0)),
                  _wspec(wa, layer, k, tn, 0, lambda i, j: j),
                  _wspec(wg, layer, k, tn, 0, lambda i, j: j + goff),
                  pl.BlockSpec((1, tn), lambda i, j: (0, j)),
                  pl.BlockSpec((1, tn), lambda i, j: (0, j + nj))],
        out_specs=out_specs,
        out_shape=out_shape,
        compiler_params=_cp("parallel", "arbitrary"),
        name="conv_pw1_glu",
    )(hn, wa, wg, b2, b2)
    return res if emit else res[0]


def _ln_swish(y, g, b):
    mu = jnp.mean(y, axis=-1, keepdims=True)
    yc = y - mu
    var = jnp.mean(yc * yc, axis=-1, keepdims=True)
    yn = yc * lax.rsqrt(var + LN_EPS) * g + b
    return yn * _sigmoid(yn)


def _conv31_kernel(u_ref, halo_ref, w_ref, b_ref, g_ref, bb_ref, o_ref, ext_ref, y3_ref, y_ref,
                   *, tm, tiles_per_seq, slab):
    seq_start = (pl.program_id(0) % tiles_per_seq) == 0
    nc = u_ref.shape[1] // HEAD_DIM
    for c in range(nc):
        cs = slice(c * HEAD_DIM, (c + 1) * HEAD_DIM)
        ext_ref[pl.ds(c, CONV_HALO, stride=slab), :] = jnp.where(seq_start, 0.0, halo_ref[:, cs])
        ext_ref[pl.ds(CONV_HALO * slab + c, tm, stride=slab), :] = u_ref[:, cs]
    off = CONV_HALO - (CONV_WIDTH - 1)
    rw = 8

    def body(i, carry):
        for ct in range(nc // 8):
            base = pl.multiple_of((i * rw + off) * slab, 8) + ct * 8
            win = [ext_ref[pl.ds(base + s * slab, 8), :] for s in range(rw + CONV_WIDTH - 1)]
            acc = [None] * rw
            for j in range(CONV_WIDTH):
                wj = w_ref[j, ct * 8:(ct + 1) * 8, :]
                for tt in range(rw):
                    term = wj * win[tt + j]
                    acc[tt] = term if j == 0 else acc[tt] + term
            bias = b_ref[ct * 8:(ct + 1) * 8, :]
            obase = pl.multiple_of(i * rw * slab, 8) + ct * 8
            for tt in range(rw):
                y3_ref[pl.ds(obase + tt * slab, 8), :] = acc[tt] + bias
        return carry

    lax.fori_loop(0, tm // rw, body, 0)
    for c in range(nc):
        y_ref[:, c * HEAD_DIM:(c + 1) * HEAD_DIM] = y3_ref[pl.ds(c, tm, stride=slab), :]
    o_ref[...] = _ln_swish(y_ref[...], g_ref[...], bb_ref[...]).astype(o_ref.dtype)


def _conv31(u, t, w, b, g, bb, tm):
    m, d = u.shape
    nc = d // HEAD_DIM
    slab = nc + 8
    row = lambda v: v.reshape(1, d).astype(F32)
    rspec = pl.BlockSpec((1, d), lambda i: (0, 0))
    hb = tm // CONV_HALO
    return pl.pallas_call(
        functools.partial(_conv31_kernel, tm=tm, tiles_per_seq=t // tm, slab=slab),
        grid=(m // tm,),
        in_specs=[pl.BlockSpec((tm, d), lambda i: (i, 0)),
                  pl.BlockSpec((CONV_HALO, d), lambda i: (jnp.maximum(i * hb - 1, 0), 0)),
                  pl.BlockSpec((CONV_WIDTH, nc, HEAD_DIM), lambda i: (0, 0, 0)),
                  pl.BlockSpec((nc, HEAD_DIM), lambda i: (0, 0)), rspec, rspec],
        out_specs=pl.BlockSpec((tm, d), lambda i: (i, 0)),
        out_shape=jax.ShapeDtypeStruct((m, d), BF16),
        scratch_shapes=[pltpu.VMEM(((tm + CONV_HALO) * slab, HEAD_DIM), F32),
                        pltpu.VMEM((tm * slab, HEAD_DIM), F32),
                        pltpu.VMEM((tm, d), F32)],
        compiler_params=_cp("parallel"),
        name="conv31_ln_swish",
    )(u, u, w.astype(F32).reshape(CONV_WIDTH, nc, HEAD_DIM), b.reshape(nc, HEAD_DIM).astype(F32),
      row(g), row(bb))


def _conv31_step_kernel(u_ref, st_ref, w_ref, b_ref, g_ref, bb_ref, o_ref):
    acc = w_ref[0:1, :] * st_ref[0]
    for j in range(1, CONV_WIDTH - 1):
        acc = acc + w_ref[j:j + 1, :] * st_ref[j]
    acc = acc + w_ref[CONV_WIDTH - 1:CONV_WIDTH, :] * u_ref[...] + b_ref[...]
    o_ref[...] = _ln_swish(acc, g_ref[...], bb_ref[...]).astype(o_ref.dtype)


def _conv31_step(u, state_t, w, b, g, bb):
    n, d = u.shape
    row = lambda v: v.reshape(1, d).astype(F32)
    return pl.pallas_call(
        _conv31_step_kernel,
        out_shape=jax.ShapeDtypeStruct((n, d), BF16),
        compiler_params=pltpu.CompilerParams(vmem_limit_bytes=VMEM_LIMIT_V7X),
        name="conv31_step",
    )(u, state_t, w.astype(F32), row(b), row(g), row(bb))


def _ffn_up_kernel(x_ref, w_ref, cwg_ref, cwv_ref, cbg_ref, cbv_ref,
                   act_ref, zlg_ref, zlv_ref, z_scr, *, t, tn, rows):
    nc = tn // HEAD_DIM
    cw = (cwg_ref, cwv_ref)
    cb = (cbg_ref, cbv_ref)
    hist = [jnp.zeros((8, HEAD_DIM), F32)] * (2 * nc)
    for p, r0 in enumerate(range(0, t, rows)):
        z = _dot(x_ref[r0:r0 + rows, :], w_ref[...])
        zc = []
        for c in range(2 * nc):
            zs = z_scr.at[p, c]
            zn = z[:, c * HEAD_DIM:(c + 1) * HEAD_DIM]
            zs[0:8, :] = hist[c]
            zs[8:, :] = zn
            hist[c] = zn[rows - 8:, :]
            half, cc = divmod(c, nc)
            ls = slice(cc * HEAD_DIM, (cc + 1) * HEAD_DIM)
            zc.append(cw[half][2:3, ls] * zn + cw[half][1:2, ls] * zs[pl.ds(7, rows, stride=1), :]
                      + cw[half][0:1, ls] * zs[pl.ds(6, rows, stride=1), :] + cb[half][:, ls])
        for c in range(nc):
            gate = zc[c]
            act_ref[r0:r0 + rows, c * HEAD_DIM:(c + 1) * HEAD_DIM] = (
                gate * _sigmoid(gate) * zc[nc + c]).astype(act_ref.dtype)
    for c in range(nc):
        zlg_ref[:, c * HEAD_DIM:(c + 1) * HEAD_DIM] = hist[c]
        zlv_ref[:, c * HEAD_DIM:(c + 1) * HEAD_DIM] = hist[nc + c]


def _ffn_up(hn, t, w, cw, cb, tn):
    m, k = hn.shape
    dff = w.shape[1] // 2
    nj = dff // tn
    cw = cw.astype(F32)
    cb2 = cb.reshape(1, 2 * dff).astype(F32)
    zl = jax.ShapeDtypeStruct((m // t, 8, dff), F32)
    rows = min(t, FFN_PART_ROWS)
    return pl.pallas_call(
        functools.partial(_ffn_up_kernel, t=t, tn=tn, rows=rows),
        grid=(m // t, nj),
        in_specs=[pl.BlockSpec((t, k), lambda i, j: (i, 0)),
                  pl.BlockSpec((k, 2 * tn), lambda i, j: (0, j)),
                  pl.BlockSpec((3, tn), lambda i, j: (0, j)),
                  pl.BlockSpec((3, tn), lambda i, j: (0, j + nj)),
                  pl.BlockSpec((1, tn), lambda i, j: (0, j)),
                  pl.BlockSpec((1, tn), lambda i, j: (0, j + nj))],
        out_specs=[pl.BlockSpec((t, tn), lambda i, j: (i, j)),
                   pl.BlockSpec((None, 8, tn), lambda i, j: (i, 0, j)),
                   pl.BlockSpec((None, 8, tn), lambda i, j: (i, 0, j))],
        out_shape=[jax.ShapeDtypeStruct((m, dff), BF16), zl, zl],
        scratch_shapes=[pltpu.VMEM((t // rows, 2 * tn // HEAD_DIM, 8 + rows, HEAD_DIM), F32)],
        compiler_params=_cp("parallel", "arbitrary"),
        name="ffn_up_conv_glu",
    )(hn, w, cw, cw, cb2, cb2)


def _ffn_up_step_kernel(x_ref, wg_ref, wv_ref, sg_ref, sv_ref, cwg_ref, cwv_ref, cbg_ref, cbv_ref,
                        act_ref, zg_ref, zv_ref, wb_ref):
    x = x_ref[...]
    tn = wg_ref.shape[1]

    def half(w_ref, c0, s_ref, cw_ref, cb_ref, z_ref):
        wb = w_ref[...].astype(BF16)
        wb_ref[:, c0:c0 + tn] = wb
        z = _dot(x, wb)
        z_ref[...] = z
        return cw_ref[2:3, :] * z + cw_ref[1:2, :] * s_ref[1] + cw_ref[0:1, :] * s_ref[0] + cb_ref[...]

    gate = half(wg_ref, 0, sg_ref, cwg_ref, cbg_ref, zg_ref)
    val = half(wv_ref, tn, sv_ref, cwv_ref, cbv_ref, zv_ref)
    act_ref[...] = (gate * _sigmoid(gate) * val).astype(act_ref.dtype)


def _ffn_up_step(hn, state_t, w, layer, cw, cb, tn):
    n, k = hn.shape
    dff = w.shape[-1] // 2
    nj = dff // tn
    cw = cw.astype(F32)
    cb2 = cb.reshape(1, 2 * dff).astype(F32)
    zs = jax.ShapeDtypeStruct((n, dff), F32)
    return pl.pallas_call(
        _ffn_up_step_kernel,
        grid=(nj,),
        in_specs=[pl.BlockSpec((n, k), lambda j: (0, 0)),
                  _wspec(w, layer, k, tn, 0, lambda j: j),
                  _wspec(w, layer, k, tn, 0, lambda j: j + nj),
                  pl.BlockSpec((2, n, tn), lambda j: (0, 0, j)),
                  pl.BlockSpec((2, n, tn), lambda j: (0, 0, j + nj)),
                  pl.BlockSpec((3, tn), lambda j: (0, j)),
                  pl.BlockSpec((3, tn), lambda j: (0, j + nj)),
                  pl.BlockSpec((1, tn), lambda j: (0, j)),
                  pl.BlockSpec((1, tn), lambda j: (0, j + nj))],
        out_specs=[pl.BlockSpec((n, tn), lambda j: (0, j)),
                   pl.BlockSpec((n, tn), lambda j: (0, j)),
                   pl.BlockSpec((n, tn), lambda j: (0, j)),
                   pl.BlockSpec((k, 2 * tn), lambda j: (0, j))],
        out_shape=[jax.ShapeDtypeStruct((n, dff), BF16), zs, zs,
                   jax.ShapeDtypeStruct((k, 2 * dff), BF16)],
        compiler_params=_cp("arbitrary"),
        name="ffn_up_step",
    )(hn, w, w, state_t, state_t, cw, cw, cb2, cb2)


def _ffn_prompt(x, t, norm_g, w_up, cw, cb, w_down):
    hn = _rmsnorm(x, norm_g, BF16)
    act, zlg, zlv = _ffn_up(hn, t, w_up, cw, cb, FFN_TN)
    last = jnp.concatenate([zlg[:, 6:8], zlv[:, 6:8]], axis=-1)
    xo = _mm_res(act, w_down, None, jnp.zeros((w_down.shape[1],), F32), x, FFN_DOWN_TM, FFN_DOWN_TN, "ffn_down")
    return xo, last


def _ffn_step(x, prev_state_t, norm_g, w_up, w_down, layer, cw, cb):
    m = x.shape[0]
    hn = _rmsnorm(x, norm_g, BF16)
    act, zg, zv, wu_b = _ffn_up_step(hn, prev_state_t, w_up, layer, cw, cb, FFN_TN)
    znew = jnp.concatenate([zg, zv], axis=-1)
    last = jnp.concatenate([prev_state_t[1][:, None], znew[:, None]], axis=1)
    xo, wd_b = _mm_res(act, w_down, layer, jnp.zeros((w_down.shape[-1],), F32), x, m, STEP_FFN_DOWN_TN,
                       "ffn_down_step")
    return xo, last, (wu_b, wd_b)


def kernel(x_prompt, x_sample, cache_a0, cache_a1, cache_a2, state_ssm, state_conv, state_ffn, norm_mix, norm_ffn, norm_final, w_in_even, w_out_even, ssm_a_re, ssm_a_im, ssm_b_re, ssm_b_im, ssm_c_re, ssm_c_im, ssm_d, ssm_log_dt, w_glu, b_glu, w_pw1, b_pw1, w_dw, b_dw, ln_g, ln_b, w_pw2, b_pw2, w_up, ffn_dw, ffn_dw_b, w_down):
    n_p, t_p, d_model = x_prompt.shape
    n_s = x_sample.shape[0]
    caches = (cache_a0, cache_a1, cache_a2)
    u_col0 = N_DIL * QKV_W
    sp = _ssm_params(ssm_a_re[0], ssm_a_im[0], ssm_b_re[0], ssm_b_im[0], ssm_c_re[0], ssm_c_im[0],
                     ssm_d[0], ssm_log_dt[0])
    ns = sp[0].shape[0]

    def ssm_state(hr, him, n):
        def unslab(h):
            return h.reshape(ns, n, SLAB_G, SSM_STATE).transpose(1, 0, 2, 3).reshape(n, ns * SLAB_G, SSM_STATE)
        return jnp.stack([unslab(hr), unslab(him)], axis=-1)[None]

    xs = x_sample.reshape(n_s, d_model)
    cos_s, sin_s = _rope_tables(n_s, PAST_LEN, 0)
    hn = _rmsnorm(xs, norm_mix[0], BF16)
    zs, w_in_b = _inproj(hn, w_in_even, 0, cos_s, sin_s, n_s, STEP_INPROJ_TN)
    attn_s = _step_attn(zs, [c[0] for c in caches])

    def slab(h):
        return h.reshape(n_s, ns, SLAB_S).transpose(1, 0, 2)

    y_s, hr_s, hi_s = _ssm_step(zs, u_col0, slab(state_ssm[0, ..., 0]), slab(state_ssm[0, ..., 1]), sp)
    gate_s, w_glu_b = _ssm_gate_step(y_s, w_glu, 0, b_glu[0], STEP_GATE_TN)
    xs, w_oa_b, w_os_b = _outproj(attn_s, gate_s, w_out_even, w_out_even, 0, xs, n_s, STEP_OUTPROJ_TN)
    xs, ffn0_s, wf0 = _ffn_step(xs, state_ffn[0].transpose(1, 0, 2), norm_ffn[0], w_up, w_down, 0,
                                ffn_dw[0], ffn_dw_b[0])
    hn = _rmsnorm(xs, norm_mix[1], BF16)
    us, w_pa_b, w_pg_b = _pw1(hn, w_pw1, w_pw1, 0, b_pw1[0], n_s, STEP_PW1_TN)
    cs = _conv31_step(us, state_conv[0].transpose(1, 0, 2), w_dw[0], b_dw[0], ln_g[0], ln_b[0])
    conv_s = jnp.concatenate([state_conv[0][:, 1:], us[:, None]], axis=1)
    xs, w_pw2_b = _mm_res(cs, w_pw2, 0, b_pw2[0], xs, n_s, STEP_PW2_TN, "conv_pw2_step")
    xs, ffn1_s, wf1 = _ffn_step(xs, state_ffn[1].transpose(1, 0, 2), norm_ffn[1], w_up, w_down, 1,
                                ffn_dw[1], ffn_dw_b[1])
    y_sample = _rmsnorm(xs, norm_final, F32).reshape(n_s, 1, d_model)

    a_s = []
    for g in range(N_DIL):
        kv = zs[:, g * QKV_W + D_A:(g + 1) * QKV_W].reshape(n_s, 1, 2, A_HEADS, HEAD_DIM)
        a_s.append(_cache_shift(caches[g][0], kv)[None])
    ssm_s = ssm_state(hr_s, hi_s, n_s)
    conv_s = conv_s[None]
    ffn_s = jnp.stack([ffn0_s, ffn1_s])

    mp = n_p * t_p
    tm = PROMPT_TM
    x = x_prompt.reshape(mp, d_model)
    cos, sin = _rope_tables(t_p, 0, 1)
    hn = _rmsnorm(x, norm_mix[0], BF16)
    z = _inproj(hn, w_in_b, None, cos, sin, tm, INPROJ_TN)
    outs, lses = [], []
    for g in range(N_DIL):
        o, lse = _band_attn(z, n_p, t_p, g)
        outs.append(o)
        lses.append(lse)
    attn = _combine(outs, lses)
    y_ssm, hlr, hli = _ssm_prompt(z, n_p, t_p, u_col0, sp)
    gate_p = _ssm_gate(y_ssm, w_glu_b, b_glu[0], GATE_TM)
    x = _outproj(attn, gate_p, w_oa_b, w_os_b, None, x, tm, OUTPROJ_TN)
    x, ffn0_p = _ffn_prompt(x, t_p, norm_ffn[0], wf0[0], ffn_dw[0], ffn_dw_b[0], wf0[1])
    hn = _rmsnorm(x, norm_mix[1], BF16)
    u = _pw1(hn, w_pa_b, w_pg_b, None, b_pw1[0], tm, PW1_TN)
    c = _conv31(u, t_p, w_dw[0], b_dw[0], ln_g[0], ln_b[0], CONV_TM)
    conv_p = u.reshape(n_p, t_p, -1)[None, :, t_p - (CONV_WIDTH - 1):]
    x = _mm_res(c, w_pw2_b, None, b_pw2[0], x, tm, PW2_TN, "conv_pw2")
    x, ffn1_p = _ffn_prompt(x, t_p, norm_ffn[1], wf1[0], ffn_dw[1], ffn_dw_b[1], wf1[1])
    y_prompt = _rmsnorm(x, norm_final, F32).reshape(n_p, t_p, d_model)

    a_p = []
    for g in range(N_DIL):
        a_p.append(_kv_rows(z, n_p, t_p, g, min(A_WINDOWS[g], PAST_LEN))[None])
    ssm_p = ssm_state(hlr, hli, n_p)
    ffn_p = jnp.stack([ffn0_p, ffn1_p])

    return (y_prompt, y_sample, a_p[0], a_s[0], a_p[1], a_s[1], a_p[2], a_s[2],
            ssm_p, ssm_s, conv_p, conv_s, ffn_p, ffn_s)
```

```python
import functools
import math

import jax
import jax.numpy as jnp
from jax import lax
from jax.experimental import pallas as pl
from jax.experimental.pallas import tpu as pltpu

F32 = jnp.float32
BF16 = jnp.bfloat16

HEAD_DIM = 128
A_HEADS = 16
D_A = A_HEADS * HEAD_DIM
A_WINDOWS = (128, 512, 2048)
A_DILATIONS = (1, 4, 16)
BAND = 128
N_DIL = 3
QKV_W = 3 * D_A
ROPE_THETA = 10000.0
PAST_LEN = 16384
SSM_GROUP = 16
SSM_STATE = 64
SLAB_G = 8
SLAB_S = SLAB_G * SSM_STATE
CHUNK = 8
CONV_WIDTH = 31
CONV_HALO = 32
RMS_EPS = 1e-6
LN_EPS = 1e-5
NEG_BIG = -1e30
VMEM_LIMIT_V7X = 56 * 1024 * 1024

PROMPT_TM = 1024
INPROJ_TN = 1024
OUTPROJ_TN = 1024
PW1_TN = 512
PW2_TN = 1024
FFN_TN = 256
FFN_PART_ROWS = 1024
FFN_DOWN_TM = 512
FFN_DOWN_TN = 512
GATE_TM = 512
CONV_TM = 256
NORM_TM = 512
COMBINE_TM = 512
STEP_INPROJ_TN = 512
STEP_GATE_TN = 512
STEP_OUTPROJ_TN = 512
STEP_PW1_TN = 256
STEP_PW2_TN = 512
STEP_FFN_DOWN_TN = 256


def _cp(*sem):
    return pltpu.CompilerParams(dimension_semantics=sem, vmem_limit_bytes=VMEM_LIMIT_V7X)


def _dot(a, b, **kw):
    return jnp.dot(a, b, preferred_element_type=F32, **kw)


def _dot_nt(a, b, **kw):
    return lax.dot_general(a, b, (((1,), (1,)), ((), ())), preferred_element_type=F32, **kw)


def _sigmoid(x):
    return 1.0 / (1.0 + jnp.exp(-x))


def _rmsnorm_kernel(x_ref, g_ref, o_ref):
    x = x_ref[...]
    ms = jnp.mean(x * x, axis=-1, keepdims=True)
    o_ref[...] = (x * lax.rsqrt(ms + RMS_EPS) * g_ref[...]).astype(o_ref.dtype)


def _rmsnorm(x, g, out_dtype):
    m, d = x.shape
    tm = min(m, NORM_TM)
    return pl.pallas_call(
        _rmsnorm_kernel,
        grid=(m // tm,),
        in_specs=[pl.BlockSpec((tm, d), lambda i: (i, 0)),
                  pl.BlockSpec((1, d), lambda i: (0, 0))],
        out_specs=pl.BlockSpec((tm, d), lambda i: (i, 0)),
        out_shape=jax.ShapeDtypeStruct((m, d), out_dtype),
        compiler_params=_cp("parallel"),
        name="rmsnorm",
    )(x, g.reshape(1, d))


def _rope_table_kernel(invf_ref, cos_ref, sin_ref, *, pos0, step):
    rows = cos_ref.shape[0]
    row = lax.broadcasted_iota(jnp.int32, (rows, HEAD_DIM), 0)
    lane = lax.broadcasted_iota(jnp.int32, (rows, HEAD_DIM), 1)
    pos = (row * step + pos0).astype(F32)
    ang = pos * invf_ref[...]
    cos_ref[...] = jnp.cos(ang)
    s = jnp.sin(ang)
    sin_ref[...] = jnp.where(lane < HEAD_DIM // 2, -s, s)


def _rope_tables(rows, pos0, step):
    half = HEAD_DIM // 2
    inv_freq = ROPE_THETA ** (-jnp.arange(half, dtype=F32) / half)
    invf = jnp.concatenate([inv_freq, inv_freq]).reshape(1, HEAD_DIM)
    shp = jax.ShapeDtypeStruct((rows, HEAD_DIM), F32)
    return pl.pallas_call(
        functools.partial(_rope_table_kernel, pos0=pos0, step=step),
        out_shape=(shp, shp),
        name="rope_tables",
    )(invf)


def _wspec(w, layer, kb, tn, kblk, jmap):
    if layer is None:
        return pl.BlockSpec((kb, tn), lambda *g: (kblk, jmap(*g)))
    return pl.BlockSpec((None, kb, tn), lambda *g: (layer, kblk, jmap(*g)))


def _use_w(w_ref, wb_ref, cols=slice(None)):
    w = w_ref[:, cols]
    if wb_ref is None:
        return w
    wb = w.astype(BF16)
    wb_ref[:, cols] = wb
    return wb


def _inproj_kernel(x_ref, w_ref, cos_ref, sin_ref, o_ref, *wb, tn, part):
    wb_ref = wb[0] if wb else None
    seg = (pl.program_id(1) * tn) // D_A
    is_rope = jnp.logical_and(seg < 3 * N_DIL, seg % 3 != 2)
    def body(rope):
        x = x_ref[...]
        for c0 in range(0, tn, part):
            acc = _dot(x, _use_w(w_ref, wb_ref, slice(c0, c0 + part)))
            for h in range(part // HEAD_DIM):
                xh = acc[:, h * HEAD_DIM:(h + 1) * HEAD_DIM]
                if rope:
                    xh = xh * cos_ref[...] + pltpu.roll(xh, HEAD_DIM // 2, 1) * sin_ref[...]
                o_ref[:, c0 + h * HEAD_DIM:c0 + (h + 1) * HEAD_DIM] = xh

    @pl.when(is_rope)
    def _():
        body(True)

    @pl.when(jnp.logical_not(is_rope))
    def _():
        body(False)


def _inproj(hn, w, layer, cos, sin, tm, tn):
    m, k = hn.shape
    n = w.shape[-1]
    ct = cos.shape[0] // tm
    emit = layer is not None
    out_specs = [pl.BlockSpec((tm, tn), lambda i, j: (i, j))]
    out_shape = [jax.ShapeDtypeStruct((m, n), F32)]
    if emit:
        out_specs.append(pl.BlockSpec((k, tn), lambda i, j: (0, j)))
        out_shape.append(jax.ShapeDtypeStruct((k, n), BF16))
    res = pl.pallas_call(
        functools.partial(_inproj_kernel, tn=tn, part=min(tn, 256)),
        grid=(m // tm, n // tn),
        in_specs=[pl.BlockSpec((tm, k), lambda i, j: (i, 0)),
                  _wspec(w, layer, k, tn, 0, lambda i, j: j),
                  pl.BlockSpec((tm, HEAD_DIM), lambda i, j: (i % ct, 0)),
                  pl.BlockSpec((tm, HEAD_DIM), lambda i, j: (i % ct, 0))],
        out_specs=out_specs,
        out_shape=out_shape,
        compiler_params=_cp("parallel", "arbitrary"),
        name="inproj_rope",
    )(hn, w, cos, sin)
    return res if emit else res[0]


def _band_attn_kernel(q_ref, kp_ref, kc_ref, vp_ref, vc_ref, o_ref, lse_ref):
    has_prev = pl.program_id(2) > 0
    qi = lax.broadcasted_iota(jnp.int32, (BAND, 2 * BAND), 0)
    ki = lax.broadcasted_iota(jnp.int32, (BAND, 2 * BAND), 1)
    valid = jnp.logical_and(jnp.logical_and(ki >= qi, ki <= qi + BAND),
                            jnp.logical_or(ki >= BAND, has_prev))
    lane = lax.broadcasted_iota(jnp.int32, (BAND, HEAD_DIM), 1)
    lse_tile = jnp.zeros((BAND, HEAD_DIM), F32)
    scale = HEAD_DIM ** -0.5
    for h in range(A_HEADS):
        sl = slice(h * HEAD_DIM, (h + 1) * HEAD_DIM)
        q = q_ref[:, sl].astype(BF16)
        k = jnp.concatenate([kp_ref[:, sl], kc_ref[:, sl]], axis=0).astype(BF16)
        v = jnp.concatenate([vp_ref[:, sl], vc_ref[:, sl]], axis=0).astype(BF16)
        s = _dot_nt(q, k) * scale
        s = jnp.where(valid, s, NEG_BIG)
        m = jnp.max(s, axis=-1, keepdims=True)
        p = jnp.exp(s - m)
        l = jnp.sum(p, axis=-1, keepdims=True)
        o_ref[:, sl] = _dot(p.astype(BF16), v) / l
        lse_tile = jnp.where(lane == h, m + jnp.log(l), lse_tile)
    lse_ref[...] = lse_tile


def _band_attn_dil_kernel(*refs, d, hpb, use_prev):
    n_in = 5 if use_prev else 3
    ins = [refs[i * hpb:(i + 1) * hpb] for i in range(n_in)]
    o_ref, lse_ref, os_, ls = refs[n_in * hpb:]
    if use_prev:
        q_refs, kp_refs, kc_refs, vp_refs, vc_refs = ins
    else:
        q_refs, kc_refs, vc_refs = ins
    hb = pl.program_id(2)
    has_prev = pl.program_id(1) > 0

    @pl.when(hb == 0)
    def _():
        lse_ref[...] = jnp.zeros(lse_ref.shape, F32)

    nk = 2 * BAND if use_prev else BAND
    qi = lax.broadcasted_iota(jnp.int32, (1, BAND, nk), 1)
    ki = lax.broadcasted_iota(jnp.int32, (1, BAND, nk), 2)
    if use_prev:
        valid = jnp.logical_and(jnp.logical_and(ki >= qi, ki <= qi + BAND),
                                jnp.logical_or(ki >= BAND, has_prev))
    else:
        valid = ki <= qi
    scale = HEAD_DIM ** -0.5

    def residues(ref):
        return [ref[pl.ds(r, BAND, stride=d), :] for r in range(d)]

    for h in range(hpb):
        q = jnp.stack(residues(q_refs[h])).astype(BF16)
        if use_prev:
            k = jnp.stack([jnp.concatenate([a, b], axis=0)
                           for a, b in zip(residues(kp_refs[h]), residues(kc_refs[h]))]).astype(BF16)
            v = jnp.stack([jnp.concatenate([a, b], axis=0)
                           for a, b in zip(residues(vp_refs[h]), residues(vc_refs[h]))]).astype(BF16)
        else:
            k = jnp.stack(residues(kc_refs[h])).astype(BF16)
            v = jnp.stack(residues(vc_refs[h])).astype(BF16)
        s = jnp.einsum('rqd,rkd->rqk', q, k, preferred_element_type=F32) * scale
        s = jnp.where(valid, s, NEG_BIG)
        m = jnp.max(s, axis=-1, keepdims=True)
        p = jnp.exp(s - m)
        l = jnp.sum(p, axis=-1, keepdims=True)
        o = jnp.einsum('rqk,rkd->rqd', p.astype(BF16), v, preferred_element_type=F32) / l
        lse_h = m + jnp.log(l)
        for r in range(d):
            os_[h, pl.ds(r, BAND, stride=d), :] = o[r]
            ls[h, pl.ds(r, BAND, stride=d), :] = jnp.broadcast_to(lse_h[r], (BAND, HEAD_DIM))
    lse = lse_ref[...]
    lane = lax.broadcasted_iota(jnp.int32, lse.shape, 1)
    for h in range(hpb):
        o_ref[:, h * HEAD_DIM:(h + 1) * HEAD_DIM] = os_[h]
        lse = jnp.where(lane == hb * hpb + h, ls[h], lse)
    lse_ref[...] = lse


def _band_attn(z, n_seq, t, g):
    d = A_DILATIONS[g]
    width = z.shape[1]
    zv = z.reshape(n_seq, t, width)
    if d == 1:
        qc = 3 * g

        def cur(off):
            return pl.BlockSpec((None, BAND, D_A), lambda n, r, b: (n, b, qc + off))

        def prev(off):
            return pl.BlockSpec((None, BAND, D_A), lambda n, r, b: (n, jnp.maximum(b - 1, 0), qc + off))

        o, lse = pl.pallas_call(
            _band_attn_kernel,
            grid=(n_seq, 1, t // BAND),
            in_specs=[cur(0), prev(1), cur(1), prev(2), cur(2)],
            out_specs=[pl.BlockSpec((None, BAND, D_A), lambda n, r, b: (n, b, 0)),
                       pl.BlockSpec((None, BAND, HEAD_DIM), lambda n, r, b: (n, b, 0))],
            out_shape=[jax.ShapeDtypeStruct((n_seq, t, D_A), F32),
                       jax.ShapeDtypeStruct((n_seq, t, HEAD_DIM), F32)],
            compiler_params=_cp("parallel", "parallel", "arbitrary"),
            name="band_attn_d1",
        )(zv, zv, zv, zv, zv)
        return o.reshape(n_seq * t, D_A), lse.reshape(n_seq * t, HEAD_DIM)

    rows = BAND * d
    nb = t // rows
    use_prev = nb > 1
    hpb = 8 if d <= 4 else 2
    cols = hpb * HEAD_DIM
    cpq = D_A // cols

    def head_blocks(off, back):
        c0 = (3 * g + off) * A_HEADS
        return [pl.BlockSpec((None, rows, HEAD_DIM),
                             lambda n, b, hb, h=h: (n, jnp.maximum(b - back, 0), c0 + hb * hpb + h))
                for h in range(hpb)]

    if use_prev:
        in_specs = (head_blocks(0, 0) + head_blocks(1, 1) + head_blocks(1, 0)
                    + head_blocks(2, 1) + head_blocks(2, 0))
    else:
        in_specs = head_blocks(0, 0) + head_blocks(1, 0) + head_blocks(2, 0)
    o, lse = pl.pallas_call(
        functools.partial(_band_attn_dil_kernel, d=d, hpb=hpb, use_prev=use_prev),
        grid=(n_seq, nb, cpq),
        in_specs=in_specs,
        out_specs=[pl.BlockSpec((None, rows, cols), lambda n, b, hb: (n, b, hb)),
                   pl.BlockSpec((None, rows, HEAD_DIM), lambda n, b, hb: (n, b, 0))],
        out_shape=[jax.ShapeDtypeStruct((n_seq, t, D_A), F32),
                   jax.ShapeDtypeStruct((n_seq, t, HEAD_DIM), F32)],
        scratch_shapes=[pltpu.VMEM((hpb, rows, HEAD_DIM), F32),
                        pltpu.VMEM((hpb, rows, HEAD_DIM), F32)],
        compiler_params=_cp("parallel", "parallel", "arbitrary"),
        name=f"band_attn_d{d}",
    )(*([zv] * len(in_specs)))
    return o.reshape(n_seq * t, D_A), lse.reshape(n_seq * t, HEAD_DIM)


def _combine_kernel(o0_ref, o1_ref, o2_ref, l0_ref, l1_ref, l2_ref, out_ref):
    a0, a1, a2 = l0_ref[...], l1_ref[...], l2_ref[...]
    m = jnp.maximum(jnp.maximum(a0, a1), a2)
    e0, e1, e2 = jnp.exp(a0 - m), jnp.exp(a1 - m), jnp.exp(a2 - m)
    inv = 1.0 / (e0 + e1 + e2)
    w0, w1, w2 = e0 * inv, e1 * inv, e2 * inv
    for h in range(A_HEADS):
        sl = slice(h * HEAD_DIM, (h + 1) * HEAD_DIM)
        out_ref[:, sl] = (w0[:, h:h + 1] * o0_ref[:, sl] + w1[:, h:h + 1] * o1_ref[:, sl]
                          + w2[:, h:h + 1] * o2_ref[:, sl]).astype(out_ref.dtype)


def _combine(os_, lses):
    m = os_[0].shape[0]
    tm = COMBINE_TM
    ob = pl.BlockSpec((tm, D_A), lambda i: (i, 0))
    lb = pl.BlockSpec((tm, HEAD_DIM), lambda i: (i, 0))
    return pl.pallas_call(
        _combine_kernel,
        grid=(m // tm,),
        in_specs=[ob, ob, ob, lb, lb, lb],
        out_specs=ob,
        out_shape=jax.ShapeDtypeStruct((m, D_A), BF16),
        compiler_params=_cp("parallel"),
        name="attn_combine",
    )(*os_, *lses)


def _step_attn_kernel(z_ref, c0_ref, c1_ref, c2_ref, o_ref):
    scale = HEAD_DIM ** -0.5
    ms, ls, accs = [], [], []
    for g, c_ref in enumerate((c0_ref, c1_ref, c2_ref)):
        base = 3 * g * A_HEADS
        q = z_ref[base:base + A_HEADS, :]
        kn = z_ref[base + A_HEADS:base + 2 * A_HEADS, :]
        vn = z_ref[base + 2 * A_HEADS:base + 3 * A_HEADS, :]
        s = jnp.sum(c_ref[:, 0] * q[None], axis=-1, keepdims=True) * scale
        s_new = jnp.sum(kn * q, axis=-1, keepdims=True) * scale
        m = jnp.maximum(jnp.max(s, axis=0), s_new)
        p = jnp.exp(s - m[None])
        p_new = jnp.exp(s_new - m)
        ls.append(jnp.sum(p, axis=0) + p_new)
        accs.append(jnp.sum(p * c_ref[:, 1], axis=0) + p_new * vn)
        ms.append(m)
    mm = jnp.maximum(jnp.maximum(ms[0], ms[1]), ms[2])
    es = [jnp.exp(m - mm) for m in ms]
    den = es[0] * ls[0] + es[1] * ls[1] + es[2] * ls[2]
    num = es[0] * accs[0] + es[1] * accs[1] + es[2] * accs[2]
    o_ref[...] = (num / den).astype(o_ref.dtype)


def _step_attn(zs, caches):
    n, width = zs.shape
    cviews, cspecs = [], []
    for g, c in enumerate(caches):
        d = A_DILATIONS[g]
        cviews.append(c.reshape(n, c.shape[1] // d, d, 2, A_HEADS, HEAD_DIM))
        cspecs.append(pl.BlockSpec((None, BAND, None, 2, A_HEADS, HEAD_DIM),
                                   lambda i: (i, 0, 0, 0, 0, 0)))
    zrows = width // HEAD_DIM
    out = pl.pallas_call(
        _step_attn_kernel,
        grid=(n,),
        in_specs=[pl.BlockSpec((None, zrows, HEAD_DIM), lambda i: (i, 0, 0))] + cspecs,
        out_specs=pl.BlockSpec((None, A_HEADS, HEAD_DIM), lambda i: (i, 0, 0)),
        out_shape=jax.ShapeDtypeStruct((n, A_HEADS, HEAD_DIM), BF16),
        compiler_params=_cp("parallel"),
        name="step_attn",
    )(zs.reshape(n, zrows, HEAD_DIM), *cviews)
    return out.reshape(n, D_A)


def _cache_shift_kernel(c_ref, nxt_ref, new_ref, o_ref, *, lb):
    last = pl.program_id(1) == pl.num_programs(1) - 1
    o_ref[0:lb - 1] = c_ref[1:lb]

    @pl.when(last)
    def _():
        o_ref[lb - 1] = new_ref[0]

    @pl.when(jnp.logical_not(last))
    def _():
        o_ref[lb - 1] = nxt_ref[0]


def _cache_shift(cache, new_kv):
    n, length = cache.shape[:2]
    lb = min(length, 4 * BAND)
    nb = length // lb
    tail = cache.shape[2:]
    zeros = (0,) * len(tail)
    return pl.pallas_call(
        functools.partial(_cache_shift_kernel, lb=lb),
        grid=(n, nb),
        in_specs=[pl.BlockSpec((None, lb) + tail, lambda i, b: (i, b) + zeros),
                  pl.BlockSpec((None, 1) + tail, lambda i, b: (i, jnp.minimum(b + 1, nb - 1) * lb) + zeros),
                  pl.BlockSpec((None, 1) + tail, lambda i, b: (i, 0) + zeros)],
        out_specs=pl.BlockSpec((None, lb) + tail, lambda i, b: (i, b) + zeros),
        out_shape=jax.ShapeDtypeStruct(cache.shape, cache.dtype),
        compiler_params=_cp("parallel", "parallel"),
        name="cache_shift",
    )(cache, cache, new_kv)


def _kv_rows_kernel(k_ref, v_ref, o_ref, *, tm):
    per = 2 * A_HEADS
    for kv, x_ref in enumerate((k_ref, v_ref)):
        for h in range(A_HEADS):
            o_ref[pl.ds(kv * A_HEADS + h, tm, stride=per), :] = x_ref[:, h * HEAD_DIM:(h + 1) * HEAD_DIM]


def _kv_rows(z, n_seq, t, g, length):
    width = z.shape[1]
    zv = z.reshape(n_seq, t, width)
    tm = min(length, 2 * BAND)
    r0 = (t - length) // tm
    c0 = 3 * g + 1
    per = 2 * A_HEADS
    out = pl.pallas_call(
        functools.partial(_kv_rows_kernel, tm=tm),
        grid=(n_seq, length // tm),
        in_specs=[pl.BlockSpec((None, tm, D_A), lambda n, i: (n, r0 + i, c0)),
                  pl.BlockSpec((None, tm, D_A), lambda n, i: (n, r0 + i, c0 + 1))],
        out_specs=pl.BlockSpec((None, tm * per, HEAD_DIM), lambda n, i: (n, i, 0)),
        out_shape=jax.ShapeDtypeStruct((n_seq, length * per, HEAD_DIM), F32),
        compiler_params=_cp("parallel", "parallel"),
        name="kv_rows",
    )(zv, zv)
    return out.reshape(n_seq, length, 2, A_HEADS, HEAD_DIM)


def _ssm_discretise(lre, lim, ldt):
    dt = jnp.exp(ldt)
    mag = jnp.exp(lre * dt)
    ar = mag * jnp.cos(lim * dt)
    ai = mag * jnp.sin(lim * dt)
    den = lre * lre + lim * lim
    zr = ((ar - 1.0) * lre + ai * lim) / den
    zi = (ai * lre - (ar - 1.0) * lim) / den
    return ar, ai, zr, zi


def _slab_mask():
    r = lax.broadcasted_iota(jnp.int32, (SLAB_G * SSM_GROUP, SLAB_S), 0) // SSM_GROUP
    c = lax.broadcasted_iota(jnp.int32, (SLAB_G * SSM_GROUP, SLAB_S), 1) // SSM_STATE
    return (r == c).astype(F32)


def _block_diag(x16, mask):
    return jnp.concatenate([x16] * SLAB_G, axis=0) * mask


def _ssm_prompt_kernel(u_ref, lre_ref, lim_ref, ldt_ref, bre_ref, bim_ref, cre_ref, cim_ref, d_ref,
                       y_ref, hlr_ref, hli_ref,
                       wer_ref, wei_ref, tz_ref, bdr_ref, bdi_ref, er_ref, ei_ref, *, n_seq, n_chunk):
    def split(a):
        ah = a.astype(BF16)
        return ah, (a - ah.astype(F32)).astype(BF16)

    def dot_nt_x3(a, b_split):
        ah, al = split(a)
        bh, bl = b_split
        return _dot_nt(ah, bh) + _dot_nt(ah, bl) + _dot_nt(al, bh)

    rows = n_seq * n_chunk

    def step_rows(j):
        return pl.ds(j, rows, stride=CHUNK)

    ar, ai, zr, zi = _ssm_discretise(lre_ref[...], lim_ref[...], ldt_ref[...])
    mask = _slab_mask()
    bre, bim = bre_ref[...], bim_ref[...]
    cre, cim = cre_ref[...], cim_ref[...]
    cre_bd = split(_block_diag(cre, mask))
    cim_bd = split(_block_diag(cim, mask))
    lanes = SLAB_G * SSM_GROUP
    tz_ref[...] = jnp.zeros(tz_ref.shape, BF16)
    pr, pim = jnp.ones_like(ar), jnp.zeros_like(ar)
    for m in range(CHUNK + 1):
        if m < CHUNK:
            wr = zr * pr - zi * pim
            wi = zr * pim + zi * pr
            we_r = _block_diag(bre * wr - bim * wi, mask)
            we_i = _block_diag(bre * wi + bim * wr, mask)
            j = CHUNK - 1 - m
            wer_ref[j * lanes:(j + 1) * lanes, :] = we_r.astype(BF16)
            wei_ref[j * lanes:(j + 1) * lanes, :] = we_i.astype(BF16)
            tz = (dot_nt_x3(we_r, cre_bd) - dot_nt_x3(we_i, cim_bd)).astype(BF16)
            for j in range(CHUNK - m):
                tz_ref[j * lanes:(j + 1) * lanes, (j + m) * lanes:(j + m + 1) * lanes] = tz
        if m >= 1:
            bdr_ref[(m - 1) * lanes:m * lanes, :] = _block_diag(cre * pr - cim * pim, mask).astype(BF16)
            bdi_ref[(m - 1) * lanes:m * lanes, :] = _block_diag(-(cre * pim + cim * pr), mask).astype(BF16)
        if m == CHUNK:
            a8r, a8i = pr, pim
        pr, pim = pr * ar - pim * ai, pr * ai + pim * ar

    xcat = jnp.concatenate([u_ref[step_rows(j), :].astype(BF16) for j in range(CHUNK)], axis=1)

    er_ref[...] = _dot(xcat, wer_ref[...])
    ei_ref[...] = _dot(xcat, wei_ref[...])

    def body(k, carry):
        out = []
        for n in range(n_seq):
            hr, him = carry[2 * n], carry[2 * n + 1]
            row = n * n_chunk + k
            e_r = er_ref[pl.ds(row, 1), :]
            e_i = ei_ref[pl.ds(row, 1), :]
            er_ref[pl.ds(row, 1), :] = hr
            ei_ref[pl.ds(row, 1), :] = him
            out.append(a8r * hr - a8i * him + e_r)
            out.append(a8r * him + a8i * hr + e_i)
        return tuple(out)

    zero = jnp.zeros((1, SLAB_S), F32)
    fin = lax.fori_loop(0, n_chunk, body, (zero,) * (2 * n_seq), unroll=2)
    for n in range(n_seq):
        hlr_ref[pl.ds(n, 1), :] = fin[2 * n]
        hli_ref[pl.ds(n, 1), :] = fin[2 * n + 1]

    y = (_dot(xcat, tz_ref[...]) + _dot_nt(er_ref[...].astype(BF16), bdr_ref[...])
         + _dot_nt(ei_ref[...].astype(BF16), bdi_ref[...]))
    for jp in range(CHUNK):
        y_ref[step_rows(jp), :] = y[:, jp * lanes:(jp + 1) * lanes] + d_ref[...] * u_ref[step_rows(jp), :]


def _ssm_params(a_re, a_im, b_re, b_im, c_re, c_im, d_skip, log_dt):
    g = a_re.shape[0]
    ns = g // SLAB_G

    def row(x):
        return x.astype(F32).reshape(ns, 1, SLAB_S)

    def chan_rows(x):
        return x.astype(F32).reshape(ns, SLAB_G, SSM_GROUP, SSM_STATE).transpose(0, 2, 1, 3) \
            .reshape(ns, SSM_GROUP, SLAB_S)

    ldt = jnp.broadcast_to(log_dt.astype(F32)[:, None], (g, SSM_STATE))
    return (row(a_re), row(a_im), row(ldt),
            chan_rows(b_re.transpose(0, 2, 1)), chan_rows(b_im.transpose(0, 2, 1)),
            chan_rows(c_re), chan_rows(c_im),
            d_skip.astype(F32).reshape(ns, 1, SLAB_G * SSM_GROUP))


def _ssm_prompt(z, n_seq, t, u_col0, sp):
    m, width = z.shape
    rows = m // CHUNK
    n_chunk = t // CHUNK
    lanes = SLAB_G * SSM_GROUP
    ns = sp[0].shape[0]
    ub = u_col0 // lanes

    def pspec(r, c):
        return pl.BlockSpec((None, r, c), lambda s: (s, 0, 0))

    y, hlr, hli = pl.pallas_call(
        functools.partial(_ssm_prompt_kernel, n_seq=n_seq, n_chunk=n_chunk),
        grid=(ns,),
        in_specs=[pl.BlockSpec((m, lanes), lambda s: (0, ub + s)),
                  pspec(1, SLAB_S), pspec(1, SLAB_S), pspec(1, SLAB_S),
                  pspec(SSM_GROUP, SLAB_S), pspec(SSM_GROUP, SLAB_S),
                  pspec(SSM_GROUP, SLAB_S), pspec(SSM_GROUP, SLAB_S), pspec(1, lanes)],
        out_specs=[pl.BlockSpec((m, lanes), lambda s: (0, s)),
                   pl.BlockSpec((None, n_seq, SLAB_S), lambda s: (s, 0, 0)),
                   pl.BlockSpec((None, n_seq, SLAB_S), lambda s: (s, 0, 0))],
        out_shape=[jax.ShapeDtypeStruct((m, ns * lanes), F32),
                   jax.ShapeDtypeStruct((ns, n_seq, SLAB_S), F32),
                   jax.ShapeDtypeStruct((ns, n_seq, SLAB_S), F32)],
        scratch_shapes=[pltpu.VMEM((CHUNK * lanes, SLAB_S), BF16),
                        pltpu.VMEM((CHUNK * lanes, SLAB_S), BF16),
                        pltpu.VMEM((CHUNK * lanes, CHUNK * lanes), BF16),
                        pltpu.VMEM((CHUNK * lanes, SLAB_S), BF16),
                        pltpu.VMEM((CHUNK * lanes, SLAB_S), BF16),
                        pltpu.VMEM((rows, SLAB_S), F32),
                        pltpu.VMEM((rows, SLAB_S), F32)],
        compiler_params=_cp("parallel"),
        name="ssm_prompt",
    )(z, *sp)
    return y, hlr, hli


def _ssm_step_kernel(u_ref, h0r_ref, h0i_ref, lre_ref, lim_ref, ldt_ref, bre_ref, bim_ref,
                     cre_ref, cim_ref, d_ref, y_ref, hr_ref, hi_ref):
    ar, ai, zr, zi = _ssm_discretise(lre_ref[...], lim_ref[...], ldt_ref[...])
    mask = _slab_mask()
    bre, bim = bre_ref[...], bim_ref[...]
    we_r = _block_diag(bre * zr - bim * zi, mask).astype(BF16)
    we_i = _block_diag(bre * zi + bim * zr, mask).astype(BF16)
    u = u_ref[...]
    ub = u.astype(BF16)
    h0r, h0i = h0r_ref[...], h0i_ref[...]
    hr = ar * h0r - ai * h0i + _dot(ub, we_r)
    him = ar * h0i + ai * h0r + _dot(ub, we_i)
    hr_ref[...] = hr
    hi_ref[...] = him
    cr_bd = _block_diag(cre_ref[...], mask).astype(BF16)
    ci_bd = _block_diag(cim_ref[...], mask).astype(BF16)
    y_ref[...] = (_dot_nt(hr.astype(BF16), cr_bd) - _dot_nt(him.astype(BF16), ci_bd)
                  + d_ref[...] * u)


def _ssm_step(zs, u_col0, h0r, h0i, sp):
    n = zs.shape[0]
    lanes = SLAB_G * SSM_GROUP
    ns = sp[0].shape[0]
    ub = u_col0 // lanes

    def pspec(r, c):
        return pl.BlockSpec((None, r, c), lambda s: (s, 0, 0))

    hspec = pl.BlockSpec((None, n, SLAB_S), lambda s: (s, 0, 0))
    return pl.pallas_call(
        _ssm_step_kernel,
        grid=(ns,),
        in_specs=[pl.BlockSpec((n, lanes), lambda s: (0, ub + s)), hspec, hspec,
                  pspec(1, SLAB_S), pspec(1, SLAB_S), pspec(1, SLAB_S),
                  pspec(SSM_GROUP, SLAB_S), pspec(SSM_GROUP, SLAB_S),
                  pspec(SSM_GROUP, SLAB_S), pspec(SSM_GROUP, SLAB_S), pspec(1, lanes)],
        out_specs=[pl.BlockSpec((n, lanes), lambda s: (0, s)), hspec, hspec],
        out_shape=[jax.ShapeDtypeStruct((n, ns * lanes), F32),
                   jax.ShapeDtypeStruct((ns, n, SLAB_S), F32),
                   jax.ShapeDtypeStruct((ns, n, SLAB_S), F32)],
        compiler_params=_cp("parallel"),
        name="ssm_step",
    )(zs, h0r, h0i, *sp)


def _gelu_tanh(x):
    return 0.5 * x * (1.0 + jnp.tanh(math.sqrt(2.0 / math.pi) * (x + 0.044715 * (x * x * x))))


def _ssm_gate_kernel(y_ref, w_ref, b_ref, o_ref):
    yg = _gelu_tanh(y_ref[...])
    gate = _dot(yg.astype(BF16), w_ref[...]) + b_ref[...]
    o_ref[...] = (yg * _sigmoid(gate)).astype(o_ref.dtype)


def _ssm_gate(y, w, b, tm):
    m, d = y.shape
    return pl.pallas_call(
        _ssm_gate_kernel,
        grid=(m // tm,),
        in_specs=[pl.BlockSpec((tm, d), lambda i: (i, 0)),
                  pl.BlockSpec((d, d), lambda i: (0, 0)),
                  pl.BlockSpec((1, d), lambda i: (0, 0))],
        out_specs=pl.BlockSpec((tm, d), lambda i: (i, 0)),
        out_shape=jax.ShapeDtypeStruct((m, d), BF16),
        compiler_params=_cp("parallel"),
        name="ssm_gate",
    )(y, w, b.reshape(1, d).astype(F32))


def _ssm_gate_step_kernel(y_ref, yc_ref, w_ref, b_ref, o_ref, wb_ref):
    yg = _gelu_tanh(y_ref[...])
    gate = _dot(yg.astype(BF16), _use_w(w_ref, wb_ref)) + b_ref[...]
    o_ref[...] = (_gelu_tanh(yc_ref[...]) * _sigmoid(gate)).astype(o_ref.dtype)


def _ssm_gate_step(y, w, layer, b, tn):
    m, d = y.shape
    return pl.pallas_call(
        _ssm_gate_step_kernel,
        grid=(d // tn,),
        in_specs=[pl.BlockSpec((m, d), lambda j: (0, 0)),
                  pl.BlockSpec((m, tn), lambda j: (0, j)),
                  _wspec(w, layer, d, tn, 0, lambda j: j),
                  pl.BlockSpec((1, tn), lambda j: (0, j))],
        out_specs=[pl.BlockSpec((m, tn), lambda j: (0, j)),
                   pl.BlockSpec((d, tn), lambda j: (0, j))],
        out_shape=[jax.ShapeDtypeStruct((m, d), BF16), jax.ShapeDtypeStruct((d, d), BF16)],
        compiler_params=_cp("arbitrary"),
        name="ssm_gate_step",
    )(y, y, w, b.reshape(1, d).astype(F32))


def _outproj_kernel(a_ref, s_ref, wa_ref, ws_ref, x_ref, o_ref, *wb):
    wab_ref, wsb_ref = wb if wb else (None, None)
    o_ref[...] = (x_ref[...] + _dot(a_ref[...], _use_w(wa_ref, wab_ref))
                  + _dot(s_ref[...], _use_w(ws_ref, wsb_ref)))


def _outproj(attn, ssm, wa, ws, layer, x, tm, tn):
    m, ka = attn.shape
    n = wa.shape[-1]
    emit = layer is not None
    out_specs = [pl.BlockSpec((tm, tn), lambda i, j: (i, j))]
    out_shape = [jax.ShapeDtypeStruct((m, n), F32)]
    if emit:
        out_specs += [pl.BlockSpec((ka, tn), lambda i, j: (0, j))] * 2
        out_shape += [jax.ShapeDtypeStruct((ka, n), BF16)] * 2
    res = pl.pallas_call(
        _outproj_kernel,
        grid=(m // tm, n // tn),
        in_specs=[pl.BlockSpec((tm, ka), lambda i, j: (i, 0)),
                  pl.BlockSpec((tm, ka), lambda i, j: (i, 0)),
                  _wspec(wa, layer, ka, tn, 0, lambda i, j: j),
                  _wspec(ws, layer, ka, tn, 1 if emit else 0, lambda i, j: j),
                  pl.BlockSpec((tm, tn), lambda i, j: (i, j))],
        out_specs=out_specs,
        out_shape=out_shape,
        compiler_params=_cp("parallel", "arbitrary"),
        name="outproj",
    )(attn, ssm, wa, ws, x)
    return res if emit else res[0]


def _mm_res_kernel(a_ref, w_ref, b_ref, x_ref, o_ref, *wb):
    wb_ref = wb[0] if wb else None
    o_ref[...] = x_ref[...] + (_dot(a_ref[...], _use_w(w_ref, wb_ref)) + b_ref[...])


def _mm_res(a, w, layer, b, x, tm, tn, name):
    m, k = a.shape
    n = w.shape[-1]
    emit = layer is not None
    out_specs = [pl.BlockSpec((tm, tn), lambda i, j: (i, j))]
    out_shape = [jax.ShapeDtypeStruct((m, n), F32)]
    if emit:
        out_specs.append(pl.BlockSpec((k, tn), lambda i, j: (0, j)))
        out_shape.append(jax.ShapeDtypeStruct((k, n), BF16))
    res = pl.pallas_call(
        _mm_res_kernel,
        grid=(m // tm, n // tn),
        in_specs=[pl.BlockSpec((tm, k), lambda i, j: (i, 0)),
                  _wspec(w, layer, k, tn, 0, lambda i, j: j),
                  pl.BlockSpec((1, tn), lambda i, j: (0, j)),
                  pl.BlockSpec((tm, tn), lambda i, j: (i, j))],
        out_specs=out_specs,
        out_shape=out_shape,
        compiler_params=_cp("parallel", "arbitrary"),
        name=name,
    )(a, w, b.reshape(1, n).astype(F32), x)
    return res if emit else res[0]


def _pw1_kernel(x_ref, wa_ref, wg_ref, ba_ref, bg_ref, o_ref, *wb):
    wab_ref, wgb_ref = wb if wb else (None, None)
    x = x_ref[...]
    a = _dot(x, _use_w(wa_ref, wab_ref)) + ba_ref[...]
    gate = _dot(x, _use_w(wg_ref, wgb_ref)) + bg_ref[...]
    o_ref[...] = a * _sigmoid(gate)


def _pw1(hn, wa, wg, layer, b, tm, tn):
    m, k = hn.shape
    n = b.shape[0] // 2
    nj = n // tn
    emit = layer is not None
    b2 = b.reshape(1, 2 * n).astype(F32)
    out_specs = [pl.BlockSpec((tm, tn), lambda i, j: (i, j))]
    out_shape = [jax.ShapeDtypeStruct((m, n), F32)]
    if emit:
        out_specs += [pl.BlockSpec((k, tn), lambda i, j: (0, j))] * 2
        out_shape += [jax.ShapeDtypeStruct((k, n), BF16)] * 2
    goff = nj if emit else 0
    res = pl.pallas_call(
        _pw1_kernel,
        grid=(m // tm, nj),
        in_specs=[pl.BlockSpec((tm, k), lambda i, j: (i, 0)),
                  _wspec(wa, layer, k, tn, 0, lambda i, j: j),
                  _wspec(wg, layer, k, tn, 0, lambda i, j: j + goff),
                  pl.BlockSpec((1, tn), lambda i, j: (0, j)),
                  pl.BlockSpec((1, tn), lambda i, j: (0, j + nj))],
        out_specs=out_specs,
        out_shape=out_shape,
        compiler_params=_cp("parallel", "arbitrary"),
        name="conv_pw1_glu",
    )(hn, wa, wg, b2, b2)
    return res if emit else res[0]


def _ln_swish(y, g, b):
    mu = jnp.mean(y, axis=-1, keepdims=True)
    yc = y - mu
    var = jnp.mean(yc * yc, axis=-1, keepdims=True)
    yn = yc * lax.rsqrt(var + LN_EPS) * g + b
    return yn * _sigmoid(yn)


def _conv31_kernel(u_ref, halo_ref, w_ref, b_ref, g_ref, bb_ref, o_ref, ext_ref, y3_ref, y_ref,
                   *, tm, tiles_per_seq, slab):
    seq_start = (pl.program_id(0) % tiles_per_seq) == 0
    nc = u_ref.shape[1] // HEAD_DIM
    for c in range(nc):
        cs = slice(c * HEAD_DIM, (c + 1) * HEAD_DIM)
        ext_ref[pl.ds(c, CONV_HALO, stride=slab), :] = jnp.where(seq_start, 0.0, halo_ref[:, cs])
        ext_ref[pl.ds(CONV_HALO * slab + c, tm, stride=slab), :] = u_ref[:, cs]
    off = CONV_HALO - (CONV_WIDTH - 1)
    rw = 8

    def body(i, carry):
        for ct in range(nc // 8):
            base = pl.multiple_of((i * rw + off) * slab, 8) + ct * 8
            win = [ext_ref[pl.ds(base + s * slab, 8), :] for s in range(rw + CONV_WIDTH - 1)]
            acc = [None] * rw
            for j in range(CONV_WIDTH):
                wj = w_ref[j, ct * 8:(ct + 1) * 8, :]
                for tt in range(rw):
                    term = wj * win[tt + j]
                    acc[tt] = term if j == 0 else acc[tt] + term
            bias = b_ref[ct * 8:(ct + 1) * 8, :]
            obase = pl.multiple_of(i * rw * slab, 8) + ct * 8
            for tt in range(rw):
                y3_ref[pl.ds(obase + tt * slab, 8), :] = acc[tt] + bias
        return carry

    lax.fori_loop(0, tm // rw, body, 0)
    for c in range(nc):
        y_ref[:, c * HEAD_DIM:(c + 1) * HEAD_DIM] = y3_ref[pl.ds(c, tm, stride=slab), :]
    o_ref[...] = _ln_swish(y_ref[...], g_ref[...], bb_ref[...]).astype(o_ref.dtype)


def _conv31(u, t, w, b, g, bb, tm):
    m, d = u.shape
    nc = d // HEAD_DIM
    slab = nc + 8
    row = lambda v: v.reshape(1, d).astype(F32)
    rspec = pl.BlockSpec((1, d), lambda i: (0, 0))
    hb = tm // CONV_HALO
    return pl.pallas_call(
        functools.partial(_conv31_kernel, tm=tm, tiles_per_seq=t // tm, slab=slab),
        grid=(m // tm,),
        in_specs=[pl.BlockSpec((tm, d), lambda i: (i, 0)),
                  pl.BlockSpec((CONV_HALO, d), lambda i: (jnp.maximum(i * hb - 1, 0), 0)),
                  pl.BlockSpec((CONV_WIDTH, nc, HEAD_DIM), lambda i: (0, 0, 0)),
                  pl.BlockSpec((nc, HEAD_DIM), lambda i: (0, 0)), rspec, rspec],
        out_specs=pl.BlockSpec((tm, d), lambda i: (i, 0)),
        out_shape=jax.ShapeDtypeStruct((m, d), BF16),
        scratch_shapes=[pltpu.VMEM(((tm + CONV_HALO) * slab, HEAD_DIM), F32),
                        pltpu.VMEM((tm * slab, HEAD_DIM), F32),
                        pltpu.VMEM((tm, d), F32)],
        compiler_params=_cp("parallel"),
        name="conv31_ln_swish",
    )(u, u, w.astype(F32).reshape(CONV_WIDTH, nc, HEAD_DIM), b.reshape(nc, HEAD_DIM).astype(F32),
      row(g), row(bb))


def _conv31_step_kernel(u_ref, st_ref, w_ref, b_ref, g_ref, bb_ref, o_ref):
    acc = w_ref[0:1, :] * st_ref[0]
    for j in range(1, CONV_WIDTH - 1):
        acc = acc + w_ref[j:j + 1, :] * st_ref[j]
    acc = acc + w_ref[CONV_WIDTH - 1:CONV_WIDTH, :] * u_ref[...] + b_ref[...]
    o_ref[...] = _ln_swish(acc, g_ref[...], bb_ref[...]).astype(o_ref.dtype)


def _conv31_step(u, state_t, w, b, g, bb):
    n, d = u.shape
    row = lambda v: v.reshape(1, d).astype(F32)
    return pl.pallas_call(
        _conv31_step_kernel,
        out_shape=jax.ShapeDtypeStruct((n, d), BF16),
        compiler_params=pltpu.CompilerParams(vmem_limit_bytes=VMEM_LIMIT_V7X),
        name="conv31_step",
    )(u, state_t, w.astype(F32), row(b), row(g), row(bb))


def _ffn_up_kernel(x_ref, w_ref, cwg_ref, cwv_ref, cbg_ref, cbv_ref,
                   act_ref, zlg_ref, zlv_ref, z_scr, *, t, tn, rows):
    nc = tn // HEAD_DIM
    cw = (cwg_ref, cwv_ref)
    cb = (cbg_ref, cbv_ref)
    hist = [jnp.zeros((8, HEAD_DIM), F32)] * (2 * nc)
    for p, r0 in enumerate(range(0, t, rows)):
        z = _dot(x_ref[r0:r0 + rows, :], w_ref[...])
        zc = []
        for c in range(2 * nc):
            zs = z_scr.at[p, c]
            zn = z[:, c * HEAD_DIM:(c + 1) * HEAD_DIM]
            zs[0:8, :] = hist[c]
            zs[8:, :] = zn
            hist[c] = zn[rows - 8:, :]
            half, cc = divmod(c, nc)
            ls = slice(cc * HEAD_DIM, (cc + 1) * HEAD_DIM)
            zc.append(cw[half][2:3, ls] * zn + cw[half][1:2, ls] * zs[pl.ds(7, rows, stride=1), :]
                      + cw[half][0:1, ls] * zs[pl.ds(6, rows, stride=1), :] + cb[half][:, ls])
        for c in range(nc):
            gate = zc[c]
            act_ref[r0:r0 + rows, c * HEAD_DIM:(c + 1) * HEAD_DIM] = (
                gate * _sigmoid(gate) * zc[nc + c]).astype(act_ref.dtype)
    for c in range(nc):
        zlg_ref[:, c * HEAD_DIM:(c + 1) * HEAD_DIM] = hist[c]
        zlv_ref[:, c * HEAD_DIM:(c + 1) * HEAD_DIM] = hist[nc + c]


def _ffn_up(hn, t, w, cw, cb, tn):
    m, k = hn.shape
    dff = w.shape[1] // 2
    nj = dff // tn
    cw = cw.astype(F32)
    cb2 = cb.reshape(1, 2 * dff).astype(F32)
    zl = jax.ShapeDtypeStruct((m // t, 8, dff), F32)
    rows = min(t, FFN_PART_ROWS)
    return pl.pallas_call(
        functools.partial(_ffn_up_kernel, t=t, tn=tn, rows=rows),
        grid=(m // t, nj),
        in_specs=[pl.BlockSpec((t, k), lambda i, j: (i, 0)),
                  pl.BlockSpec((k, 2 * tn), lambda i, j: (0, j)),
                  pl.BlockSpec((3, tn), lambda i, j: (0, j)),
                  pl.BlockSpec((3, tn), lambda i, j: (0, j + nj)),
                  pl.BlockSpec((1, tn), lambda i, j: (0, j)),
                  pl.BlockSpec((1, tn), lambda i, j: (0, j + nj))],
        out_specs=[pl.BlockSpec((t, tn), lambda i, j: (i, j)),
                   pl.BlockSpec((None, 8, tn), lambda i, j: (i, 0, j)),
                   pl.BlockSpec((None, 8, tn), lambda i, j: (i, 0, j))],
        out_shape=[jax.ShapeDtypeStruct((m, dff), BF16), zl, zl],
        scratch_shapes=[pltpu.VMEM((t // rows, 2 * tn // HEAD_DIM, 8 + rows, HEAD_DIM), F32)],
        compiler_params=_cp("parallel", "arbitrary"),
        name="ffn_up_conv_glu",
    )(hn, w, cw, cw, cb2, cb2)


def _ffn_up_step_kernel(x_ref, wg_ref, wv_ref, sg_ref, sv_ref, cwg_ref, cwv_ref, cbg_ref, cbv_ref,
                        act_ref, zg_ref, zv_ref, wb_ref):
    x = x_ref[...]
    tn = wg_ref.shape[1]

    def half(w_ref, c0, s_ref, cw_ref, cb_ref, z_ref):
        wb = w_ref[...].astype(BF16)
        wb_ref[:, c0:c0 + tn] = wb
        z = _dot(x, wb)
        z_ref[...] = z
        return cw_ref[2:3, :] * z + cw_ref[1:2, :] * s_ref[1] + cw_ref[0:1, :] * s_ref[0] + cb_ref[...]

    gate = half(wg_ref, 0, sg_ref, cwg_ref, cbg_ref, zg_ref)
    val = half(wv_ref, tn, sv_ref, cwv_ref, cbv_ref, zv_ref)
    act_ref[...] = (gate * _sigmoid(gate) * val).astype(act_ref.dtype)


def _ffn_up_step(hn, state_t, w, layer, cw, cb, tn):
    n, k = hn.shape
    dff = w.shape[-1] // 2
    nj = dff // tn
    cw = cw.astype(F32)
    cb2 = cb.reshape(1, 2 * dff).astype(F32)
    zs = jax.ShapeDtypeStruct((n, dff), F32)
    return pl.pallas_call(
        _ffn_up_step_kernel,
        grid=(nj,),
        in_specs=[pl.BlockSpec((n, k), lambda j: (0, 0)),
                  _wspec(w, layer, k, tn, 0, lambda j: j),
                  _wspec(w, layer, k, tn, 0, lambda j: j + nj),
                  pl.BlockSpec((2, n, tn), lambda j: (0, 0, j)),
                  pl.BlockSpec((2, n, tn), lambda j: (0, 0, j + nj)),
                  pl.BlockSpec((3, tn), lambda j: (0, j)),
                  pl.BlockSpec((3, tn), lambda j: (0, j + nj)),
                  pl.BlockSpec((1, tn), lambda j: (0, j)),
                  pl.BlockSpec((1, tn), lambda j: (0, j + nj))],
        out_specs=[pl.BlockSpec((n, tn), lambda j: (0, j)),
                   pl.BlockSpec((n, tn), lambda j: (0, j)),
                   pl.BlockSpec((n, tn), lambda j: (0, j)),
                   pl.BlockSpec((k, 2 * tn), lambda j: (0, j))],
        out_shape=[jax.ShapeDtypeStruct((n, dff), BF16), zs, zs,
                   jax.ShapeDtypeStruct((k, 2 * dff), BF16)],
        compiler_params=_cp("arbitrary"),
        name="ffn_up_step",
    )(hn, w, w, state_t, state_t, cw, cw, cb2, cb2)


def _ffn_prompt(x, t, norm_g, w_up, cw, cb, w_down):
    hn = _rmsnorm(x, norm_g, BF16)
    act, zlg, zlv = _ffn_up(hn, t, w_up, cw, cb, FFN_TN)
    last = jnp.concatenate([zlg[:, 6:8], zlv[:, 6:8]], axis=-1)
    xo = _mm_res(act, w_down, None, jnp.zeros((w_down.shape[1],), F32), x, FFN_DOWN_TM, FFN_DOWN_TN, "ffn_down")
    return xo, last


def _ffn_step(x, prev_state_t, norm_g, w_up, w_down, layer, cw, cb):
    m = x.shape[0]
    hn = _rmsnorm(x, norm_g, BF16)
    act, zg, zv, wu_b = _ffn_up_step(hn, prev_state_t, w_up, layer, cw, cb, FFN_TN)
    znew = jnp.concatenate([zg, zv], axis=-1)
    last = jnp.concatenate([prev_state_t[1][:, None], znew[:, None]], axis=1)
    xo, wd_b = _mm_res(act, w_down, layer, jnp.zeros((w_down.shape[-1],), F32), x, m, STEP_FFN_DOWN_TN,
                       "ffn_down_step")
    return xo, last, (wu_b, wd_b)


def kernel(x_prompt, x_sample, cache_a0, cache_a1, cache_a2, state_ssm, state_conv, state_ffn, norm_mix, norm_ffn, norm_final, w_in_even, w_out_even, ssm_a_re, ssm_a_im, ssm_b_re, ssm_b_im, ssm_c_re, ssm_c_im, ssm_d, ssm_log_dt, w_glu, b_glu, w_pw1, b_pw1, w_dw, b_dw, ln_g, ln_b, w_pw2, b_pw2, w_up, ffn_dw, ffn_dw_b, w_down):
    n_p, t_p, d_model = x_prompt.shape
    n_s = x_sample.shape[0]
    caches = (cache_a0, cache_a1, cache_a2)
    u_col0 = N_DIL * QKV_W
    sp = _ssm_params(ssm_a_re[0], ssm_a_im[0], ssm_b_re[0], ssm_b_im[0], ssm_c_re[0], ssm_c_im[0],
                     ssm_d[0], ssm_log_dt[0])
    ns = sp[0].shape[0]

    def ssm_state(hr, him, n):
        def unslab(h):
            return h.reshape(ns, n, SLAB_G, SSM_STATE).transpose(1, 0, 2, 3).reshape(n, ns * SLAB_G, SSM_STATE)
        return jnp.stack([unslab(hr), unslab(him)], axis=-1)[None]

    xs = x_sample.reshape(n_s, d_model)
    cos_s, sin_s = _rope_tables(n_s, PAST_LEN, 0)
    hn = _rmsnorm(xs, norm_mix[0], BF16)
    zs, w_in_b = _inproj(hn, w_in_even, 0, cos_s, sin_s, n_s, STEP_INPROJ_TN)
    attn_s = _step_attn(zs, [c[0] for c in caches])

    def slab(h):
        return h.reshape(n_s, ns, SLAB_S).transpose(1, 0, 2)

    y_s, hr_s, hi_s = _ssm_step(zs, u_col0, slab(state_ssm[0, ..., 0]), slab(state_ssm[0, ..., 1]), sp)
    gate_s, w_glu_b = _ssm_gate_step(y_s, w_glu, 0, b_glu[0], STEP_GATE_TN)
    xs, w_oa_b, w_os_b = _outproj(attn_s, gate_s, w_out_even, w_out_even, 0, xs, n_s, STEP_OUTPROJ_TN)
    xs, ffn0_s, wf0 = _ffn_step(xs, state_ffn[0].transpose(1, 0, 2), norm_ffn[0], w_up, w_down, 0,
                                ffn_dw[0], ffn_dw_b[0])
    hn = _rmsnorm(xs, norm_mix[1], BF16)
    us, w_pa_b, w_pg_b = _pw1(hn, w_pw1, w_pw1, 0, b_pw1[0], n_s, STEP_PW1_TN)
    cs = _conv31_step(us, state_conv[0].transpose(1, 0, 2), w_dw[0], b_dw[0], ln_g[0], ln_b[0])
    conv_s = jnp.concatenate([state_conv[0][:, 1:], us[:, None]], axis=1)
    xs, w_pw2_b = _mm_res(cs, w_pw2, 0, b_pw2[0], xs, n_s, STEP_PW2_TN, "conv_pw2_step")
    xs, ffn1_s, wf1 = _ffn_step(xs, state_ffn[1].transpose(1, 0, 2), norm_ffn[1], w_up, w_down, 1,
                                ffn_dw[1], ffn_dw_b[1])
    y_sample = _rmsnorm(xs, norm_final, F32).reshape(n_s, 1, d_model)

    a_s = []
    for g in range(N_DIL):
        kv = zs[:, g * QKV_W + D_A:(g + 1) * QKV_W].reshape(n_s, 1, 2, A_HEADS, HEAD_DIM)
        a_s.append(_cache_shift(caches[g][0], kv)[None])
    ssm_s = ssm_state(hr_s, hi_s, n_s)
    conv_s = conv_s[None]
    ffn_s = jnp.stack([ffn0_s, ffn1_s])

    mp = n_p * t_p
    tm = PROMPT_TM
    x = x_prompt.reshape(mp, d_model)
    cos, sin = _rope_tables(t_p, 0, 1)
    hn = _rmsnorm(x, norm_mix[0], BF16)
    z = _inproj(hn, w_in_b, None, cos, sin, tm, INPROJ_TN)
    outs, lses = [], []
    for g in range(N_DIL):
        o, lse = _band_attn(z, n_p, t_p, g)
        outs.append(o)
        lses.append(lse)
    attn = _combine(outs, lses)
    y_ssm, hlr, hli = _ssm_prompt(z, n_p, t_p, u_col0, sp)
    gate_p = _ssm_gate(y_ssm, w_glu_b, b_glu[0], GATE_TM)
    x = _outproj(attn, gate_p, w_oa_b, w_os_b, None, x, tm, OUTPROJ_TN)
    x, ffn0_p = _ffn_prompt(x, t_p, norm_ffn[0], wf0[0], ffn_dw[0], ffn_dw_b[0], wf0[1])
    hn = _rmsnorm(x, norm_mix[1], BF16)
    u = _pw1(hn, w_pa_b, w_pg_b, None, b_pw1[0], tm, PW1_TN)
    c = _conv31(u, t_p, w_dw[0], b_dw[0], ln_g[0], ln_b[0], CONV_TM)
    conv_p = u.reshape(n_p, t_p, -1)[None, :, t_p - (CONV_WIDTH - 1):]
    x = _mm_res(c, w_pw2_b, None, b_pw2[0], x, tm, PW2_TN, "conv_pw2")
    x, ffn1_p = _ffn_prompt(x, t_p, norm_ffn[1], wf1[0], ffn_dw[1], ffn_dw_b[1], wf1[1])
    y_prompt = _rmsnorm(x, norm_final, F32).reshape(n_p, t_p, d_model)

    a_p = []
    for g in range(N_DIL):
        a_p.append(_kv_rows(z, n_p, t_p, g, min(A_WINDOWS[g], PAST_LEN))[None])
    ssm_p = ssm_state(hlr, hli, n_p)
    ffn_p = jnp.stack([ffn0_p, ffn1_p])

    return (y_prompt, y_sample, a_p[0], a_s[0], a_p[1], a_s[1], a_p[2], a_s[2],
            ssm_p, ssm_s, conv_p, conv_s, ffn_p, ffn_s)
```

```python
import functools
import math

import jax
import jax.numpy as jnp
from jax import lax
from jax.experimental import pallas as pl
from jax.experimental.pallas import tpu as pltpu

F32 = jnp.float32
BF16 = jnp.bfloat16

HEAD_DIM = 128
A_HEADS = 16
D_A = A_HEADS * HEAD_DIM
A_WINDOWS = (128, 512, 2048)
A_DILATIONS = (1, 4, 16)
BAND = 128
N_DIL = 3
QKV_W = 3 * D_A
ROPE_THETA = 10000.0
PAST_LEN = 16384
SSM_GROUP = 16
SSM_STATE = 64
SLAB_G = 8
SLAB_S = SLAB_G * SSM_STATE
CHUNK = 8
CONV_WIDTH = 31
CONV_HALO = 32
RMS_EPS = 1e-6
LN_EPS = 1e-5
NEG_BIG = -1e30
VMEM_LIMIT_V7X = 56 * 1024 * 1024

PROMPT_TM = 1024
INPROJ_TN = 1024
OUTPROJ_TN = 1024
PW1_TN = 512
PW2_TN = 1024
FFN_TN = 256
FFN_PART_ROWS = 1024
FFN_DOWN_TM = 512
FFN_DOWN_TN = 512
GATE_TM = 512
CONV_TM = 256
NORM_TM = 512
COMBINE_TM = 512
STEP_INPROJ_TN = 512
STEP_GATE_TN = 512
STEP_OUTPROJ_TN = 512
STEP_PW1_TN = 256
STEP_PW2_TN = 512
STEP_FFN_DOWN_TN = 256


def _cp(*sem):
    return pltpu.CompilerParams(dimension_semantics=sem, vmem_limit_bytes=VMEM_LIMIT_V7X)


def _dot(a, b, **kw):
    return jnp.dot(a, b, preferred_element_type=F32, **kw)


def _dot_nt(a, b, **kw):
    return lax.dot_general(a, b, (((1,), (1,)), ((), ())), preferred_element_type=F32, **kw)


def _sigmoid(x):
    return 1.0 / (1.0 + jnp.exp(-x))


def _rmsnorm_kernel(x_ref, g_ref, o_ref):
    x = x_ref[...]
    ms = jnp.mean(x * x, axis=-1, keepdims=True)
    o_ref[...] = (x * lax.rsqrt(ms + RMS_EPS) * g_ref[...]).astype(o_ref.dtype)


def _rmsnorm(x, g, out_dtype):
    m, d = x.shape
    tm = min(m, NORM_TM)
    return pl.pallas_call(
        _rmsnorm_kernel,
        grid=(m // tm,),
        in_specs=[pl.BlockSpec((tm, d), lambda i: (i, 0)),
                  pl.BlockSpec((1, d), lambda i: (0, 0))],
        out_specs=pl.BlockSpec((tm, d), lambda i: (i, 0)),
        out_shape=jax.ShapeDtypeStruct((m, d), out_dtype),
        compiler_params=_cp("parallel"),
        name="rmsnorm",
    )(x, g.reshape(1, d))


def _rope_table_kernel(invf_ref, cos_ref, sin_ref, *, pos0, step):
    rows = cos_ref.shape[0]
    row = lax.broadcasted_iota(jnp.int32, (rows, HEAD_DIM), 0)
    lane = lax.broadcasted_iota(jnp.int32, (rows, HEAD_DIM), 1)
    pos = (row * step + pos0).astype(F32)
    ang = pos * invf_ref[...]
    cos_ref[...] = jnp.cos(ang)
    s = jnp.sin(ang)
    sin_ref[...] = jnp.where(lane < HEAD_DIM // 2, -s, s)


def _rope_tables(rows, pos0, step):
    half = HEAD_DIM // 2
    inv_freq = ROPE_THETA ** (-jnp.arange(half, dtype=F32) / half)
    invf = jnp.concatenate([inv_freq, inv_freq]).reshape(1, HEAD_DIM)
    shp = jax.ShapeDtypeStruct((rows, HEAD_DIM), F32)
    return pl.pallas_call(
        functools.partial(_rope_table_kernel, pos0=pos0, step=step),
        out_shape=(shp, shp),
        name="rope_tables",
    )(invf)


def _wspec(w, layer, kb, tn, kblk, jmap):
    if layer is None:
        return pl.BlockSpec((kb, tn), lambda *g: (kblk, jmap(*g)))
    return pl.BlockSpec((None, kb, tn), lambda *g: (layer, kblk, jmap(*g)))


def _use_w(w_ref, wb_ref, cols=slice(None)):
    w = w_ref[:, cols]
    if wb_ref is None:
        return w
    wb = w.astype(BF16)
    wb_ref[:, cols] = wb
    return wb


def _inproj_kernel(x_ref, w_ref, cos_ref, sin_ref, o_ref, *wb, tn, part):
    wb_ref = wb[0] if wb else None
    seg = (pl.program_id(1) * tn) // D_A
    is_rope = jnp.logical_and(seg < 3 * N_DIL, seg % 3 != 2)
    def body(rope):
        x = x_ref[...]
        for c0 in range(0, tn, part):
            acc = _dot(x, _use_w(w_ref, wb_ref, slice(c0, c0 + part)))
            for h in range(part // HEAD_DIM):
                xh = acc[:, h * HEAD_DIM:(h + 1) * HEAD_DIM]
                if rope:
                    xh = xh * cos_ref[...] + pltpu.roll(xh, HEAD_DIM // 2, 1) * sin_ref[...]
                o_ref[:, c0 + h * HEAD_DIM:c0 + (h + 1) * HEAD_DIM] = xh

    @pl.when(is_rope)
    def _():
        body(True)

    @pl.when(jnp.logical_not(is_rope))
    def _():
        body(False)


def _inproj(hn, w, layer, cos, sin, tm, tn):
    m, k = hn.shape
    n = w.shape[-1]
    ct = cos.shape[0] // tm
    emit = layer is not None
    out_specs = [pl.BlockSpec((tm, tn), lambda i, j: (i, j))]
    out_shape = [jax.ShapeDtypeStruct((m, n), F32)]
    if emit:
        out_specs.append(pl.BlockSpec((k, tn), lambda i, j: (0, j)))
        out_shape.append(jax.ShapeDtypeStruct((k, n), BF16))
    res = pl.pallas_call(
        functools.partial(_inproj_kernel, tn=tn, part=min(tn, 256)),
        grid=(m // tm, n // tn),
        in_specs=[pl.BlockSpec((tm, k), lambda i, j: (i, 0)),
                  _wspec(w, layer, k, tn, 0, lambda i, j: j),
                  pl.BlockSpec((tm, HEAD_DIM), lambda i, j: (i % ct, 0)),
                  pl.BlockSpec((tm, HEAD_DIM), lambda i, j: (i % ct, 0))],
        out_specs=out_specs,
        out_shape=out_shape,
        compiler_params=_cp("parallel", "arbitrary"),
        name="inproj_rope",
    )(hn, w, cos, sin)
    return res if emit else res[0]


def _band_attn_kernel(q_ref, kp_ref, kc_ref, vp_ref, vc_ref, o_ref, lse_ref):
    has_prev = pl.program_id(2) > 0
    qi = lax.broadcasted_iota(jnp.int32, (BAND, 2 * BAND), 0)
    ki = lax.broadcasted_iota(jnp.int32, (BAND, 2 * BAND), 1)
    valid = jnp.logical_and(jnp.logical_and(ki >= qi, ki <= qi + BAND),
                            jnp.logical_or(ki >= BAND, has_prev))
    lane = lax.broadcasted_iota(jnp.int32, (BAND, HEAD_DIM), 1)
    lse_tile = jnp.zeros((BAND, HEAD_DIM), F32)
    scale = HEAD_DIM ** -0.5
    for h in range(A_HEADS):
        sl = slice(h * HEAD_DIM, (h + 1) * HEAD_DIM)
        q = q_ref[:, sl].astype(BF16)
        k = jnp.concatenate([kp_ref[:, sl], kc_ref[:, sl]], axis=0).astype(BF16)
        v = jnp.concatenate([vp_ref[:, sl], vc_ref[:, sl]], axis=0).astype(BF16)
        s = _dot_nt(q, k) * scale
        s = jnp.where(valid, s, NEG_BIG)
        m = jnp.max(s, axis=-1, keepdims=True)
        p = jnp.exp(s - m)
        l = jnp.sum(p, axis=-1, keepdims=True)
        o_ref[:, sl] = _dot(p.astype(BF16), v) / l
        lse_tile = jnp.where(lane == h, m + jnp.log(l), lse_tile)
    lse_ref[...] = lse_tile


def _band_attn_dil_kernel(*refs, d, hpb, use_prev):
    n_in = 5 if use_prev else 3
    ins = [refs[i * hpb:(i + 1) * hpb] for i in range(n_in)]
    o_ref, lse_ref, os_, ls = refs[n_in * hpb:]
    if use_prev:
        q_refs, kp_refs, kc_refs, vp_refs, vc_refs = ins
    else:
        q_refs, kc_refs, vc_refs = ins
    hb = pl.program_id(2)
    has_prev = pl.program_id(1) > 0

    @pl.when(hb == 0)
    def _():
        lse_ref[...] = jnp.zeros(lse_ref.shape, F32)

    nk = 2 * BAND if use_prev else BAND
    qi = lax.broadcasted_iota(jnp.int32, (1, BAND, nk), 1)
    ki = lax.broadcasted_iota(jnp.int32, (1, BAND, nk), 2)
    if use_prev:
        valid = jnp.logical_and(jnp.logical_and(ki >= qi, ki <= qi + BAND),
                                jnp.logical_or(ki >= BAND, has_prev))
    else:
        valid = ki <= qi
    scale = HEAD_DIM ** -0.5

    def residues(ref):
        return [ref[pl.ds(r, BAND, stride=d), :] for r in range(d)]

    for h in range(hpb):
        q = jnp.stack(residues(q_refs[h])).astype(BF16)
        if use_prev:
            k = jnp.stack([jnp.concatenate([a, b], axis=0)
                           for a, b in zip(residues(kp_refs[h]), residues(kc_refs[h]))]).astype(BF16)
            v = jnp.stack([jnp.concatenate([a, b], axis=0)
                           for a, b in zip(residues(vp_refs[h]), residues(vc_refs[h]))]).astype(BF16)
        else:
            k = jnp.stack(residues(kc_refs[h])).astype(BF16)
            v = jnp.stack(residues(vc_refs[h])).astype(BF16)
        s = jnp.einsum('rqd,rkd->rqk', q, k, preferred_element_type=F32) * scale
        s = jnp.where(valid, s, NEG_BIG)
        m = jnp.max(s, axis=-1, keepdims=True)
        p = jnp.exp(s - m)
        l = jnp.sum(p, axis=-1, keepdims=True)
        o = jnp.einsum('rqk,rkd->rqd', p.astype(BF16), v, preferred_element_type=F32) / l
        lse_h = m + jnp.log(l)
        for r in range(d):
            os_[h, pl.ds(r, BAND, stride=d), :] = o[r]
            ls[h, pl.ds(r, BAND, stride=d), :] = jnp.broadcast_to(lse_h[r], (BAND, HEAD_DIM))
    lse = lse_ref[...]
    lane = lax.broadcasted_iota(jnp.int32, lse.shape, 1)
    for h in range(hpb):
        o_ref[:, h * HEAD_DIM:(h + 1) * HEAD_DIM] = os_[h]
        lse = jnp.where(lane == hb * hpb + h, ls[h], lse)
    lse_ref[...] = lse


def _band_attn(z, n_seq, t, g):
    d = A_DILATIONS[g]
    width = z.shape[1]
    zv = z.reshape(n_seq, t, width)
    if d == 1:
        qc = 3 * g

        def cur(off):
            return pl.BlockSpec((None, BAND, D_A), lambda n, r, b: (n, b, qc + off))

        def prev(off):
            return pl.BlockSpec((None, BAND, D_A), lambda n, r, b: (n, jnp.maximum(b - 1, 0), qc + off))

        o, lse = pl.pallas_call(
            _band_attn_kernel,
            grid=(n_seq, 1, t // BAND),
            in_specs=[cur(0), prev(1), cur(1), prev(2), cur(2)],
            out_specs=[pl.BlockSpec((None, BAND, D_A), lambda n, r, b: (n, b, 0)),
                       pl.BlockSpec((None, BAND, HEAD_DIM), lambda n, r, b: (n, b, 0))],
            out_shape=[jax.ShapeDtypeStruct((n_seq, t, D_A), F32),
                       jax.ShapeDtypeStruct((n_seq, t, HEAD_DIM), F32)],
            compiler_params=_cp("parallel", "parallel", "arbitrary"),
            name="band_attn_d1",
        )(zv, zv, zv, zv, zv)
        return o.reshape(n_seq * t, D_A), lse.reshape(n_seq * t, HEAD_DIM)

    rows = BAND * d
    nb = t // rows
    use_prev = nb > 1
    hpb = 8 if d <= 4 else 2
    cols = hpb * HEAD_DIM
    cpq = D_A // cols

    def head_blocks(off, back):
        c0 = (3 * g + off) * A_HEADS
        return [pl.BlockSpec((None, rows, HEAD_DIM),
                             lambda n, b, hb, h=h: (n, jnp.maximum(b - back, 0), c0 + hb * hpb + h))
                for h in range(hpb)]

    if use_prev:
        in_specs = (head_blocks(0, 0) + head_blocks(1, 1) + head_blocks(1, 0)
                    + head_blocks(2, 1) + head_blocks(2, 0))
    else:
        in_specs = head_blocks(0, 0) + head_blocks(1, 0) + head_blocks(2, 0)
    o, lse = pl.pallas_call(
        functools.partial(_band_attn_dil_kernel, d=d, hpb=hpb, use_prev=use_prev),
        grid=(n_seq, nb, cpq),
        in_specs=in_specs,
        out_specs=[pl.BlockSpec((None, rows, cols), lambda n, b, hb: (n, b, hb)),
                   pl.BlockSpec((None, rows, HEAD_DIM), lambda n, b, hb: (n, b, 0))],
        out_shape=[jax.ShapeDtypeStruct((n_seq, t, D_A), F32),
                   jax.ShapeDtypeStruct((n_seq, t, HEAD_DIM), F32)],
        scratch_shapes=[pltpu.VMEM((hpb, rows, HEAD_DIM), F32),
                        pltpu.VMEM((hpb, rows, HEAD_DIM), F32)],
        compiler_params=_cp("parallel", "parallel", "arbitrary"),
        name=f"band_attn_d{d}",
    )(*([zv] * len(in_specs)))
    return o.reshape(n_seq * t, D_A), lse.reshape(n_seq * t, HEAD_DIM)


def _combine_kernel(o0_ref, o1_ref, o2_ref, l0_ref, l1_ref, l2_ref, out_ref):
    a0, a1, a2 = l0_ref[...], l1_ref[...], l2_ref[...]
    m = jnp.maximum(jnp.maximum(a0, a1), a2)
    e0, e1, e2 = jnp.exp(a0 - m), jnp.exp(a1 - m), jnp.exp(a2 - m)
    inv = 1.0 / (e0 + e1 + e2)
    w0, w1, w2 = e0 * inv, e1 * inv, e2 * inv
    for h in range(A_HEADS):
        sl = slice(h * HEAD_DIM, (h + 1) * HEAD_DIM)
        out_ref[:, sl] = (w0[:, h:h + 1] * o0_ref[:, sl] + w1[:, h:h + 1] * o1_ref[:, sl]
                          + w2[:, h:h + 1] * o2_ref[:, sl]).astype(out_ref.dtype)


def _combine(os_, lses):
    m = os_[0].shape[0]
    tm = COMBINE_TM
    ob = pl.BlockSpec((tm, D_A), lambda i: (i, 0))
    lb = pl.BlockSpec((tm, HEAD_DIM), lambda i: (i, 0))
    return pl.pallas_call(
        _combine_kernel,
        grid=(m // tm,),
        in_specs=[ob, ob, ob, lb, lb, lb],
        out_specs=ob,
        out_shape=jax.ShapeDtypeStruct((m, D_A), BF16),
        compiler_params=_cp("parallel"),
        name="attn_combine",
    )(*os_, *lses)


def _step_attn_kernel(z_ref, c0_ref, c1_ref, c2_ref, o_ref):
    scale = HEAD_DIM ** -0.5
    ms, ls, accs = [], [], []
    for g, c_ref in enumerate((c0_ref, c1_ref, c2_ref)):
        base = 3 * g * A_HEADS
        q = z_ref[base:base + A_HEADS, :]
        kn = z_ref[base + A_HEADS:base + 2 * A_HEADS, :]
        vn = z_ref[base + 2 * A_HEADS:base + 3 * A_HEADS, :]
        s = jnp.sum(c_ref[:, 0] * q[None], axis=-1, keepdims=True) * scale
        s_new = jnp.sum(kn * q, axis=-1, keepdims=True) * scale
        m = jnp.maximum(jnp.max(s, axis=0), s_new)
        p = jnp.exp(s - m[None])
        p_new = jnp.exp(s_new - m)
        ls.append(jnp.sum(p, axis=0) + p_new)
        accs.append(jnp.sum(p * c_ref[:, 1], axis=0) + p_new * vn)
        ms.append(m)
    mm = jnp.maximum(jnp.maximum(ms[0], ms[1]), ms[2])
    es = [jnp.exp(m - mm) for m in ms]
    den = es[0] * ls[0] + es[1] * ls[1] + es[2] * ls[2]
    num = es[0] * accs[0] + es[1] * accs[1] + es[2] * accs[2]
    o_ref[...] = (num / den).astype(o_ref.dtype)


def _step_attn(zs, caches):
    n, width = zs.shape
    cviews, cspecs = [], []
    for g, c in enumerate(caches):
        d = A_DILATIONS[g]
        cviews.append(c.reshape(n, c.shape[1] // d, d, 2, A_HEADS, HEAD_DIM))
        cspecs.append(pl.BlockSpec((None, BAND, None, 2, A_HEADS, HEAD_DIM),
                                   lambda i: (i, 0, 0, 0, 0, 0)))
    zrows = width // HEAD_DIM
    out = pl.pallas_call(
        _step_attn_kernel,
        grid=(n,),
        in_specs=[pl.BlockSpec((None, zrows, HEAD_DIM), lambda i: (i, 0, 0))] + cspecs,
        out_specs=pl.BlockSpec((None, A_HEADS, HEAD_DIM), lambda i: (i, 0, 0)),
        out_shape=jax.ShapeDtypeStruct((n, A_HEADS, HEAD_DIM), BF16),
        compiler_params=_cp("parallel"),
        name="step_attn",
    )(zs.reshape(n, zrows, HEAD_DIM), *cviews)
    return out.reshape(n, D_A)


def _cache_shift_kernel(c_ref, nxt_ref, new_ref, o_ref, *, lb):
    last = pl.program_id(1) == pl.num_programs(1) - 1
    o_ref[0:lb - 1] = c_ref[1:lb]

    @pl.when(last)
    def _():
        o_ref[lb - 1] = new_ref[0]

    @pl.when(jnp.logical_not(last))
    def _():
        o_ref[lb - 1] = nxt_ref[0]


def _cache_shift(cache, new_kv):
    n, length = cache.shape[:2]
    lb = min(length, 4 * BAND)
    nb = length // lb
    tail = cache.shape[2:]
    zeros = (0,) * len(tail)
    return pl.pallas_call(
        functools.partial(_cache_shift_kernel, lb=lb),
        grid=(n, nb),
        in_specs=[pl.BlockSpec((None, lb) + tail, lambda i, b: (i, b) + zeros),
                  pl.BlockSpec((None, 1) + tail, lambda i, b: (i, jnp.minimum(b + 1, nb - 1) * lb) + zeros),
                  pl.BlockSpec((None, 1) + tail, lambda i, b: (i, 0) + zeros)],
        out_specs=pl.BlockSpec((None, lb) + tail, lambda i, b: (i, b) + zeros),
        out_shape=jax.ShapeDtypeStruct(cache.shape, cache.dtype),
        compiler_params=_cp("parallel", "parallel"),
        name="cache_shift",
    )(cache, cache, new_kv)


def _kv_rows_kernel(k_ref, v_ref, o_ref, scr, *, tm):
    per = 2 * A_HEADS
    pad = per + 8
    for kv, x_ref in enumerate((k_ref, v_ref)):
        for h in range(A_HEADS):
            scr[pl.ds(kv * A_HEADS + h, tm, stride=pad), :] = x_ref[:, h * HEAD_DIM:(h + 1) * HEAD_DIM]

    def body(p, carry):
        o_ref[pl.ds(pl.multiple_of(p * per, 8), per), :] = scr[pl.ds(pl.multiple_of(p * pad, 8), per), :]
        return carry

    lax.fori_loop(0, tm, body, 0, unroll=8)


def _kv_rows(z, n_seq, t, g, length):
    width = z.shape[1]
    zv = z.reshape(n_seq, t, width)
    tm = min(length, 2 * BAND)
    r0 = (t - length) // tm
    c0 = 3 * g + 1
    per = 2 * A_HEADS
    out = pl.pallas_call(
        functools.partial(_kv_rows_kernel, tm=tm),
        grid=(n_seq, length // tm),
        in_specs=[pl.BlockSpec((None, tm, D_A), lambda n, i: (n, r0 + i, c0)),
                  pl.BlockSpec((None, tm, D_A), lambda n, i: (n, r0 + i, c0 + 1))],
        out_specs=pl.BlockSpec((None, tm * per, HEAD_DIM), lambda n, i: (n, i, 0)),
        out_shape=jax.ShapeDtypeStruct((n_seq, length * per, HEAD_DIM), F32),
        scratch_shapes=[pltpu.VMEM((tm * (per + 8), HEAD_DIM), F32)],
        compiler_params=_cp("parallel", "parallel"),
        name="kv_rows",
    )(zv, zv)
    return out.reshape(n_seq, length, 2, A_HEADS, HEAD_DIM)


def _ssm_discretise(lre, lim, ldt):
    dt = jnp.exp(ldt)
    mag = jnp.exp(lre * dt)
    ar = mag * jnp.cos(lim * dt)
    ai = mag * jnp.sin(lim * dt)
    den = lre * lre + lim * lim
    zr = ((ar - 1.0) * lre + ai * lim) / den
    zi = (ai * lre - (ar - 1.0) * lim) / den
    return ar, ai, zr, zi


def _slab_mask():
    r = lax.broadcasted_iota(jnp.int32, (SLAB_G * SSM_GROUP, SLAB_S), 0) // SSM_GROUP
    c = lax.broadcasted_iota(jnp.int32, (SLAB_G * SSM_GROUP, SLAB_S), 1) // SSM_STATE
    return (r == c).astype(F32)


def _block_diag(x16, mask):
    return jnp.concatenate([x16] * SLAB_G, axis=0) * mask


def _ssm_prompt_kernel(u_ref, lre_ref, lim_ref, ldt_ref, bre_ref, bim_ref, cre_ref, cim_ref, d_ref,
                       y_ref, hlr_ref, hli_ref,
                       wer_ref, wei_ref, tz_ref, bdr_ref, bdi_ref, er_ref, ei_ref, *, n_seq, n_chunk):
    def split(a):
        ah = a.astype(BF16)
        return ah, (a - ah.astype(F32)).astype(BF16)

    def dot_nt_x3(a, b_split):
        ah, al = split(a)
        bh, bl = b_split
        return _dot_nt(ah, bh) + _dot_nt(ah, bl) + _dot_nt(al, bh)

    rows = n_seq * n_chunk

    def step_rows(j):
        return pl.ds(j, rows, stride=CHUNK)

    ar, ai, zr, zi = _ssm_discretise(lre_ref[...], lim_ref[...], ldt_ref[...])
    mask = _slab_mask()
    bre, bim = bre_ref[...], bim_ref[...]
    cre, cim = cre_ref[...], cim_ref[...]
    cre_bd = split(_block_diag(cre, mask))
    cim_bd = split(_block_diag(cim, mask))
    lanes = SLAB_G * SSM_GROUP
    tz_ref[...] = jnp.zeros(tz_ref.shape, BF16)
    pr, pim = jnp.ones_like(ar), jnp.zeros_like(ar)
    for m in range(CHUNK + 1):
        if m < CHUNK:
            wr = zr * pr - zi * pim
            wi = zr * pim + zi * pr
            we_r = _block_diag(bre * wr - bim * wi, mask)
            we_i = _block_diag(bre * wi + bim * wr, mask)
            j = CHUNK - 1 - m
            wer_ref[j * lanes:(j + 1) * lanes, :] = we_r.astype(BF16)
            wei_ref[j * lanes:(j + 1) * lanes, :] = we_i.astype(BF16)
            tz = (dot_nt_x3(we_r, cre_bd) - dot_nt_x3(we_i, cim_bd)).astype(BF16)
            for j in range(CHUNK - m):
                tz_ref[j * lanes:(j + 1) * lanes, (j + m) * lanes:(j + m + 1) * lanes] = tz
        if m >= 1:
            bdr_ref[(m - 1) * lanes:m * lanes, :] = _block_diag(cre * pr - cim * pim, mask).astype(BF16)
            bdi_ref[(m - 1) * lanes:m * lanes, :] = _block_diag(-(cre * pim + cim * pr), mask).astype(BF16)
        if m == CHUNK:
            a8r, a8i = pr, pim
        pr, pim = pr * ar - pim * ai, pr * ai + pim * ar

    xcat = jnp.concatenate([u_ref[step_rows(j), :].astype(BF16) for j in range(CHUNK)], axis=1)

    er_ref[...] = _dot(xcat, wer_ref[...])
    ei_ref[...] = _dot(xcat, wei_ref[...])

    def body(k, carry):
        out = []
        for n in range(n_seq):
            hr, him = carry[2 * n], carry[2 * n + 1]
            row = n * n_chunk + k
            e_r = er_ref[pl.ds(row, 1), :]
            e_i = ei_ref[pl.ds(row, 1), :]
            er_ref[pl.ds(row, 1), :] = hr
            ei_ref[pl.ds(row, 1), :] = him
            out.append(a8r * hr - a8i * him + e_r)
            out.append(a8r * him + a8i * hr + e_i)
        return tuple(out)

    zero = jnp.zeros((1, SLAB_S), F32)
    fin = lax.fori_loop(0, n_chunk, body, (zero,) * (2 * n_seq), unroll=2)
    for n in range(n_seq):
        hlr_ref[pl.ds(n, 1), :] = fin[2 * n]
        hli_ref[pl.ds(n, 1), :] = fin[2 * n + 1]

    y = (_dot(xcat, tz_ref[...]) + _dot_nt(er_ref[...].astype(BF16), bdr_ref[...])
         + _dot_nt(ei_ref[...].astype(BF16), bdi_ref[...]))
    for jp in range(CHUNK):
        y_ref[step_rows(jp), :] = y[:, jp * lanes:(jp + 1) * lanes] + d_ref[...] * u_ref[step_rows(jp), :]


def _ssm_params(a_re, a_im, b_re, b_im, c_re, c_im, d_skip, log_dt):
    g = a_re.shape[0]
    ns = g // SLAB_G

    def row(x):
        return x.astype(F32).reshape(ns, 1, SLAB_S)

    def chan_rows(x):
        return x.astype(F32).reshape(ns, SLAB_G, SSM_GROUP, SSM_STATE).transpose(0, 2, 1, 3) \
            .reshape(ns, SSM_GROUP, SLAB_S)

    ldt = jnp.broadcast_to(log_dt.astype(F32)[:, None], (g, SSM_STATE))
    return (row(a_re), row(a_im), row(ldt),
            chan_rows(b_re.transpose(0, 2, 1)), chan_rows(b_im.transpose(0, 2, 1)),
            chan_rows(c_re), chan_rows(c_im),
            d_skip.astype(F32).reshape(ns, 1, SLAB_G * SSM_GROUP))


def _ssm_prompt(z, n_seq, t, u_col0, sp):
    m, width = z.shape
    rows = m // CHUNK
    n_chunk = t // CHUNK
    lanes = SLAB_G * SSM_GROUP
    ns = sp[0].shape[0]
    ub = u_col0 // lanes

    def pspec(r, c):
        return pl.BlockSpec((None, r, c), lambda s: (s, 0, 0))

    y, hlr, hli = pl.pallas_call(
        functools.partial(_ssm_prompt_kernel, n_seq=n_seq, n_chunk=n_chunk),
        grid=(ns,),
        in_specs=[pl.BlockSpec((m, lanes), lambda s: (0, ub + s)),
                  pspec(1, SLAB_S), pspec(1, SLAB_S), pspec(1, SLAB_S),
                  pspec(SSM_GROUP, SLAB_S), pspec(SSM_GROUP, SLAB_S),
                  pspec(SSM_GROUP, SLAB_S), pspec(SSM_GROUP, SLAB_S), pspec(1, lanes)],
        out_specs=[pl.BlockSpec((m, lanes), lambda s: (0, s)),
                   pl.BlockSpec((None, n_seq, SLAB_S), lambda s: (s, 0, 0)),
                   pl.BlockSpec((None, n_seq, SLAB_S), lambda s: (s, 0, 0))],
        out_shape=[jax.ShapeDtypeStruct((m, ns * lanes), F32),
                   jax.ShapeDtypeStruct((ns, n_seq, SLAB_S), F32),
                   jax.ShapeDtypeStruct((ns, n_seq, SLAB_S), F32)],
        scratch_shapes=[pltpu.VMEM((CHUNK * lanes, SLAB_S), BF16),
                        pltpu.VMEM((CHUNK * lanes, SLAB_S), BF16),
                        pltpu.VMEM((CHUNK * lanes, CHUNK * lanes), BF16),
                        pltpu.VMEM((CHUNK * lanes, SLAB_S), BF16),
                        pltpu.VMEM((CHUNK * lanes, SLAB_S), BF16),
                        pltpu.VMEM((rows, SLAB_S), F32),
                        pltpu.VMEM((rows, SLAB_S), F32)],
        compiler_params=_cp("parallel"),
        name="ssm_prompt",
    )(z, *sp)
    return y, hlr, hli


def _ssm_step_kernel(u_ref, h0r_ref, h0i_ref, lre_ref, lim_ref, ldt_ref, bre_ref, bim_ref,
                     cre_ref, cim_ref, d_ref, y_ref, hr_ref, hi_ref):
    ar, ai, zr, zi = _ssm_discretise(lre_ref[...], lim_ref[...], ldt_ref[...])
    mask = _slab_mask()
    bre, bim = bre_ref[...], bim_ref[...]
    we_r = _block_diag(bre * zr - bim * zi, mask).astype(BF16)
    we_i = _block_diag(bre * zi + bim * zr, mask).astype(BF16)
    u = u_ref[...]
    ub = u.astype(BF16)
    h0r, h0i = h0r_ref[...], h0i_ref[...]
    hr = ar * h0r - ai * h0i + _dot(ub, we_r)
    him = ar * h0i + ai * h0r + _dot(ub, we_i)
    hr_ref[...] = hr
    hi_ref[...] = him
    cr_bd = _block_diag(cre_ref[...], mask).astype(BF16)
    ci_bd = _block_diag(cim_ref[...], mask).astype(BF16)
    y_ref[...] = (_dot_nt(hr.astype(BF16), cr_bd) - _dot_nt(him.astype(BF16), ci_bd)
                  + d_ref[...] * u)


def _ssm_step(zs, u_col0, h0r, h0i, sp):
    n = zs.shape[0]
    lanes = SLAB_G * SSM_GROUP
    ns = sp[0].shape[0]
    ub = u_col0 // lanes

    def pspec(r, c):
        return pl.BlockSpec((None, r, c), lambda s: (s, 0, 0))

    hspec = pl.BlockSpec((None, n, SLAB_S), lambda s: (s, 0, 0))
    return pl.pallas_call(
        _ssm_step_kernel,
        grid=(ns,),
        in_specs=[pl.BlockSpec((n, lanes), lambda s: (0, ub + s)), hspec, hspec,
                  pspec(1, SLAB_S), pspec(1, SLAB_S), pspec(1, SLAB_S),
                  pspec(SSM_GROUP, SLAB_S), pspec(SSM_GROUP, SLAB_S),
                  pspec(SSM_GROUP, SLAB_S), pspec(SSM_GROUP, SLAB_S), pspec(1, lanes)],
        out_specs=[pl.BlockSpec((n, lanes), lambda s: (0, s)), hspec, hspec],
        out_shape=[jax.ShapeDtypeStruct((n, ns * lanes), F32),
                   jax.ShapeDtypeStruct((ns, n, SLAB_S), F32),
                   jax.ShapeDtypeStruct((ns, n, SLAB_S), F32)],
        compiler_params=_cp("parallel"),
        name="ssm_step",
    )(zs, h0r, h0i, *sp)


def _gelu_tanh(x):
    return 0.5 * x * (1.0 + jnp.tanh(math.sqrt(2.0 / math.pi) * (x + 0.044715 * (x * x * x))))


def _ssm_gate_kernel(y_ref, w_ref, b_ref, o_ref):
    yg = _gelu_tanh(y_ref[...])
    gate = _dot(yg.astype(BF16), w_ref[...]) + b_ref[...]
    o_ref[...] = (yg * _sigmoid(gate)).astype(o_ref.dtype)


def _ssm_gate(y, w, b, tm):
    m, d = y.shape
    return pl.pallas_call(
        _ssm_gate_kernel,
        grid=(m // tm,),
        in_specs=[pl.BlockSpec((tm, d), lambda i: (i, 0)),
                  pl.BlockSpec((d, d), lambda i: (0, 0)),
                  pl.BlockSpec((1, d), lambda i: (0, 0))],
        out_specs=pl.BlockSpec((tm, d), lambda i: (i, 0)),
        out_shape=jax.ShapeDtypeStruct((m, d), BF16),
        compiler_params=_cp("parallel"),
        name="ssm_gate",
    )(y, w, b.reshape(1, d).astype(F32))


def _ssm_gate_step_kernel(y_ref, yc_ref, w_ref, b_ref, o_ref, wb_ref):
    yg = _gelu_tanh(y_ref[...])
    gate = _dot(yg.astype(BF16), _use_w(w_ref, wb_ref)) + b_ref[...]
    o_ref[...] = (_gelu_tanh(yc_ref[...]) * _sigmoid(gate)).astype(o_ref.dtype)


def _ssm_gate_step(y, w, layer, b, tn):
    m, d = y.shape
    return pl.pallas_call(
        _ssm_gate_step_kernel,
        grid=(d // tn,),
        in_specs=[pl.BlockSpec((m, d), lambda j: (0, 0)),
                  pl.BlockSpec((m, tn), lambda j: (0, j)),
                  _wspec(w, layer, d, tn, 0, lambda j: j),
                  pl.BlockSpec((1, tn), lambda j: (0, j))],
        out_specs=[pl.BlockSpec((m, tn), lambda j: (0, j)),
                   pl.BlockSpec((d, tn), lambda j: (0, j))],
        out_shape=[jax.ShapeDtypeStruct((m, d), BF16), jax.ShapeDtypeStruct((d, d), BF16)],
        compiler_params=_cp("arbitrary"),
        name="ssm_gate_step",
    )(y, y, w, b.reshape(1, d).astype(F32))


def _outproj_kernel(a_ref, s_ref, wa_ref, ws_ref, x_ref, o_ref, *wb):
    wab_ref, wsb_ref = wb if wb else (None, None)
    o_ref[...] = (x_ref[...] + _dot(a_ref[...], _use_w(wa_ref, wab_ref))
                  + _dot(s_ref[...], _use_w(ws_ref, wsb_ref)))


def _outproj(attn, ssm, wa, ws, layer, x, tm, tn):
    m, ka = attn.shape
    n = wa.shape[-1]
    emit = layer is not None
    out_specs = [pl.BlockSpec((tm, tn), lambda i, j: (i, j))]
    out_shape = [jax.ShapeDtypeStruct((m, n), F32)]
    if emit:
        out_specs += [pl.BlockSpec((ka, tn), lambda i, j: (0, j))] * 2
        out_shape += [jax.ShapeDtypeStruct((ka, n), BF16)] * 2
    res = pl.pallas_call(
        _outproj_kernel,
        grid=(m // tm, n // tn),
        in_specs=[pl.BlockSpec((tm, ka), lambda i, j: (i, 0)),
                  pl.BlockSpec((tm, ka), lambda i, j: (i, 0)),
                  _wspec(wa, layer, ka, tn, 0, lambda i, j: j),
                  _wspec(ws, layer, ka, tn, 1 if emit else 0, lambda i, j: j),
                  pl.BlockSpec((tm, tn), lambda i, j: (i, j))],
        out_specs=out_specs,
        out_shape=out_shape,
        compiler_params=_cp("parallel", "arbitrary"),
        name="outproj",
    )(attn, ssm, wa, ws, x)
    return res if emit else res[0]


def _mm_res_kernel(a_ref, w_ref, b_ref, x_ref, o_ref, *wb):
    wb_ref = wb[0] if wb else None
    o_ref[...] = x_ref[...] + (_dot(a_ref[...], _use_w(w_ref, wb_ref)) + b_ref[...])


def _mm_res(a, w, layer, b, x, tm, tn, name):
    m, k = a.shape
    n = w.shape[-1]
    emit = layer is not None
    out_specs = [pl.BlockSpec((tm, tn), lambda i, j: (i, j))]
    out_shape = [jax.ShapeDtypeStruct((m, n), F32)]
    if emit:
        out_specs.append(pl.BlockSpec((k, tn), lambda i, j: (0, j)))
        out_shape.append(jax.ShapeDtypeStruct((k, n), BF16))
    res = pl.pallas_call(
        _mm_res_kernel,
        grid=(m // tm, n // tn),
        in_specs=[pl.BlockSpec((tm, k), lambda i, j: (i, 0)),
                  _wspec(w, layer, k, tn, 0, lambda i, j: j),
                  pl.BlockSpec((1, tn), lambda i, j: (0, j)),
                  pl.BlockSpec((tm, tn), lambda i, j: (i, j))],
        out_specs=out_specs,
        out_shape=out_shape,
        compiler_params=_cp("parallel", "arbitrary"),
        name=name,
    )(a, w, b.reshape(1, n).astype(F32), x)
    return res if emit else res[0]


def _pw1_kernel(x_ref, wa_ref, wg_ref, ba_ref, bg_ref, o_ref, *wb):
    wab_ref, wgb_ref = wb if wb else (None, None)
    x = x_ref[...]
    a = _dot(x, _use_w(wa_ref, wab_ref)) + ba_ref[...]
    gate = _dot(x, _use_w(wg_ref, wgb_ref)) + bg_ref[...]
    o_ref[...] = a * _sigmoid(gate)


def _pw1(hn, wa, wg, layer, b, tm, tn):
    m, k = hn.shape
    n = b.shape[0] // 2
    nj = n // tn
    emit = layer is not None
    b2 = b.reshape(1, 2 * n).astype(F32)
    out_specs = [pl.BlockSpec((tm, tn), lambda i, j: (i, j))]
    out_shape = [jax.ShapeDtypeStruct((m, n), F32)]
    if emit:
        out_specs += [pl.BlockSpec((k, tn), lambda i, j: (0, j))] * 2
        out_shape += [jax.ShapeDtypeStruct((k, n), BF16)] * 2
    goff = nj if emit else 0
    res = pl.pallas_call(
        _pw1_kernel,
        grid=(m // tm, nj),
        in_specs=[pl.BlockSpec((tm, k), lambda i, j: (i, 0)),
                  _wspec(wa, layer, k, tn, 0, lambda i, j: j),
                  _wspec(wg, layer, k, tn, 0, lambda i, j: j + goff),
                  pl.BlockSpec((1, tn), lambda i, j: (0, j)),
                  pl.BlockSpec((1, tn), lambda i, j: (0, j + nj))],
        out_specs=out_specs,
        out_shape=out_shape,
        compiler_params=_cp("parallel", "arbitrary"),
        name="conv_pw1_glu",
    )(hn, wa, wg, b2, b2)
    return res if emit else res[0]


def _ln_swish(y, g, b):
    mu = jnp.mean(y, axis=-1, keepdims=True)
    yc = y - mu
    var = jnp.mean(yc * yc, axis=-1, keepdims=True)
    yn = yc * lax.rsqrt(var + LN_EPS) * g + b
    return yn * _sigmoid(yn)


def _conv31_kernel(u_ref, halo_ref, w_ref, b_ref, g_ref, bb_ref, o_ref, ext_ref, y3_ref, y_ref,
                   *, tm, tiles_per_seq, slab):
    seq_start = (pl.program_id(0) % tiles_per_seq) == 0
    nc = u_ref.shape[1] // HEAD_DIM
    for c in range(nc):
        cs = slice(c * HEAD_DIM, (c + 1) * HEAD_DIM)
        ext_ref[pl.ds(c, CONV_HALO, stride=slab), :] = jnp.where(seq_start, 0.0, halo_ref[:, cs])
        ext_ref[pl.ds(CONV_HALO * slab + c, tm, stride=slab), :] = u_ref[:, cs]
    off = CONV_HALO - (CONV_WIDTH - 1)
    rw = 8

    def body(i, carry):
        for ct in range(nc // 8):
            base = pl.multiple_of((i * rw + off) * slab, 8) + ct * 8
            win = [ext_ref[pl.ds(base + s * slab, 8), :] for s in range(rw + CONV_WIDTH - 1)]
            acc = [None] * rw
            for j in range(CONV_WIDTH):
                wj = w_ref[j, ct * 8:(ct + 1) * 8, :]
                for tt in range(rw):
                    term = wj * win[tt + j]
                    acc[tt] = term if j == 0 else acc[tt] + term
            bias = b_ref[ct * 8:(ct + 1) * 8, :]
            obase = pl.multiple_of(i * rw * slab, 8) + ct * 8
            for tt in range(rw):
                y3_ref[pl.ds(obase + tt * slab, 8), :] = acc[tt] + bias
        return carry

    lax.fori_loop(0, tm // rw, body, 0)
    for c in range(nc):
        y_ref[:, c * HEAD_DIM:(c + 1) * HEAD_DIM] = y3_ref[pl.ds(c, tm, stride=slab), :]
    o_ref[...] = _ln_swish(y_ref[...], g_ref[...], bb_ref[...]).astype(o_ref.dtype)


def _conv31(u, t, w, b, g, bb, tm):
    m, d = u.shape
    nc = d // HEAD_DIM
    slab = nc + 8
    row = lambda v: v.reshape(1, d).astype(F32)
    rspec = pl.BlockSpec((1, d), lambda i: (0, 0))
    hb = tm // CONV_HALO
    return pl.pallas_call(
        functools.partial(_conv31_kernel, tm=tm, tiles_per_seq=t // tm, slab=slab),
        grid=(m // tm,),
        in_specs=[pl.BlockSpec((tm, d), lambda i: (i, 0)),
                  pl.BlockSpec((CONV_HALO, d), lambda i: (jnp.maximum(i * hb - 1, 0), 0)),
                  pl.BlockSpec((CONV_WIDTH, nc, HEAD_DIM), lambda i: (0, 0, 0)),
                  pl.BlockSpec((nc, HEAD_DIM), lambda i: (0, 0)), rspec, rspec],
        out_specs=pl.BlockSpec((tm, d), lambda i: (i, 0)),
        out_shape=jax.ShapeDtypeStruct((m, d), BF16),
        scratch_shapes=[pltpu.VMEM(((tm + CONV_HALO) * slab, HEAD_DIM), F32),
                        pltpu.VMEM((tm * slab, HEAD_DIM), F32),
                        pltpu.VMEM((tm, d), F32)],
        compiler_params=_cp("parallel"),
        name="conv31_ln_swish",
    )(u, u, w.astype(F32).reshape(CONV_WIDTH, nc, HEAD_DIM), b.reshape(nc, HEAD_DIM).astype(F32),
      row(g), row(bb))


def _conv31_step_kernel(u_ref, st_ref, w_ref, b_ref, g_ref, bb_ref, o_ref):
    acc = w_ref[0:1, :] * st_ref[0]
    for j in range(1, CONV_WIDTH - 1):
        acc = acc + w_ref[j:j + 1, :] * st_ref[j]
    acc = acc + w_ref[CONV_WIDTH - 1:CONV_WIDTH, :] * u_ref[...] + b_ref[...]
    o_ref[...] = _ln_swish(acc, g_ref[...], bb_ref[...]).astype(o_ref.dtype)


def _conv31_step(u, state_t, w, b, g, bb):
    n, d = u.shape
    row = lambda v: v.reshape(1, d).astype(F32)
    return pl.pallas_call(
        _conv31_step_kernel,
        out_shape=jax.ShapeDtypeStruct((n, d), BF16),
        compiler_params=pltpu.CompilerParams(vmem_limit_bytes=VMEM_LIMIT_V7X),
        name="conv31_step",
    )(u, state_t, w.astype(F32), row(b), row(g), row(bb))


def _ffn_up_kernel(x_ref, w_ref, cwg_ref, cwv_ref, cbg_ref, cbv_ref,
                   act_ref, zlg_ref, zlv_ref, z_scr, *, t, tn, rows):
    nc = tn // HEAD_DIM
    cw = (cwg_ref, cwv_ref)
    cb = (cbg_ref, cbv_ref)
    hist = [jnp.zeros((8, HEAD_DIM), F32)] * (2 * nc)
    for p, r0 in enumerate(range(0, t, rows)):
        z = _dot(x_ref[r0:r0 + rows, :], w_ref[...])
        zc = []
        for c in range(2 * nc):
            zs = z_scr.at[p, c]
            zn = z[:, c * HEAD_DIM:(c + 1) * HEAD_DIM]
            zs[0:8, :] = hist[c]
            zs[8:, :] = zn
            hist[c] = zn[rows - 8:, :]
            half, cc = divmod(c, nc)
            ls = slice(cc * HEAD_DIM, (cc + 1) * HEAD_DIM)
            zc.append(cw[half][2:3, ls] * zn + cw[half][1:2, ls] * zs[pl.ds(7, rows, stride=1), :]
                      + cw[half][0:1, ls] * zs[pl.ds(6, rows, stride=1), :] + cb[half][:, ls])
        for c in range(nc):
            gate = zc[c]
            act_ref[r0:r0 + rows, c * HEAD_DIM:(c + 1) * HEAD_DIM] = (
                gate * _sigmoid(gate) * zc[nc + c]).astype(act_ref.dtype)
    for c in range(nc):
        zlg_ref[:, c * HEAD_DIM:(c + 1) * HEAD_DIM] = hist[c]
        zlv_ref[:, c * HEAD_DIM:(c + 1) * HEAD_DIM] = hist[nc + c]


def _ffn_up(hn, t, w, cw, cb, tn):
    m, k = hn.shape
    dff = w.shape[1] // 2
    nj = dff // tn
    cw = cw.astype(F32)
    cb2 = cb.reshape(1, 2 * dff).astype(F32)
    zl = jax.ShapeDtypeStruct((m // t, 8, dff), F32)
    rows = min(t, FFN_PART_ROWS)
    return pl.pallas_call(
        functools.partial(_ffn_up_kernel, t=t, tn=tn, rows=rows),
        grid=(m // t, nj),
        in_specs=[pl.BlockSpec((t, k), lambda i, j: (i, 0)),
                  pl.BlockSpec((k, 2 * tn), lambda i, j: (0, j)),
                  pl.BlockSpec((3, tn), lambda i, j: (0, j)),
                  pl.BlockSpec((3, tn), lambda i, j: (0, j + nj)),
                  pl.BlockSpec((1, tn), lambda i, j: (0, j)),
                  pl.BlockSpec((1, tn), lambda i, j: (0, j + nj))],
        out_specs=[pl.BlockSpec((t, tn), lambda i, j: (i, j)),
                   pl.BlockSpec((None, 8, tn), lambda i, j: (i, 0, j)),
                   pl.BlockSpec((None, 8, tn), lambda i, j: (i, 0, j))],
        out_shape=[jax.ShapeDtypeStruct((m, dff), BF16), zl, zl],
        scratch_shapes=[pltpu.VMEM((t // rows, 2 * tn // HEAD_DIM, 8 + rows, HEAD_DIM), F32)],
        compiler_params=_cp("parallel", "arbitrary"),
        name="ffn_up_conv_glu",
    )(hn, w, cw, cw, cb2, cb2)


def _ffn_up_step_kernel(x_ref, wg_ref, wv_ref, sg_ref, sv_ref, cwg_ref, cwv_ref, cbg_ref, cbv_ref,
                        act_ref, zg_ref, zv_ref, wb_ref):
    x = x_ref[...]
    tn = wg_ref.shape[1]

    def half(w_ref, c0, s_ref, cw_ref, cb_ref, z_ref):
        wb = w_ref[...].astype(BF16)
        wb_ref[:, c0:c0 + tn] = wb
        z = _dot(x, wb)
        z_ref[...] = z
        return cw_ref[2:3, :] * z + cw_ref[1:2, :] * s_ref[1] + cw_ref[0:1, :] * s_ref[0] + cb_ref[...]

    gate = half(wg_ref, 0, sg_ref, cwg_ref, cbg_ref, zg_ref)
    val = half(wv_ref, tn, sv_ref, cwv_ref, cbv_ref, zv_ref)
    act_ref[...] = (gate * _sigmoid(gate) * val).astype(act_ref.dtype)


def _ffn_up_step(hn, state_t, w, layer, cw, cb, tn):
    n, k = hn.shape
    dff = w.shape[-1] // 2
    nj = dff // tn
    cw = cw.astype(F32)
    cb2 = cb.reshape(1, 2 * dff).astype(F32)
    zs = jax.ShapeDtypeStruct((n, dff), F32)
    return pl.pallas_call(
        _ffn_up_step_kernel,
        grid=(nj,),
        in_specs=[pl.BlockSpec((n, k), lambda j: (0, 0)),
                  _wspec(w, layer, k, tn, 0, lambda j: j),
                  _wspec(w, layer, k, tn, 0, lambda j: j + nj),
                  pl.BlockSpec((2, n, tn), lambda j: (0, 0, j)),
                  pl.BlockSpec((2, n, tn), lambda j: (0, 0, j + nj)),
                  pl.BlockSpec((3, tn), lambda j: (0, j)),
                  pl.BlockSpec((3, tn), lambda j: (0, j + nj)),
                  pl.BlockSpec((1, tn), lambda j: (0, j)),
                  pl.BlockSpec((1, tn), lambda j: (0, j + nj))],
        out_specs=[pl.BlockSpec((n, tn), lambda j: (0, j)),
                   pl.BlockSpec((n, tn), lambda j: (0, j)),
                   pl.BlockSpec((n, tn), lambda j: (0, j)),
                   pl.BlockSpec((k, 2 * tn), lambda j: (0, j))],
        out_shape=[jax.ShapeDtypeStruct((n, dff), BF16), zs, zs,
                   jax.ShapeDtypeStruct((k, 2 * dff), BF16)],
        compiler_params=_cp("arbitrary"),
        name="ffn_up_step",
    )(hn, w, w, state_t, state_t, cw, cw, cb2, cb2)


def _ffn_prompt(x, t, norm_g, w_up, cw, cb, w_down):
    hn = _rmsnorm(x, norm_g, BF16)
    act, zlg, zlv = _ffn_up(hn, t, w_up, cw, cb, FFN_TN)
    last = jnp.concatenate([zlg[:, 6:8], zlv[:, 6:8]], axis=-1)
    xo = _mm_res(act, w_down, None, jnp.zeros((w_down.shape[1],), F32), x, FFN_DOWN_TM, FFN_DOWN_TN, "ffn_down")
    return xo, last


def _ffn_step(x, prev_state_t, norm_g, w_up, w_down, layer, cw, cb):
    m = x.shape[0]
    hn = _rmsnorm(x, norm_g, BF16)
    act, zg, zv, wu_b = _ffn_up_step(hn, prev_state_t, w_up, layer, cw, cb, FFN_TN)
    znew = jnp.concatenate([zg, zv], axis=-1)
    last = jnp.concatenate([prev_state_t[1][:, None], znew[:, None]], axis=1)
    xo, wd_b = _mm_res(act, w_down, layer, jnp.zeros((w_down.shape[-1],), F32), x, m, STEP_FFN_DOWN_TN,
                       "ffn_down_step")
    return xo, last, (wu_b, wd_b)


def kernel(x_prompt, x_sample, cache_a0, cache_a1, cache_a2, state_ssm, state_conv, state_ffn, norm_mix, norm_ffn, norm_final, w_in_even, w_out_even, ssm_a_re, ssm_a_im, ssm_b_re, ssm_b_im, ssm_c_re, ssm_c_im, ssm_d, ssm_log_dt, w_glu, b_glu, w_pw1, b_pw1, w_dw, b_dw, ln_g, ln_b, w_pw2, b_pw2, w_up, ffn_dw, ffn_dw_b, w_down):
    n_p, t_p, d_model = x_prompt.shape
    n_s = x_sample.shape[0]
    caches = (cache_a0, cache_a1, cache_a2)
    u_col0 = N_DIL * QKV_W
    sp = _ssm_params(ssm_a_re[0], ssm_a_im[0], ssm_b_re[0], ssm_b_im[0], ssm_c_re[0], ssm_c_im[0],
                     ssm_d[0], ssm_log_dt[0])
    ns = sp[0].shape[0]

    def ssm_state(hr, him, n):
        def unslab(h):
            return h.reshape(ns, n, SLAB_G, SSM_STATE).transpose(1, 0, 2, 3).reshape(n, ns * SLAB_G, SSM_STATE)
        return jnp.stack([unslab(hr), unslab(him)], axis=-1)[None]

    xs = x_sample.reshape(n_s, d_model)
    cos_s, sin_s = _rope_tables(n_s, PAST_LEN, 0)
    hn = _rmsnorm(xs, norm_mix[0], BF16)
    zs, w_in_b = _inproj(hn, w_in_even, 0, cos_s, sin_s, n_s, STEP_INPROJ_TN)
    attn_s = _step_attn(zs, [c[0] for c in caches])

    def slab(h):
        return h.reshape(n_s, ns, SLAB_S).transpose(1, 0, 2)

    y_s, hr_s, hi_s = _ssm_step(zs, u_col0, slab(state_ssm[0, ..., 0]), slab(state_ssm[0, ..., 1]), sp)
    gate_s, w_glu_b = _ssm_gate_step(y_s, w_glu, 0, b_glu[0], STEP_GATE_TN)
    xs, w_oa_b, w_os_b = _outproj(attn_s, gate_s, w_out_even, w_out_even, 0, xs, n_s, STEP_OUTPROJ_TN)
    xs, ffn0_s, wf0 = _ffn_step(xs, state_ffn[0].transpose(1, 0, 2), norm_ffn[0], w_up, w_down, 0,
                                ffn_dw[0], ffn_dw_b[0])
    hn = _rmsnorm(xs, norm_mix[1], BF16)
    us, w_pa_b, w_pg_b = _pw1(hn, w_pw1, w_pw1, 0, b_pw1[0], n_s, STEP_PW1_TN)
    cs = _conv31_step(us, state_conv[0].transpose(1, 0, 2), w_dw[0], b_dw[0], ln_g[0], ln_b[0])
    conv_s = jnp.concatenate([state_conv[0][:, 1:], us[:, None]], axis=1)
    xs, w_pw2_b = _mm_res(cs, w_pw2, 0, b_pw2[0], xs, n_s, STEP_PW2_TN, "conv_pw2_step")
    xs, ffn1_s, wf1 = _ffn_step(xs, state_ffn[1].transpose(1, 0, 2), norm_ffn[1], w_up, w_down, 1,
                                ffn_dw[1], ffn_dw_b[1])
    y_sample = _rmsnorm(xs, norm_final, F32).reshape(n_s, 1, d_model)

    a_s = []
    for g in range(N_DIL):
        kv = zs[:, g * QKV_W + D_A:(g + 1) * QKV_W].reshape(n_s, 1, 2, A_HEADS, HEAD_DIM)
        a_s.append(_cache_shift(caches[g][0], kv)[None])
    ssm_s = ssm_state(hr_s, hi_s, n_s)
    conv_s = conv_s[None]
    ffn_s = jnp.stack([ffn0_s, ffn1_s])

    mp = n_p * t_p
    tm = PROMPT_TM
    x = x_prompt.reshape(mp, d_model)
    cos, sin = _rope_tables(t_p, 0, 1)
    hn = _rmsnorm(x, norm_mix[0], BF16)
    z = _inproj(hn, w_in_b, None, cos, sin, tm, INPROJ_TN)
    outs, lses = [], []
    for g in range(N_DIL):
        o, lse = _band_attn(z, n_p, t_p, g)
        outs.append(o)
        lses.append(lse)
    attn = _combine(outs, lses)
    y_ssm, hlr, hli = _ssm_prompt(z, n_p, t_p, u_col0, sp)
    gate_p = _ssm_gate(y_ssm, w_glu_b, b_glu[0], GATE_TM)
    x = _outproj(attn, gate_p, w_oa_b, w_os_b, None, x, tm, OUTPROJ_TN)
    x, ffn0_p = _ffn_prompt(x, t_p, norm_ffn[0], wf0[0], ffn_dw[0], ffn_dw_b[0], wf0[1])
    hn = _rmsnorm(x, norm_mix[1], BF16)
    u = _pw1(hn, w_pa_b, w_pg_b, None, b_pw1[0], tm, PW1_TN)
    c = _conv31(u, t_p, w_dw[0], b_dw[0], ln_g[0], ln_b[0], CONV_TM)
    conv_p = u.reshape(n_p, t_p, -1)[None, :, t_p - (CONV_WIDTH - 1):]
    x = _mm_res(c, w_pw2_b, None, b_pw2[0], x, tm, PW2_TN, "conv_pw2")
    x, ffn1_p = _ffn_prompt(x, t_p, norm_ffn[1], wf1[0], ffn_dw[1], ffn_dw_b[1], wf1[1])
    y_prompt = _rmsnorm(x, norm_final, F32).reshape(n_p, t_p, d_model)

    a_p = []
    for g in range(N_DIL):
        a_p.append(_kv_rows(z, n_p, t_p, g, min(A_WINDOWS[g], PAST_LEN))[None])
    ssm_p = ssm_state(hlr, hli, n_p)
    ffn_p = jnp.stack([ffn0_p, ffn1_p])

    return (y_prompt, y_sample, a_p[0], a_s[0], a_p[1], a_s[1], a_p[2], a_s[2],
            ssm_p, ssm_s, conv_p, conv_s, ffn_p, ffn_s)
```
